```python
import math
import jax
import jax.numpy as jnp
from jax import lax
import numpy as np

D_MODEL = 1024
BATCH = 8
SEQ = 2048
DEPTH = 2
DEC_BATCH = 128
DEC_SEQ = 1
PAST_LEN = 16384
PAGE_SIZE = 128

N_EVEN = (DEPTH + 1) // 2
N_ODD = DEPTH // 2
HALF = D_MODEL // 2

GLA_H = 4
GLA_DV = HALF // GLA_H
GLA_DK = GLA_DV // 2
GLA_KW = GLA_H * GLA_DK
GLA_VW = GLA_H * GLA_DV
GLA_LR = 16
GLA_TAU = 16.0
CHUNK = 64

LRU_W = HALF
LRU_H = 8
LRU_BW = LRU_W // LRU_H
CONV_W = 4
RG_C = 8.0

S5_W = HALF
S5_GH = 16
S5_G = S5_W // S5_GH
S5_P = 64

HG_H = 4
HG_D = HALF // HG_H
HG_W = HG_H * HG_D

D_FF = ((8 * D_MODEL // 3 + 255) // 256) * 256

EVEN_SPLITS = (GLA_KW, GLA_KW, GLA_VW, GLA_VW, GLA_LR, LRU_W, LRU_W)
ODD_SPLITS = (S5_W, HG_W, HG_W, HG_W, HG_W)
EVEN_IN = sum(EVEN_SPLITS)
ODD_IN = sum(ODD_SPLITS)
EVEN_OUT = GLA_VW + LRU_W
ODD_OUT = S5_W + HG_W

ALPHA = (2.0 * DEPTH) ** 0.25
BETA = (8.0 * DEPTH) ** -0.25
LN_EPS = 1e-5
RMS_EPS = 1e-6

kernel_name = 'hybrid_gla_rglru_s5_hgrn2_step'


def split_cols(z, sizes):
    idx = [int(s) for s in np.cumsum(sizes)[:-1]]
    return jnp.split(z, idx, axis=-1)


def layer_norm(x, g, b):
    xf = x.astype(jnp.float32)
    mu = jnp.mean(xf, axis=-1, keepdims=True)
    var = jnp.mean(jnp.square(xf - mu), axis=-1, keepdims=True)
    return (xf - mu) * lax.rsqrt(var + LN_EPS) * g + b


def rms_norm(x, g):
    xf = x.astype(jnp.float32)
    return xf * lax.rsqrt(jnp.mean(xf * xf, axis=-1, keepdims=True) + RMS_EPS) * g


def swiglu(h, w_in, w_out):
    gate, up = jnp.split(jnp.einsum('bld,df->blf', h, w_in), 2, axis=-1)
    return jnp.einsum('blf,fd->bld', jax.nn.silu(gate) * up, w_out)


def gated_linear_attention(q, k, v, log_a, s0):
    bsz, L = q.shape[0], q.shape[1]
    c = min(CHUNK, L)
    n = -(-L // c)
    pad = n * c - L

    def blocks(t):
        t = jnp.pad(t.astype(jnp.float32), [(0, 0), (0, pad)] + [(0, 0)] * (t.ndim - 2))
        return jnp.moveaxis(t.reshape((bsz, n, c) + t.shape[2:]), 1, 0)

    qb, kb, vb, gb = blocks(q), blocks(k), blocks(v), blocks(log_a)
    cum = jnp.cumsum(gb, axis=2)
    causal = jnp.tril(jnp.ones((c, c), dtype=bool))[:, :, None, None]

    def step(s, blk):
        qc, kc, vc, bc = blk
        rel = jnp.where(causal, bc[:, :, None] - bc[:, None, :], -jnp.inf)
        scores = jnp.einsum('btshk,bthk,bshk->bhts', jnp.exp(rel), qc, kc)
        last = bc[:, -1]
        o = (jnp.einsum('bhts,bshv->bthv', scores, vc)
             + jnp.einsum('bthk,bhkv->bthv', qc * jnp.exp(bc), s))
        s_new = (jnp.exp(last)[..., None] * s
                 + jnp.einsum('bshk,bshv->bhkv', kc * jnp.exp(last[:, None] - bc), vc))
        return s_new, o

    s_fin, ob = lax.scan(step, s0.astype(jnp.float32), (qb, kb, vb, cum))
    o = jnp.moveaxis(ob, 0, 1).reshape((bsz, n * c) + ob.shape[3:])[:, :L]
    return o, s_fin


def causal_depthwise_conv(buf, x, w, b):
    L = x.shape[1]
    xp = jnp.concatenate([buf, x], axis=1)
    y = b + xp[:, 0:L] * w[0]
    for j in range(1, CONV_W):
        y = y + xp[:, j:j + L] * w[j]
    return y, xp[:, xp.shape[1] - (CONV_W - 1):]


def s5_ssm(u, s_re, s_im, lam_re, lam_im, log_dt, b_re, b_im, c_re, c_im, d_skip):
    f32 = jnp.float32
    bsz, L = u.shape[0], u.shape[1]
    uf = u.astype(f32).reshape(bsz, L, S5_G, S5_GH)
    dt = jnp.exp(log_dt.astype(f32))[:, None]
    lr_, li_ = lam_re.astype(f32), lam_im.astype(f32)
    mag = jnp.exp(lr_ * dt)
    ab_re, ab_im = mag * jnp.cos(li_ * dt), mag * jnp.sin(li_ * dt)
    den = lr_ * lr_ + li_ * li_
    nr, ni = ab_re - 1.0, ab_im
    f_re = (nr * lr_ + ni * li_) / den
    f_im = (ni * lr_ - nr * li_) / den
    bb_re = f_re[..., None] * b_re - f_im[..., None] * b_im
    bb_im = f_re[..., None] * b_im + f_im[..., None] * b_re
    bu_re = jnp.einsum('blgh,gph->blgp', uf, bb_re)
    bu_im = jnp.einsum('blgh,gph->blgp', uf, bb_im)
    a_re = jnp.broadcast_to(ab_re, bu_re.shape)
    a_im = jnp.broadcast_to(ab_im, bu_re.shape)

    def combine(e1, e2):
        a1r, a1i, b1r, b1i = e1
        a2r, a2i, b2r, b2i = e2
        return (a1r * a2r - a1i * a2i, a1r * a2i + a1i * a2r,
                a2r * b1r - a2i * b1i + b2r, a2r * b1i + a2i * b1r + b2i)

    _, _, hr, hi = lax.associative_scan(combine, (a_re, a_im, bu_re, bu_im), axis=1)
    steps = jnp.arange(1, L + 1, dtype=f32)[:, None, None]
    pm = jnp.exp(lr_ * dt * steps)
    ang = li_ * dt * steps
    p_re, p_im = pm * jnp.cos(ang), pm * jnp.sin(ang)
    s0r, s0i = s_re.astype(f32)[:, None], s_im.astype(f32)[:, None]
    hr = hr + p_re * s0r - p_im * s0i
    hi = hi + p_re * s0i + p_im * s0r
    y = (jnp.einsum('ghp,blgp->blgh', c_re, hr) - jnp.einsum('ghp,blgp->blgh', c_im, hi)
         + d_skip.reshape(S5_G, S5_GH) * uf)
    return y.reshape(bsz, L, S5_W), hr[:, -1], hi[:, -1]


def even_mixer(h, s_gla, s_conv, s_lru, w_in, gla_w_lr, gla_b_lr, gla_norm_g,
               conv_w, conv_b, lru_w_r, lru_b_r, lru_w_i, lru_b_i, lru_lam, w_out):
    bsz, L = h.shape[0], h.shape[1]
    z = jnp.einsum('bld,de->ble', h, w_in).astype(jnp.float32)
    q, k, v, g, lr, xr, gr = split_cols(z, EVEN_SPLITS)
    log_alpha = jax.nn.log_sigmoid(jnp.einsum('blr,rk->blk', lr, gla_w_lr) + gla_b_lr) / GLA_TAU
    o, s_gla_new = gated_linear_attention(
        q.reshape(bsz, L, GLA_H, GLA_DK) * (GLA_DK ** -0.5),
        k.reshape(bsz, L, GLA_H, GLA_DK),
        v.reshape(bsz, L, GLA_H, GLA_DV),
        log_alpha.reshape(bsz, L, GLA_H, GLA_DK), s_gla)
    y_gla = rms_norm(o, gla_norm_g).reshape(bsz, L, GLA_VW) * jax.nn.silu(g)
    xc, s_conv_new = causal_depthwise_conv(s_conv.astype(jnp.float32), xr, conv_w, conv_b)
    xb = xc.reshape(bsz, L, LRU_H, LRU_BW)
    r = jax.nn.sigmoid(jnp.einsum('blhi,hij->blhj', xb, lru_w_r).reshape(bsz, L, LRU_W) + lru_b_r)
    i = jax.nn.sigmoid(jnp.einsum('blhi,hij->blhj', xb, lru_w_i).reshape(bsz, L, LRU_W) + lru_b_i)
    log_a = -RG_C * r * jax.nn.softplus(-lru_lam.astype(jnp.float32))
    u = jnp.sqrt(-jnp.expm1(2.0 * log_a)) * (i * xc)

    def lru_step(h_prev, inp):
        a_t, u_t = inp
        h_t = a_t * h_prev + u_t
        return h_t, h_t

    s_lru_new, hs = lax.scan(lru_step, s_lru.astype(jnp.float32),
                             (jnp.swapaxes(jnp.exp(log_a), 0, 1), jnp.swapaxes(u, 0, 1)))
    y_lru = jnp.swapaxes(hs, 0, 1) * jax.nn.gelu(gr)
    out = jnp.einsum('ble,ed->bld', jnp.concatenate([y_gla, y_lru], axis=-1), w_out)
    return out, s_gla_new, s_conv_new, s_lru_new


def odd_mixer(h, s_re, s_im, s_hg, lower_bound, w_in, lam_re, lam_im, log_dt, b_re, b_im,
              c_re, c_im, d_skip, w_glu, b_glu, hg_norm_g, w_out):
    bsz, L = h.shape[0], h.shape[1]
    z = jnp.einsum('bld,de->ble', h, w_in).astype(jnp.float32)
    u, q, f, i, g = split_cols(z, ODD_SPLITS)
    y, s_re_new, s_im_new = s5_ssm(u, s_re, s_im, lam_re, lam_im, log_dt, b_re, b_im,
                                   c_re, c_im, d_skip)
    y = jax.nn.gelu(y)
    y_s5 = y * jax.nn.sigmoid(jnp.einsum('ble,ef->blf', y, w_glu) + b_glu)
    forget = lower_bound + (1.0 - lower_bound) * jax.nn.sigmoid(f)
    hd = lambda t: t.reshape(bsz, L, HG_H, HG_D)
    o, s_hg_new = gated_linear_attention(hd(jax.nn.silu(q)), hd(1.0 - forget), hd(i),
                                         hd(jnp.log(forget)), s_hg)
    y_hg = rms_norm(o, hg_norm_g).reshape(bsz, L, HG_W) * jax.nn.silu(g)
    out = jnp.einsum('ble,ed->bld', jnp.concatenate([y_s5, y_hg], axis=-1), w_out)
    return out, s_re_new, s_im_new, s_hg_new


def setup_inputs(seed: int = 0) -> dict:
    key = jax.random.key(seed)
    ks = iter(jax.random.split(key, 64))
    f32 = jnp.float32

    def nrm(shape, scale):
        return jax.random.normal(next(ks), shape, f32) * scale

    x_prompt = nrm((BATCH, SEQ, D_MODEL), 1.0)
    x_sample = nrm((DEC_BATCH, DEC_SEQ, D_MODEL), 1.0)
    c_prompt = nrm((BATCH, D_MODEL), 1.0)
    c_sample = nrm((DEC_BATCH, D_MODEL), 1.0)
    state_gla = nrm((N_EVEN, DEC_BATCH, GLA_H, GLA_DK, GLA_DV), 1.0)
    state_rglru_conv = nrm((N_EVEN, DEC_BATCH, CONV_W - 1, LRU_W), 1.0)
    state_rglru_h = nrm((N_EVEN, DEC_BATCH, LRU_W), 0.5)
    state_s5_re = nrm((N_ODD, DEC_BATCH, S5_G, S5_P), 0.1)
    state_s5_im = nrm((N_ODD, DEC_BATCH, S5_G, S5_P), 0.1)
    state_hgrn = nrm((N_ODD, DEC_BATCH, HG_H, HG_D, HG_D), 0.4)

    ev_w_in = nrm((N_EVEN, D_MODEL, EVEN_IN), D_MODEL ** -0.5)
    ev_gla_w_lr = nrm((N_EVEN, GLA_LR, GLA_KW), GLA_LR ** -0.5)
    ev_gla_b_lr = nrm((N_EVEN, GLA_KW), 0.1)
    ev_gla_norm_g = 1.0 + nrm((N_EVEN, GLA_DV), 0.1)
    ev_conv_w = nrm((N_EVEN, CONV_W, LRU_W), CONV_W ** -0.5)
    ev_conv_b = nrm((N_EVEN, LRU_W), 0.01)
    ev_lru_w_r = nrm((N_EVEN, LRU_H, LRU_BW, LRU_BW), LRU_BW ** -0.5)
    ev_lru_b_r = nrm((N_EVEN, LRU_W), 0.01)
    ev_lru_w_i = nrm((N_EVEN, LRU_H, LRU_BW, LRU_BW), LRU_BW ** -0.5)
    ev_lru_b_i = nrm((N_EVEN, LRU_W), 0.01)
    a_c = jax.random.uniform(next(ks), (N_EVEN, LRU_W), f32, minval=0.9, maxval=0.999)
    sig = a_c ** (1.0 / RG_C)
    ev_lru_lam = jnp.log(sig) - jnp.log1p(-sig)
    ev_w_out = nrm((N_EVEN, EVEN_OUT, D_MODEL), BETA * EVEN_OUT ** -0.5)

    od_w_in = nrm((N_ODD, D_MODEL, ODD_IN), D_MODEL ** -0.5)
    od_s5_lam_re = -0.5 + nrm((N_ODD, S5_G, S5_P), 0.01)
    od_s5_lam_im = jnp.pi * jnp.arange(S5_P, dtype=f32) + nrm((N_ODD, S5_G, S5_P), 0.01)
    od_s5_log_dt = jax.random.uniform(next(ks), (N_ODD, S5_G), f32,
                                      minval=math.log(0.001), maxval=math.log(0.1))
    od_s5_b_re = nrm((N_ODD, S5_G, S5_P, S5_GH), (2.0 * S5_GH) ** -0.5)
    od_s5_b_im = nrm((N_ODD, S5_G, S5_P, S5_GH), (2.0 * S5_GH) ** -0.5)
    od_s5_c_re = nrm((N_ODD, S5_G, S5_GH, S5_P), S5_P ** -0.5)
    od_s5_c_im = nrm((N_ODD, S5_G, S5_GH, S5_P), S5_P ** -0.5)
    od_s5_d = nrm((N_ODD, S5_W), 0.5)
    od_s5_w_glu = nrm((N_ODD, S5_W, S5_W), S5_W ** -0.5)
    od_s5_b_glu = nrm((N_ODD, S5_W), 0.01)
    hg_lb_logits = nrm((DEPTH, HG_W), 0.5)
    od_hg_norm_g = 1.0 + nrm((N_ODD, HG_D), 0.1)
    od_w_out = nrm((N_ODD, ODD_OUT, D_MODEL), BETA * ODD_OUT ** -0.5)

    w_ada = nrm((DEPTH, D_MODEL, 6 * D_MODEL), 0.5 * D_MODEL ** -0.5)
    b_ada = nrm((DEPTH, 6 * D_MODEL), 0.01)
    ln_g = 1.0 + nrm((DEPTH, 2, D_MODEL), 0.1)
    ln_b = nrm((DEPTH, 2, D_MODEL), 0.01)
    ffn_w_in = nrm((DEPTH, D_MODEL, 2 * D_FF), D_MODEL ** -0.5)
    ffn_w_out = nrm((DEPTH, D_FF, D_MODEL), BETA * D_FF ** -0.5)

    return {
        'x_prompt': x_prompt, 'x_sample': x_sample, 'c_prompt': c_prompt, 'c_sample': c_sample,
        'state_gla': state_gla, 'state_rglru_conv': state_rglru_conv,
        'state_rglru_h': state_rglru_h, 'state_s5_re': state_s5_re,
        'state_s5_im': state_s5_im, 'state_hgrn': state_hgrn,
        'ev_w_in': ev_w_in, 'ev_gla_w_lr': ev_gla_w_lr, 'ev_gla_b_lr': ev_gla_b_lr,
        'ev_gla_norm_g': ev_gla_norm_g, 'ev_conv_w': ev_conv_w, 'ev_conv_b': ev_conv_b,
        'ev_lru_w_r': ev_lru_w_r, 'ev_lru_b_r': ev_lru_b_r, 'ev_lru_w_i': ev_lru_w_i,
        'ev_lru_b_i': ev_lru_b_i, 'ev_lru_lam': ev_lru_lam, 'ev_w_out': ev_w_out,
        'od_w_in': od_w_in, 'od_s5_lam_re': od_s5_lam_re, 'od_s5_lam_im': od_s5_lam_im,
        'od_s5_log_dt': od_s5_log_dt, 'od_s5_b_re': od_s5_b_re, 'od_s5_b_im': od_s5_b_im,
        'od_s5_c_re': od_s5_c_re, 'od_s5_c_im': od_s5_c_im, 'od_s5_d': od_s5_d,
        'od_s5_w_glu': od_s5_w_glu, 'od_s5_b_glu': od_s5_b_glu,
        'hg_lb_logits': hg_lb_logits, 'od_hg_norm_g': od_hg_norm_g, 'od_w_out': od_w_out,
        'w_ada': w_ada, 'b_ada': b_ada, 'ln_g': ln_g, 'ln_b': ln_b,
        'ffn_w_in': ffn_w_in, 'ffn_w_out': ffn_w_out,
    }


def reference(x_prompt, x_sample, c_prompt, c_sample,
              state_gla, state_rglru_conv, state_rglru_h, state_s5_re, state_s5_im, state_hgrn,
              ev_w_in, ev_gla_w_lr, ev_gla_b_lr, ev_gla_norm_g, ev_conv_w, ev_conv_b,
              ev_lru_w_r, ev_lru_b_r, ev_lru_w_i, ev_lru_b_i, ev_lru_lam, ev_w_out,
              od_w_in, od_s5_lam_re, od_s5_lam_im, od_s5_log_dt, od_s5_b_re, od_s5_b_im,
              od_s5_c_re, od_s5_c_im, od_s5_d, od_s5_w_glu, od_s5_b_glu,
              hg_lb_logits, od_hg_norm_g, od_w_out,
              w_ada, b_ada, ln_g, ln_b, ffn_w_in, ffn_w_out):
    f32 = jnp.float32
    sm = jax.nn.softmax(hg_lb_logits.astype(f32), axis=0)
    lower_bounds = jnp.cumsum(sm, axis=0) - sm[0]

    def run(x, c, s_gla, s_conv, s_lru, s_re, s_im, s_hg):
        x = x.astype(f32)
        cond = jax.nn.silu(c.astype(f32))
        n_gla, n_conv, n_lru, n_re, n_im, n_hg = [], [], [], [], [], []
        for l in range(DEPTH):
            mod = jnp.einsum('bd,de->be', cond, w_ada[l]) + b_ada[l]
            sh_m, sc_m, gt_m, sh_f, sc_f, gt_f = [m[:, None, :] for m in jnp.split(mod, 6, axis=-1)]
            h = x * (1.0 + sc_m) + sh_m
            if l % 2 == 0:
                e = l // 2
                out, t1, t2, t3 = even_mixer(
                    h, s_gla[e], s_conv[e], s_lru[e], ev_w_in[e], ev_gla_w_lr[e], ev_gla_b_lr[e],
                    ev_gla_norm_g[e], ev_conv_w[e], ev_conv_b[e], ev_lru_w_r[e], ev_lru_b_r[e],
                    ev_lru_w_i[e], ev_lru_b_i[e], ev_lru_lam[e], ev_w_out[e])
                n_gla.append(t1)
                n_conv.append(t2)
                n_lru.append(t3)
            else:
                o = l // 2
                out, t1, t2, t3 = odd_mixer(
                    h, s_re[o], s_im[o], s_hg[o], lower_bounds[l], od_w_in[o], od_s5_lam_re[o],
                    od_s5_lam_im[o], od_s5_log_dt[o], od_s5_b_re[o], od_s5_b_im[o], od_s5_c_re[o],
                    od_s5_c_im[o], od_s5_d[o], od_s5_w_glu[o], od_s5_b_glu[o], od_hg_norm_g[o],
                    od_w_out[o])
                n_re.append(t1)
                n_im.append(t2)
                n_hg.append(t3)
            x = layer_norm(ALPHA * x + (1.0 + gt_m) * out, ln_g[l, 0], ln_b[l, 0])
            h = x * (1.0 + sc_f) + sh_f
            x = layer_norm(ALPHA * x + (1.0 + gt_f) * swiglu(h, ffn_w_in[l], ffn_w_out[l]),
                           ln_g[l, 1], ln_b[l, 1])
        return (x, jnp.stack(n_gla), jnp.stack(n_conv), jnp.stack(n_lru),
                jnp.stack(n_re), jnp.stack(n_im), jnp.stack(n_hg))

    bp = x_prompt.shape[0]
    zeros = lambda shape: jnp.zeros(shape, f32)
    y_p, gla_p, conv_p, lru_p, re_p, im_p, hg_p = run(
        x_prompt, c_prompt,
        zeros((N_EVEN, bp, GLA_H, GLA_DK, GLA_DV)), zeros((N_EVEN, bp, CONV_W - 1, LRU_W)),
        zeros((N_EVEN, bp, LRU_W)), zeros((N_ODD, bp, S5_G, S5_P)),
        zeros((N_ODD, bp, S5_G, S5_P)), zeros((N_ODD, bp, HG_H, HG_D, HG_D)))
    y_s, gla_s, conv_s, lru_s, re_s, im_s, hg_s = run(
        x_sample, c_sample, state_gla, state_rglru_conv, state_rglru_h,
        state_s5_re, state_s5_im, state_hgrn)
    return (y_p.astype(x_prompt.dtype), y_s.astype(x_sample.dtype), gla_p, gla_s, conv_p, conv_s,
            lru_p, lru_s, re_p, re_s, im_p, im_s, hg_p, hg_s)
```

```python
import functools
import math

import numpy as np
import jax
import jax.numpy as jnp
from jax import lax
from jax.experimental import pallas as pl
from jax.experimental.pallas import tpu as pltpu

f32 = jnp.float32
bf16 = jnp.bfloat16

D_MODEL = 1024
DEPTH = 2
HALF = D_MODEL // 2
GLA_H = 4
GLA_DV = HALF // GLA_H
GLA_DK = GLA_DV // 2
GLA_KW = GLA_H * GLA_DK
GLA_VW = GLA_H * GLA_DV
GLA_LR = 16
GLA_TAU = 16.0
LRU_W = HALF
LRU_H = 8
LRU_BW = LRU_W // LRU_H
CONV_W = 4
RG_C = 8.0
S5_W = HALF
S5_GH = 16
S5_G = S5_W // S5_GH
S5_P = 64
S5_N = S5_G * S5_P
HG_H = 4
HG_D = HALF // HG_H
HG_W = HG_H * HG_D
D_FF = ((8 * D_MODEL // 3 + 255) // 256) * 256
ALPHA = (2.0 * DEPTH) ** 0.25
LN_EPS = 1e-5
RMS_EPS = 1e-6

LANES = 128
SUBLANES = 8
VMEM_LIMIT = 56 * 1024 * 1024

CHUNK = 64
N_LEVELS = 6
FF_CHUNK = 256
EVEN_N = 2688
ODD_N = 2560


def _sigmoid(x):
    return jax.nn.sigmoid(x)


def _silu(x):
    return x * jax.nn.sigmoid(x)


def _gelu(x):
    c = math.sqrt(2.0 / math.pi)
    return x * (0.5 * (1.0 + jnp.tanh(c * (x + 0.044715 * (x * x * x)))))


def _softplus(x):
    return jnp.maximum(x, 0.0) + jnp.log1p(jnp.exp(-jnp.abs(x)))


def _log_sigmoid(x):
    return -_softplus(-x)


def _expm1(x):
    u = jnp.exp(x)
    um1 = u - 1.0
    lg = jnp.log(u)
    safe = jnp.where(lg == 0.0, 1.0, lg)
    return jnp.where(um1 == 0.0, x, jnp.where(um1 == -1.0, -1.0, um1 * x / safe))


def _layer_norm(x, g, b):
    mu = jnp.mean(x, axis=-1, keepdims=True)
    xc = x - mu
    var = jnp.mean(xc * xc, axis=-1, keepdims=True)
    return xc * lax.rsqrt(var + LN_EPS) * g + b


def _dot(a, b):
    return jnp.dot(a, b, preferred_element_type=f32)


def _dot_nt(a, b):
    return lax.dot_general(a, b, (((1,), (1,)), ((), ())), preferred_element_type=f32)


def _dot_tn(a, b):
    return lax.dot_general(a, b, (((0,), (0,)), ((), ())), preferred_element_type=f32)


def _const_spec(shape):
    nd = len(shape)
    return pl.BlockSpec(shape, lambda *_: (0,) * nd, pipeline_mode=pl.Buffered(1))


def _params(*sem):
    return pltpu.CompilerParams(dimension_semantics=sem, vmem_limit_bytes=VMEM_LIMIT)


def _mod_kernel(c_ref, w_ref, b_ref, o_ref):
    cond = _silu(c_ref[...]).astype(bf16)
    o_ref[...] = _dot(cond, w_ref[...].astype(bf16)) + b_ref[...]


def _modulation(c_all, w_ada, b_ada):
    rows = c_all.shape[0]
    tn = 1536
    return pl.pallas_call(
        _mod_kernel,
        out_shape=jax.ShapeDtypeStruct((DEPTH, rows, 6 * D_MODEL), f32),
        grid=(DEPTH, 6 * D_MODEL // tn),
        in_specs=[pl.BlockSpec((rows, D_MODEL), lambda l, j: (0, 0)),
                  pl.BlockSpec((None, D_MODEL, tn), lambda l, j: (l, 0, j)),
                  pl.BlockSpec((None, 1, tn), lambda l, j: (l, 0, j))],
        out_specs=pl.BlockSpec((None, rows, tn), lambda l, j: (l, 0, j)),
        compiler_params=_params("parallel", "parallel"),
        name="adaln_mod",
    )(c_all, w_ada, b_ada.reshape(DEPTH, 1, 6 * D_MODEL))


def _mod_spec(rows_per_block, tiles_per_batch, j):
    return pl.BlockSpec((None, rows_per_block, D_MODEL), lambda i: (i // tiles_per_batch, 0, j))


def _inproj_kernel(x_ref, sc_ref, sh_ref, w_ref, z_ref):
    h = x_ref[...] * (1.0 + sc_ref[...]) + sh_ref[...]
    z_ref[...] = _dot(h.astype(bf16), w_ref[...])


def _inproj(x, mod, w, tm, tiles_per_batch):
    t, n = x.shape[0], w.shape[1]
    r = mod.shape[1]
    return pl.pallas_call(
        _inproj_kernel,
        out_shape=jax.ShapeDtypeStruct((t, n), f32),
        grid=(t // tm,),
        in_specs=[pl.BlockSpec((tm, D_MODEL), lambda i: (i, 0)),
                  _mod_spec(r, tiles_per_batch, 1), _mod_spec(r, tiles_per_batch, 0),
                  _const_spec((D_MODEL, n))],
        out_specs=pl.BlockSpec((tm, n), lambda i: (i, 0)),
        compiler_params=_params("parallel"),
        name="in_proj",
    )(x, mod, mod, w)


def _outffn_kernel(x_ref, ya_ref, yb_ref, gtm_ref, shf_ref, scf_ref, gtf_ref,
                   wo_ref, lng_ref, lnb_ref, wi_ref, wf_ref, o_ref):
    mix = (_dot(ya_ref[...].astype(bf16), wo_ref[0:HALF, :])
           + _dot(yb_ref[...].astype(bf16), wo_ref[HALF:D_MODEL, :]))
    x1 = _layer_norm(ALPHA * x_ref[...] + (1.0 + gtm_ref[...]) * mix, lng_ref[0:1, :], lnb_ref[0:1, :])
    h = (x1 * (1.0 + scf_ref[...]) + shf_ref[...]).astype(bf16)
    acc = jnp.zeros(x1.shape, f32)
    for c in range(D_FF // FF_CHUNK):
        lo = c * FF_CHUNK
        gate = _dot(h, wi_ref[:, lo:lo + FF_CHUNK])
        up = _dot(h, wi_ref[:, D_FF + lo:D_FF + lo + FF_CHUNK])
        act = (_silu(gate) * up).astype(bf16)
        acc = acc + _dot(act, wf_ref[lo:lo + FF_CHUNK, :])
    o_ref[...] = _layer_norm(ALPHA * x1 + (1.0 + gtf_ref[...]) * acc, lng_ref[1:2, :], lnb_ref[1:2, :])


def _outffn(x, ya, yb, mod, w_out, ln_g, ln_b, w_in, w_f, tm, tiles_per_batch):
    t = x.shape[0]
    r = mod.shape[1]
    row = lambda w: pl.BlockSpec((tm, w), lambda i: (i, 0))
    return pl.pallas_call(
        _outffn_kernel,
        out_shape=jax.ShapeDtypeStruct((t, D_MODEL), f32),
        grid=(t // tm,),
        in_specs=[row(D_MODEL), row(HALF), row(HALF),
                  _mod_spec(r, tiles_per_batch, 2), _mod_spec(r, tiles_per_batch, 3),
                  _mod_spec(r, tiles_per_batch, 4), _mod_spec(r, tiles_per_batch, 5),
                  _const_spec((D_MODEL, D_MODEL)), _const_spec((2, D_MODEL)), _const_spec((2, D_MODEL)),
                  _const_spec((D_MODEL, 2 * D_FF)), _const_spec((D_FF, D_MODEL))],
        out_specs=row(D_MODEL),
        compiler_params=_params("parallel"),
        name="outproj_ffn",
    )(x, ya, yb, mod, mod, mod, mod, w_out, ln_g, ln_b, w_in, w_f)


def _level_tables():
    c = CHUNK
    t = np.arange(c)[:, None]
    u = np.arange(c)[None, :]
    rows, pmask, rmask = [], [], []
    for lvl in range(N_LEVELS):
        m = 1 << lvl
        anchor = (t // (2 * m)) * (2 * m) + m - 1
        right = (t % (2 * m)) >= m
        mat = np.where(right & (u > anchor) & (u <= t), 1.0, 0.0) - np.where(~right & (u > t) & (u <= anchor), 1.0, 0.0)
        rows.append(mat)
        s = u
        pm = right & ((s % (2 * m)) < m) & ((t // (2 * m)) == (s // (2 * m)))
        pmask.append(pm.astype(np.float32))
        rmask.append(np.broadcast_to(right.astype(np.float32), (c, LANES)))
    rows.append((u <= t).astype(np.float64))
    rows.append((u > t).astype(np.float64))
    mall = np.concatenate(rows, axis=0)
    mall3 = np.concatenate([mall, mall, mall], axis=1)
    return (jnp.asarray(mall3, dtype=bf16), jnp.asarray(np.stack(pmask), dtype=f32),
            jnp.asarray(np.stack(rmask), dtype=f32))


def _split3(g):
    g1 = g.astype(bf16)
    r1 = g - g1.astype(f32)
    g2 = r1.astype(bf16)
    r2 = r1 - g2.astype(f32)
    return jnp.concatenate([g1, g2, r2.astype(bf16)], axis=0)


def _hgrn_lower_bound(lg_ref, layer):
    rows = [lg_ref[i:i + 1, :] for i in range(DEPTH)]
    mx = functools.reduce(jnp.maximum, rows)
    ex = [jnp.exp(r - mx) for r in rows]
    den = functools.reduce(lambda a, b: a + b, ex)
    sm = [e / den for e in ex]
    cs = sm[0]
    for i in range(1, layer + 1):
        cs = cs + sm[i]
    return cs - sm[0]


def _gla_prologue(kind, refs, rows):
    if kind == "gla":
        qk_ref, v_ref, gate_ref, lr_ref, wlr_ref, blr_ref = refs
        q = qk_ref[rows, 0:GLA_KW] * (GLA_DK ** -0.5)
        k = qk_ref[rows, GLA_KW:2 * GLA_KW]
        pre = _dot(lr_ref[rows, :].astype(bf16), wlr_ref[...]) + blr_ref[...]
        g = _log_sigmoid(pre) * (1.0 / GLA_TAU)
        return q, k, g
    q_ref, f_ref, lb = refs
    forget = lb + (1.0 - lb) * _sigmoid(f_ref[rows, :])
    return _silu(q_ref[rows, :]), 1.0 - forget, jnp.log(forget)


def _gla_chunk_kernel(kind, dk, n_heads, *refs):
    if kind == "gla":
        (qk_ref, v_ref, gate_ref, lr_ref, wlr_ref, blr_ref, ng_ref, mall_ref, pm_ref, rm_ref,
         y_ref, s_ref, st_ref) = refs
        pro = (qk_ref, v_ref, gate_ref, lr_ref, wlr_ref, blr_ref)
        lb = None
    else:
        (q_ref, f_ref, v_ref, gate_ref, lg_ref, ng_ref, mall_ref, pm_ref, rm_ref,
         y_ref, s_ref, st_ref) = refs
        lb = _hgrn_lower_bound(lg_ref, 1)
        pro = (q_ref, f_ref, lb)
    t_idx = pl.program_id(1)
    n_tiles = pl.num_programs(1)
    lt = y_ref.shape[0]
    hps = LANES // dk
    n_slabs = n_heads // hps
    lane = lax.broadcasted_iota(jnp.int32, (1, LANES), 1)

    @pl.when(t_idx == 0)
    def _():
        st_ref[...] = jnp.zeros(st_ref.shape, f32)

    def chunk(c, carry):
        rows = pl.ds(pl.multiple_of(c * CHUNK, CHUNK), CHUNK)
        q_all, k_all, g_all = _gla_prologue(kind, pro, rows)
        for sl in range(n_slabs):
            cs = slice(sl * LANES, (sl + 1) * LANES)
            q_s, k_s, g_s = q_all[:, cs], k_all[:, cs], g_all[:, cs]
            d_all = _dot(mall_ref[...], _split3(g_s))
            bcum = d_all[N_LEVELS * CHUNK:(N_LEVELS + 1) * CHUNK]
            brev = d_all[(N_LEVELS + 1) * CHUNK:(N_LEVELS + 2) * CHUNK]
            ys = []
            for lvl in range(N_LEVELS):
                d = d_all[lvl * CHUNK:(lvl + 1) * CHUNK]
                x = jnp.exp(jnp.minimum(d, -d))
                ys.append((x * jnp.where(rm_ref[lvl] > 0.5, q_s, k_s)).astype(bf16))
            qb = q_s * jnp.exp(bcum)
            kst = k_s * jnp.exp(brev)
            dec = jnp.exp(bcum[CHUNK - 1:CHUNK, :])
            qk_prod = q_s * k_s
            for hh in range(hps):
                h = sl * hps + hh
                if hps == 1:
                    msk = None
                    mul = lambda a: a
                else:
                    msk = ((lane >= hh * dk) & (lane < (hh + 1) * dk))
                    mul = lambda a, msk=msk: jnp.where(msk, a, jnp.zeros_like(a))
                p = jnp.zeros((CHUNK, CHUNK), f32)
                for lvl in range(N_LEVELS):
                    p = p + pm_ref[lvl] * _dot_nt(mul(ys[lvl]), ys[lvl])
                rd = jnp.sum(mul(qk_prod), axis=-1, keepdims=True)
                v_h = v_ref[rows, h * LANES:(h + 1) * LANES]
                vb = v_h.astype(bf16)
                st = st_ref[h]
                o = (_dot(p.astype(bf16), vb) + rd * v_h
                     + _dot_nt(mul(qb).astype(bf16), st.astype(bf16)))
                st_ref[h] = st * dec + _dot_tn(vb, mul(kst).astype(bf16))
                o = o * lax.rsqrt(jnp.mean(o * o, axis=-1, keepdims=True) + RMS_EPS) * ng_ref[...]
                y_ref[rows, h * LANES:(h + 1) * LANES] = o * _silu(gate_ref[rows, h * LANES:(h + 1) * LANES])
        return carry

    lax.fori_loop(0, lt // CHUNK, chunk, 0)

    @pl.when(t_idx == n_tiles - 1)
    def _():
        for h in range(n_heads):
            off = (h % hps) * dk
            s_ref[h] = st_ref[h].T[off:off + dk, :]


def _gla_prompt(kind, z, col0, extra, norm_g, batch, seq, lt):
    dk = GLA_DK if kind == "gla" else HG_D
    n_heads = GLA_H if kind == "gla" else HG_H
    tiles = seq // lt
    mall, pmask, rmask = _level_tables()
    blk = lambda width, cb: pl.BlockSpec((lt, width), lambda b, t: (b * tiles + t, cb))
    if kind == "gla":
        wlr, blr = extra
        ins = [z, z, z, z, wlr, blr]
        specs = [blk(2 * GLA_KW, 0), blk(GLA_VW, 1), blk(GLA_VW, 2), blk(LANES, col0),
                 _const_spec(wlr.shape), _const_spec(blr.shape)]
    else:
        (lg,) = extra
        ins = [z, z, z, z, lg]
        specs = [blk(HG_W, 1), blk(HG_W, 2), blk(HG_W, 3), blk(HG_W, 4), _const_spec(lg.shape)]
    ins += [norm_g, mall, pmask, rmask]
    specs += [_const_spec(norm_g.shape), _const_spec(mall.shape), _const_spec(pmask.shape), _const_spec(rmask.shape)]
    return pl.pallas_call(
        functools.partial(_gla_chunk_kernel, kind, dk, n_heads),
        out_shape=(jax.ShapeDtypeStruct((batch * seq, HALF), f32),
                   jax.ShapeDtypeStruct((batch, n_heads, dk, LANES), f32)),
        grid=(batch, tiles),
        in_specs=specs,
        out_specs=(pl.BlockSpec((lt, HALF), lambda b, t: (b * tiles + t, 0)),
                   pl.BlockSpec((None, n_heads, dk, LANES), lambda b, t: (b, 0, 0, 0))),
        scratch_shapes=[pltpu.VMEM((n_heads, LANES, LANES), f32)],
        compiler_params=_params("parallel", "arbitrary"),
        name=kind + "_prompt",
    )(*ins)


def _lru_gates(xc, wr_ref, wi_ref, br_ref, bi_ref, lam_ref):
    xb = xc.astype(bf16)
    r = _sigmoid(_dot(xb, wr_ref[...]) + br_ref[...])
    i = _sigmoid(_dot(xb, wi_ref[...]) + bi_ref[...])
    log_a = (-RG_C) * r * _softplus(-lam_ref[...])
    return jnp.exp(log_a), jnp.sqrt(-_expm1(2.0 * log_a)) * (i * xc)


def _lru_prompt_kernel(x_ref, gr_ref, cw_ref, cb_ref, wr_ref, wi_ref, br_ref, bi_ref, lam_ref,
                       y_ref, conv_ref, h_ref, xbuf, a_scr, h_scr, car):
    t_idx = pl.program_id(1)
    lt = x_ref.shape[0]

    @pl.when(t_idx == 0)
    def _():
        xbuf[0:SUBLANES, :] = jnp.zeros((SUBLANES, LRU_W), f32)
        car[...] = jnp.zeros(car.shape, f32)

    x = x_ref[...]
    xbuf[SUBLANES:SUBLANES + lt, :] = x
    xc = cb_ref[...] + cw_ref[CONV_W - 1:CONV_W, :] * x
    for j in range(CONV_W - 1):
        off = SUBLANES - (CONV_W - 1) + j
        xc = xc + cw_ref[j:j + 1, :] * xbuf[off:off + lt, :]
    xbuf[0:SUBLANES, :] = xbuf[lt:lt + SUBLANES, :]

    a, hh = _lru_gates(xc, wr_ref, wi_ref, br_ref, bi_ref, lam_ref)
    rig = lax.broadcasted_iota(jnp.int32, (lt, LRU_W), 0) % SUBLANES
    for d in (1, 2, 4):
        m = rig >= d
        hh = jnp.where(m, hh + a * pltpu.roll(hh, d, 0), hh)
        a = jnp.where(m, a * pltpu.roll(a, d, 0), a)
    a_scr[...] = a
    h_scr[...] = hh

    def group(g, c):
        rows = pl.ds(pl.multiple_of(g * SUBLANES, SUBLANES), SUBLANES)
        hg = h_scr[rows, :] + a_scr[rows, :] * c
        h_scr[rows, :] = hg
        return jnp.broadcast_to(hg[SUBLANES - 1:SUBLANES, :], (SUBLANES, LRU_W))

    c = lax.fori_loop(0, lt // SUBLANES, group, car[...])
    car[...] = c
    y_ref[...] = h_scr[...] * _gelu(gr_ref[...])
    conv_ref[...] = x[lt - (CONV_W - 1):lt, :]
    h_ref[...] = c[0:1, :]


def _lru_prompt(z, xr_cb, gr_cb, p, batch, seq, lt):
    tiles = seq // lt
    blk = lambda cb: pl.BlockSpec((lt, LRU_W), lambda b, t: (b * tiles + t, cb))
    consts = [p["conv_w"], p["conv_b"], p["w_r"], p["w_i"], p["b_r"], p["b_i"], p["lam"]]
    return pl.pallas_call(
        _lru_prompt_kernel,
        out_shape=(jax.ShapeDtypeStruct((batch * seq, LRU_W), f32),
                   jax.ShapeDtypeStruct((batch, CONV_W - 1, LRU_W), f32),
                   jax.ShapeDtypeStruct((batch, 1, LRU_W), f32)),
        grid=(batch, tiles),
        in_specs=[blk(xr_cb), blk(gr_cb)] + [_const_spec(c.shape) for c in consts],
        out_specs=(pl.BlockSpec((lt, LRU_W), lambda b, t: (b * tiles + t, 0)),
                   pl.BlockSpec((None, CONV_W - 1, LRU_W), lambda b, t: (b, 0, 0)),
                   pl.BlockSpec((None, 1, LRU_W), lambda b, t: (b, 0, 0))),
        scratch_shapes=[pltpu.VMEM((lt + SUBLANES, LRU_W), f32), pltpu.VMEM((lt, LRU_W), f32),
                        pltpu.VMEM((lt, LRU_W), f32), pltpu.VMEM((SUBLANES, LRU_W), f32)],
        compiler_params=_params("parallel", "arbitrary"),
        name="lru_prompt",
    )(z, z, *consts)


def _s5_param_kernel(lr_ref, li_ref, ldt_ref, btr_ref, bti_ref, pwr_ref, pwi_ref, bbr_ref, bbi_ref):
    lr_, li_ = lr_ref[...], li_ref[...]
    dt = jnp.exp(ldt_ref[...])
    mag = jnp.exp(lr_ * dt)
    ar, ai = mag * jnp.cos(li_ * dt), mag * jnp.sin(li_ * dt)
    den = lr_ * lr_ + li_ * li_
    nr, ni = ar - 1.0, ai
    fr = (nr * lr_ + ni * li_) / den
    fi = (ni * lr_ - nr * li_) / den
    bbr_ref[...] = fr[:, None, :] * btr_ref[...] - fi[:, None, :] * bti_ref[...]
    bbi_ref[...] = fr[:, None, :] * bti_ref[...] + fi[:, None, :] * btr_ref[...]
    pr, pi = ar, ai
    pwr_ref[0], pwi_ref[0] = pr, pi
    for n in range(1, SUBLANES):
        pr, pi = pr * ar - pi * ai, pr * ai + pi * ar
        pwr_ref[n], pwi_ref[n] = pr, pi


def _s5_params(lam_re, lam_im, log_dt, b_re, b_im, c_re, c_im):
    btr = jnp.swapaxes(b_re, 1, 2)
    bti = jnp.swapaxes(b_im, 1, 2)
    shp = jax.ShapeDtypeStruct
    pwr, pwi, bbr, bbi = pl.pallas_call(
        _s5_param_kernel,
        out_shape=(shp((SUBLANES, S5_G, S5_P), f32), shp((SUBLANES, S5_G, S5_P), f32),
                   shp((S5_G, S5_GH, S5_P), f32), shp((S5_G, S5_GH, S5_P), f32)),
        name="s5_params",
    )(lam_re, lam_im, log_dt.reshape(S5_G, 1), btr, bti)
    gpb = LANES // S5_GH
    nblk = S5_G // gpb
    eye = jnp.eye(gpb, dtype=f32)

    def bdiag_in(bb):
        x = bb.reshape(nblk, gpb, S5_GH, S5_P)
        return jnp.einsum('jghp,gk->jghkp', x, eye).reshape(nblk, gpb * S5_GH, gpb * S5_P)

    def bdiag_out(cc):
        x = cc.reshape(nblk, gpb, S5_GH, S5_P)
        return jnp.einsum('jghp,gk->jgpkh', x, eye).reshape(nblk, gpb * S5_P, gpb * S5_GH)

    b_bd = jnp.concatenate([bdiag_in(bbr), bdiag_in(bbi)], axis=2).astype(bf16)
    c_bd = jnp.concatenate([bdiag_out(c_re), bdiag_out(-c_im)], axis=1).astype(bf16)
    return pwr.reshape(SUBLANES, S5_N), pwi.reshape(SUBLANES, S5_N), b_bd, c_bd


S5_BLK = S5_N // (S5_G // (LANES // S5_GH))


def _s5_input(u, bbd_ref, j):
    bu = _dot(u[:, j * LANES:(j + 1) * LANES].astype(bf16), bbd_ref[j])
    return bu[:, 0:S5_BLK], bu[:, S5_BLK:2 * S5_BLK]


def _s5_output(u, hr, hi, cbd_ref, d_ref, wg_ref, bg_ref):
    ys = []
    for j in range(S5_N // S5_BLK):
        cs = slice(j * S5_BLK, (j + 1) * S5_BLK)
        hcat = jnp.concatenate([hr[:, cs], hi[:, cs]], axis=1).astype(bf16)
        ys.append(_dot(hcat, cbd_ref[j]))
    y = _gelu(jnp.concatenate(ys, axis=1) + d_ref[...] * u)
    return y * _sigmoid(_dot(y.astype(bf16), wg_ref[...]) + bg_ref[...])


def _s5_prompt_kernel(u_ref, pwr_ref, pwi_ref, bbd_ref, cbd_ref, d_ref, wg_ref, bg_ref,
                      y_ref, sr_ref, si_ref, hr_s, hi_s, car_r, car_i):
    t_idx = pl.program_id(1)
    lt = u_ref.shape[0]

    @pl.when(t_idx == 0)
    def _():
        car_r[...] = jnp.zeros(car_r.shape, f32)
        car_i[...] = jnp.zeros(car_i.shape, f32)

    u = u_ref[...]
    rig = lax.broadcasted_iota(jnp.int32, (lt, S5_BLK), 0) % SUBLANES
    for j in range(S5_N // S5_BLK):
        cs = slice(j * S5_BLK, (j + 1) * S5_BLK)
        hr, hi = _s5_input(u, bbd_ref, j)
        for d in (1, 2, 4):
            m = rig >= d
            pr, pi = pwr_ref[d - 1:d, cs], pwi_ref[d - 1:d, cs]
            sr, si = pltpu.roll(hr, d, 0), pltpu.roll(hi, d, 0)
            hr, hi = (hr + jnp.where(m, pr * sr - pi * si, 0.0),
                      hi + jnp.where(m, pr * si + pi * sr, 0.0))
        hr_s[:, cs] = hr
        hi_s[:, cs] = hi

    tr, ti = pwr_ref[...], pwi_ref[...]

    def group(g, c):
        cr, ci = c
        rows = pl.ds(pl.multiple_of(g * SUBLANES, SUBLANES), SUBLANES)
        gr_ = hr_s[rows, :] + tr * cr - ti * ci
        gi_ = hi_s[rows, :] + tr * ci + ti * cr
        hr_s[rows, :] = gr_
        hi_s[rows, :] = gi_
        return (jnp.broadcast_to(gr_[SUBLANES - 1:SUBLANES, :], (SUBLANES, S5_N)),
                jnp.broadcast_to(gi_[SUBLANES - 1:SUBLANES, :], (SUBLANES, S5_N)))

    cr, ci = lax.fori_loop(0, lt // SUBLANES, group, (car_r[...], car_i[...]))
    car_r[...] = cr
    car_i[...] = ci
    y_ref[...] = _s5_output(u, hr_s[...], hi_s[...], cbd_ref, d_ref, wg_ref, bg_ref)
    sr_ref[...] = cr[0:1, :]
    si_ref[...] = ci[0:1, :]


def _s5_prompt(z, u_cb, tabs, p, batch, seq, lt):
    tiles = seq // lt
    pwr, pwi, b_bd, c_bd = tabs
    consts = [pwr, pwi, b_bd, c_bd, p["d"], p["w_glu"], p["b_glu"]]
    return pl.pallas_call(
        _s5_prompt_kernel,
        out_shape=(jax.ShapeDtypeStruct((batch * seq, S5_W), f32),
                   jax.ShapeDtypeStruct((batch, 1, S5_N), f32),
                   jax.ShapeDtypeStruct((batch, 1, S5_N), f32)),
        grid=(batch, tiles),
        in_specs=[pl.BlockSpec((lt, S5_W), lambda b, t: (b * tiles + t, u_cb))]
        + [_const_spec(c.shape) for c in consts],
        out_specs=(pl.BlockSpec((lt, S5_W), lambda b, t: (b * tiles + t, 0)),
                   pl.BlockSpec((None, 1, S5_N), lambda b, t: (b, 0, 0)),
                   pl.BlockSpec((None, 1, S5_N), lambda b, t: (b, 0, 0))),
        scratch_shapes=[pltpu.VMEM((lt, S5_N), f32), pltpu.VMEM((lt, S5_N), f32),
                        pltpu.VMEM((SUBLANES, S5_N), f32), pltpu.VMEM((SUBLANES, S5_N), f32)],
        compiler_params=_params("parallel", "arbitrary"),
        name="s5_prompt",
    )(z, *consts)


def _gla_step_kernel(kind, dk, *refs):
    if kind == "gla":
        q_ref, k_ref, v_ref, gate_ref, lr_ref, wlr_ref, blr_ref, ng_ref, s_ref, y_ref, so_ref, o_scr = refs
        sl = pl.program_id(0)
        q = q_ref[...] * (GLA_DK ** -0.5)
        k = k_ref[...]
        pre_all = _dot(lr_ref[...].astype(bf16), wlr_ref[...]) + blr_ref[...]
        pre = jnp.where(sl == 0, pre_all[:, 0:LANES], pre_all[:, LANES:2 * LANES])
        a = jnp.exp(_log_sigmoid(pre) * (1.0 / GLA_TAU))
    else:
        q_ref, f_ref, v_ref, gate_ref, lg_ref, ng_ref, s_ref, y_ref, so_ref, o_scr = refs
        sl = pl.program_id(0)
        lb_all = _hgrn_lower_bound(lg_ref, 1)
        lb = lb_all[:, 0:LANES]
        for j in range(1, HG_H):
            lb = jnp.where(sl == j, lb_all[:, j * LANES:(j + 1) * LANES], lb)
        a = lb + (1.0 - lb) * _sigmoid(f_ref[...])
        q = _silu(q_ref[...])
        k = 1.0 - a
    hps = LANES // dk
    nb = q.shape[0]
    qat, kt, at = (q * a).T, k.T, a.T
    lane = lax.broadcasted_iota(jnp.int32, (1, LANES), 1)
    v = v_ref[...]
    for hh in range(hps):
        r0 = hh * dk
        msk = (lane >= r0) & (lane < r0 + dk)
        qk = jnp.sum(jnp.where(msk, q * k, 0.0), axis=-1, keepdims=True)
        for b in range(nb):
            s = s_ref[b, hh]
            col = lambda x: jnp.broadcast_to(x[r0:r0 + dk, b:b + 1], (dk, LANES))
            vr = v[b:b + 1, hh * LANES:(hh + 1) * LANES]
            so_ref[b, hh] = col(at) * s + col(kt) * vr
            oi = jnp.sum(col(qat) * s, axis=0, keepdims=True)
            o_scr[b:b + 1, hh * LANES:(hh + 1) * LANES] = jnp.broadcast_to(qk[b:b + 1, :], (1, LANES)) * vr + oi
    for hh in range(hps):
        cs = slice(hh * LANES, (hh + 1) * LANES)
        o = o_scr[:, cs]
        o = o * lax.rsqrt(jnp.mean(o * o, axis=-1, keepdims=True) + RMS_EPS) * ng_ref[...]
        y_ref[:, cs] = o * _silu(gate_ref[:, cs])


def _gla_step(kind, z, extra, norm_g, state):
    nb = z.shape[0]
    dk = GLA_DK if kind == "gla" else HG_D
    n_heads = GLA_H if kind == "gla" else HG_H
    hps = LANES // dk
    n_slabs = n_heads // hps
    vw = hps * LANES
    zb = lambda width, cb0: pl.BlockSpec((nb, width), lambda s: (0, cb0 + s))
    if kind == "gla":
        wlr, blr = extra
        ins = [z, z, z, z, z, wlr, blr]
        specs = [zb(LANES, 0), zb(LANES, GLA_KW // LANES), zb(vw, GLA_VW // vw * 1), zb(vw, GLA_VW // vw * 2),
                 pl.BlockSpec((nb, LANES), lambda s: (0, (EVEN_N - LANES) // LANES)),
                 _const_spec(wlr.shape), _const_spec(blr.shape)]
    else:
        (lg,) = extra
        ins = [z, z, z, z, lg]
        specs = [zb(LANES, HG_W // LANES * 1), zb(LANES, HG_W // LANES * 2), zb(vw, HG_W // vw * 3),
                 zb(vw, HG_W // vw * 4), _const_spec(lg.shape)]
    ins += [norm_g, state]
    specs += [_const_spec(norm_g.shape), pl.BlockSpec((nb, hps, dk, LANES), lambda s: (0, s, 0, 0))]
    return pl.pallas_call(
        functools.partial(_gla_step_kernel, kind, dk),
        out_shape=(jax.ShapeDtypeStruct((nb, HALF), f32), jax.ShapeDtypeStruct(state.shape, f32)),
        grid=(n_slabs,),
        in_specs=specs,
        out_specs=(pl.BlockSpec((nb, vw), lambda s: (0, s)),
                   pl.BlockSpec((nb, hps, dk, LANES), lambda s: (0, s, 0, 0))),
        scratch_shapes=[pltpu.VMEM((nb, vw), f32)],
        compiler_params=_params("parallel"),
        name=kind + "_step",
    )(*ins)


def _lru_step_kernel(x_ref, gr_ref, c0_ref, c1_ref, c2_ref, h0_ref, cw_ref, cb_ref, wr_ref, wi_ref,
                     br_ref, bi_ref, lam_ref, y_ref, h_ref):
    xc = (cb_ref[...] + cw_ref[0:1, :] * c0_ref[...] + cw_ref[1:2, :] * c1_ref[...]
          + cw_ref[2:3, :] * c2_ref[...] + cw_ref[3:4, :] * x_ref[...])
    a, u = _lru_gates(xc, wr_ref, wi_ref, br_ref, bi_ref, lam_ref)
    h = a * h0_ref[...] + u
    h_ref[...] = h
    y_ref[...] = h * _gelu(gr_ref[...])


def _lru_step(z, xr_cb, gr_cb, conv_rows, h0, p):
    nb = z.shape[0]
    full = lambda: pl.BlockSpec((nb, LRU_W), lambda i: (0, 0))
    consts = [p["conv_w"], p["conv_b"], p["w_r"], p["w_i"], p["b_r"], p["b_i"], p["lam"]]
    return pl.pallas_call(
        _lru_step_kernel,
        out_shape=(jax.ShapeDtypeStruct((nb, LRU_W), f32), jax.ShapeDtypeStruct((nb, LRU_W), f32)),
        grid=(1,),
        in_specs=[pl.BlockSpec((nb, LRU_W), lambda i: (0, xr_cb)), pl.BlockSpec((nb, LRU_W), lambda i: (0, gr_cb)),
                  full(), full(), full(), full()] + [_const_spec(c.shape) for c in consts],
        out_specs=(full(), full()),
        compiler_params=_params("arbitrary"),
        name="lru_step",
    )(z, z, *conv_rows, h0, *consts)


def _s5_step_kernel(u_ref, s0r_ref, s0i_ref, pwr_ref, pwi_ref, bbd_ref, cbd_ref, d_ref, wg_ref, bg_ref,
                    y_ref, sr_ref, si_ref):
    u = u_ref[...]
    for j in range(S5_N // S5_BLK):
        cs = slice(j * S5_BLK, (j + 1) * S5_BLK)
        bur, bui = _s5_input(u, bbd_ref, j)
        ar, ai = pwr_ref[0:1, cs], pwi_ref[0:1, cs]
        s0r, s0i = s0r_ref[:, cs], s0i_ref[:, cs]
        sr_ref[:, cs] = bur + ar * s0r - ai * s0i
        si_ref[:, cs] = bui + ar * s0i + ai * s0r
    y_ref[...] = _s5_output(u, sr_ref[...], si_ref[...], cbd_ref, d_ref, wg_ref, bg_ref)


def _s5_step(z, u_cb, s0r, s0i, tabs, p):
    nb = z.shape[0]
    pwr, pwi, b_bd, c_bd = tabs
    consts = [pwr, pwi, b_bd, c_bd, p["d"], p["w_glu"], p["b_glu"]]
    st = lambda: pl.BlockSpec((nb, S5_N), lambda i: (0, 0))
    return pl.pallas_call(
        _s5_step_kernel,
        out_shape=(jax.ShapeDtypeStruct((nb, S5_W), f32), jax.ShapeDtypeStruct((nb, S5_N), f32),
                   jax.ShapeDtypeStruct((nb, S5_N), f32)),
        grid=(1,),
        in_specs=[pl.BlockSpec((nb, S5_W), lambda i: (0, u_cb)), st(), st()] + [_const_spec(c.shape) for c in consts],
        out_specs=(pl.BlockSpec((nb, S5_W), lambda i: (0, 0)), st(), st()),
        compiler_params=_params("arbitrary"),
        name="s5_step",
    )(z, s0r, s0i, *consts)


def _block_diag(w):
    h, n, _ = w.shape
    return jnp.einsum('hij,hk->hikj', w, jnp.eye(h, dtype=w.dtype)).reshape(h * n, h * n)


def kernel(x_prompt, x_sample, c_prompt, c_sample, state_gla, state_rglru_conv, state_rglru_h, state_s5_re, state_s5_im, state_hgrn, ev_w_in, ev_gla_w_lr, ev_gla_b_lr, ev_gla_norm_g, ev_conv_w, ev_conv_b, ev_lru_w_r, ev_lru_b_r, ev_lru_w_i, ev_lru_b_i, ev_lru_lam, ev_w_out, od_w_in, od_s5_lam_re, od_s5_lam_im, od_s5_log_dt, od_s5_b_re, od_s5_b_im, od_s5_c_re, od_s5_c_im, od_s5_d, od_s5_w_glu, od_s5_b_glu, hg_lb_logits, od_hg_norm_g, od_w_out, w_ada, b_ada, ln_g, ln_b, ffn_w_in, ffn_w_out):
    bp, seq = x_prompt.shape[0], x_prompt.shape[1]
    bs = x_sample.shape[0]
    assert x_sample.shape[1] == 1 and seq % CHUNK == 0
    tp = bp * seq
    tm = min(512, seq)
    lt = min(512, seq)
    lt_s5 = min(256, seq)
    row2 = lambda a: a.reshape(1, -1)

    mod = _modulation(jnp.concatenate([c_prompt, c_sample], axis=0).astype(f32), w_ada, b_ada)
    mod_p = mod[:, :bp].reshape(DEPTH, bp, 1, 6 * D_MODEL)
    mod_s = mod[:, bp:].reshape(DEPTH, 1, bs, 6 * D_MODEL)

    xp = x_prompt.astype(f32).reshape(tp, D_MODEL)
    xs = x_sample.astype(f32).reshape(bs, D_MODEL)
    outs_p = {k: [] for k in ("gla", "conv", "lru", "re", "im", "hg")}
    outs_s = {k: [] for k in ("gla", "conv", "lru", "re", "im", "hg")}

    for l in range(DEPTH):
        w_f_in = ffn_w_in[l].astype(bf16)
        w_f_out = ffn_w_out[l].astype(bf16)
        if l % 2 == 0:
            e = l // 2
            w = ev_w_in[e]
            lr0 = 2 * GLA_KW + 2 * GLA_VW
            w_in = jnp.concatenate(
                [w[:, :lr0], w[:, lr0 + GLA_LR:], w[:, lr0:lr0 + GLA_LR],
                 jnp.zeros((D_MODEL, EVEN_N - w.shape[1]), w.dtype)], axis=1).astype(bf16)
            w_out = ev_w_out[e].astype(bf16)
            wlr = jnp.concatenate([ev_gla_w_lr[e], jnp.zeros((LANES - GLA_LR, GLA_KW), f32)], axis=0).astype(bf16)
            gla_extra = (wlr, row2(ev_gla_b_lr[e]))
            ng = row2(ev_gla_norm_g[e])
            lru_p = dict(conv_w=ev_conv_w[e], conv_b=row2(ev_conv_b[e]),
                         w_r=_block_diag(ev_lru_w_r[e]).astype(bf16), w_i=_block_diag(ev_lru_w_i[e]).astype(bf16),
                         b_r=row2(ev_lru_b_r[e]), b_i=row2(ev_lru_b_i[e]), lam=row2(ev_lru_lam[e]))
            xr_cb, gr_cb = lr0 // LRU_W, lr0 // LRU_W + 1
            z = _inproj(xp, mod_p[l], w_in, tm, seq // tm)
            ya, s_gla = _gla_prompt("gla", z, (EVEN_N - LANES) // LANES, gla_extra, ng, bp, seq, lt)
            yb, s_conv, s_h = _lru_prompt(z, xr_cb, gr_cb, lru_p, bp, seq, lt)
            xp = _outffn(xp, ya, yb, mod_p[l], w_out, ln_g[l], ln_b[l], w_f_in, w_f_out, tm, seq // tm)
            outs_p["gla"].append(s_gla)
            outs_p["conv"].append(s_conv)
            outs_p["lru"].append(s_h.reshape(bp, LRU_W))
            z = _inproj(xs, mod_s[l], w_in, bs, 1)
            ya, s_gla = _gla_step("gla", z, gla_extra, ng, state_gla[e])
            cs = state_rglru_conv[e].astype(f32)
            yb, s_h = _lru_step(z, xr_cb, gr_cb, [cs[:, j] for j in range(CONV_W - 1)],
                                state_rglru_h[e].astype(f32), lru_p)
            xs = _outffn(xs, ya, yb, mod_s[l], w_out, ln_g[l], ln_b[l], w_f_in, w_f_out, bs, 1)
            outs_s["gla"].append(s_gla)
            outs_s["conv"].append(jnp.stack([cs[:, 1], cs[:, 2], z[:, xr_cb * LRU_W:(xr_cb + 1) * LRU_W]], axis=1))
            outs_s["lru"].append(s_h)
        else:
            o = l // 2
            w_in = od_w_in[o].astype(bf16)
            w_out = od_w_out[o].astype(bf16)
            tabs = _s5_params(od_s5_lam_re[o], od_s5_lam_im[o], od_s5_log_dt[o], od_s5_b_re[o], od_s5_b_im[o],
                              od_s5_c_re[o], od_s5_c_im[o])
            s5_p = dict(d=row2(od_s5_d[o]), w_glu=od_s5_w_glu[o].astype(bf16), b_glu=row2(od_s5_b_glu[o]))
            hg_extra = (hg_lb_logits.astype(f32),)
            ng = row2(od_hg_norm_g[o])
            z = _inproj(xp, mod_p[l], w_in, tm, seq // tm)
            ya, s_re, s_im = _s5_prompt(z, 0, tabs, s5_p, bp, seq, lt_s5)
            yb, s_hg = _gla_prompt("hgrn", z, 0, hg_extra, ng, bp, seq, lt)
            xp = _outffn(xp, ya, yb, mod_p[l], w_out, ln_g[l], ln_b[l], w_f_in, w_f_out, tm, seq // tm)
            outs_p["re"].append(s_re.reshape(bp, S5_G, S5_P))
            outs_p["im"].append(s_im.reshape(bp, S5_G, S5_P))
            outs_p["hg"].append(s_hg)
            z = _inproj(xs, mod_s[l], w_in, bs, 1)
            ya, s_re, s_im = _s5_step(z, 0, state_s5_re[o].astype(f32).reshape(bs, S5_N),
                                      state_s5_im[o].astype(f32).reshape(bs, S5_N), tabs, s5_p)
            yb, s_hg = _gla_step("hgrn", z, hg_extra, ng, state_hgrn[o])
            xs = _outffn(xs, ya, yb, mod_s[l], w_out, ln_g[l], ln_b[l], w_f_in, w_f_out, bs, 1)
            outs_s["re"].append(s_re.reshape(bs, S5_G, S5_P))
            outs_s["im"].append(s_im.reshape(bs, S5_G, S5_P))
            outs_s["hg"].append(s_hg)

    st = lambda d, k: jnp.stack(d[k])
    return (xp.reshape(bp, seq, D_MODEL).astype(x_prompt.dtype), xs.reshape(bs, 1, D_MODEL).astype(x_sample.dtype),
            st(outs_p, "gla"), st(outs_s, "gla"), st(outs_p, "conv"), st(outs_s, "conv"),
            st(outs_p, "lru"), st(outs_s, "lru"), st(outs_p, "re"), st(outs_s, "re"),
            st(outs_p, "im"), st(outs_s, "im"), st(outs_p, "hg"), st(outs_s, "hg"))
```

```python
import functools
import math

import numpy as np
import jax
import jax.numpy as jnp
from jax import lax
from jax.experimental import pallas as pl
from jax.experimental.pallas import tpu as pltpu

f32 = jnp.float32
bf16 = jnp.bfloat16

D_MODEL = 1024
DEPTH = 2
HALF = D_MODEL // 2
GLA_H = 4
GLA_DV = HALF // GLA_H
GLA_DK = GLA_DV // 2
GLA_KW = GLA_H * GLA_DK
GLA_VW = GLA_H * GLA_DV
GLA_LR = 16
GLA_TAU = 16.0
LRU_W = HALF
LRU_H = 8
LRU_BW = LRU_W // LRU_H
CONV_W = 4
RG_C = 8.0
S5_W = HALF
S5_GH = 16
S5_G = S5_W // S5_GH
S5_P = 64
S5_N = S5_G * S5_P
HG_H = 4
HG_D = HALF // HG_H
HG_W = HG_H * HG_D
D_FF = ((8 * D_MODEL // 3 + 255) // 256) * 256
ALPHA = (2.0 * DEPTH) ** 0.25
LN_EPS = 1e-5
RMS_EPS = 1e-6

LANES = 128
SUBLANES = 8
VMEM_LIMIT = 56 * 1024 * 1024

CHUNK = 64
N_LEVELS = 6
FF_CHUNK = 256
EVEN_N = 2688
ODD_N = 2560


def _sigmoid(x):
    return jax.nn.sigmoid(x)


def _silu(x):
    return x * jax.nn.sigmoid(x)


def _gelu(x):
    c = math.sqrt(2.0 / math.pi)
    return x * (0.5 * (1.0 + jnp.tanh(c * (x + 0.044715 * (x * x * x)))))


def _softplus(x):
    return jnp.maximum(x, 0.0) + jnp.log1p(jnp.exp(-jnp.abs(x)))


def _log_sigmoid(x):
    return -_softplus(-x)


def _expm1(x):
    u = jnp.exp(x)
    um1 = u - 1.0
    lg = jnp.log(u)
    safe = jnp.where(lg == 0.0, 1.0, lg)
    return jnp.where(um1 == 0.0, x, jnp.where(um1 == -1.0, -1.0, um1 * x / safe))


def _layer_norm(x, g, b):
    mu = jnp.mean(x, axis=-1, keepdims=True)
    xc = x - mu
    var = jnp.mean(xc * xc, axis=-1, keepdims=True)
    return xc * lax.rsqrt(var + LN_EPS) * g + b


def _dot(a, b):
    return jnp.dot(a, b, preferred_element_type=f32)


def _dot_nt(a, b):
    return lax.dot_general(a, b, (((1,), (1,)), ((), ())), preferred_element_type=f32)


def _dot_tn(a, b):
    return lax.dot_general(a, b, (((0,), (0,)), ((), ())), preferred_element_type=f32)


def _const_spec(shape):
    nd = len(shape)
    return pl.BlockSpec(shape, lambda *_: (0,) * nd, pipeline_mode=pl.Buffered(1))


def _params(*sem):
    return pltpu.CompilerParams(dimension_semantics=sem, vmem_limit_bytes=VMEM_LIMIT)


def _mod_kernel(c_ref, w_ref, b_ref, o_ref):
    cond = _silu(c_ref[...]).astype(bf16)
    o_ref[...] = _dot(cond, w_ref[...].astype(bf16)) + b_ref[...]


def _modulation(c_all, w_ada, b_ada):
    rows = c_all.shape[0]
    tn = 1536
    return pl.pallas_call(
        _mod_kernel,
        out_shape=jax.ShapeDtypeStruct((DEPTH, rows, 6 * D_MODEL), f32),
        grid=(DEPTH, 6 * D_MODEL // tn),
        in_specs=[pl.BlockSpec((rows, D_MODEL), lambda l, j: (0, 0)),
                  pl.BlockSpec((None, D_MODEL, tn), lambda l, j: (l, 0, j)),
                  pl.BlockSpec((None, 1, tn), lambda l, j: (l, 0, j))],
        out_specs=pl.BlockSpec((None, rows, tn), lambda l, j: (l, 0, j)),
        compiler_params=_params("parallel", "parallel"),
        name="adaln_mod",
    )(c_all, w_ada, b_ada.reshape(DEPTH, 1, 6 * D_MODEL))


def _mod_spec(rows_per_block, tiles_per_batch, j):
    return pl.BlockSpec((None, rows_per_block, D_MODEL), lambda i: (i // tiles_per_batch, 0, j))


def _inproj_kernel(x_ref, sc_ref, sh_ref, w_ref, z_ref):
    h = x_ref[...] * (1.0 + sc_ref[...]) + sh_ref[...]
    z_ref[...] = _dot(h.astype(bf16), w_ref[...])


def _inproj(x, mod, w, tm, tiles_per_batch):
    t, n = x.shape[0], w.shape[1]
    r = mod.shape[1]
    return pl.pallas_call(
        _inproj_kernel,
        out_shape=jax.ShapeDtypeStruct((t, n), f32),
        grid=(t // tm,),
        in_specs=[pl.BlockSpec((tm, D_MODEL), lambda i: (i, 0)),
                  _mod_spec(r, tiles_per_batch, 1), _mod_spec(r, tiles_per_batch, 0),
                  _const_spec((D_MODEL, n))],
        out_specs=pl.BlockSpec((tm, n), lambda i: (i, 0)),
        compiler_params=_params("parallel"),
        name="in_proj",
    )(x, mod, mod, w)


def _outffn_kernel(x_ref, ya_ref, yb_ref, gtm_ref, shf_ref, scf_ref, gtf_ref,
                   wo_ref, lng_ref, lnb_ref, wi_ref, wf_ref, o_ref):
    mix = (_dot(ya_ref[...].astype(bf16), wo_ref[0:HALF, :])
           + _dot(yb_ref[...].astype(bf16), wo_ref[HALF:D_MODEL, :]))
    x1 = _layer_norm(ALPHA * x_ref[...] + (1.0 + gtm_ref[...]) * mix, lng_ref[0:1, :], lnb_ref[0:1, :])
    h = (x1 * (1.0 + scf_ref[...]) + shf_ref[...]).astype(bf16)
    acc = jnp.zeros(x1.shape, f32)
    act = None
    n_ff = D_FF // FF_CHUNK
    for c in range(n_ff + 1):
        lo = c * FF_CHUNK
        if c < n_ff:
            gate = _dot(h, wi_ref[:, lo:lo + FF_CHUNK])
            up = _dot(h, wi_ref[:, D_FF + lo:D_FF + lo + FF_CHUNK])
        if act is not None:
            acc = acc + _dot(act, wf_ref[lo - FF_CHUNK:lo, :])
        if c < n_ff:
            act = (_silu(gate) * up).astype(bf16)
    o_ref[...] = _layer_norm(ALPHA * x1 + (1.0 + gtf_ref[...]) * acc, lng_ref[1:2, :], lnb_ref[1:2, :])


def _outffn(x, ya, yb, mod, w_out, ln_g, ln_b, w_in, w_f, tm, tiles_per_batch):
    t = x.shape[0]
    r = mod.shape[1]
    row = lambda w: pl.BlockSpec((tm, w), lambda i: (i, 0))
    return pl.pallas_call(
        _outffn_kernel,
        out_shape=jax.ShapeDtypeStruct((t, D_MODEL), f32),
        grid=(t // tm,),
        in_specs=[row(D_MODEL), row(HALF), row(HALF),
                  _mod_spec(r, tiles_per_batch, 2), _mod_spec(r, tiles_per_batch, 3),
                  _mod_spec(r, tiles_per_batch, 4), _mod_spec(r, tiles_per_batch, 5),
                  _const_spec((D_MODEL, D_MODEL)), _const_spec((2, D_MODEL)), _const_spec((2, D_MODEL)),
                  _const_spec((D_MODEL, 2 * D_FF)), _const_spec((D_FF, D_MODEL))],
        out_specs=row(D_MODEL),
        compiler_params=_params("parallel"),
        name="outproj_ffn",
    )(x, ya, yb, mod, mod, mod, mod, w_out, ln_g, ln_b, w_in, w_f)


def _level_tables():
    c = CHUNK
    t = np.arange(c)[:, None]
    u = np.arange(c)[None, :]
    pmask, sgn = [], []
    for lvl in range(N_LEVELS):
        m = 1 << lvl
        right = (t % (2 * m)) >= m
        pm = right & ((u % (2 * m)) < m) & ((t // (2 * m)) == (u // (2 * m)))
        pmask.append(pm.astype(np.float32))
        sgn.append(np.broadcast_to(np.where(right, 1.0, -1.0).astype(np.float32), (c, LANES)))
    tri = (u <= t).astype(np.float32)
    tri3 = np.concatenate([tri, tri, tri], axis=1)
    return (jnp.asarray(tri3, dtype=bf16), jnp.asarray(np.stack(pmask), dtype=f32),
            jnp.asarray(np.stack(sgn), dtype=f32))


LOG2E = 1.0 / math.log(2.0)


def _anchor(b, lvl):
    m = 1 << lvl
    if lvl == 1:
        b3 = b.reshape(CHUNK // SUBLANES, SUBLANES, LANES)
        sub = lax.broadcasted_iota(jnp.int32, b3.shape, 1)
        a = jnp.where(sub < 4, jnp.broadcast_to(b3[:, 1:2, :], b3.shape), jnp.broadcast_to(b3[:, 5:6, :], b3.shape))
        return a.reshape(CHUNK, LANES)
    b3 = b.reshape(CHUNK // (2 * m), 2 * m, LANES)
    return jnp.broadcast_to(b3[:, m - 1:m, :], b3.shape).reshape(CHUNK, LANES)


def _split3(g):
    g1 = g.astype(bf16)
    r1 = g - g1.astype(f32)
    g2 = r1.astype(bf16)
    r2 = r1 - g2.astype(f32)
    return jnp.concatenate([g1, g2, r2.astype(bf16)], axis=0)


def _hgrn_lower_bound(lg_ref, layer):
    rows = [lg_ref[i:i + 1, :] for i in range(DEPTH)]
    mx = functools.reduce(jnp.maximum, rows)
    ex = [jnp.exp(r - mx) for r in rows]
    den = functools.reduce(lambda a, b: a + b, ex)
    sm = [e / den for e in ex]
    cs = sm[0]
    for i in range(1, layer + 1):
        cs = cs + sm[i]
    return cs - sm[0]


def _gla_prologue(kind, refs, rows):
    if kind == "gla":
        qk_ref, lr_ref, wlr_ref, blr_ref = refs
        q = qk_ref[rows, 0:GLA_KW] * (GLA_DK ** -0.5)
        k = qk_ref[rows, GLA_KW:2 * GLA_KW]
        pre = _dot(lr_ref[rows, :].astype(bf16), wlr_ref[...]) + blr_ref[...]
        g = _log_sigmoid(pre) * (LOG2E / GLA_TAU)
        return q, k, g
    q_ref, f_ref, lb = refs
    forget = lb + (1.0 - lb) * _sigmoid(f_ref[rows, :])
    return _silu(q_ref[rows, :]), 1.0 - forget, jnp.log2(forget)


def _gla_chunk_kernel(kind, dk, n_heads, layer, *refs):
    if kind == "gla":
        (qk_ref, v_ref, gate_ref, lr_ref, wlr_ref, blr_ref, ng_ref, tri_ref, pm_ref, sg_ref,
         y_ref, s_ref, st_ref) = refs
        pro = (qk_ref, lr_ref, wlr_ref, blr_ref)
    else:
        (q_ref, f_ref, v_ref, gate_ref, lg_ref, ng_ref, tri_ref, pm_ref, sg_ref,
         y_ref, s_ref, st_ref) = refs
        pro = (q_ref, f_ref, _hgrn_lower_bound(lg_ref, layer))
    t_idx = pl.program_id(1)
    n_tiles = pl.num_programs(1)
    lt = y_ref.shape[0]
    hps = LANES // dk
    n_slabs = n_heads // hps
    lane = lax.broadcasted_iota(jnp.int32, (1, LANES), 1)
    eye = (lax.broadcasted_iota(jnp.int32, (CHUNK, CHUNK), 0)
           == lax.broadcasted_iota(jnp.int32, (CHUNK, CHUNK), 1))

    @pl.when(t_idx == 0)
    def _():
        st_ref[...] = jnp.zeros(st_ref.shape, f32)

    def head_mask(hh):
        if hps == 1:
            return lambda a: a
        msk = (lane >= hh * dk) & (lane < (hh + 1) * dk)
        return lambda a: jnp.where(msk, a, jnp.zeros_like(a))

    def rows_of(c):
        return pl.ds(pl.multiple_of(c * CHUNK, CHUNK), CHUNK)

    def cumsums(c):
        q_all, k_all, g_all = _gla_prologue(kind, pro, rows_of(c))
        out = []
        for sl in range(n_slabs):
            cs = slice(sl * LANES, (sl + 1) * LANES)
            q_s, k_s, g_s = q_all[:, cs], k_all[:, cs], g_all[:, cs]
            out.append((q_s, k_s, g_s, _dot(tri_ref[...], _split3(g_s))))
        return out

    def score_dots(pre):
        raw, qbs, ksts, decs, rds = [], [], [], [], []
        for sl in range(n_slabs):
            q_s, k_s, g_s, b = pre[sl]
            ys = [(jnp.where(sg_ref[0] > 0.0, jnp.exp2(g_s) * q_s, k_s)).astype(bf16)]
            for lvl in range(1, N_LEVELS):
                sg = sg_ref[lvl]
                x = jnp.exp2((b - _anchor(b, lvl)) * sg)
                ys.append((x * jnp.where(sg > 0.0, q_s, k_s)).astype(bf16))
            blast = jnp.broadcast_to(b[CHUNK - 1:CHUNK, :], b.shape)
            qb = (q_s * jnp.exp2(b)).astype(bf16)
            kst = (k_s * jnp.exp2(blast - b)).astype(bf16)
            decs.append(jnp.exp2(b[CHUNK - 1:CHUNK, :]))
            qk_prod = q_s * k_s
            for hh in range(hps):
                mul = head_mask(hh)
                rds.append(jnp.sum(mul(qk_prod), axis=-1, keepdims=True))
                raw.append([_dot_nt(mul(ys[lvl]), ys[lvl]) for lvl in range(N_LEVELS)])
                qbs.append(mul(qb))
                ksts.append(mul(kst))
        return raw, rds, tuple(qbs), tuple(ksts), tuple(decs)

    def combine(raw, rds):
        ps = []
        for h in range(n_heads):
            p = jnp.where(eye, rds[h], 0.0)
            for lvl in range(N_LEVELS):
                p = p + pm_ref[lvl] * raw[h][lvl]
            ps.append(p.astype(bf16))
        return tuple(ps)

    def apply_dots(c, sc):
        ps, qbs, ksts, decs = sc
        outs = []
        for h in range(n_heads):
            vb = v_ref[rows_of(c), h * LANES:(h + 1) * LANES].astype(bf16)
            st = st_ref[h]
            outs.append(_dot(ps[h], vb) + _dot_nt(qbs[h], st.astype(bf16)))
            st_ref[h] = st * decs[h // hps] + _dot_tn(vb, ksts[h])
        return outs

    def finish(c, outs):
        for h in range(n_heads):
            cs = slice(h * LANES, (h + 1) * LANES)
            o = outs[h]
            o = o * lax.rsqrt(jnp.mean(o * o, axis=-1, keepdims=True) + RMS_EPS) * ng_ref[...]
            y_ref[rows_of(c), cs] = o * _silu(gate_ref[rows_of(c), cs])

    def first():
        raw, rds, qbs, ksts, decs = score_dots(cumsums(0))
        return combine(raw, rds), qbs, ksts, decs

    def body(c, sc):
        pre = cumsums(c + 1)
        outs = apply_dots(c, sc)
        raw, rds, qbs, ksts, decs = score_dots(pre)
        finish(c, outs)
        return combine(raw, rds), qbs, ksts, decs

    n_chunks = lt // CHUNK
    last = lax.fori_loop(0, n_chunks - 1, body, first())
    finish(n_chunks - 1, apply_dots(n_chunks - 1, last))

    @pl.when(t_idx == n_tiles - 1)
    def _():
        for h in range(n_heads):
            off = (h % hps) * dk
            s_ref[h] = st_ref[h].T[off:off + dk, :]


def _gla_prompt(kind, z, col0, extra, norm_g, batch, seq, lt, layer):
    dk = GLA_DK if kind == "gla" else HG_D
    n_heads = GLA_H if kind == "gla" else HG_H
    tiles = seq // lt
    mall, pmask, rmask = _level_tables()
    blk = lambda width, cb: pl.BlockSpec((lt, width), lambda b, t: (b * tiles + t, cb))
    if kind == "gla":
        wlr, blr = extra
        ins = [z, z, z, z, wlr, blr]
        specs = [blk(2 * GLA_KW, 0), blk(GLA_VW, 1), blk(GLA_VW, 2), blk(LANES, col0),
                 _const_spec(wlr.shape), _const_spec(blr.shape)]
    else:
        (lg,) = extra
        ins = [z, z, z, z, lg]
        specs = [blk(HG_W, 1), blk(HG_W, 2), blk(HG_W, 3), blk(HG_W, 4), _const_spec(lg.shape)]
    ins += [norm_g, mall, pmask, rmask]
    specs += [_const_spec(norm_g.shape), _const_spec(mall.shape), _const_spec(pmask.shape), _const_spec(rmask.shape)]
    return pl.pallas_call(
        functools.partial(_gla_chunk_kernel, kind, dk, n_heads, layer),
        out_shape=(jax.ShapeDtypeStruct((batch * seq, HALF), f32),
                   jax.ShapeDtypeStruct((batch, n_heads, dk, LANES), f32)),
        grid=(batch, tiles),
        in_specs=specs,
        out_specs=(pl.BlockSpec((lt, HALF), lambda b, t: (b * tiles + t, 0)),
                   pl.BlockSpec((None, n_heads, dk, LANES), lambda b, t: (b, 0, 0, 0))),
        scratch_shapes=[pltpu.VMEM((n_heads, LANES, LANES), f32)],
        compiler_params=_params("parallel", "arbitrary"),
        name=kind + "_prompt",
    )(*ins)


def _lru_gates(xc, wr_ref, wi_ref, br_ref, bi_ref, lam_ref):
    xb = xc.astype(bf16)
    r = _sigmoid(_dot(xb, wr_ref[...]) + br_ref[...])
    i = _sigmoid(_dot(xb, wi_ref[...]) + bi_ref[...])
    log_a = (-RG_C) * r * _softplus(-lam_ref[...])
    return jnp.exp(log_a), jnp.sqrt(-_expm1(2.0 * log_a)) * (i * xc)


def _lru_prompt_kernel(x_ref, gr_ref, cw_ref, cb_ref, wr_ref, wi_ref, br_ref, bi_ref, lam_ref,
                       y_ref, conv_ref, h_ref, xbuf, a_scr, h_scr, car):
    t_idx = pl.program_id(1)
    lt = x_ref.shape[0]

    @pl.when(t_idx == 0)
    def _():
        xbuf[0:SUBLANES, :] = jnp.zeros((SUBLANES, LRU_W), f32)
        car[...] = jnp.zeros(car.shape, f32)

    x = x_ref[...]
    xbuf[SUBLANES:SUBLANES + lt, :] = x
    xc = cb_ref[...] + cw_ref[CONV_W - 1:CONV_W, :] * x
    for j in range(CONV_W - 1):
        off = SUBLANES - (CONV_W - 1) + j
        xc = xc + cw_ref[j:j + 1, :] * xbuf[off:off + lt, :]
    xbuf[0:SUBLANES, :] = xbuf[lt:lt + SUBLANES, :]

    a, hh = _lru_gates(xc, wr_ref, wi_ref, br_ref, bi_ref, lam_ref)
    rig = lax.broadcasted_iota(jnp.int32, (lt, LRU_W), 0) % SUBLANES
    for d in (1, 2, 4):
        m = rig >= d
        hh = jnp.where(m, hh + a * pltpu.roll(hh, d, 0), hh)
        a = jnp.where(m, a * pltpu.roll(a, d, 0), a)
    a_scr[...] = a
    h_scr[...] = hh

    def group(g, c):
        rows = pl.ds(pl.multiple_of(g * SUBLANES, SUBLANES), SUBLANES)
        hg = h_scr[rows, :] + a_scr[rows, :] * c
        h_scr[rows, :] = hg
        return jnp.broadcast_to(hg[SUBLANES - 1:SUBLANES, :], (SUBLANES, LRU_W))

    c = lax.fori_loop(0, lt // SUBLANES, group, car[...])
    car[...] = c
    y_ref[...] = h_scr[...] * _gelu(gr_ref[...])
    conv_ref[...] = x[lt - (CONV_W - 1):lt, :]
    h_ref[...] = c[0:1, :]


def _lru_prompt(z, xr_cb, gr_cb, p, batch, seq, lt):
    tiles = seq // lt
    blk = lambda cb: pl.BlockSpec((lt, LRU_W), lambda b, t: (b * tiles + t, cb))
    consts = [p["conv_w"], p["conv_b"], p["w_r"], p["w_i"], p["b_r"], p["b_i"], p["lam"]]
    return pl.pallas_call(
        _lru_prompt_kernel,
        out_shape=(jax.ShapeDtypeStruct((batch * seq, LRU_W), f32),
                   jax.ShapeDtypeStruct((batch, CONV_W - 1, LRU_W), f32),
                   jax.ShapeDtypeStruct((batch, 1, LRU_W), f32)),
        grid=(batch, tiles),
        in_specs=[blk(xr_cb), blk(gr_cb)] + [_const_spec(c.shape) for c in consts],
        out_specs=(pl.BlockSpec((lt, LRU_W), lambda b, t: (b * tiles + t, 0)),
                   pl.BlockSpec((None, CONV_W - 1, LRU_W), lambda b, t: (b, 0, 0)),
                   pl.BlockSpec((None, 1, LRU_W), lambda b, t: (b, 0, 0))),
        scratch_shapes=[pltpu.VMEM((lt + SUBLANES, LRU_W), f32), pltpu.VMEM((lt, LRU_W), f32),
                        pltpu.VMEM((lt, LRU_W), f32), pltpu.VMEM((SUBLANES, LRU_W), f32)],
        compiler_params=_params("parallel", "arbitrary"),
        name="lru_prompt",
    )(z, z, *consts)


def _s5_param_kernel(lr_ref, li_ref, ldt_ref, btr_ref, bti_ref, pwr_ref, pwi_ref, bbr_ref, bbi_ref):
    lr_, li_ = lr_ref[...], li_ref[...]
    dt = jnp.exp(ldt_ref[...])
    mag = jnp.exp(lr_ * dt)
    ar, ai = mag * jnp.cos(li_ * dt), mag * jnp.sin(li_ * dt)
    den = lr_ * lr_ + li_ * li_
    nr, ni = ar - 1.0, ai
    fr = (nr * lr_ + ni * li_) / den
    fi = (ni * lr_ - nr * li_) / den
    bbr_ref[...] = fr[:, None, :] * btr_ref[...] - fi[:, None, :] * bti_ref[...]
    bbi_ref[...] = fr[:, None, :] * bti_ref[...] + fi[:, None, :] * btr_ref[...]
    pr, pi = ar, ai
    pwr_ref[0], pwi_ref[0] = pr, pi
    for n in range(1, SUBLANES):
        pr, pi = pr * ar - pi * ai, pr * ai + pi * ar
        pwr_ref[n], pwi_ref[n] = pr, pi


def _s5_params(lam_re, lam_im, log_dt, b_re, b_im, c_re, c_im):
    btr = jnp.swapaxes(b_re, 1, 2)
    bti = jnp.swapaxes(b_im, 1, 2)
    shp = jax.ShapeDtypeStruct
    pwr, pwi, bbr, bbi = pl.pallas_call(
        _s5_param_kernel,
        out_shape=(shp((SUBLANES, S5_G, S5_P), f32), shp((SUBLANES, S5_G, S5_P), f32),
                   shp((S5_G, S5_GH, S5_P), f32), shp((S5_G, S5_GH, S5_P), f32)),
        name="s5_params",
    )(lam_re, lam_im, log_dt.reshape(S5_G, 1), btr, bti)
    gpb = LANES // S5_GH
    nblk = S5_G // gpb
    eye = jnp.eye(gpb, dtype=f32)

    def bdiag_in(bb):
        x = bb.reshape(nblk, gpb, S5_GH, S5_P)
        return jnp.einsum('jghp,gk->jghkp', x, eye).reshape(nblk, gpb * S5_GH, gpb * S5_P)

    def bdiag_out(cc):
        x = cc.reshape(nblk, gpb, S5_GH, S5_P)
        return jnp.einsum('jghp,gk->jgpkh', x, eye).reshape(nblk, gpb * S5_P, gpb * S5_GH)

    b_bd = jnp.concatenate([bdiag_in(bbr), bdiag_in(bbi)], axis=2).astype(bf16)
    c_bd = jnp.concatenate([bdiag_out(c_re), bdiag_out(-c_im)], axis=1).astype(bf16)
    return pwr.reshape(SUBLANES, S5_N), pwi.reshape(SUBLANES, S5_N), b_bd, c_bd


S5_BLK = S5_N // (S5_G // (LANES // S5_GH))


def _s5_input(u, bbd_ref, j):
    bu = _dot(u[:, j * LANES:(j + 1) * LANES].astype(bf16), bbd_ref[j])
    return bu[:, 0:S5_BLK], bu[:, S5_BLK:2 * S5_BLK]


def _s5_output(u, hr, hi, cbd_ref, d_ref, wg_ref, bg_ref):
    ys = []
    for j in range(S5_N // S5_BLK):
        cs = slice(j * S5_BLK, (j + 1) * S5_BLK)
        hcat = jnp.concatenate([hr[:, cs], hi[:, cs]], axis=1).astype(bf16)
        ys.append(_dot(hcat, cbd_ref[j]))
    y = _gelu(jnp.concatenate(ys, axis=1) + d_ref[...] * u)
    return y * _sigmoid(_dot(y.astype(bf16), wg_ref[...]) + bg_ref[...])


def _s5_prompt_kernel(u_ref, pwr_ref, pwi_ref, bbd_ref, cbd_ref, d_ref, wg_ref, bg_ref,
                      y_ref, sr_ref, si_ref, hr_s, hi_s, car_r, car_i):
    t_idx = pl.program_id(1)
    lt = u_ref.shape[0]

    @pl.when(t_idx == 0)
    def _():
        car_r[...] = jnp.zeros(car_r.shape, f32)
        car_i[...] = jnp.zeros(car_i.shape, f32)

    u = u_ref[...]
    rig = lax.broadcasted_iota(jnp.int32, (lt, S5_BLK), 0) % SUBLANES
    for j in range(S5_N // S5_BLK):
        cs = slice(j * S5_BLK, (j + 1) * S5_BLK)
        hr, hi = _s5_input(u, bbd_ref, j)
        for d in (1, 2, 4):
            m = rig >= d
            pr, pi = pwr_ref[d - 1:d, cs], pwi_ref[d - 1:d, cs]
            sr, si = pltpu.roll(hr, d, 0), pltpu.roll(hi, d, 0)
            hr, hi = (hr + jnp.where(m, pr * sr - pi * si, 0.0),
                      hi + jnp.where(m, pr * si + pi * sr, 0.0))
        hr_s[:, cs] = hr
        hi_s[:, cs] = hi

    tr, ti = pwr_ref[...], pwi_ref[...]

    def group(g, c):
        cr, ci = c
        rows = pl.ds(pl.multiple_of(g * SUBLANES, SUBLANES), SUBLANES)
        gr_ = hr_s[rows, :] + tr * cr - ti * ci
        gi_ = hi_s[rows, :] + tr * ci + ti * cr
        hr_s[rows, :] = gr_
        hi_s[rows, :] = gi_
        return (jnp.broadcast_to(gr_[SUBLANES - 1:SUBLANES, :], (SUBLANES, S5_N)),
                jnp.broadcast_to(gi_[SUBLANES - 1:SUBLANES, :], (SUBLANES, S5_N)))

    cr, ci = lax.fori_loop(0, lt // SUBLANES, group, (car_r[...], car_i[...]))
    car_r[...] = cr
    car_i[...] = ci
    y_ref[...] = _s5_output(u, hr_s[...], hi_s[...], cbd_ref, d_ref, wg_ref, bg_ref)
    sr_ref[...] = cr[0:1, :]
    si_ref[...] = ci[0:1, :]


def _s5_prompt(z, u_cb, tabs, p, batch, seq, lt):
    tiles = seq // lt
    pwr, pwi, b_bd, c_bd = tabs
    consts = [pwr, pwi, b_bd, c_bd, p["d"], p["w_glu"], p["b_glu"]]
    return pl.pallas_call(
        _s5_prompt_kernel,
        out_shape=(jax.ShapeDtypeStruct((batch * seq, S5_W), f32),
                   jax.ShapeDtypeStruct((batch, 1, S5_N), f32),
                   jax.ShapeDtypeStruct((batch, 1, S5_N), f32)),
        grid=(batch, tiles),
        in_specs=[pl.BlockSpec((lt, S5_W), lambda b, t: (b * tiles + t, u_cb))]
        + [_const_spec(c.shape) for c in consts],
        out_specs=(pl.BlockSpec((lt, S5_W), lambda b, t: (b * tiles + t, 0)),
                   pl.BlockSpec((None, 1, S5_N), lambda b, t: (b, 0, 0)),
                   pl.BlockSpec((None, 1, S5_N), lambda b, t: (b, 0, 0))),
        scratch_shapes=[pltpu.VMEM((lt, S5_N), f32), pltpu.VMEM((lt, S5_N), f32),
                        pltpu.VMEM((SUBLANES, S5_N), f32), pltpu.VMEM((SUBLANES, S5_N), f32)],
        compiler_params=_params("parallel", "arbitrary"),
        name="s5_prompt",
    )(z, *consts)


def _gla_step_kernel(kind, dk, layer, *refs):
    if kind == "gla":
        q_ref, k_ref, v_ref, gate_ref, lr_ref, wlr_ref, blr_ref, ng_ref, s_ref, y_ref, so_ref, o_scr = refs
        sl = pl.program_id(0)
        q = q_ref[...] * (GLA_DK ** -0.5)
        k = k_ref[...]
        pre_all = _dot(lr_ref[...].astype(bf16), wlr_ref[...]) + blr_ref[...]
        pre = jnp.where(sl == 0, pre_all[:, 0:LANES], pre_all[:, LANES:2 * LANES])
        a = jnp.exp(_log_sigmoid(pre) * (1.0 / GLA_TAU))
    else:
        q_ref, f_ref, v_ref, gate_ref, lg_ref, ng_ref, s_ref, y_ref, so_ref, o_scr = refs
        sl = pl.program_id(0)
        lb_all = _hgrn_lower_bound(lg_ref, layer)
        lb = lb_all[:, 0:LANES]
        for j in range(1, HG_H):
            lb = jnp.where(sl == j, lb_all[:, j * LANES:(j + 1) * LANES], lb)
        a = lb + (1.0 - lb) * _sigmoid(f_ref[...])
        q = _silu(q_ref[...])
        k = 1.0 - a
    hps = LANES // dk
    nb = q.shape[0]
    qat, kt, at = (q * a).T, k.T, a.T
    lane = lax.broadcasted_iota(jnp.int32, (1, LANES), 1)
    v = v_ref[...]
    for hh in range(hps):
        r0 = hh * dk
        msk = (lane >= r0) & (lane < r0 + dk)
        qk = jnp.sum(jnp.where(msk, q * k, 0.0), axis=-1, keepdims=True)
        for b in range(nb):
            s = s_ref[b, hh]
            col = lambda x: jnp.broadcast_to(x[r0:r0 + dk, b:b + 1], (dk, LANES))
            vr = v[b:b + 1, hh * LANES:(hh + 1) * LANES]
            so_ref[b, hh] = col(at) * s + col(kt) * vr
            oi = jnp.sum(col(qat) * s, axis=0, keepdims=True)
            o_scr[b:b + 1, hh * LANES:(hh + 1) * LANES] = jnp.broadcast_to(qk[b:b + 1, :], (1, LANES)) * vr + oi
    for hh in range(hps):
        cs = slice(hh * LANES, (hh + 1) * LANES)
        o = o_scr[:, cs]
        o = o * lax.rsqrt(jnp.mean(o * o, axis=-1, keepdims=True) + RMS_EPS) * ng_ref[...]
        y_ref[:, cs] = o * _silu(gate_ref[:, cs])


def _gla_step(kind, z, extra, norm_g, state, layer):
    nb = z.shape[0]
    dk = GLA_DK if kind == "gla" else HG_D
    n_heads = GLA_H if kind == "gla" else HG_H
    hps = LANES // dk
    n_slabs = n_heads // hps
    vw = hps * LANES
    zb = lambda width, cb0: pl.BlockSpec((nb, width), lambda s: (0, cb0 + s))
    if kind == "gla":
        wlr, blr = extra
        ins = [z, z, z, z, z, wlr, blr]
        specs = [zb(LANES, 0), zb(LANES, GLA_KW // LANES), zb(vw, GLA_VW // vw * 1), zb(vw, GLA_VW // vw * 2),
                 pl.BlockSpec((nb, LANES), lambda s: (0, (EVEN_N - LANES) // LANES)),
                 _const_spec(wlr.shape), _const_spec(blr.shape)]
    else:
        (lg,) = extra
        ins = [z, z, z, z, lg]
        specs = [zb(LANES, HG_W // LANES * 1), zb(LANES, HG_W // LANES * 2), zb(vw, HG_W // vw * 3),
                 zb(vw, HG_W // vw * 4), _const_spec(lg.shape)]
    ins += [norm_g, state]
    specs += [_const_spec(norm_g.shape), pl.BlockSpec((nb, hps, dk, LANES), lambda s: (0, s, 0, 0))]
    return pl.pallas_call(
        functools.partial(_gla_step_kernel, kind, dk, layer),
        out_shape=(jax.ShapeDtypeStruct((nb, HALF), f32), jax.ShapeDtypeStruct(state.shape, f32)),
        grid=(n_slabs,),
        in_specs=specs,
        out_specs=(pl.BlockSpec((nb, vw), lambda s: (0, s)),
                   pl.BlockSpec((nb, hps, dk, LANES), lambda s: (0, s, 0, 0))),
        scratch_shapes=[pltpu.VMEM((nb, vw), f32)],
        compiler_params=_params("parallel"),
        name=kind + "_step",
    )(*ins)


def _lru_step_kernel(x_ref, gr_ref, c0_ref, c1_ref, c2_ref, h0_ref, cw_ref, cb_ref, wr_ref, wi_ref,
                     br_ref, bi_ref, lam_ref, y_ref, h_ref):
    xc = (cb_ref[...] + cw_ref[0:1, :] * c0_ref[...] + cw_ref[1:2, :] * c1_ref[...]
          + cw_ref[2:3, :] * c2_ref[...] + cw_ref[3:4, :] * x_ref[...])
    a, u = _lru_gates(xc, wr_ref, wi_ref, br_ref, bi_ref, lam_ref)
    h = a * h0_ref[...] + u
    h_ref[...] = h
    y_ref[...] = h * _gelu(gr_ref[...])


def _lru_step(z, xr_cb, gr_cb, conv_rows, h0, p):
    nb = z.shape[0]
    full = lambda: pl.BlockSpec((nb, LRU_W), lambda i: (0, 0))
    consts = [p["conv_w"], p["conv_b"], p["w_r"], p["w_i"], p["b_r"], p["b_i"], p["lam"]]
    return pl.pallas_call(
        _lru_step_kernel,
        out_shape=(jax.ShapeDtypeStruct((nb, LRU_W), f32), jax.ShapeDtypeStruct((nb, LRU_W), f32)),
        grid=(1,),
        in_specs=[pl.BlockSpec((nb, LRU_W), lambda i: (0, xr_cb)), pl.BlockSpec((nb, LRU_W), lambda i: (0, gr_cb)),
                  full(), full(), full(), full()] + [_const_spec(c.shape) for c in consts],
        out_specs=(full(), full()),
        compiler_params=_params("arbitrary"),
        name="lru_step",
    )(z, z, *conv_rows, h0, *consts)


def _s5_step_kernel(u_ref, s0r_ref, s0i_ref, pwr_ref, pwi_ref, bbd_ref, cbd_ref, d_ref, wg_ref, bg_ref,
                    y_ref, sr_ref, si_ref):
    u = u_ref[...]
    for j in range(S5_N // S5_BLK):
        cs = slice(j * S5_BLK, (j + 1) * S5_BLK)
        bur, bui = _s5_input(u, bbd_ref, j)
        ar, ai = pwr_ref[0:1, cs], pwi_ref[0:1, cs]
        s0r, s0i = s0r_ref[:, cs], s0i_ref[:, cs]
        sr_ref[:, cs] = bur + ar * s0r - ai * s0i
        si_ref[:, cs] = bui + ar * s0i + ai * s0r
    y_ref[...] = _s5_output(u, sr_ref[...], si_ref[...], cbd_ref, d_ref, wg_ref, bg_ref)


def _s5_step(z, u_cb, s0r, s0i, tabs, p):
    nb = z.shape[0]
    pwr, pwi, b_bd, c_bd = tabs
    consts = [pwr, pwi, b_bd, c_bd, p["d"], p["w_glu"], p["b_glu"]]
    st = lambda: pl.BlockSpec((nb, S5_N), lambda i: (0, 0))
    return pl.pallas_call(
        _s5_step_kernel,
        out_shape=(jax.ShapeDtypeStruct((nb, S5_W), f32), jax.ShapeDtypeStruct((nb, S5_N), f32),
                   jax.ShapeDtypeStruct((nb, S5_N), f32)),
        grid=(1,),
        in_specs=[pl.BlockSpec((nb, S5_W), lambda i: (0, u_cb)), st(), st()] + [_const_spec(c.shape) for c in consts],
        out_specs=(pl.BlockSpec((nb, S5_W), lambda i: (0, 0)), st(), st()),
        compiler_params=_params("arbitrary"),
        name="s5_step",
    )(z, s0r, s0i, *consts)


def _block_diag(w):
    h, n, _ = w.shape
    return jnp.einsum('hij,hk->hikj', w, jnp.eye(h, dtype=w.dtype)).reshape(h * n, h * n)


def kernel(x_prompt, x_sample, c_prompt, c_sample, state_gla, state_rglru_conv, state_rglru_h, state_s5_re, state_s5_im, state_hgrn, ev_w_in, ev_gla_w_lr, ev_gla_b_lr, ev_gla_norm_g, ev_conv_w, ev_conv_b, ev_lru_w_r, ev_lru_b_r, ev_lru_w_i, ev_lru_b_i, ev_lru_lam, ev_w_out, od_w_in, od_s5_lam_re, od_s5_lam_im, od_s5_log_dt, od_s5_b_re, od_s5_b_im, od_s5_c_re, od_s5_c_im, od_s5_d, od_s5_w_glu, od_s5_b_glu, hg_lb_logits, od_hg_norm_g, od_w_out, w_ada, b_ada, ln_g, ln_b, ffn_w_in, ffn_w_out):
    bp, seq = x_prompt.shape[0], x_prompt.shape[1]
    bs = x_sample.shape[0]
    assert x_sample.shape[1] == 1 and seq % CHUNK == 0
    tp = bp * seq
    tm = min(512, seq)
    lt = min(512, seq)
    lt_gla = min(1024, seq)
    lt_s5 = min(256, seq)
    row2 = lambda a: a.reshape(1, -1)

    mod = _modulation(jnp.concatenate([c_prompt, c_sample], axis=0).astype(f32), w_ada, b_ada)
    mod_p = mod[:, :bp].reshape(DEPTH, bp, 1, 6 * D_MODEL)
    mod_s = mod[:, bp:].reshape(DEPTH, 1, bs, 6 * D_MODEL)

    xp = x_prompt.astype(f32).reshape(tp, D_MODEL)
    xs = x_sample.astype(f32).reshape(bs, D_MODEL)
    outs_p = {k: [] for k in ("gla", "conv", "lru", "re", "im", "hg")}
    outs_s = {k: [] for k in ("gla", "conv", "lru", "re", "im", "hg")}

    for l in range(DEPTH):
        w_f_in = ffn_w_in[l].astype(bf16)
        w_f_out = ffn_w_out[l].astype(bf16)
        if l % 2 == 0:
            e = l // 2
            w = ev_w_in[e]
            lr0 = 2 * GLA_KW + 2 * GLA_VW
            w_in = jnp.concatenate(
                [w[:, :lr0], w[:, lr0 + GLA_LR:], w[:, lr0:lr0 + GLA_LR],
                 jnp.zeros((D_MODEL, EVEN_N - w.shape[1]), w.dtype)], axis=1).astype(bf16)
            w_out = ev_w_out[e].astype(bf16)
            wlr = jnp.concatenate([ev_gla_w_lr[e], jnp.zeros((LANES - GLA_LR, GLA_KW), f32)], axis=0).astype(bf16)
            gla_extra = (wlr, row2(ev_gla_b_lr[e]))
            ng = row2(ev_gla_norm_g[e])
            lru_p = dict(conv_w=ev_conv_w[e], conv_b=row2(ev_conv_b[e]),
                         w_r=_block_diag(ev_lru_w_r[e]).astype(bf16), w_i=_block_diag(ev_lru_w_i[e]).astype(bf16),
                         b_r=row2(ev_lru_b_r[e]), b_i=row2(ev_lru_b_i[e]), lam=row2(ev_lru_lam[e]))
            xr_cb, gr_cb = lr0 // LRU_W, lr0 // LRU_W + 1
            z = _inproj(xp, mod_p[l], w_in, tm, seq // tm)
            ya, s_gla = _gla_prompt("gla", z, (EVEN_N - LANES) // LANES, gla_extra, ng, bp, seq, lt_gla, l)
            yb, s_conv, s_h = _lru_prompt(z, xr_cb, gr_cb, lru_p, bp, seq, lt)
            xp = _outffn(xp, ya, yb, mod_p[l], w_out, ln_g[l], ln_b[l], w_f_in, w_f_out, tm, seq // tm)
            outs_p["gla"].append(s_gla)
            outs_p["conv"].append(s_conv)
            outs_p["lru"].append(s_h.reshape(bp, LRU_W))
            z = _inproj(xs, mod_s[l], w_in, bs, 1)
            ya, s_gla = _gla_step("gla", z, gla_extra, ng, state_gla[e], l)
            cs = state_rglru_conv[e].astype(f32)
            yb, s_h = _lru_step(z, xr_cb, gr_cb, [cs[:, j] for j in range(CONV_W - 1)],
                                state_rglru_h[e].astype(f32), lru_p)
            xs = _outffn(xs, ya, yb, mod_s[l], w_out, ln_g[l], ln_b[l], w_f_in, w_f_out, bs, 1)
            outs_s["gla"].append(s_gla)
            outs_s["conv"].append(jnp.stack([cs[:, 1], cs[:, 2], z[:, xr_cb * LRU_W:(xr_cb + 1) * LRU_W]], axis=1))
            outs_s["lru"].append(s_h)
        else:
            o = l // 2
            w_in = od_w_in[o].astype(bf16)
            w_out = od_w_out[o].astype(bf16)
            tabs = _s5_params(od_s5_lam_re[o], od_s5_lam_im[o], od_s5_log_dt[o], od_s5_b_re[o], od_s5_b_im[o],
                              od_s5_c_re[o], od_s5_c_im[o])
            s5_p = dict(d=row2(od_s5_d[o]), w_glu=od_s5_w_glu[o].astype(bf16), b_glu=row2(od_s5_b_glu[o]))
            hg_extra = (hg_lb_logits.astype(f32),)
            ng = row2(od_hg_norm_g[o])
            z = _inproj(xp, mod_p[l], w_in, tm, seq // tm)
            ya, s_re, s_im = _s5_prompt(z, 0, tabs, s5_p, bp, seq, lt_s5)
            yb, s_hg = _gla_prompt("hgrn", z, 0, hg_extra, ng, bp, seq, lt_gla, l)
            xp = _outffn(xp, ya, yb, mod_p[l], w_out, ln_g[l], ln_b[l], w_f_in, w_f_out, tm, seq // tm)
            outs_p["re"].append(s_re.reshape(bp, S5_G, S5_P))
            outs_p["im"].append(s_im.reshape(bp, S5_G, S5_P))
            outs_p["hg"].append(s_hg)
            z = _inproj(xs, mod_s[l], w_in, bs, 1)
            ya, s_re, s_im = _s5_step(z, 0, state_s5_re[o].astype(f32).reshape(bs, S5_N),
                                      state_s5_im[o].astype(f32).reshape(bs, S5_N), tabs, s5_p)
            yb, s_hg = _gla_step("hgrn", z, hg_extra, ng, state_hgrn[o], l)
            xs = _outffn(xs, ya, yb, mod_s[l], w_out, ln_g[l], ln_b[l], w_f_in, w_f_out, bs, 1)
            outs_s["re"].append(s_re.reshape(bs, S5_G, S5_P))
            outs_s["im"].append(s_im.reshape(bs, S5_G, S5_P))
            outs_s["hg"].append(s_hg)

    st = lambda d, k: jnp.stack(d[k])
    return (xp.reshape(bp, seq, D_MODEL).astype(x_prompt.dtype), xs.reshape(bs, 1, D_MODEL).astype(x_sample.dtype),
            st(outs_p, "gla"), st(outs_s, "gla"), st(outs_p, "conv"), st(outs_s, "conv"),
            st(outs_p, "lru"), st(outs_s, "lru"), st(outs_p, "re"), st(outs_s, "re"),
            st(outs_p, "im"), st(outs_s, "im"), st(outs_p, "hg"), st(outs_s, "hg"))
```

```python
import functools
import math

import numpy as np
import jax
import jax.numpy as jnp
from jax import lax
from jax.experimental import pallas as pl
from jax.experimental.pallas import tpu as pltpu

f32 = jnp.float32
bf16 = jnp.bfloat16

D_MODEL = 1024
DEPTH = 2
HALF = D_MODEL // 2
GLA_H = 4
GLA_DV = HALF // GLA_H
GLA_DK = GLA_DV // 2
GLA_KW = GLA_H * GLA_DK
GLA_VW = GLA_H * GLA_DV
GLA_LR = 16
GLA_TAU = 16.0
LRU_W = HALF
LRU_H = 8
LRU_BW = LRU_W // LRU_H
CONV_W = 4
RG_C = 8.0
S5_W = HALF
S5_GH = 16
S5_G = S5_W // S5_GH
S5_P = 64
S5_N = S5_G * S5_P
HG_H = 4
HG_D = HALF // HG_H
HG_W = HG_H * HG_D
D_FF = ((8 * D_MODEL // 3 + 255) // 256) * 256
ALPHA = (2.0 * DEPTH) ** 0.25
LN_EPS = 1e-5
RMS_EPS = 1e-6

LANES = 128
SUBLANES = 8
VMEM_LIMIT = 56 * 1024 * 1024

CHUNK = 64
N_LEVELS = 6
FF_CHUNK = 256
EVEN_N = 2688
ODD_N = 2560


def _sigmoid(x):
    return jax.nn.sigmoid(x)


def _silu(x):
    return x * jax.nn.sigmoid(x)


def _gelu(x):
    c = math.sqrt(2.0 / math.pi)
    return x * (0.5 * (1.0 + jnp.tanh(c * (x + 0.044715 * (x * x * x)))))


def _softplus(x):
    return jnp.maximum(x, 0.0) + jnp.log1p(jnp.exp(-jnp.abs(x)))


def _log_sigmoid(x):
    return -_softplus(-x)


def _expm1(x):
    u = jnp.exp(x)
    um1 = u - 1.0
    lg = jnp.log(u)
    safe = jnp.where(lg == 0.0, 1.0, lg)
    return jnp.where(um1 == 0.0, x, jnp.where(um1 == -1.0, -1.0, um1 * x / safe))


def _layer_norm(x, g, b):
    mu = jnp.mean(x, axis=-1, keepdims=True)
    xc = x - mu
    var = jnp.mean(xc * xc, axis=-1, keepdims=True)
    return xc * lax.rsqrt(var + LN_EPS) * g + b


def _dot(a, b):
    return jnp.dot(a, b, preferred_element_type=f32)


def _dot_nt(a, b):
    return lax.dot_general(a, b, (((1,), (1,)), ((), ())), preferred_element_type=f32)


def _dot_tn(a, b):
    return lax.dot_general(a, b, (((0,), (0,)), ((), ())), preferred_element_type=f32)


def _const_spec(shape):
    nd = len(shape)
    return pl.BlockSpec(shape, lambda *_: (0,) * nd, pipeline_mode=pl.Buffered(1))


def _params(*sem):
    return pltpu.CompilerParams(dimension_semantics=sem, vmem_limit_bytes=VMEM_LIMIT)


def _mod_kernel(c_ref, w_ref, b_ref, o_ref):
    cond = _silu(c_ref[...]).astype(bf16)
    o_ref[...] = _dot(cond, w_ref[...].astype(bf16)) + b_ref[...]


def _modulation(c_all, w_ada, b_ada):
    rows = c_all.shape[0]
    tn = 1536
    return pl.pallas_call(
        _mod_kernel,
        out_shape=jax.ShapeDtypeStruct((DEPTH, rows, 6 * D_MODEL), f32),
        grid=(DEPTH, 6 * D_MODEL // tn),
        in_specs=[pl.BlockSpec((rows, D_MODEL), lambda l, j: (0, 0)),
                  pl.BlockSpec((None, D_MODEL, tn), lambda l, j: (l, 0, j)),
                  pl.BlockSpec((None, 1, tn), lambda l, j: (l, 0, j))],
        out_specs=pl.BlockSpec((None, rows, tn), lambda l, j: (l, 0, j)),
        compiler_params=_params("parallel", "parallel"),
        name="adaln_mod",
    )(c_all, w_ada, b_ada.reshape(DEPTH, 1, 6 * D_MODEL))


def _mod_spec(rows_per_block, tiles_per_batch, j):
    return pl.BlockSpec((None, rows_per_block, D_MODEL), lambda i: (i // tiles_per_batch, 0, j))


def _inproj_kernel(x_ref, sc_ref, sh_ref, w_ref, *z_refs):
    h = (x_ref[...] * (1.0 + sc_ref[...]) + sh_ref[...]).astype(bf16)
    lo = 0
    for z_ref in z_refs:
        z_ref[...] = _dot(h, w_ref[:, lo:lo + z_ref.shape[1]])
        lo += z_ref.shape[1]


def _inproj(x, mod, w, tm, tiles_per_batch, widths):
    t, n = x.shape[0], w.shape[1]
    assert sum(widths) == n
    r = mod.shape[1]
    return pl.pallas_call(
        _inproj_kernel,
        out_shape=tuple(jax.ShapeDtypeStruct((t, wd), f32) for wd in widths),
        grid=(t // tm,),
        in_specs=[pl.BlockSpec((tm, D_MODEL), lambda i: (i, 0)),
                  _mod_spec(r, tiles_per_batch, 1), _mod_spec(r, tiles_per_batch, 0),
                  _const_spec((D_MODEL, n))],
        out_specs=tuple(pl.BlockSpec((tm, wd), lambda i: (i, 0)) for wd in widths),
        compiler_params=_params("parallel"),
        name="in_proj",
    )(x, mod, mod, w)


def _outffn_kernel(x_ref, ya_ref, yb_ref, gtm_ref, shf_ref, scf_ref, gtf_ref,
                   wo_ref, lng_ref, lnb_ref, wi_ref, wf_ref, o_ref):
    mix = (_dot(ya_ref[...].astype(bf16), wo_ref[0:HALF, :])
           + _dot(yb_ref[...].astype(bf16), wo_ref[HALF:D_MODEL, :]))
    x1 = _layer_norm(ALPHA * x_ref[...] + (1.0 + gtm_ref[...]) * mix, lng_ref[0:1, :], lnb_ref[0:1, :])
    h = (x1 * (1.0 + scf_ref[...]) + shf_ref[...]).astype(bf16)
    acc = jnp.zeros(x1.shape, f32)
    act = None
    n_ff = D_FF // FF_CHUNK
    for c in range(n_ff + 1):
        lo = c * FF_CHUNK
        if c < n_ff:
            gate = _dot(h, wi_ref[:, lo:lo + FF_CHUNK])
            up = _dot(h, wi_ref[:, D_FF + lo:D_FF + lo + FF_CHUNK])
        if act is not None:
            acc = acc + _dot(act, wf_ref[lo - FF_CHUNK:lo, :])
        if c < n_ff:
            act = (_silu(gate) * up).astype(bf16)
    o_ref[...] = _layer_norm(ALPHA * x1 + (1.0 + gtf_ref[...]) * acc, lng_ref[1:2, :], lnb_ref[1:2, :])


def _outffn(x, ya, yb, mod, w_out, ln_g, ln_b, w_in, w_f, tm, tiles_per_batch):
    t = x.shape[0]
    r = mod.shape[1]
    row = lambda w: pl.BlockSpec((tm, w), lambda i: (i, 0))
    return pl.pallas_call(
        _outffn_kernel,
        out_shape=jax.ShapeDtypeStruct((t, D_MODEL), f32),
        grid=(t // tm,),
        in_specs=[row(D_MODEL), row(HALF), row(HALF),
                  _mod_spec(r, tiles_per_batch, 2), _mod_spec(r, tiles_per_batch, 3),
                  _mod_spec(r, tiles_per_batch, 4), _mod_spec(r, tiles_per_batch, 5),
                  _const_spec((D_MODEL, D_MODEL)), _const_spec((2, D_MODEL)), _const_spec((2, D_MODEL)),
                  _const_spec((D_MODEL, 2 * D_FF)), _const_spec((D_FF, D_MODEL))],
        out_specs=row(D_MODEL),
        compiler_params=_params("parallel"),
        name="outproj_ffn",
    )(x, ya, yb, mod, mod, mod, mod, w_out, ln_g, ln_b, w_in, w_f)


def _level_tables():
    c = CHUNK
    t = np.arange(c)[:, None]
    u = np.arange(c)[None, :]
    pmask, sgn = [], []
    for lvl in range(N_LEVELS):
        m = 1 << lvl
        right = (t % (2 * m)) >= m
        pm = right & ((u % (2 * m)) < m) & ((t // (2 * m)) == (u // (2 * m)))
        pmask.append(pm.astype(np.float32))
        sgn.append(np.broadcast_to(np.where(right, 1.0, -1.0).astype(np.float32), (c, LANES)))
    tri = (u <= t).astype(np.float32)
    tri3 = np.concatenate([tri, tri, tri], axis=1)
    return (jnp.asarray(tri3, dtype=bf16), jnp.asarray(np.stack(pmask), dtype=f32),
            jnp.asarray(np.stack(sgn), dtype=f32))


LOG2E = 1.0 / math.log(2.0)


def _anchor(b, lvl):
    m = 1 << lvl
    if lvl == 1:
        b3 = b.reshape(CHUNK // SUBLANES, SUBLANES, LANES)
        sub = lax.broadcasted_iota(jnp.int32, b3.shape, 1)
        a = jnp.where(sub < 4, jnp.broadcast_to(b3[:, 1:2, :], b3.shape), jnp.broadcast_to(b3[:, 5:6, :], b3.shape))
        return a.reshape(CHUNK, LANES)
    b3 = b.reshape(CHUNK // (2 * m), 2 * m, LANES)
    return jnp.broadcast_to(b3[:, m - 1:m, :], b3.shape).reshape(CHUNK, LANES)


def _split3(g):
    g1 = g.astype(bf16)
    r1 = g - g1.astype(f32)
    g2 = r1.astype(bf16)
    r2 = r1 - g2.astype(f32)
    return jnp.concatenate([g1, g2, r2.astype(bf16)], axis=0)


def _hgrn_lower_bound(lg_ref, layer):
    rows = [lg_ref[i:i + 1, :] for i in range(DEPTH)]
    mx = functools.reduce(jnp.maximum, rows)
    ex = [jnp.exp(r - mx) for r in rows]
    den = functools.reduce(lambda a, b: a + b, ex)
    sm = [e / den for e in ex]
    cs = sm[0]
    for i in range(1, layer + 1):
        cs = cs + sm[i]
    return cs - sm[0]


def _gla_prologue(kind, refs, rows):
    if kind == "gla":
        qk_ref, lr_ref, wlr_ref, blr_ref = refs
        q = qk_ref[rows, 0:GLA_KW] * (GLA_DK ** -0.5)
        k = qk_ref[rows, GLA_KW:2 * GLA_KW]
        pre = _dot(lr_ref[rows, :].astype(bf16), wlr_ref[...]) + blr_ref[...]
        g = _log_sigmoid(pre) * (LOG2E / GLA_TAU)
        return q, k, g
    q_ref, f_ref, lb = refs
    forget = lb + (1.0 - lb) * _sigmoid(f_ref[rows, :])
    return _silu(q_ref[rows, :]), 1.0 - forget, jnp.log2(forget)


def _gla_chunk_kernel(kind, dk, n_heads, layer, *refs):
    if kind == "gla":
        (qk_ref, v_ref, gate_ref, lr_ref, wlr_ref, blr_ref, ng_ref, tri_ref, pm_ref, sg_ref,
         y_ref, s_ref, st_ref) = refs
        pro = (qk_ref, lr_ref, wlr_ref, blr_ref)
    else:
        (q_ref, f_ref, v_ref, gate_ref, lg_ref, ng_ref, tri_ref, pm_ref, sg_ref,
         y_ref, s_ref, st_ref) = refs
        pro = (q_ref, f_ref, _hgrn_lower_bound(lg_ref, layer))
    t_idx = pl.program_id(1)
    n_tiles = pl.num_programs(1)
    lt = y_ref.shape[0]
    hps = LANES // dk
    n_slabs = n_heads // hps
    lane = lax.broadcasted_iota(jnp.int32, (1, LANES), 1)
    eye = (lax.broadcasted_iota(jnp.int32, (CHUNK, CHUNK), 0)
           == lax.broadcasted_iota(jnp.int32, (CHUNK, CHUNK), 1))

    @pl.when(t_idx == 0)
    def _():
        st_ref[...] = jnp.zeros(st_ref.shape, f32)

    def head_mask(hh):
        if hps == 1:
            return lambda a: a
        msk = (lane >= hh * dk) & (lane < (hh + 1) * dk)
        return lambda a: jnp.where(msk, a, jnp.zeros_like(a))

    def rows_of(c):
        return pl.ds(pl.multiple_of(c * CHUNK, CHUNK), CHUNK)

    def cumsums(c):
        q_all, k_all, g_all = _gla_prologue(kind, pro, rows_of(c))
        out = []
        for sl in range(n_slabs):
            cs = slice(sl * LANES, (sl + 1) * LANES)
            q_s, k_s, g_s = q_all[:, cs], k_all[:, cs], g_all[:, cs]
            out.append((q_s, k_s, g_s, _dot(tri_ref[...], _split3(g_s))))
        return out

    def score_dots(pre):
        raw, qbs, ksts, decs, rds = [], [], [], [], []
        for sl in range(n_slabs):
            q_s, k_s, g_s, b = pre[sl]
            ys = [(jnp.where(sg_ref[0] > 0.0, jnp.exp2(g_s) * q_s, k_s)).astype(bf16)]
            for lvl in range(1, N_LEVELS):
                sg = sg_ref[lvl]
                x = jnp.exp2((b - _anchor(b, lvl)) * sg)
                ys.append((x * jnp.where(sg > 0.0, q_s, k_s)).astype(bf16))
            blast = jnp.broadcast_to(b[CHUNK - 1:CHUNK, :], b.shape)
            qb = (q_s * jnp.exp2(b)).astype(bf16)
            kst = (k_s * jnp.exp2(blast - b)).astype(bf16)
            decs.append(jnp.exp2(b[CHUNK - 1:CHUNK, :]))
            qk_prod = q_s * k_s
            for hh in range(hps):
                mul = head_mask(hh)
                rds.append(jnp.sum(mul(qk_prod), axis=-1, keepdims=True))
                raw.append([_dot_nt(mul(ys[lvl]), ys[lvl]) for lvl in range(N_LEVELS)])
                qbs.append(mul(qb))
                ksts.append(mul(kst))
        return raw, rds, tuple(qbs), tuple(ksts), tuple(decs)

    def combine(raw, rds):
        ps = []
        for h in range(n_heads):
            p = jnp.where(eye, rds[h], 0.0)
            for lvl in range(N_LEVELS):
                p = p + pm_ref[lvl] * raw[h][lvl]
            ps.append(p.astype(bf16))
        return tuple(ps)

    def apply_dots(c, sc):
        ps, qbs, ksts, decs = sc
        outs = []
        for h in range(n_heads):
            vb = v_ref[rows_of(c), h * LANES:(h + 1) * LANES].astype(bf16)
            st = st_ref[h]
            outs.append(_dot(ps[h], vb) + _dot_nt(qbs[h], st.astype(bf16)))
            st_ref[h] = st * decs[h // hps] + _dot_tn(vb, ksts[h])
        return outs

    def finish(c, outs):
        for h in range(n_heads):
            cs = slice(h * LANES, (h + 1) * LANES)
            o = outs[h]
            o = o * lax.rsqrt(jnp.mean(o * o, axis=-1, keepdims=True) + RMS_EPS) * ng_ref[...]
            y_ref[rows_of(c), cs] = o * _silu(gate_ref[rows_of(c), cs])

    def first():
        raw, rds, qbs, ksts, decs = score_dots(cumsums(0))
        return combine(raw, rds), qbs, ksts, decs

    def body(c, sc):
        pre = cumsums(c + 1)
        outs = apply_dots(c, sc)
        raw, rds, qbs, ksts, decs = score_dots(pre)
        finish(c, outs)
        return combine(raw, rds), qbs, ksts, decs

    n_chunks = lt // CHUNK
    last = lax.fori_loop(0, n_chunks - 1, body, first())
    finish(n_chunks - 1, apply_dots(n_chunks - 1, last))

    @pl.when(t_idx == n_tiles - 1)
    def _():
        for h in range(n_heads):
            off = (h % hps) * dk
            s_ref[h] = st_ref[h].T[off:off + dk, :]


def _gla_prompt(kind, z, col0, extra, norm_g, batch, seq, lt, layer):
    dk = GLA_DK if kind == "gla" else HG_D
    n_heads = GLA_H if kind == "gla" else HG_H
    tiles = seq // lt
    mall, pmask, rmask = _level_tables()
    blk = lambda width, cb: pl.BlockSpec((lt, width), lambda b, t: (b * tiles + t, cb))
    if kind == "gla":
        wlr, blr = extra
        ins = [z, z, z, z, wlr, blr]
        specs = [blk(2 * GLA_KW, 0), blk(GLA_VW, 1), blk(GLA_VW, 2), blk(LANES, col0),
                 _const_spec(wlr.shape), _const_spec(blr.shape)]
    else:
        (lg,) = extra
        ins = [z, z, z, z, lg]
        specs = [blk(HG_W, col0), blk(HG_W, col0 + 1), blk(HG_W, col0 + 2), blk(HG_W, col0 + 3), _const_spec(lg.shape)]
    ins += [norm_g, mall, pmask, rmask]
    specs += [_const_spec(norm_g.shape), _const_spec(mall.shape), _const_spec(pmask.shape), _const_spec(rmask.shape)]
    return pl.pallas_call(
        functools.partial(_gla_chunk_kernel, kind, dk, n_heads, layer),
        out_shape=(jax.ShapeDtypeStruct((batch * seq, HALF), f32),
                   jax.ShapeDtypeStruct((batch, n_heads, dk, LANES), f32)),
        grid=(batch, tiles),
        in_specs=specs,
        out_specs=(pl.BlockSpec((lt, HALF), lambda b, t: (b * tiles + t, 0)),
                   pl.BlockSpec((None, n_heads, dk, LANES), lambda b, t: (b, 0, 0, 0))),
        scratch_shapes=[pltpu.VMEM((n_heads, LANES, LANES), f32)],
        compiler_params=_params("parallel", "arbitrary"),
        name=kind + "_prompt",
    )(*ins)


def _lru_gates(xc, wr_ref, wi_ref, br_ref, bi_ref, lam_ref):
    xb = xc.astype(bf16)
    r = _sigmoid(_dot(xb, wr_ref[...]) + br_ref[...])
    i = _sigmoid(_dot(xb, wi_ref[...]) + bi_ref[...])
    log_a = (-RG_C) * r * _softplus(-lam_ref[...])
    return jnp.exp(log_a), jnp.sqrt(-_expm1(2.0 * log_a)) * (i * xc)


def _lru_prompt_kernel(x_ref, gr_ref, cw_ref, cb_ref, wr_ref, wi_ref, br_ref, bi_ref, lam_ref,
                       y_ref, conv_ref, h_ref, xbuf, a_scr, h_scr, car):
    t_idx = pl.program_id(1)
    lt = x_ref.shape[0]

    @pl.when(t_idx == 0)
    def _():
        xbuf[0:SUBLANES, :] = jnp.zeros((SUBLANES, LRU_W), f32)
        car[...] = jnp.zeros(car.shape, f32)

    x = x_ref[...]
    xbuf[SUBLANES:SUBLANES + lt, :] = x
    xc = cb_ref[...] + cw_ref[CONV_W - 1:CONV_W, :] * x
    for j in range(CONV_W - 1):
        off = SUBLANES - (CONV_W - 1) + j
        xc = xc + cw_ref[j:j + 1, :] * xbuf[off:off + lt, :]
    xbuf[0:SUBLANES, :] = xbuf[lt:lt + SUBLANES, :]

    a, hh = _lru_gates(xc, wr_ref, wi_ref, br_ref, bi_ref, lam_ref)
    rig = lax.broadcasted_iota(jnp.int32, (lt, LRU_W), 0) % SUBLANES
    for d in (1, 2, 4):
        m = rig >= d
        hh = jnp.where(m, hh + a * pltpu.roll(hh, d, 0), hh)
        a = jnp.where(m, a * pltpu.roll(a, d, 0), a)
    a_scr[...] = a
    h_scr[...] = hh

    def group(g, c):
        rows = pl.ds(pl.multiple_of(g * SUBLANES, SUBLANES), SUBLANES)
        hg = h_scr[rows, :] + a_scr[rows, :] * c
        h_scr[rows, :] = hg
        return jnp.broadcast_to(hg[SUBLANES - 1:SUBLANES, :], (SUBLANES, LRU_W))

    c = lax.fori_loop(0, lt // SUBLANES, group, car[...])
    car[...] = c
    y_ref[...] = h_scr[...] * _gelu(gr_ref[...])
    conv_ref[...] = x[lt - (CONV_W - 1):lt, :]
    h_ref[...] = c[0:1, :]


def _lru_prompt(z, xr_cb, gr_cb, p, batch, seq, lt):
    tiles = seq // lt
    blk = lambda cb: pl.BlockSpec((lt, LRU_W), lambda b, t: (b * tiles + t, cb))
    consts = [p["conv_w"], p["conv_b"], p["w_r"], p["w_i"], p["b_r"], p["b_i"], p["lam"]]
    return pl.pallas_call(
        _lru_prompt_kernel,
        out_shape=(jax.ShapeDtypeStruct((batch * seq, LRU_W), f32),
                   jax.ShapeDtypeStruct((batch, CONV_W - 1, LRU_W), f32),
                   jax.ShapeDtypeStruct((batch, 1, LRU_W), f32)),
        grid=(batch, tiles),
        in_specs=[blk(xr_cb), blk(gr_cb)] + [_const_spec(c.shape) for c in consts],
        out_specs=(pl.BlockSpec((lt, LRU_W), lambda b, t: (b * tiles + t, 0)),
                   pl.BlockSpec((None, CONV_W - 1, LRU_W), lambda b, t: (b, 0, 0)),
                   pl.BlockSpec((None, 1, LRU_W), lambda b, t: (b, 0, 0))),
        scratch_shapes=[pltpu.VMEM((lt + SUBLANES, LRU_W), f32), pltpu.VMEM((lt, LRU_W), f32),
                        pltpu.VMEM((lt, LRU_W), f32), pltpu.VMEM((SUBLANES, LRU_W), f32)],
        compiler_params=_params("parallel", "arbitrary"),
        name="lru_prompt",
    )(z, z, *consts)


S5_R = 4
S5_TAB = 8


def _s5_param_kernel(lr_ref, li_ref, ldt_ref, btr_ref, bti_ref, cr_ref, ci_ref,
                     a1r_ref, a1i_ref, pcr_ref, pci_ref, xr_ref, xi_ref, wcr_ref, wci_ref):
    lr_, li_ = lr_ref[...], li_ref[...]
    dt = jnp.exp(ldt_ref[...])
    mag = jnp.exp(lr_ * dt)
    ar, ai = mag * jnp.cos(li_ * dt), mag * jnp.sin(li_ * dt)
    den = lr_ * lr_ + li_ * li_
    nr, ni = ar - 1.0, ai
    fr = (nr * lr_ + ni * li_) / den
    fi = (ni * lr_ - nr * li_) / den
    bbr = fr[:, None, :] * btr_ref[...] - fi[:, None, :] * bti_ref[...]
    bbi = fr[:, None, :] * bti_ref[...] + fi[:, None, :] * btr_ref[...]
    a1r_ref[...], a1i_ref[...] = ar, ai
    cr, ci = cr_ref[...], ci_ref[...]
    pr, pi = jnp.ones_like(ar), jnp.zeros_like(ai)
    for n in range(S5_R * S5_TAB + 1):
        if n < S5_R:
            xr_ref[n] = pr[:, None, :] * bbr - pi[:, None, :] * bbi
            xi_ref[n] = pr[:, None, :] * bbi + pi[:, None, :] * bbr
        if 1 <= n <= S5_R:
            wcr_ref[n - 1] = cr * pr[:, None, :] - ci * pi[:, None, :]
            wci_ref[n - 1] = -(cr * pi[:, None, :] + ci * pr[:, None, :])
        if n >= S5_R and n % S5_R == 0:
            pcr_ref[n // S5_R - 1], pci_ref[n // S5_R - 1] = pr, pi
        pr, pi = pr * ar - pi * ai, pr * ai + pi * ar


def _toeplitz_kernel(x_ref, c_ref, m_ref):
    for j in range(x_ref.shape[0]):
        m_ref[j] = _dot(x_ref[j], c_ref[j])


def _s5_params(lam_re, lam_im, log_dt, b_re, b_im, c_re, c_im):
    btr = jnp.swapaxes(b_re, 1, 2)
    bti = jnp.swapaxes(b_im, 1, 2)
    shp = jax.ShapeDtypeStruct
    gp = (S5_G, S5_P)
    ghp = (S5_G, S5_GH, S5_P)
    a1r, a1i, pcr, pci, xr, xi, wcr, wci = pl.pallas_call(
        _s5_param_kernel,
        out_shape=(shp(gp, f32), shp(gp, f32), shp((S5_TAB,) + gp, f32), shp((S5_TAB,) + gp, f32),
                   shp((S5_R,) + ghp, f32), shp((S5_R,) + ghp, f32), shp((S5_R,) + ghp, f32), shp((S5_R,) + ghp, f32)),
        name="s5_params",
    )(lam_re, lam_im, log_dt.reshape(S5_G, 1), btr, bti, c_re, c_im)
    gpb = LANES // S5_GH
    nblk = S5_G // gpb
    eye = jnp.eye(gpb, dtype=f32)

    def bdiag_in(bb):
        x = bb.reshape(bb.shape[:-3] + (nblk, gpb, S5_GH, S5_P))
        return jnp.einsum('...jghp,gk->...jghkp', x, eye).reshape(bb.shape[:-3] + (nblk, gpb * S5_GH, gpb * S5_P))

    def bdiag_out(cc):
        x = cc.reshape(cc.shape[:-3] + (nblk, gpb, S5_GH, S5_P))
        return jnp.einsum('...jghp,gk->...jgpkh', x, eye).reshape(cc.shape[:-3] + (nblk, gpb * S5_P, gpb * S5_GH))

    x_bd = jnp.concatenate([bdiag_in(xr), bdiag_in(xi)], axis=-1).astype(bf16)
    c_bd = jnp.concatenate([bdiag_out(c_re), bdiag_out(-c_im)], axis=-2).astype(bf16)
    wc_bd = jnp.concatenate([bdiag_out(wcr), bdiag_out(wci)], axis=-2).astype(bf16)
    lag = pl.pallas_call(
        _toeplitz_kernel,
        out_shape=shp((S5_R, nblk, LANES, LANES), f32),
        grid=(S5_R,),
        in_specs=[pl.BlockSpec((None, nblk, LANES, 2 * gpb * S5_P), lambda d: (d, 0, 0, 0)),
                  _const_spec(c_bd.shape)],
        out_specs=pl.BlockSpec((None, nblk, LANES, LANES), lambda d: (d, 0, 0, 0)),
        compiler_params=_params("parallel"),
        name="s5_toeplitz",
    )(x_bd, c_bd)
    zero = jnp.zeros((nblk, LANES, LANES), f32)
    toep = jnp.concatenate(
        [jnp.concatenate([lag[r - s] if r >= s else zero for r in range(S5_R)], axis=2) for s in range(S5_R)],
        axis=1).astype(bf16)
    w_x = jnp.concatenate([x_bd[S5_R - 1 - s] for s in range(S5_R)], axis=1)
    w_c = jnp.concatenate([wc_bd[r] for r in range(S5_R)], axis=2)
    return dict(a1r=a1r.reshape(1, S5_N), a1i=a1i.reshape(1, S5_N),
                pcr=pcr.reshape(S5_TAB, S5_N), pci=pci.reshape(S5_TAB, S5_N),
                b_bd=x_bd[0], c_bd=c_bd, w_x=w_x, toep=toep, w_c=w_c)


S5_BLK = S5_N // (S5_G // (LANES // S5_GH))


def _s5_input(u, bbd_ref, j):
    bu = _dot(u[:, j * LANES:(j + 1) * LANES].astype(bf16), bbd_ref[j])
    return bu[:, 0:S5_BLK], bu[:, S5_BLK:2 * S5_BLK]


def _s5_output(u, hr, hi, cbd_ref, d_ref, wg_ref, bg_ref):
    ys = []
    for j in range(S5_N // S5_BLK):
        cs = slice(j * S5_BLK, (j + 1) * S5_BLK)
        hcat = jnp.concatenate([hr[:, cs], hi[:, cs]], axis=1).astype(bf16)
        ys.append(_dot(hcat, cbd_ref[j]))
    y = _gelu(jnp.concatenate(ys, axis=1) + d_ref[...] * u)
    return y * _sigmoid(_dot(y.astype(bf16), wg_ref[...]) + bg_ref[...])


def _s5_prompt_kernel(u_ref, wx_ref, tp_ref, wc_ref, pcr_ref, pci_ref, d_ref, wg_ref, bg_ref,
                      y_ref, sr_ref, si_ref, er_s, ei_s, y_s):
    t_idx = pl.program_id(1)
    cr = u_ref.shape[0]
    nblk = S5_N // S5_BLK
    gw = S5_R * LANES

    @pl.when(t_idx == 0)
    def _():
        er_s[0:SUBLANES, :] = jnp.zeros((SUBLANES, S5_N), f32)
        ei_s[0:SUBLANES, :] = jnp.zeros((SUBLANES, S5_N), f32)

    ub = u_ref[...].astype(bf16)
    sub = lax.broadcasted_iota(jnp.int32, (SUBLANES, S5_BLK), 0)
    for j in range(nblk):
        cs = slice(j * S5_BLK, (j + 1) * S5_BLK)
        uj = jnp.concatenate([ub[:, s * S5_W + j * LANES:s * S5_W + (j + 1) * LANES] for s in range(S5_R)], axis=1)
        x = _dot(uj, wx_ref[j])
        y_s[:, j * gw:(j + 1) * gw] = _dot(uj, tp_ref[j])
        hr = x[:, 0:S5_BLK].reshape(cr // SUBLANES, SUBLANES, S5_BLK)
        hi = x[:, S5_BLK:2 * S5_BLK].reshape(cr // SUBLANES, SUBLANES, S5_BLK)
        for d in (1, 2, 4):
            pr = jnp.where(sub >= d, pcr_ref[d - 1:d, cs], 0.0)
            pi = jnp.where(sub >= d, pci_ref[d - 1:d, cs], 0.0)
            sr, si = pltpu.roll(hr, d, 1), pltpu.roll(hi, d, 1)
            hr, hi = hr + (pr * sr - pi * si), hi + (pr * si + pi * sr)
        er_s[SUBLANES:SUBLANES + cr, cs] = hr.reshape(cr, S5_BLK)
        ei_s[SUBLANES:SUBLANES + cr, cs] = hi.reshape(cr, S5_BLK)

    tr, ti = pcr_ref[...], pci_ref[...]
    last = pl.ds(SUBLANES - 1, 1)
    c0 = (jnp.broadcast_to(er_s[last, :], (SUBLANES, S5_N)), jnp.broadcast_to(ei_s[last, :], (SUBLANES, S5_N)))

    def group(g, c):
        c_r, c_i = c
        rows = pl.ds(pl.multiple_of((g + 1) * SUBLANES, SUBLANES), SUBLANES)
        g_r = er_s[rows, :] + tr * c_r - ti * c_i
        g_i = ei_s[rows, :] + tr * c_i + ti * c_r
        er_s[rows, :] = g_r
        ei_s[rows, :] = g_i
        return (jnp.broadcast_to(g_r[SUBLANES - 1:SUBLANES, :], (SUBLANES, S5_N)),
                jnp.broadcast_to(g_i[SUBLANES - 1:SUBLANES, :], (SUBLANES, S5_N)))

    c_r, c_i = lax.fori_loop(0, cr // SUBLANES, group, c0)

    for j in range(nblk):
        cs = slice(j * S5_BLK, (j + 1) * S5_BLK)
        prev = pl.ds(SUBLANES - 1, cr)
        sj = jnp.concatenate([er_s[prev, cs], ei_s[prev, cs]], axis=1).astype(bf16)
        y_s[:, j * gw:(j + 1) * gw] += _dot(sj, wc_ref[j])
    er_s[last, :] = c_r[0:1, :]
    ei_s[last, :] = c_i[0:1, :]
    sr_ref[...] = c_r[0:1, :]
    si_ref[...] = c_i[0:1, :]

    for r in range(S5_R):
        y = jnp.concatenate([y_s[:, j * gw + r * LANES:j * gw + (r + 1) * LANES] for j in range(nblk)], axis=1)
        y = _gelu(y + d_ref[...] * u_ref[:, r * S5_W:(r + 1) * S5_W])
        y_ref[:, r * S5_W:(r + 1) * S5_W] = y * _sigmoid(_dot(y.astype(bf16), wg_ref[...]) + bg_ref[...])


def _s5_prompt(u, tabs, p, batch, seq, cr):
    rows = seq // S5_R
    tiles = rows // cr
    uf = u.reshape(batch * rows, S5_R * S5_W)
    consts = [tabs["w_x"], tabs["toep"], tabs["w_c"], tabs["pcr"], tabs["pci"], p["d"], p["w_glu"], p["b_glu"]]
    y, s_re, s_im = pl.pallas_call(
        _s5_prompt_kernel,
        out_shape=(jax.ShapeDtypeStruct((batch * rows, S5_R * S5_W), f32),
                   jax.ShapeDtypeStruct((batch, 1, S5_N), f32),
                   jax.ShapeDtypeStruct((batch, 1, S5_N), f32)),
        grid=(batch, tiles),
        in_specs=[pl.BlockSpec((cr, S5_R * S5_W), lambda b, t: (b * tiles + t, 0))]
        + [_const_spec(c.shape) for c in consts],
        out_specs=(pl.BlockSpec((cr, S5_R * S5_W), lambda b, t: (b * tiles + t, 0)),
                   pl.BlockSpec((None, 1, S5_N), lambda b, t: (b, 0, 0)),
                   pl.BlockSpec((None, 1, S5_N), lambda b, t: (b, 0, 0))),
        scratch_shapes=[pltpu.VMEM((cr + SUBLANES, S5_N), f32), pltpu.VMEM((cr + SUBLANES, S5_N), f32),
                        pltpu.VMEM((cr, S5_R * S5_W), f32)],
        compiler_params=_params("parallel", "arbitrary"),
        name="s5_prompt",
    )(uf, *consts)
    return y.reshape(batch * seq, S5_W), s_re, s_im


def _gla_step_kernel(kind, dk, layer, *refs):
    if kind == "gla":
        q_ref, k_ref, v_ref, gate_ref, lr_ref, wlr_ref, blr_ref, ng_ref, s_ref, y_ref, so_ref, o_scr = refs
        sl = pl.program_id(0)
        q = q_ref[...] * (GLA_DK ** -0.5)
        k = k_ref[...]
        pre_all = _dot(lr_ref[...].astype(bf16), wlr_ref[...]) + blr_ref[...]
        pre = jnp.where(sl == 0, pre_all[:, 0:LANES], pre_all[:, LANES:2 * LANES])
        a = jnp.exp(_log_sigmoid(pre) * (1.0 / GLA_TAU))
    else:
        q_ref, f_ref, v_ref, gate_ref, lg_ref, ng_ref, s_ref, y_ref, so_ref, o_scr = refs
        sl = pl.program_id(0)
        lb_all = _hgrn_lower_bound(lg_ref, layer)
        lb = lb_all[:, 0:LANES]
        for j in range(1, HG_H):
            lb = jnp.where(sl == j, lb_all[:, j * LANES:(j + 1) * LANES], lb)
        a = lb + (1.0 - lb) * _sigmoid(f_ref[...])
        q = _silu(q_ref[...])
        k = 1.0 - a
    hps = LANES // dk
    nb = q.shape[0]
    qat, kt, at = (q * a).T, k.T, a.T
    lane = lax.broadcasted_iota(jnp.int32, (1, LANES), 1)
    v = v_ref[...]
    for hh in range(hps):
        r0 = hh * dk
        msk = (lane >= r0) & (lane < r0 + dk)
        qk = jnp.sum(jnp.where(msk, q * k, 0.0), axis=-1, keepdims=True)
        for b in range(nb):
            s = s_ref[b, hh]
            col = lambda x: jnp.broadcast_to(x[r0:r0 + dk, b:b + 1], (dk, LANES))
            vr = v[b:b + 1, hh * LANES:(hh + 1) * LANES]
            so_ref[b, hh] = col(at) * s + col(kt) * vr
            oi = jnp.sum(col(qat) * s, axis=0, keepdims=True)
            o_scr[b:b + 1, hh * LANES:(hh + 1) * LANES] = jnp.broadcast_to(qk[b:b + 1, :], (1, LANES)) * vr + oi
    for hh in range(hps):
        cs = slice(hh * LANES, (hh + 1) * LANES)
        o = o_scr[:, cs]
        o = o * lax.rsqrt(jnp.mean(o * o, axis=-1, keepdims=True) + RMS_EPS) * ng_ref[...]
        y_ref[:, cs] = o * _silu(gate_ref[:, cs])


def _gla_step(kind, z, extra, norm_g, state, layer):
    nb = z.shape[0]
    dk = GLA_DK if kind == "gla" else HG_D
    n_heads = GLA_H if kind == "gla" else HG_H
    hps = LANES // dk
    n_slabs = n_heads // hps
    vw = hps * LANES
    zb = lambda width, cb0: pl.BlockSpec((nb, width), lambda s: (0, cb0 + s))
    if kind == "gla":
        wlr, blr = extra
        ins = [z, z, z, z, z, wlr, blr]
        specs = [zb(LANES, 0), zb(LANES, GLA_KW // LANES), zb(vw, GLA_VW // vw * 1), zb(vw, GLA_VW // vw * 2),
                 pl.BlockSpec((nb, LANES), lambda s: (0, (EVEN_N - LANES) // LANES)),
                 _const_spec(wlr.shape), _const_spec(blr.shape)]
    else:
        (lg,) = extra
        ins = [z, z, z, z, lg]
        specs = [zb(LANES, 0), zb(LANES, HG_W // LANES), zb(vw, HG_W // vw * 2),
                 zb(vw, HG_W // vw * 3), _const_spec(lg.shape)]
    ins += [norm_g, state]
    specs += [_const_spec(norm_g.shape), pl.BlockSpec((nb, hps, dk, LANES), lambda s: (0, s, 0, 0))]
    return pl.pallas_call(
        functools.partial(_gla_step_kernel, kind, dk, layer),
        out_shape=(jax.ShapeDtypeStruct((nb, HALF), f32), jax.ShapeDtypeStruct(state.shape, f32)),
        grid=(n_slabs,),
        in_specs=specs,
        out_specs=(pl.BlockSpec((nb, vw), lambda s: (0, s)),
                   pl.BlockSpec((nb, hps, dk, LANES), lambda s: (0, s, 0, 0))),
        scratch_shapes=[pltpu.VMEM((nb, vw), f32)],
        compiler_params=_params("parallel"),
        name=kind + "_step",
    )(*ins)


def _lru_step_kernel(x_ref, gr_ref, c0_ref, c1_ref, c2_ref, h0_ref, cw_ref, cb_ref, wr_ref, wi_ref,
                     br_ref, bi_ref, lam_ref, y_ref, h_ref):
    xc = (cb_ref[...] + cw_ref[0:1, :] * c0_ref[...] + cw_ref[1:2, :] * c1_ref[...]
          + cw_ref[2:3, :] * c2_ref[...] + cw_ref[3:4, :] * x_ref[...])
    a, u = _lru_gates(xc, wr_ref, wi_ref, br_ref, bi_ref, lam_ref)
    h = a * h0_ref[...] + u
    h_ref[...] = h
    y_ref[...] = h * _gelu(gr_ref[...])


def _lru_step(z, xr_cb, gr_cb, conv_rows, h0, p):
    nb = z.shape[0]
    full = lambda: pl.BlockSpec((nb, LRU_W), lambda i: (0, 0))
    consts = [p["conv_w"], p["conv_b"], p["w_r"], p["w_i"], p["b_r"], p["b_i"], p["lam"]]
    return pl.pallas_call(
        _lru_step_kernel,
        out_shape=(jax.ShapeDtypeStruct((nb, LRU_W), f32), jax.ShapeDtypeStruct((nb, LRU_W), f32)),
        grid=(1,),
        in_specs=[pl.BlockSpec((nb, LRU_W), lambda i: (0, xr_cb)), pl.BlockSpec((nb, LRU_W), lambda i: (0, gr_cb)),
                  full(), full(), full(), full()] + [_const_spec(c.shape) for c in consts],
        out_specs=(full(), full()),
        compiler_params=_params("arbitrary"),
        name="lru_step",
    )(z, z, *conv_rows, h0, *consts)


def _s5_step_kernel(u_ref, s0r_ref, s0i_ref, a1r_ref, a1i_ref, bbd_ref, cbd_ref, d_ref, wg_ref, bg_ref,
                    y_ref, sr_ref, si_ref):
    u = u_ref[...]
    for j in range(S5_N // S5_BLK):
        cs = slice(j * S5_BLK, (j + 1) * S5_BLK)
        bur, bui = _s5_input(u, bbd_ref, j)
        ar, ai = a1r_ref[:, cs], a1i_ref[:, cs]
        s0r, s0i = s0r_ref[:, cs], s0i_ref[:, cs]
        sr_ref[:, cs] = bur + ar * s0r - ai * s0i
        si_ref[:, cs] = bui + ar * s0i + ai * s0r
    y_ref[...] = _s5_output(u, sr_ref[...], si_ref[...], cbd_ref, d_ref, wg_ref, bg_ref)


def _s5_step(u, s0r, s0i, tabs, p):
    nb = u.shape[0]
    consts = [tabs["a1r"], tabs["a1i"], tabs["b_bd"], tabs["c_bd"], p["d"], p["w_glu"], p["b_glu"]]
    st = lambda: pl.BlockSpec((nb, S5_N), lambda i: (0, 0))
    return pl.pallas_call(
        _s5_step_kernel,
        out_shape=(jax.ShapeDtypeStruct((nb, S5_W), f32), jax.ShapeDtypeStruct((nb, S5_N), f32),
                   jax.ShapeDtypeStruct((nb, S5_N), f32)),
        grid=(1,),
        in_specs=[pl.BlockSpec((nb, S5_W), lambda i: (0, 0)), st(), st()] + [_const_spec(c.shape) for c in consts],
        out_specs=(pl.BlockSpec((nb, S5_W), lambda i: (0, 0)), st(), st()),
        compiler_params=_params("arbitrary"),
        name="s5_step",
    )(u, s0r, s0i, *consts)


def _block_diag(w):
    h, n, _ = w.shape
    return jnp.einsum('hij,hk->hikj', w, jnp.eye(h, dtype=w.dtype)).reshape(h * n, h * n)


def kernel(x_prompt, x_sample, c_prompt, c_sample, state_gla, state_rglru_conv, state_rglru_h, state_s5_re, state_s5_im, state_hgrn, ev_w_in, ev_gla_w_lr, ev_gla_b_lr, ev_gla_norm_g, ev_conv_w, ev_conv_b, ev_lru_w_r, ev_lru_b_r, ev_lru_w_i, ev_lru_b_i, ev_lru_lam, ev_w_out, od_w_in, od_s5_lam_re, od_s5_lam_im, od_s5_log_dt, od_s5_b_re, od_s5_b_im, od_s5_c_re, od_s5_c_im, od_s5_d, od_s5_w_glu, od_s5_b_glu, hg_lb_logits, od_hg_norm_g, od_w_out, w_ada, b_ada, ln_g, ln_b, ffn_w_in, ffn_w_out):
    bp, seq = x_prompt.shape[0], x_prompt.shape[1]
    bs = x_sample.shape[0]
    assert x_sample.shape[1] == 1 and seq % CHUNK == 0
    tp = bp * seq
    tm = min(512, seq)
    lt = min(512, seq)
    lt_gla = min(1024, seq)
    cr_s5 = min(256, seq // S5_R)
    row2 = lambda a: a.reshape(1, -1)

    mod = _modulation(jnp.concatenate([c_prompt, c_sample], axis=0).astype(f32), w_ada, b_ada)
    mod_p = mod[:, :bp].reshape(DEPTH, bp, 1, 6 * D_MODEL)
    mod_s = mod[:, bp:].reshape(DEPTH, 1, bs, 6 * D_MODEL)

    xp = x_prompt.astype(f32).reshape(tp, D_MODEL)
    xs = x_sample.astype(f32).reshape(bs, D_MODEL)
    outs_p = {k: [] for k in ("gla", "conv", "lru", "re", "im", "hg")}
    outs_s = {k: [] for k in ("gla", "conv", "lru", "re", "im", "hg")}

    for l in range(DEPTH):
        w_f_in = ffn_w_in[l].astype(bf16)
        w_f_out = ffn_w_out[l].astype(bf16)
        if l % 2 == 0:
            e = l // 2
            w = ev_w_in[e]
            lr0 = 2 * GLA_KW + 2 * GLA_VW
            w_in = jnp.concatenate(
                [w[:, :lr0], w[:, lr0 + GLA_LR:], w[:, lr0:lr0 + GLA_LR],
                 jnp.zeros((D_MODEL, EVEN_N - w.shape[1]), w.dtype)], axis=1).astype(bf16)
            w_out = ev_w_out[e].astype(bf16)
            wlr = jnp.concatenate([ev_gla_w_lr[e], jnp.zeros((LANES - GLA_LR, GLA_KW), f32)], axis=0).astype(bf16)
            gla_extra = (wlr, row2(ev_gla_b_lr[e]))
            ng = row2(ev_gla_norm_g[e])
            lru_p = dict(conv_w=ev_conv_w[e], conv_b=row2(ev_conv_b[e]),
                         w_r=_block_diag(ev_lru_w_r[e]).astype(bf16), w_i=_block_diag(ev_lru_w_i[e]).astype(bf16),
                         b_r=row2(ev_lru_b_r[e]), b_i=row2(ev_lru_b_i[e]), lam=row2(ev_lru_lam[e]))
            xr_cb, gr_cb = lr0 // LRU_W, lr0 // LRU_W + 1
            (z,) = _inproj(xp, mod_p[l], w_in, tm, seq // tm, (EVEN_N,))
            ya, s_gla = _gla_prompt("gla", z, (EVEN_N - LANES) // LANES, gla_extra, ng, bp, seq, lt_gla, l)
            yb, s_conv, s_h = _lru_prompt(z, xr_cb, gr_cb, lru_p, bp, seq, lt)
            xp = _outffn(xp, ya, yb, mod_p[l], w_out, ln_g[l], ln_b[l], w_f_in, w_f_out, tm, seq // tm)
            outs_p["gla"].append(s_gla)
            outs_p["conv"].append(s_conv)
            outs_p["lru"].append(s_h.reshape(bp, LRU_W))
            (z,) = _inproj(xs, mod_s[l], w_in, bs, 1, (EVEN_N,))
            ya, s_gla = _gla_step("gla", z, gla_extra, ng, state_gla[e], l)
            cs = state_rglru_conv[e].astype(f32)
            yb, s_h = _lru_step(z, xr_cb, gr_cb, [cs[:, j] for j in range(CONV_W - 1)],
                                state_rglru_h[e].astype(f32), lru_p)
            xs = _outffn(xs, ya, yb, mod_s[l], w_out, ln_g[l], ln_b[l], w_f_in, w_f_out, bs, 1)
            outs_s["gla"].append(s_gla)
            outs_s["conv"].append(jnp.stack([cs[:, 1], cs[:, 2], z[:, xr_cb * LRU_W:(xr_cb + 1) * LRU_W]], axis=1))
            outs_s["lru"].append(s_h)
        else:
            o = l // 2
            w_in = od_w_in[o].astype(bf16)
            w_out = od_w_out[o].astype(bf16)
            tabs = _s5_params(od_s5_lam_re[o], od_s5_lam_im[o], od_s5_log_dt[o], od_s5_b_re[o], od_s5_b_im[o],
                              od_s5_c_re[o], od_s5_c_im[o])
            s5_p = dict(d=row2(od_s5_d[o]), w_glu=od_s5_w_glu[o].astype(bf16), b_glu=row2(od_s5_b_glu[o]))
            hg_extra = (hg_lb_logits.astype(f32),)
            ng = row2(od_hg_norm_g[o])
            u, z = _inproj(xp, mod_p[l], w_in, tm, seq // tm, (S5_W, ODD_N - S5_W))
            ya, s_re, s_im = _s5_prompt(u, tabs, s5_p, bp, seq, cr_s5)
            yb, s_hg = _gla_prompt("hgrn", z, 0, hg_extra, ng, bp, seq, lt_gla, l)
            xp = _outffn(xp, ya, yb, mod_p[l], w_out, ln_g[l], ln_b[l], w_f_in, w_f_out, tm, seq // tm)
            outs_p["re"].append(s_re.reshape(bp, S5_G, S5_P))
            outs_p["im"].append(s_im.reshape(bp, S5_G, S5_P))
            outs_p["hg"].append(s_hg)
            u, z = _inproj(xs, mod_s[l], w_in, bs, 1, (S5_W, ODD_N - S5_W))
            ya, s_re, s_im = _s5_step(u, state_s5_re[o].astype(f32).reshape(bs, S5_N),
                                      state_s5_im[o].astype(f32).reshape(bs, S5_N), tabs, s5_p)
            yb, s_hg = _gla_step("hgrn", z, hg_extra, ng, state_hgrn[o], l)
            xs = _outffn(xs, ya, yb, mod_s[l], w_out, ln_g[l], ln_b[l], w_f_in, w_f_out, bs, 1)
            outs_s["re"].append(s_re.reshape(bs, S5_G, S5_P))
            outs_s["im"].append(s_im.reshape(bs, S5_G, S5_P))
            outs_s["hg"].append(s_hg)

    st = lambda d, k: jnp.stack(d[k])
    return (xp.reshape(bp, seq, D_MODEL).astype(x_prompt.dtype), xs.reshape(bs, 1, D_MODEL).astype(x_sample.dtype),
            st(outs_p, "gla"), st(outs_s, "gla"), st(outs_p, "conv"), st(outs_s, "conv"),
            st(outs_p, "lru"), st(outs_s, "lru"), st(outs_p, "re"), st(outs_s, "re"),
            st(outs_p, "im"), st(outs_s, "im"), st(outs_p, "hg"), st(outs_s, "hg"))
```

```python
import functools
import math

import numpy as np
import jax
import jax.numpy as jnp
from jax import lax
from jax.experimental import pallas as pl
from jax.experimental.pallas import tpu as pltpu

f32 = jnp.float32
bf16 = jnp.bfloat16

D_MODEL = 1024
DEPTH = 2
HALF = D_MODEL // 2
GLA_H = 4
GLA_DV = HALF // GLA_H
GLA_DK = GLA_DV // 2
GLA_KW = GLA_H * GLA_DK
GLA_VW = GLA_H * GLA_DV
GLA_LR = 16
GLA_TAU = 16.0
LRU_W = HALF
LRU_H = 8
LRU_BW = LRU_W // LRU_H
CONV_W = 4
RG_C = 8.0
S5_W = HALF
S5_GH = 16
S5_G = S5_W // S5_GH
S5_P = 64
S5_N = S5_G * S5_P
HG_H = 4
HG_D = HALF // HG_H
HG_W = HG_H * HG_D
D_FF = ((8 * D_MODEL // 3 + 255) // 256) * 256
ALPHA = (2.0 * DEPTH) ** 0.25
LN_EPS = 1e-5
RMS_EPS = 1e-6

LANES = 128
SUBLANES = 8
VMEM_LIMIT = 56 * 1024 * 1024

CHUNK = 64
N_LEVELS = 6
FF_CHUNK = 256
EVEN_N = 2688
ODD_N = 2560


def _sigmoid(x):
    return 0.5 * jnp.tanh(0.5 * x) + 0.5


def _silu(x):
    return x * _sigmoid(x)


def _gelu(x):
    c = math.sqrt(2.0 / math.pi)
    return x * (0.5 * (1.0 + jnp.tanh(c * (x + 0.044715 * (x * x * x)))))


def _softplus(x):
    return jnp.maximum(x, 0.0) + jnp.log1p(jnp.exp(-jnp.abs(x)))


def _log_sigmoid(x):
    return -_softplus(-x)


def _expm1(x):
    u = jnp.exp(x)
    um1 = u - 1.0
    lg = jnp.log(u)
    safe = jnp.where(lg == 0.0, 1.0, lg)
    return jnp.where(um1 == 0.0, x, jnp.where(um1 == -1.0, -1.0, um1 * x / safe))


def _layer_norm(x, g, b):
    mu = jnp.mean(x, axis=-1, keepdims=True)
    xc = x - mu
    var = jnp.mean(xc * xc, axis=-1, keepdims=True)
    return xc * lax.rsqrt(var + LN_EPS) * g + b


def _dot(a, b):
    return jnp.dot(a, b, preferred_element_type=f32)


def _dot_nt(a, b):
    return lax.dot_general(a, b, (((1,), (1,)), ((), ())), preferred_element_type=f32)


def _dot_tn(a, b):
    return lax.dot_general(a, b, (((0,), (0,)), ((), ())), preferred_element_type=f32)


def _const_spec(shape):
    nd = len(shape)
    return pl.BlockSpec(shape, lambda *_: (0,) * nd, pipeline_mode=pl.Buffered(1))


def _params(*sem):
    return pltpu.CompilerParams(dimension_semantics=sem, vmem_limit_bytes=VMEM_LIMIT)


def _mod_kernel(c_ref, w_ref, b_ref, o_ref):
    cond = _silu(c_ref[...]).astype(bf16)
    o_ref[...] = _dot(cond, w_ref[...].astype(bf16)) + b_ref[...]


def _modulation(c_all, w_ada, b_ada):
    rows = c_all.shape[0]
    tn = 1536
    return pl.pallas_call(
        _mod_kernel,
        out_shape=jax.ShapeDtypeStruct((DEPTH, rows, 6 * D_MODEL), f32),
        grid=(DEPTH, 6 * D_MODEL // tn),
        in_specs=[pl.BlockSpec((rows, D_MODEL), lambda l, j: (0, 0)),
                  pl.BlockSpec((None, D_MODEL, tn), lambda l, j: (l, 0, j)),
                  pl.BlockSpec((None, 1, tn), lambda l, j: (l, 0, j))],
        out_specs=pl.BlockSpec((None, rows, tn), lambda l, j: (l, 0, j)),
        compiler_params=_params("parallel", "parallel"),
        name="adaln_mod",
    )(c_all, w_ada, b_ada.reshape(DEPTH, 1, 6 * D_MODEL))


def _mod_spec(rows_per_block, tiles_per_batch, j):
    return pl.BlockSpec((None, rows_per_block, D_MODEL), lambda i: (i // tiles_per_batch, 0, j))


def _inproj_kernel(x_ref, sc_ref, sh_ref, w_ref, *z_refs):
    h = (x_ref[...] * (1.0 + sc_ref[...]) + sh_ref[...]).astype(bf16)
    lo = 0
    for z_ref in z_refs:
        z_ref[...] = _dot(h, w_ref[:, lo:lo + z_ref.shape[1]])
        lo += z_ref.shape[1]


def _inproj(x, mod, w, tm, tiles_per_batch, widths):
    t, n = x.shape[0], w.shape[1]
    assert sum(widths) == n
    r = mod.shape[1]
    return pl.pallas_call(
        _inproj_kernel,
        out_shape=tuple(jax.ShapeDtypeStruct((t, wd), f32) for wd in widths),
        grid=(t // tm,),
        in_specs=[pl.BlockSpec((tm, D_MODEL), lambda i: (i, 0)),
                  _mod_spec(r, tiles_per_batch, 1), _mod_spec(r, tiles_per_batch, 0),
                  _const_spec((D_MODEL, n))],
        out_specs=tuple(pl.BlockSpec((tm, wd), lambda i: (i, 0)) for wd in widths),
        compiler_params=_params("parallel"),
        name="in_proj",
    )(x, mod, mod, w)


def _outffn_kernel(n_y, x_ref, *refs):
    y_refs = refs[:n_y]
    gtm_ref, shf_ref, scf_ref, gtf_ref, wo_ref, lng_ref, lnb_ref, wi_ref, wf_ref, o_ref = refs[n_y:]
    y = jnp.concatenate([r[...].astype(bf16) for r in y_refs], axis=1)
    mix = _dot(y, wo_ref[...])
    x1 = _layer_norm(ALPHA * x_ref[...] + (1.0 + gtm_ref[...]) * mix, lng_ref[0:1, :], lnb_ref[0:1, :])
    h = (x1 * (1.0 + scf_ref[...]) + shf_ref[...]).astype(bf16)
    acc = jnp.zeros(x1.shape, f32)
    for c in range(D_FF // FF_CHUNK):
        lo = c * FF_CHUNK
        gate = _dot(h, wi_ref[:, lo:lo + FF_CHUNK])
        up = _dot(h, wi_ref[:, D_FF + lo:D_FF + lo + FF_CHUNK])
        act = (_silu(gate) * up).astype(bf16)
        acc = acc + _dot(act, wf_ref[lo:lo + FF_CHUNK, :])
    o_ref[...] = _layer_norm(ALPHA * x1 + (1.0 + gtf_ref[...]) * acc, lng_ref[1:2, :], lnb_ref[1:2, :])


def _outffn(x, ys, mod, w_out, ln_g, ln_b, w_in, w_f, tm, tiles_per_batch):
    t = x.shape[0]
    r = mod.shape[1]
    assert sum(y.shape[1] for y in ys) == D_MODEL
    row = lambda w: pl.BlockSpec((tm, w), lambda i: (i, 0))
    return pl.pallas_call(
        functools.partial(_outffn_kernel, len(ys)),
        out_shape=jax.ShapeDtypeStruct((t, D_MODEL), f32),
        grid=(t // tm,),
        in_specs=[row(D_MODEL)] + [row(y.shape[1]) for y in ys] + [
                  _mod_spec(r, tiles_per_batch, 2), _mod_spec(r, tiles_per_batch, 3),
                  _mod_spec(r, tiles_per_batch, 4), _mod_spec(r, tiles_per_batch, 5),
                  _const_spec((D_MODEL, D_MODEL)), _const_spec((2, D_MODEL)), _const_spec((2, D_MODEL)),
                  _const_spec((D_MODEL, 2 * D_FF)), _const_spec((D_FF, D_MODEL))],
        out_specs=row(D_MODEL),
        compiler_params=_params("parallel"),
        name="outproj_ffn",
    )(x, *ys, mod, mod, mod, mod, w_out, ln_g, ln_b, w_in, w_f)


def _level_tables():
    c = CHUNK
    t = np.arange(c)[:, None]
    u = np.arange(c)[None, :]
    pmask, sgn = [], []
    for lvl in range(N_LEVELS):
        m = 1 << lvl
        right = (t % (2 * m)) >= m
        pm = right & ((u % (2 * m)) < m) & ((t // (2 * m)) == (u // (2 * m)))
        pmask.append(pm.astype(np.float32))
        sgn.append(np.broadcast_to(np.where(right, 1.0, -1.0).astype(np.float32), (c, LANES)))
    tri = (u <= t).astype(np.float32)
    tri3 = np.concatenate([tri, tri, tri], axis=1)
    return (jnp.asarray(tri3, dtype=bf16), jnp.asarray(np.stack(pmask), dtype=f32),
            jnp.asarray(np.stack(sgn), dtype=f32))


LOG2E = 1.0 / math.log(2.0)


def _anchor(b, lvl):
    m = 1 << lvl
    if lvl == 1:
        b3 = b.reshape(CHUNK // SUBLANES, SUBLANES, LANES)
        sub = lax.broadcasted_iota(jnp.int32, b3.shape, 1)
        a = jnp.where(sub < 4, jnp.broadcast_to(b3[:, 1:2, :], b3.shape), jnp.broadcast_to(b3[:, 5:6, :], b3.shape))
        return a.reshape(CHUNK, LANES)
    b3 = b.reshape(CHUNK // (2 * m), 2 * m, LANES)
    return jnp.broadcast_to(b3[:, m - 1:m, :], b3.shape).reshape(CHUNK, LANES)


def _split3(g):
    g1 = g.astype(bf16)
    r1 = g - g1.astype(f32)
    g2 = r1.astype(bf16)
    r2 = r1 - g2.astype(f32)
    return jnp.concatenate([g1, g2, r2.astype(bf16)], axis=0)


def _hgrn_lower_bound(lg_ref, layer):
    rows = [lg_ref[i:i + 1, :] for i in range(DEPTH)]
    mx = functools.reduce(jnp.maximum, rows)
    ex = [jnp.exp(r - mx) for r in rows]
    den = functools.reduce(lambda a, b: a + b, ex)
    sm = [e / den for e in ex]
    cs = sm[0]
    for i in range(1, layer + 1):
        cs = cs + sm[i]
    return cs - sm[0]


def _gla_prologue(kind, refs, rows):
    if kind == "gla":
        qk_ref, lr_ref, wlr_ref, blr_ref = refs
        q = qk_ref[rows, 0:GLA_KW] * (GLA_DK ** -0.5)
        k = qk_ref[rows, GLA_KW:2 * GLA_KW]
        pre = _dot(lr_ref[rows, :].astype(bf16), wlr_ref[...]) + blr_ref[...]
        g = _log_sigmoid(pre) * (LOG2E / GLA_TAU)
        return q, k, g
    q_ref, f_ref, lb = refs
    forget = lb + (1.0 - lb) * _sigmoid(f_ref[rows, :])
    return _silu(q_ref[rows, :]), 1.0 - forget, jnp.log2(forget)


def _gla_chunk_kernel(kind, dk, n_heads, layer, *refs):
    if kind == "gla":
        (qk_ref, v_ref, gate_ref, lr_ref, wlr_ref, blr_ref, ng_ref, tri_ref, pm_ref, sg_ref,
         y_ref, s_ref, st_ref) = refs
        pro = (qk_ref, lr_ref, wlr_ref, blr_ref)
    else:
        (q_ref, f_ref, v_ref, gate_ref, lg_ref, ng_ref, tri_ref, pm_ref, sg_ref,
         y_ref, s_ref, st_ref) = refs
        pro = (q_ref, f_ref, _hgrn_lower_bound(lg_ref, layer))
    t_idx = pl.program_id(1)
    n_tiles = pl.num_programs(1)
    lt = y_ref.shape[0]
    hps = LANES // dk
    n_slabs = n_heads // hps
    lane = lax.broadcasted_iota(jnp.int32, (1, LANES), 1)
    eye = (lax.broadcasted_iota(jnp.int32, (CHUNK, CHUNK), 0)
           == lax.broadcasted_iota(jnp.int32, (CHUNK, CHUNK), 1))

    @pl.when(t_idx == 0)
    def _():
        st_ref[...] = jnp.zeros(st_ref.shape, f32)

    def head_mask(hh):
        if hps == 1:
            return lambda a: a
        msk = (lane >= hh * dk) & (lane < (hh + 1) * dk)
        return lambda a: jnp.where(msk, a, jnp.zeros_like(a))

    def rows_of(c):
        return pl.ds(pl.multiple_of(c * CHUNK, CHUNK), CHUNK)

    def cumsums(c):
        q_all, k_all, g_all = _gla_prologue(kind, pro, rows_of(c))
        out = []
        for sl in range(n_slabs):
            cs = slice(sl * LANES, (sl + 1) * LANES)
            q_s, k_s, g_s = q_all[:, cs], k_all[:, cs], g_all[:, cs]
            out.append((q_s, k_s, g_s, _dot(tri_ref[...], _split3(g_s))))
        return out

    def score_dots(pre):
        raw, qbs, ksts, decs, rds = [], [], [], [], []
        for sl in range(n_slabs):
            q_s, k_s, g_s, b = pre[sl]
            ys = [(jnp.where(sg_ref[0] > 0.0, jnp.exp2(g_s) * q_s, k_s)).astype(bf16)]
            for lvl in range(1, N_LEVELS):
                sg = sg_ref[lvl]
                x = jnp.exp2((b - _anchor(b, lvl)) * sg)
                ys.append((x * jnp.where(sg > 0.0, q_s, k_s)).astype(bf16))
            blast = jnp.broadcast_to(b[CHUNK - 1:CHUNK, :], b.shape)
            qb = (q_s * jnp.exp2(b)).astype(bf16)
            kst = (k_s * jnp.exp2(blast - b)).astype(bf16)
            decs.append(jnp.exp2(b[CHUNK - 1:CHUNK, :]))
            qk_prod = q_s * k_s
            for hh in range(hps):
                mul = head_mask(hh)
                rds.append(jnp.sum(mul(qk_prod), axis=-1, keepdims=True))
                raw.append([_dot_nt(mul(ys[lvl]), ys[lvl]) for lvl in range(N_LEVELS)])
                qbs.append(mul(qb))
                ksts.append(mul(kst))
        return raw, rds, tuple(qbs), tuple(ksts), tuple(decs)

    def combine(raw, rds):
        ps = []
        for h in range(n_heads):
            p = jnp.where(eye, rds[h], 0.0)
            for lvl in range(N_LEVELS):
                p = p + pm_ref[lvl] * raw[h][lvl]
            ps.append(p.astype(bf16))
        return tuple(ps)

    def apply_dots(c, sc):
        ps, qbs, ksts, decs = sc
        outs = []
        for h in range(n_heads):
            vb = v_ref[rows_of(c), h * LANES:(h + 1) * LANES].astype(bf16)
            st = st_ref[h]
            outs.append(_dot(ps[h], vb) + _dot_nt(qbs[h], st.astype(bf16)))
            st_ref[h] = st * decs[h // hps] + _dot_tn(vb, ksts[h])
        return outs

    def finish(c, outs):
        for h in range(n_heads):
            cs = slice(h * LANES, (h + 1) * LANES)
            o = outs[h]
            o = o * lax.rsqrt(jnp.mean(o * o, axis=-1, keepdims=True) + RMS_EPS) * ng_ref[...]
            y_ref[rows_of(c), cs] = o * _silu(gate_ref[rows_of(c), cs])

    def first():
        raw, rds, qbs, ksts, decs = score_dots(cumsums(0))
        return combine(raw, rds), qbs, ksts, decs

    def body(c, sc):
        pre = cumsums(c + 1)
        outs = apply_dots(c, sc)
        raw, rds, qbs, ksts, decs = score_dots(pre)
        finish(c, outs)
        return combine(raw, rds), qbs, ksts, decs

    n_chunks = lt // CHUNK
    last = lax.fori_loop(0, n_chunks - 1, body, first())
    finish(n_chunks - 1, apply_dots(n_chunks - 1, last))

    @pl.when(t_idx == n_tiles - 1)
    def _():
        for h in range(n_heads):
            off = (h % hps) * dk
            s_ref[h] = st_ref[h].T[off:off + dk, :]


def _gla_prompt(kind, z, col0, extra, norm_g, batch, seq, lt, layer):
    dk = GLA_DK if kind == "gla" else HG_D
    n_heads = GLA_H if kind == "gla" else HG_H
    tiles = seq // lt
    mall, pmask, rmask = _level_tables()
    blk = lambda width, cb: pl.BlockSpec((lt, width), lambda b, t: (b * tiles + t, cb))
    if kind == "gla":
        wlr, blr = extra
        ins = [z, z, z, z, wlr, blr]
        specs = [blk(2 * GLA_KW, 0), blk(GLA_VW, 1), blk(GLA_VW, 2), blk(LANES, col0),
                 _const_spec(wlr.shape), _const_spec(blr.shape)]
    else:
        (lg,) = extra
        ins = [z, z, z, z, lg]
        specs = [blk(HG_W, col0), blk(HG_W, col0 + 1), blk(HG_W, col0 + 2), blk(HG_W, col0 + 3), _const_spec(lg.shape)]
    ins += [norm_g, mall, pmask, rmask]
    specs += [_const_spec(norm_g.shape), _const_spec(mall.shape), _const_spec(pmask.shape), _const_spec(rmask.shape)]
    return pl.pallas_call(
        functools.partial(_gla_chunk_kernel, kind, dk, n_heads, layer),
        out_shape=(jax.ShapeDtypeStruct((batch * seq, HALF), f32),
                   jax.ShapeDtypeStruct((batch, n_heads, dk, LANES), f32)),
        grid=(batch, tiles),
        in_specs=specs,
        out_specs=(pl.BlockSpec((lt, HALF), lambda b, t: (b * tiles + t, 0)),
                   pl.BlockSpec((None, n_heads, dk, LANES), lambda b, t: (b, 0, 0, 0))),
        scratch_shapes=[pltpu.VMEM((n_heads, LANES, LANES), f32)],
        compiler_params=_params("parallel", "arbitrary"),
        name=kind + "_prompt",
    )(*ins)


def _lru_gates(xc, wr_ref, wi_ref, br_ref, bi_ref, lam_ref):
    xb = xc.astype(bf16)
    r = _sigmoid(_dot(xb, wr_ref[...]) + br_ref[...])
    i = _sigmoid(_dot(xb, wi_ref[...]) + bi_ref[...])
    log_a = (-RG_C) * r * _softplus(-lam_ref[...])
    return jnp.exp(log_a), jnp.sqrt(-_expm1(2.0 * log_a)) * (i * xc)


def _lru_prompt_kernel(x_ref, gr_ref, cw_ref, cb_ref, wr_ref, wi_ref, br_ref, bi_ref, lam_ref,
                       y_ref, conv_ref, h_ref, xbuf, a_scr, h_scr, car):
    t_idx = pl.program_id(1)
    lt = x_ref.shape[0]

    @pl.when(t_idx == 0)
    def _():
        xbuf[0:SUBLANES, :] = jnp.zeros((SUBLANES, LRU_W), f32)
        car[...] = jnp.zeros(car.shape, f32)

    x = x_ref[...]
    xbuf[SUBLANES:SUBLANES + lt, :] = x
    xc = cb_ref[...] + cw_ref[CONV_W - 1:CONV_W, :] * x
    for j in range(CONV_W - 1):
        off = SUBLANES - (CONV_W - 1) + j
        xc = xc + cw_ref[j:j + 1, :] * xbuf[off:off + lt, :]
    xbuf[0:SUBLANES, :] = xbuf[lt:lt + SUBLANES, :]

    a, hh = _lru_gates(xc, wr_ref, wi_ref, br_ref, bi_ref, lam_ref)
    grp = (lt // SUBLANES, SUBLANES, LRU_W)
    a, hh = a.reshape(grp), hh.reshape(grp)
    sub = lax.broadcasted_iota(jnp.int32, (SUBLANES, LRU_W), 0)
    for d in (1, 2, 4):
        m = sub >= d
        hh = hh + jnp.where(m, a, 0.0) * pltpu.roll(hh, d, 1)
        a = a * jnp.where(m, pltpu.roll(a, d, 1), 1.0)
    a_scr[...] = a.reshape(lt, LRU_W)
    h_scr[...] = hh.reshape(lt, LRU_W)

    def group(g, c):
        rows = pl.ds(pl.multiple_of(g * SUBLANES, SUBLANES), SUBLANES)
        hg = h_scr[rows, :] + a_scr[rows, :] * c
        h_scr[rows, :] = hg
        return jnp.broadcast_to(hg[SUBLANES - 1:SUBLANES, :], (SUBLANES, LRU_W))

    c = lax.fori_loop(0, lt // SUBLANES, group, car[...])
    car[...] = c
    y_ref[...] = h_scr[...] * _gelu(gr_ref[...])
    conv_ref[...] = x[lt - (CONV_W - 1):lt, :]
    h_ref[...] = c[0:1, :]


def _lru_prompt(z, xr_cb, gr_cb, p, batch, seq, lt):
    tiles = seq // lt
    blk = lambda cb: pl.BlockSpec((lt, LRU_W), lambda b, t: (b * tiles + t, cb))
    consts = [p["conv_w"], p["conv_b"], p["w_r"], p["w_i"], p["b_r"], p["b_i"], p["lam"]]
    return pl.pallas_call(
        _lru_prompt_kernel,
        out_shape=(jax.ShapeDtypeStruct((batch * seq, LRU_W), f32),
                   jax.ShapeDtypeStruct((batch, CONV_W - 1, LRU_W), f32),
                   jax.ShapeDtypeStruct((batch, 1, LRU_W), f32)),
        grid=(batch, tiles),
        in_specs=[blk(xr_cb), blk(gr_cb)] + [_const_spec(c.shape) for c in consts],
        out_specs=(pl.BlockSpec((lt, LRU_W), lambda b, t: (b * tiles + t, 0)),
                   pl.BlockSpec((None, CONV_W - 1, LRU_W), lambda b, t: (b, 0, 0)),
                   pl.BlockSpec((None, 1, LRU_W), lambda b, t: (b, 0, 0))),
        scratch_shapes=[pltpu.VMEM((lt + SUBLANES, LRU_W), f32), pltpu.VMEM((lt, LRU_W), f32),
                        pltpu.VMEM((lt, LRU_W), f32), pltpu.VMEM((SUBLANES, LRU_W), f32)],
        compiler_params=_params("parallel", "arbitrary"),
        name="lru_prompt",
    )(z, z, *consts)


S5_R = 4
S5_TAB = 8


def _s5_param_kernel(lr_ref, li_ref, ldt_ref, btr_ref, bti_ref, cr_ref, ci_ref,
                     a1r_ref, a1i_ref, pcr_ref, pci_ref, xr_ref, xi_ref, wcr_ref, wci_ref):
    lr_, li_ = lr_ref[...], li_ref[...]
    dt = jnp.exp(ldt_ref[...])
    mag = jnp.exp(lr_ * dt)
    ar, ai = mag * jnp.cos(li_ * dt), mag * jnp.sin(li_ * dt)
    den = lr_ * lr_ + li_ * li_
    nr, ni = ar - 1.0, ai
    fr = (nr * lr_ + ni * li_) / den
    fi = (ni * lr_ - nr * li_) / den
    bbr = fr[:, None, :] * btr_ref[...] - fi[:, None, :] * bti_ref[...]
    bbi = fr[:, None, :] * bti_ref[...] + fi[:, None, :] * btr_ref[...]
    a1r_ref[...], a1i_ref[...] = ar, ai
    cr, ci = cr_ref[...], ci_ref[...]
    pr, pi = jnp.ones_like(ar), jnp.zeros_like(ai)
    for n in range(S5_R * S5_TAB + 1):
        if n < S5_R:
            xr_ref[n] = pr[:, None, :] * bbr - pi[:, None, :] * bbi
            xi_ref[n] = pr[:, None, :] * bbi + pi[:, None, :] * bbr
        if 1 <= n <= S5_R:
            wcr_ref[n - 1] = cr * pr[:, None, :] - ci * pi[:, None, :]
            wci_ref[n - 1] = -(cr * pi[:, None, :] + ci * pr[:, None, :])
        if n >= S5_R and n % S5_R == 0:
            pcr_ref[n // S5_R - 1], pci_ref[n // S5_R - 1] = pr, pi
        pr, pi = pr * ar - pi * ai, pr * ai + pi * ar


def _toeplitz_kernel(x_ref, c_ref, m_ref):
    for j in range(x_ref.shape[0]):
        m_ref[j] = _dot(x_ref[j], c_ref[j])


def _s5_params(lam_re, lam_im, log_dt, b_re, b_im, c_re, c_im):
    btr = jnp.swapaxes(b_re, 1, 2)
    bti = jnp.swapaxes(b_im, 1, 2)
    shp = jax.ShapeDtypeStruct
    gp = (S5_G, S5_P)
    ghp = (S5_G, S5_GH, S5_P)
    a1r, a1i, pcr, pci, xr, xi, wcr, wci = pl.pallas_call(
        _s5_param_kernel,
        out_shape=(shp(gp, f32), shp(gp, f32), shp((S5_TAB,) + gp, f32), shp((S5_TAB,) + gp, f32),
                   shp((S5_R,) + ghp, f32), shp((S5_R,) + ghp, f32), shp((S5_R,) + ghp, f32), shp((S5_R,) + ghp, f32)),
        name="s5_params",
    )(lam_re, lam_im, log_dt.reshape(S5_G, 1), btr, bti, c_re, c_im)
    gpb = LANES // S5_GH
    nblk = S5_G // gpb
    eye = jnp.eye(gpb, dtype=f32)

    def bdiag_in(bb):
        x = bb.reshape(bb.shape[:-3] + (nblk, gpb, S5_GH, S5_P))
        return jnp.einsum('...jghp,gk->...jghkp', x, eye).reshape(bb.shape[:-3] + (nblk, gpb * S5_GH, gpb * S5_P))

    def bdiag_out(cc):
        x = cc.reshape(cc.shape[:-3] + (nblk, gpb, S5_GH, S5_P))
        return jnp.einsum('...jghp,gk->...jgpkh', x, eye).reshape(cc.shape[:-3] + (nblk, gpb * S5_P, gpb * S5_GH))

    x_bd = jnp.concatenate([bdiag_in(xr), bdiag_in(xi)], axis=-1).astype(bf16)
    c_bd = jnp.concatenate([bdiag_out(c_re), bdiag_out(-c_im)], axis=-2).astype(bf16)
    wc_bd = jnp.concatenate([bdiag_out(wcr), bdiag_out(wci)], axis=-2).astype(bf16)
    lag = pl.pallas_call(
        _toeplitz_kernel,
        out_shape=shp((S5_R, nblk, LANES, LANES), f32),
        grid=(S5_R,),
        in_specs=[pl.BlockSpec((None, nblk, LANES, 2 * gpb * S5_P), lambda d: (d, 0, 0, 0)),
                  _const_spec(c_bd.shape)],
        out_specs=pl.BlockSpec((None, nblk, LANES, LANES), lambda d: (d, 0, 0, 0)),
        compiler_params=_params("parallel"),
        name="s5_toeplitz",
    )(x_bd, c_bd)
    zero = jnp.zeros((nblk, LANES, LANES), f32)
    toep = jnp.concatenate(
        [jnp.concatenate([lag[r - s] if r >= s else zero for r in range(S5_R)], axis=2) for s in range(S5_R)],
        axis=1).astype(bf16)
    w_x = jnp.concatenate([x_bd[S5_R - 1 - s] for s in range(S5_R)], axis=1)
    w_c = jnp.concatenate([wc_bd[r] for r in range(S5_R)], axis=2)
    return dict(a1r=a1r.reshape(1, S5_N), a1i=a1i.reshape(1, S5_N),
                pcr=pcr.reshape(S5_TAB, S5_N), pci=pci.reshape(S5_TAB, S5_N),
                b_bd=x_bd[0], c_bd=c_bd, w_x=w_x, toep=toep, w_c=w_c)


S5_BLK = S5_N // (S5_G // (LANES // S5_GH))


def _s5_input(u, bbd_ref, j):
    bu = _dot(u[:, j * LANES:(j + 1) * LANES].astype(bf16), bbd_ref[j])
    return bu[:, 0:S5_BLK], bu[:, S5_BLK:2 * S5_BLK]


def _s5_output(u, hr, hi, cbd_ref, d_ref, wg_ref, bg_ref):
    ys = []
    for j in range(S5_N // S5_BLK):
        cs = slice(j * S5_BLK, (j + 1) * S5_BLK)
        hcat = jnp.concatenate([hr[:, cs], hi[:, cs]], axis=1).astype(bf16)
        ys.append(_dot(hcat, cbd_ref[j]))
    y = _gelu(jnp.concatenate(ys, axis=1) + d_ref[...] * u)
    return y * _sigmoid(_dot(y.astype(bf16), wg_ref[...]) + bg_ref[...])


def _s5_prompt_kernel(*refs):
    nblk = S5_N // S5_BLK
    u_refs = refs[:nblk]
    wx_ref, tp_ref, wc_ref, pcr_ref, pci_ref, d_ref, wg_ref, bg_ref = refs[nblk:nblk + 8]
    y_refs = refs[nblk + 8:2 * nblk + 8]
    sr_ref, si_ref, er_s, ei_s, y_s = refs[2 * nblk + 8:]
    t_idx = pl.program_id(1)
    cr = u_refs[0].shape[0] // S5_R
    gw = S5_R * LANES

    @pl.when(t_idx == 0)
    def _():
        er_s[0:SUBLANES, :] = jnp.zeros((SUBLANES, S5_N), f32)
        ei_s[0:SUBLANES, :] = jnp.zeros((SUBLANES, S5_N), f32)

    def token(s):
        return pl.ds(s, cr, stride=S5_R)

    sub = lax.broadcasted_iota(jnp.int32, (SUBLANES, S5_BLK), 0)
    for j in range(nblk):
        cs = slice(j * S5_BLK, (j + 1) * S5_BLK)
        uj = jnp.concatenate([u_refs[j][token(s), :] for s in range(S5_R)], axis=1).astype(bf16)
        x = _dot(uj, wx_ref[j])
        y_s[:, j * gw:(j + 1) * gw] = _dot(uj, tp_ref[j])
        hr = x[:, 0:S5_BLK].reshape(cr // SUBLANES, SUBLANES, S5_BLK)
        hi = x[:, S5_BLK:2 * S5_BLK].reshape(cr // SUBLANES, SUBLANES, S5_BLK)
        for d in (1, 2, 4):
            pr = jnp.where(sub >= d, pcr_ref[d - 1:d, cs], 0.0)
            pi = jnp.where(sub >= d, pci_ref[d - 1:d, cs], 0.0)
            sr, si = pltpu.roll(hr, d, 1), pltpu.roll(hi, d, 1)
            hr, hi = hr + (pr * sr - pi * si), hi + (pr * si + pi * sr)
        er_s[SUBLANES:SUBLANES + cr, cs] = hr.reshape(cr, S5_BLK)
        ei_s[SUBLANES:SUBLANES + cr, cs] = hi.reshape(cr, S5_BLK)

    tr, ti = pcr_ref[...], pci_ref[...]
    last = pl.ds(SUBLANES - 1, 1)
    c0 = (jnp.broadcast_to(er_s[last, :], (SUBLANES, S5_N)), jnp.broadcast_to(ei_s[last, :], (SUBLANES, S5_N)))

    def group(g, c):
        c_r, c_i = c
        rows = pl.ds(pl.multiple_of((g + 1) * SUBLANES, SUBLANES), SUBLANES)
        g_r = er_s[rows, :] + tr * c_r - ti * c_i
        g_i = ei_s[rows, :] + tr * c_i + ti * c_r
        er_s[rows, :] = g_r
        ei_s[rows, :] = g_i
        return (jnp.broadcast_to(g_r[SUBLANES - 1:SUBLANES, :], (SUBLANES, S5_N)),
                jnp.broadcast_to(g_i[SUBLANES - 1:SUBLANES, :], (SUBLANES, S5_N)))

    c_r, c_i = lax.fori_loop(0, cr // SUBLANES, group, c0)

    for j in range(nblk):
        cs = slice(j * S5_BLK, (j + 1) * S5_BLK)
        prev = pl.ds(SUBLANES - 1, cr)
        sj = jnp.concatenate([er_s[prev, cs], ei_s[prev, cs]], axis=1).astype(bf16)
        y_s[:, j * gw:(j + 1) * gw] += _dot(sj, wc_ref[j])
    er_s[last, :] = c_r[0:1, :]
    ei_s[last, :] = c_i[0:1, :]
    sr_ref[...] = c_r[0:1, :]
    si_ref[...] = c_i[0:1, :]

    for r in range(S5_R):
        y = jnp.concatenate([y_s[:, j * gw + r * LANES:j * gw + (r + 1) * LANES] for j in range(nblk)], axis=1)
        u_r = jnp.concatenate([u_refs[j][token(r), :] for j in range(nblk)], axis=1)
        y = _gelu(y + d_ref[...] * u_r)
        y = y * _sigmoid(_dot(y.astype(bf16), wg_ref[...]) + bg_ref[...])
        for j in range(nblk):
            y_refs[j][token(r), :] = y[:, j * LANES:(j + 1) * LANES]


def _s5_prompt(us, tabs, p, batch, seq, cr):
    nblk = len(us)
    tiles = seq // (S5_R * cr)
    consts = [tabs["w_x"], tabs["toep"], tabs["w_c"], tabs["pcr"], tabs["pci"], p["d"], p["w_glu"], p["b_glu"]]
    blk = pl.BlockSpec((S5_R * cr, LANES), lambda b, t: (b * tiles + t, 0))
    state = pl.BlockSpec((None, 1, S5_N), lambda b, t: (b, 0, 0))
    outs = pl.pallas_call(
        _s5_prompt_kernel,
        out_shape=tuple(jax.ShapeDtypeStruct((batch * seq, LANES), f32) for _ in range(nblk))
        + (jax.ShapeDtypeStruct((batch, 1, S5_N), f32), jax.ShapeDtypeStruct((batch, 1, S5_N), f32)),
        grid=(batch, tiles),
        in_specs=[blk] * nblk + [_const_spec(c.shape) for c in consts],
        out_specs=(blk,) * nblk + (state, state),
        scratch_shapes=[pltpu.VMEM((cr + SUBLANES, S5_N), f32), pltpu.VMEM((cr + SUBLANES, S5_N), f32),
                        pltpu.VMEM((cr, S5_R * S5_W), f32)],
        compiler_params=_params("parallel", "arbitrary"),
        name="s5_prompt",
    )(*us, *consts)
    return list(outs[:nblk]), outs[nblk], outs[nblk + 1]


def _gla_step_kernel(kind, dk, layer, *refs):
    if kind == "gla":
        q_ref, k_ref, v_ref, gate_ref, lr_ref, wlr_ref, blr_ref, ng_ref, s_ref, y_ref, so_ref, o_scr = refs
        sl = pl.program_id(0)
        q = q_ref[...] * (GLA_DK ** -0.5)
        k = k_ref[...]
        pre_all = _dot(lr_ref[...].astype(bf16), wlr_ref[...]) + blr_ref[...]
        pre = jnp.where(sl == 0, pre_all[:, 0:LANES], pre_all[:, LANES:2 * LANES])
        a = jnp.exp(_log_sigmoid(pre) * (1.0 / GLA_TAU))
    else:
        q_ref, f_ref, v_ref, gate_ref, lg_ref, ng_ref, s_ref, y_ref, so_ref, o_scr = refs
        sl = pl.program_id(0)
        lb_all = _hgrn_lower_bound(lg_ref, layer)
        lb = lb_all[:, 0:LANES]
        for j in range(1, HG_H):
            lb = jnp.where(sl == j, lb_all[:, j * LANES:(j + 1) * LANES], lb)
        a = lb + (1.0 - lb) * _sigmoid(f_ref[...])
        q = _silu(q_ref[...])
        k = 1.0 - a
    hps = LANES // dk
    nb = q.shape[0]
    qat, kt, at = (q * a).T, k.T, a.T
    lane = lax.broadcasted_iota(jnp.int32, (1, LANES), 1)
    v = v_ref[...]
    for hh in range(hps):
        r0 = hh * dk
        msk = (lane >= r0) & (lane < r0 + dk)
        qk = jnp.sum(jnp.where(msk, q * k, 0.0), axis=-1, keepdims=True)
        for b in range(nb):
            s = s_ref[b, hh]
            col = lambda x: jnp.broadcast_to(x[r0:r0 + dk, b:b + 1], (dk, LANES))
            vr = v[b:b + 1, hh * LANES:(hh + 1) * LANES]
            so_ref[b, hh] = col(at) * s + col(kt) * vr
            oi = jnp.sum(col(qat) * s, axis=0, keepdims=True)
            o_scr[b:b + 1, hh * LANES:(hh + 1) * LANES] = jnp.broadcast_to(qk[b:b + 1, :], (1, LANES)) * vr + oi
    for hh in range(hps):
        cs = slice(hh * LANES, (hh + 1) * LANES)
        o = o_scr[:, cs]
        o = o * lax.rsqrt(jnp.mean(o * o, axis=-1, keepdims=True) + RMS_EPS) * ng_ref[...]
        y_ref[:, cs] = o * _silu(gate_ref[:, cs])


def _gla_step(kind, z, extra, norm_g, state, layer):
    nb = z.shape[0]
    dk = GLA_DK if kind == "gla" else HG_D
    n_heads = GLA_H if kind == "gla" else HG_H
    hps = LANES // dk
    n_slabs = n_heads // hps
    vw = hps * LANES
    zb = lambda width, cb0: pl.BlockSpec((nb, width), lambda s: (0, cb0 + s))
    if kind == "gla":
        wlr, blr = extra
        ins = [z, z, z, z, z, wlr, blr]
        specs = [zb(LANES, 0), zb(LANES, GLA_KW // LANES), zb(vw, GLA_VW // vw * 1), zb(vw, GLA_VW // vw * 2),
                 pl.BlockSpec((nb, LANES), lambda s: (0, (EVEN_N - LANES) // LANES)),
                 _const_spec(wlr.shape), _const_spec(blr.shape)]
    else:
        (lg,) = extra
        ins = [z, z, z, z, lg]
        specs = [zb(LANES, 0), zb(LANES, HG_W // LANES), zb(vw, HG_W // vw * 2),
                 zb(vw, HG_W // vw * 3), _const_spec(lg.shape)]
    ins += [norm_g, state]
    specs += [_const_spec(norm_g.shape), pl.BlockSpec((nb, hps, dk, LANES), lambda s: (0, s, 0, 0))]
    return pl.pallas_call(
        functools.partial(_gla_step_kernel, kind, dk, layer),
        out_shape=(jax.ShapeDtypeStruct((nb, HALF), f32), jax.ShapeDtypeStruct(state.shape, f32)),
        grid=(n_slabs,),
        in_specs=specs,
        out_specs=(pl.BlockSpec((nb, vw), lambda s: (0, s)),
                   pl.BlockSpec((nb, hps, dk, LANES), lambda s: (0, s, 0, 0))),
        scratch_shapes=[pltpu.VMEM((nb, vw), f32)],
        compiler_params=_params("parallel"),
        name=kind + "_step",
    )(*ins)


def _lru_step_kernel(x_ref, gr_ref, c0_ref, c1_ref, c2_ref, h0_ref, cw_ref, cb_ref, wr_ref, wi_ref,
                     br_ref, bi_ref, lam_ref, y_ref, h_ref):
    xc = (cb_ref[...] + cw_ref[0:1, :] * c0_ref[...] + cw_ref[1:2, :] * c1_ref[...]
          + cw_ref[2:3, :] * c2_ref[...] + cw_ref[3:4, :] * x_ref[...])
    a, u = _lru_gates(xc, wr_ref, wi_ref, br_ref, bi_ref, lam_ref)
    h = a * h0_ref[...] + u
    h_ref[...] = h
    y_ref[...] = h * _gelu(gr_ref[...])


def _lru_step(z, xr_cb, gr_cb, conv_rows, h0, p):
    nb = z.shape[0]
    full = lambda: pl.BlockSpec((nb, LRU_W), lambda i: (0, 0))
    consts = [p["conv_w"], p["conv_b"], p["w_r"], p["w_i"], p["b_r"], p["b_i"], p["lam"]]
    return pl.pallas_call(
        _lru_step_kernel,
        out_shape=(jax.ShapeDtypeStruct((nb, LRU_W), f32), jax.ShapeDtypeStruct((nb, LRU_W), f32)),
        grid=(1,),
        in_specs=[pl.BlockSpec((nb, LRU_W), lambda i: (0, xr_cb)), pl.BlockSpec((nb, LRU_W), lambda i: (0, gr_cb)),
                  full(), full(), full(), full()] + [_const_spec(c.shape) for c in consts],
        out_specs=(full(), full()),
        compiler_params=_params("arbitrary"),
        name="lru_step",
    )(z, z, *conv_rows, h0, *consts)


def _s5_step_kernel(u_ref, s0r_ref, s0i_ref, a1r_ref, a1i_ref, bbd_ref, cbd_ref, d_ref, wg_ref, bg_ref,
                    y_ref, sr_ref, si_ref):
    u = u_ref[...]
    for j in range(S5_N // S5_BLK):
        cs = slice(j * S5_BLK, (j + 1) * S5_BLK)
        bur, bui = _s5_input(u, bbd_ref, j)
        ar, ai = a1r_ref[:, cs], a1i_ref[:, cs]
        s0r, s0i = s0r_ref[:, cs], s0i_ref[:, cs]
        sr_ref[:, cs] = bur + ar * s0r - ai * s0i
        si_ref[:, cs] = bui + ar * s0i + ai * s0r
    y_ref[...] = _s5_output(u, sr_ref[...], si_ref[...], cbd_ref, d_ref, wg_ref, bg_ref)


def _s5_step(u, s0r, s0i, tabs, p):
    nb = u.shape[0]
    consts = [tabs["a1r"], tabs["a1i"], tabs["b_bd"], tabs["c_bd"], p["d"], p["w_glu"], p["b_glu"]]
    st = lambda: pl.BlockSpec((nb, S5_N), lambda i: (0, 0))
    return pl.pallas_call(
        _s5_step_kernel,
        out_shape=(jax.ShapeDtypeStruct((nb, S5_W), f32), jax.ShapeDtypeStruct((nb, S5_N), f32),
                   jax.ShapeDtypeStruct((nb, S5_N), f32)),
        grid=(1,),
        in_specs=[pl.BlockSpec((nb, S5_W), lambda i: (0, 0)), st(), st()] + [_const_spec(c.shape) for c in consts],
        out_specs=(pl.BlockSpec((nb, S5_W), lambda i: (0, 0)), st(), st()),
        compiler_params=_params("arbitrary"),
        name="s5_step",
    )(u, s0r, s0i, *consts)


def _block_diag(w):
    h, n, _ = w.shape
    return jnp.einsum('hij,hk->hikj', w, jnp.eye(h, dtype=w.dtype)).reshape(h * n, h * n)


def kernel(x_prompt, x_sample, c_prompt, c_sample, state_gla, state_rglru_conv, state_rglru_h, state_s5_re, state_s5_im, state_hgrn, ev_w_in, ev_gla_w_lr, ev_gla_b_lr, ev_gla_norm_g, ev_conv_w, ev_conv_b, ev_lru_w_r, ev_lru_b_r, ev_lru_w_i, ev_lru_b_i, ev_lru_lam, ev_w_out, od_w_in, od_s5_lam_re, od_s5_lam_im, od_s5_log_dt, od_s5_b_re, od_s5_b_im, od_s5_c_re, od_s5_c_im, od_s5_d, od_s5_w_glu, od_s5_b_glu, hg_lb_logits, od_hg_norm_g, od_w_out, w_ada, b_ada, ln_g, ln_b, ffn_w_in, ffn_w_out):
    bp, seq = x_prompt.shape[0], x_prompt.shape[1]
    bs = x_sample.shape[0]
    assert x_sample.shape[1] == 1 and seq % CHUNK == 0
    tp = bp * seq
    tm = min(512, seq)
    lt = min(512, seq)
    lt_gla = min(1024, seq)
    cr_s5 = min(256, seq // S5_R)
    row2 = lambda a: a.reshape(1, -1)

    mod = _modulation(jnp.concatenate([c_prompt, c_sample], axis=0).astype(f32), w_ada, b_ada)
    mod_p = mod[:, :bp].reshape(DEPTH, bp, 1, 6 * D_MODEL)
    mod_s = mod[:, bp:].reshape(DEPTH, 1, bs, 6 * D_MODEL)

    xp = x_prompt.astype(f32).reshape(tp, D_MODEL)
    xs = x_sample.astype(f32).reshape(bs, D_MODEL)
    outs_p = {k: [] for k in ("gla", "conv", "lru", "re", "im", "hg")}
    outs_s = {k: [] for k in ("gla", "conv", "lru", "re", "im", "hg")}

    for l in range(DEPTH):
        w_f_in = ffn_w_in[l].astype(bf16)
        w_f_out = ffn_w_out[l].astype(bf16)
        if l % 2 == 0:
            e = l // 2
            w = ev_w_in[e]
            lr0 = 2 * GLA_KW + 2 * GLA_VW
            w_in = jnp.concatenate(
                [w[:, :lr0], w[:, lr0 + GLA_LR:], w[:, lr0:lr0 + GLA_LR],
                 jnp.zeros((D_MODEL, EVEN_N - w.shape[1]), w.dtype)], axis=1).astype(bf16)
            w_out = ev_w_out[e].astype(bf16)
            wlr = jnp.concatenate([ev_gla_w_lr[e], jnp.zeros((LANES - GLA_LR, GLA_KW), f32)], axis=0).astype(bf16)
            gla_extra = (wlr, row2(ev_gla_b_lr[e]))
            ng = row2(ev_gla_norm_g[e])
            lru_p = dict(conv_w=ev_conv_w[e], conv_b=row2(ev_conv_b[e]),
                         w_r=_block_diag(ev_lru_w_r[e]).astype(bf16), w_i=_block_diag(ev_lru_w_i[e]).astype(bf16),
                         b_r=row2(ev_lru_b_r[e]), b_i=row2(ev_lru_b_i[e]), lam=row2(ev_lru_lam[e]))
            xr_cb, gr_cb = lr0 // LRU_W, lr0 // LRU_W + 1
            (z,) = _inproj(xp, mod_p[l], w_in, tm, seq // tm, (EVEN_N,))
            ya, s_gla = _gla_prompt("gla", z, (EVEN_N - LANES) // LANES, gla_extra, ng, bp, seq, lt_gla, l)
            yb, s_conv, s_h = _lru_prompt(z, xr_cb, gr_cb, lru_p, bp, seq, lt)
            xp = _outffn(xp, [ya, yb], mod_p[l], w_out, ln_g[l], ln_b[l], w_f_in, w_f_out, tm, seq // tm)
            outs_p["gla"].append(s_gla)
            outs_p["conv"].append(s_conv)
            outs_p["lru"].append(s_h.reshape(bp, LRU_W))
            (z,) = _inproj(xs, mod_s[l], w_in, bs, 1, (EVEN_N,))
            ya, s_gla = _gla_step("gla", z, gla_extra, ng, state_gla[e], l)
            cs = state_rglru_conv[e].astype(f32)
            yb, s_h = _lru_step(z, xr_cb, gr_cb, [cs[:, j] for j in range(CONV_W - 1)],
                                state_rglru_h[e].astype(f32), lru_p)
            xs = _outffn(xs, [ya, yb], mod_s[l], w_out, ln_g[l], ln_b[l], w_f_in, w_f_out, bs, 1)
            outs_s["gla"].append(s_gla)
            outs_s["conv"].append(jnp.stack([cs[:, 1], cs[:, 2], z[:, xr_cb * LRU_W:(xr_cb + 1) * LRU_W]], axis=1))
            outs_s["lru"].append(s_h)
        else:
            o = l // 2
            w_in = od_w_in[o].astype(bf16)
            w_out = od_w_out[o].astype(bf16)
            tabs = _s5_params(od_s5_lam_re[o], od_s5_lam_im[o], od_s5_log_dt[o], od_s5_b_re[o], od_s5_b_im[o],
                              od_s5_c_re[o], od_s5_c_im[o])
            s5_p = dict(d=row2(od_s5_d[o]), w_glu=od_s5_w_glu[o].astype(bf16), b_glu=row2(od_s5_b_glu[o]))
            hg_extra = (hg_lb_logits.astype(f32),)
            ng = row2(od_hg_norm_g[o])
            *us, z = _inproj(xp, mod_p[l], w_in, tm, seq // tm, (LANES,) * (S5_W // LANES) + (ODD_N - S5_W,))
            yas, s_re, s_im = _s5_prompt(us, tabs, s5_p, bp, seq, cr_s5)
            yb, s_hg = _gla_prompt("hgrn", z, 0, hg_extra, ng, bp, seq, lt_gla, l)
            xp = _outffn(xp, yas + [yb], mod_p[l], w_out, ln_g[l], ln_b[l], w_f_in, w_f_out, tm, seq // tm)
            outs_p["re"].append(s_re.reshape(bp, S5_G, S5_P))
            outs_p["im"].append(s_im.reshape(bp, S5_G, S5_P))
            outs_p["hg"].append(s_hg)
            u, z = _inproj(xs, mod_s[l], w_in, bs, 1, (S5_W, ODD_N - S5_W))
            ya, s_re, s_im = _s5_step(u, state_s5_re[o].astype(f32).reshape(bs, S5_N),
                                      state_s5_im[o].astype(f32).reshape(bs, S5_N), tabs, s5_p)
            yb, s_hg = _gla_step("hgrn", z, hg_extra, ng, state_hgrn[o], l)
            xs = _outffn(xs, [ya, yb], mod_s[l], w_out, ln_g[l], ln_b[l], w_f_in, w_f_out, bs, 1)
            outs_s["re"].append(s_re.reshape(bs, S5_G, S5_P))
            outs_s["im"].append(s_im.reshape(bs, S5_G, S5_P))
            outs_s["hg"].append(s_hg)

    st = lambda d, k: jnp.stack(d[k])
    return (xp.reshape(bp, seq, D_MODEL).astype(x_prompt.dtype), xs.reshape(bs, 1, D_MODEL).astype(x_sample.dtype),
            st(outs_p, "gla"), st(outs_s, "gla"), st(outs_p, "conv"), st(outs_s, "conv"),
            st(outs_p, "lru"), st(outs_s, "lru"), st(outs_p, "re"), st(outs_s, "re"),
            st(outs_p, "im"), st(outs_s, "im"), st(outs_p, "hg"), st(outs_s, "hg"))
```

```python
import functools
import math

import numpy as np
import jax
import jax.numpy as jnp
from jax import lax
from jax.experimental import pallas as pl
from jax.experimental.pallas import tpu as pltpu

f32 = jnp.float32
bf16 = jnp.bfloat16

D_MODEL = 1024
DEPTH = 2
HALF = D_MODEL // 2
GLA_H = 4
GLA_DV = HALF // GLA_H
GLA_DK = GLA_DV // 2
GLA_KW = GLA_H * GLA_DK
GLA_VW = GLA_H * GLA_DV
GLA_LR = 16
GLA_TAU = 16.0
LRU_W = HALF
LRU_H = 8
LRU_BW = LRU_W // LRU_H
CONV_W = 4
RG_C = 8.0
S5_W = HALF
S5_GH = 16
S5_G = S5_W // S5_GH
S5_P = 64
S5_N = S5_G * S5_P
HG_H = 4
HG_D = HALF // HG_H
HG_W = HG_H * HG_D
D_FF = ((8 * D_MODEL // 3 + 255) // 256) * 256
ALPHA = (2.0 * DEPTH) ** 0.25
LN_EPS = 1e-5
RMS_EPS = 1e-6

LANES = 128
SUBLANES = 8
VMEM_LIMIT = 56 * 1024 * 1024

CHUNK = 64
N_LEVELS = 6
FINE_LEVELS = 3
GLA_UNROLL = 3
FF_CHUNK = 256
EVEN_N = 2688
ODD_N = 2560


def _sigmoid(x):
    return 0.5 * jnp.tanh(0.5 * x) + 0.5


def _silu(x):
    return x * _sigmoid(x)


def _gelu(x):
    c = math.sqrt(2.0 / math.pi)
    return x * (0.5 * (1.0 + jnp.tanh(c * (x + 0.044715 * (x * x * x)))))


def _softplus(x):
    return jnp.maximum(x, 0.0) + jnp.log1p(jnp.exp(-jnp.abs(x)))


def _log_sigmoid(x):
    return -_softplus(-x)


def _expm1(x):
    u = jnp.exp(x)
    um1 = u - 1.0
    lg = jnp.log(u)
    safe = jnp.where(lg == 0.0, 1.0, lg)
    return jnp.where(um1 == 0.0, x, jnp.where(um1 == -1.0, -1.0, um1 * x / safe))


def _layer_norm(x, g, b):
    mu = jnp.mean(x, axis=-1, keepdims=True)
    xc = x - mu
    var = jnp.mean(xc * xc, axis=-1, keepdims=True)
    return xc * lax.rsqrt(var + LN_EPS) * g + b


def _dot(a, b):
    return jnp.dot(a, b, preferred_element_type=f32)


def _dot_nt(a, b):
    return lax.dot_general(a, b, (((1,), (1,)), ((), ())), preferred_element_type=f32)


def _dot_tn(a, b):
    return lax.dot_general(a, b, (((0,), (0,)), ((), ())), preferred_element_type=f32)


def _const_spec(shape):
    nd = len(shape)
    return pl.BlockSpec(shape, lambda *_: (0,) * nd, pipeline_mode=pl.Buffered(1))


def _params(*sem):
    return pltpu.CompilerParams(dimension_semantics=sem, vmem_limit_bytes=VMEM_LIMIT)


def _mod_kernel(c_ref, w_ref, b_ref, o_ref):
    cond = _silu(c_ref[...]).astype(bf16)
    o_ref[...] = _dot(cond, w_ref[...].astype(bf16)) + b_ref[...]


def _modulation(c_all, w_ada, b_ada):
    rows = c_all.shape[0]
    tn = 1536
    return pl.pallas_call(
        _mod_kernel,
        out_shape=jax.ShapeDtypeStruct((DEPTH, rows, 6 * D_MODEL), f32),
        grid=(DEPTH, 6 * D_MODEL // tn),
        in_specs=[pl.BlockSpec((rows, D_MODEL), lambda l, j: (0, 0)),
                  pl.BlockSpec((None, D_MODEL, tn), lambda l, j: (l, 0, j)),
                  pl.BlockSpec((None, 1, tn), lambda l, j: (l, 0, j))],
        out_specs=pl.BlockSpec((None, rows, tn), lambda l, j: (l, 0, j)),
        compiler_params=_params("parallel", "parallel"),
        name="adaln_mod",
    )(c_all, w_ada, b_ada.reshape(DEPTH, 1, 6 * D_MODEL))


def _mod_spec(rows_per_block, tiles_per_batch, j):
    return pl.BlockSpec((None, rows_per_block, D_MODEL), lambda i: (i // tiles_per_batch, 0, j))


def _inproj_kernel(x_ref, sc_ref, sh_ref, w_ref, *z_refs):
    h = (x_ref[...] * (1.0 + sc_ref[...]) + sh_ref[...]).astype(bf16)
    lo = 0
    for z_ref in z_refs:
        z_ref[...] = _dot(h, w_ref[:, lo:lo + z_ref.shape[1]])
        lo += z_ref.shape[1]


def _inproj(x, mod, w, tm, tiles_per_batch, widths):
    t, n = x.shape[0], w.shape[1]
    assert sum(widths) == n
    r = mod.shape[1]
    return pl.pallas_call(
        _inproj_kernel,
        out_shape=tuple(jax.ShapeDtypeStruct((t, wd), f32) for wd in widths),
        grid=(t // tm,),
        in_specs=[pl.BlockSpec((tm, D_MODEL), lambda i: (i, 0)),
                  _mod_spec(r, tiles_per_batch, 1), _mod_spec(r, tiles_per_batch, 0),
                  _const_spec((D_MODEL, n))],
        out_specs=tuple(pl.BlockSpec((tm, wd), lambda i: (i, 0)) for wd in widths),
        compiler_params=_params("parallel"),
        name="in_proj",
    )(x, mod, mod, w)


def _outffn_kernel(n_y, x_ref, *refs):
    y_refs = refs[:n_y]
    gtm_ref, shf_ref, scf_ref, gtf_ref, wo_ref, lng_ref, lnb_ref, wi_ref, wf_ref, o_ref = refs[n_y:]
    y = jnp.concatenate([r[...].astype(bf16) for r in y_refs], axis=1)
    mix = _dot(y, wo_ref[...])
    x1 = _layer_norm(ALPHA * x_ref[...] + (1.0 + gtm_ref[...]) * mix, lng_ref[0:1, :], lnb_ref[0:1, :])
    h = (x1 * (1.0 + scf_ref[...]) + shf_ref[...]).astype(bf16)
    acc = jnp.zeros(x1.shape, f32)
    for c in range(D_FF // FF_CHUNK):
        lo = c * FF_CHUNK
        gate = _dot(h, wi_ref[:, lo:lo + FF_CHUNK])
        up = _dot(h, wi_ref[:, D_FF + lo:D_FF + lo + FF_CHUNK])
        act = (_silu(gate) * up).astype(bf16)
        acc = acc + _dot(act, wf_ref[lo:lo + FF_CHUNK, :])
    o_ref[...] = _layer_norm(ALPHA * x1 + (1.0 + gtf_ref[...]) * acc, lng_ref[1:2, :], lnb_ref[1:2, :])


def _outffn(x, ys, mod, w_out, ln_g, ln_b, w_in, w_f, layer, tm, tiles_per_batch):
    t = x.shape[0]
    r = mod.shape[1]
    assert sum(y.shape[1] for y in ys) == D_MODEL
    row = lambda w: pl.BlockSpec((tm, w), lambda i: (i, 0))
    stacked = lambda a: pl.BlockSpec((None,) + a.shape[1:], lambda i: (layer, 0, 0), pipeline_mode=pl.Buffered(1))
    return pl.pallas_call(
        functools.partial(_outffn_kernel, len(ys)),
        out_shape=jax.ShapeDtypeStruct((t, D_MODEL), f32),
        grid=(t // tm,),
        in_specs=[row(D_MODEL)] + [row(y.shape[1]) for y in ys] + [
                  _mod_spec(r, tiles_per_batch, 2), _mod_spec(r, tiles_per_batch, 3),
                  _mod_spec(r, tiles_per_batch, 4), _mod_spec(r, tiles_per_batch, 5),
                  _const_spec((D_MODEL, D_MODEL)), _const_spec((2, D_MODEL)), _const_spec((2, D_MODEL)),
                  stacked(w_in), stacked(w_f)],
        out_specs=row(D_MODEL),
        compiler_params=_params("parallel"),
        name="outproj_ffn",
    )(x, *ys, mod, mod, mod, mod, w_out, ln_g, ln_b, w_in, w_f)


def _level_tables():
    c = CHUNK
    t = np.arange(c)[:, None]
    u = np.arange(c)[None, :]
    pmask, sgn = [], []
    for lvl in range(N_LEVELS):
        m = 1 << lvl
        right = (t % (2 * m)) >= m
        pm = right & ((u % (2 * m)) < m) & ((t // (2 * m)) == (u // (2 * m)))
        pmask.append(pm.astype(np.float32))
        sgn.append(np.broadcast_to(np.where(right, 1.0, -1.0).astype(np.float32), (c, LANES)))
    tri = (u <= t).astype(np.float32)
    tri3 = np.concatenate([tri, tri, tri], axis=1)
    return (jnp.asarray(tri3, dtype=bf16), jnp.asarray(np.stack(pmask), dtype=f32),
            jnp.asarray(np.stack(sgn), dtype=f32))


LOG2E = 1.0 / math.log(2.0)


def _anchor(b, lvl):
    m = 1 << lvl
    b3 = b.reshape(CHUNK // SUBLANES, SUBLANES, LANES)
    if 4 * m == SUBLANES:
        sub = lax.broadcasted_iota(jnp.int32, b3.shape, 1)
        a = jnp.where(sub < 2 * m, jnp.broadcast_to(b3[:, m - 1:m, :], b3.shape),
                      jnp.broadcast_to(b3[:, 3 * m - 1:3 * m, :], b3.shape))
        return a.reshape(CHUNK, LANES)
    assert 2 * m == SUBLANES
    return jnp.broadcast_to(b3[:, m - 1:m, :], b3.shape).reshape(CHUNK, LANES)


def _split3(g):
    g1 = g.astype(bf16)
    r1 = g - g1.astype(f32)
    g2 = r1.astype(bf16)
    r2 = r1 - g2.astype(f32)
    return jnp.concatenate([g1, g2, r2.astype(bf16)], axis=0)


def _hgrn_lower_bound(lg_ref, layer):
    rows = [lg_ref[i:i + 1, :] for i in range(DEPTH)]
    mx = functools.reduce(jnp.maximum, rows)
    ex = [jnp.exp(r - mx) for r in rows]
    den = functools.reduce(lambda a, b: a + b, ex)
    sm = [e / den for e in ex]
    cs = sm[0]
    for i in range(1, layer + 1):
        cs = cs + sm[i]
    return cs - sm[0]


def _gla_prologue(kind, refs, rows):
    if kind == "gla":
        qk_ref, lr_ref, wlr_ref, blr_ref = refs
        q = qk_ref[rows, 0:GLA_KW] * (GLA_DK ** -0.5)
        k = qk_ref[rows, GLA_KW:2 * GLA_KW]
        pre = _dot(lr_ref[rows, :].astype(bf16), wlr_ref[...]) + blr_ref[...]
        g = _log_sigmoid(pre) * (LOG2E / GLA_TAU)
        return q, k, g
    q_ref, f_ref, lb = refs
    forget = lb + (1.0 - lb) * _sigmoid(f_ref[rows, :])
    return _silu(q_ref[rows, :]), 1.0 - forget, jnp.log2(forget)


def _gla_chunk_kernel(kind, dk, n_heads, layer, *refs):
    if kind == "gla":
        (qk_ref, v_ref, gate_ref, lr_ref, wlr_ref, blr_ref, ng_ref, tri_ref, pm_ref, sg_ref,
         y_ref, s_ref, st_ref) = refs
        pro = (qk_ref, lr_ref, wlr_ref, blr_ref)
    else:
        (q_ref, f_ref, v_ref, gate_ref, lg_ref, ng_ref, tri_ref, pm_ref, sg_ref,
         y_ref, s_ref, st_ref) = refs
        pro = (q_ref, f_ref, _hgrn_lower_bound(lg_ref, layer))
    t_idx = pl.program_id(1)
    n_tiles = pl.num_programs(1)
    lt = y_ref.shape[0]
    hps = LANES // dk
    n_slabs = n_heads // hps
    lane = lax.broadcasted_iota(jnp.int32, (1, LANES), 1)
    eye = (lax.broadcasted_iota(jnp.int32, (CHUNK, CHUNK), 0)
           == lax.broadcasted_iota(jnp.int32, (CHUNK, CHUNK), 1))

    @pl.when(t_idx == 0)
    def _():
        st_ref[...] = jnp.zeros(st_ref.shape, f32)

    def head_mask(hh):
        if hps == 1:
            return lambda a: a
        msk = (lane >= hh * dk) & (lane < (hh + 1) * dk)
        return lambda a: jnp.where(msk, a, jnp.zeros_like(a))

    def rows_of(c):
        return pl.ds(pl.multiple_of(c * CHUNK, CHUNK), CHUNK)

    def cumsums(qkg):
        q_all, k_all, g_all = qkg
        out = []
        for sl in range(n_slabs):
            cs = slice(sl * LANES, (sl + 1) * LANES)
            q_s, k_s, g_s = q_all[:, cs], k_all[:, cs], g_all[:, cs]
            out.append((q_s, k_s, g_s, _dot(tri_ref[...], _split3(g_s))))
        return out

    def score_dots(pre):
        raw, qbs, ksts, decs, rds = [], [], [], [], []
        right = [sg_ref[lvl] > 0.0 for lvl in range(FINE_LEVELS)]
        for sl in range(n_slabs):
            q_s, k_s, g_s, b = pre[sl]
            ys = [(jnp.where(right[0], jnp.exp2(g_s) * q_s, k_s)).astype(bf16)]
            for lvl in range(1, FINE_LEVELS):
                x = jnp.exp2((b - _anchor(b, lvl)) * sg_ref[lvl])
                ys.append((x * jnp.where(right[lvl], q_s, k_s)).astype(bf16))
            for lvl in range(FINE_LEVELS, N_LEVELS):
                m = 1 << lvl
                parts = []
                for r0 in range(0, CHUNK, SUBLANES):
                    a0 = (r0 // (2 * m)) * (2 * m) + m - 1
                    anc = jnp.broadcast_to(b[a0:a0 + 1, :], (SUBLANES, LANES))
                    bj = b[r0:r0 + SUBLANES, :]
                    if r0 % (2 * m) >= m:
                        parts.append(jnp.exp2(bj - anc) * q_s[r0:r0 + SUBLANES, :])
                    else:
                        parts.append(jnp.exp2(anc - bj) * k_s[r0:r0 + SUBLANES, :])
                ys.append(jnp.concatenate(parts, axis=0).astype(bf16))
            blast = jnp.broadcast_to(b[CHUNK - 1:CHUNK, :], b.shape)
            qb = (q_s * jnp.exp2(b)).astype(bf16)
            kst = (k_s * jnp.exp2(blast - b)).astype(bf16)
            decs.append(jnp.exp2(b[CHUNK - 1:CHUNK, :]))
            qk_prod = q_s * k_s
            for hh in range(hps):
                mul = head_mask(hh)
                rds.append(jnp.sum(mul(qk_prod), axis=-1, keepdims=True))
                raw.append([_dot_nt(mul(ys[lvl]), ys[lvl]) for lvl in range(N_LEVELS)])
                qbs.append(mul(qb))
                ksts.append(mul(kst))
        return raw, rds, tuple(qbs), tuple(ksts), tuple(decs)

    def combine(raw, rds):
        ps = []
        for h in range(n_heads):
            p = jnp.where(eye, rds[h], 0.0)
            for lvl in range(N_LEVELS):
                p = p + pm_ref[lvl] * raw[h][lvl]
            ps.append(p.astype(bf16))
        return tuple(ps)

    def apply_dots(c, sc):
        ps, qbs, ksts, decs = sc
        outs = []
        for h in range(n_heads):
            vb = v_ref[rows_of(c), h * LANES:(h + 1) * LANES].astype(bf16)
            st = st_ref[h]
            outs.append(_dot(ps[h], vb) + _dot_nt(qbs[h], st.astype(bf16)))
            st_ref[h] = st * decs[h // hps] + _dot_tn(vb, ksts[h])
        return outs

    def finish(c, outs):
        for h in range(n_heads):
            cs = slice(h * LANES, (h + 1) * LANES)
            o = outs[h]
            o = o * lax.rsqrt(jnp.mean(o * o, axis=-1, keepdims=True) + RMS_EPS) * ng_ref[...]
            y_ref[rows_of(c), cs] = o * _silu(gate_ref[rows_of(c), cs])

    def first():
        raw, rds, qbs, ksts, decs = score_dots(cumsums(_gla_prologue(kind, pro, rows_of(0))))
        return combine(raw, rds), qbs, ksts, decs

    def body(c, sc):
        qkg = _gla_prologue(kind, pro, rows_of(c + 1))
        if kind == "gla":
            outs = apply_dots(c, sc)
            pre = cumsums(qkg)
        else:
            pre = cumsums(qkg)
            outs = apply_dots(c, sc)
        raw, rds, qbs, ksts, decs = score_dots(pre)
        finish(c, outs)
        return combine(raw, rds), qbs, ksts, decs

    n_chunks = lt // CHUNK
    last = lax.fori_loop(0, n_chunks - 1, body, first(), unroll=min(GLA_UNROLL, n_chunks - 1))
    finish(n_chunks - 1, apply_dots(n_chunks - 1, last))

    @pl.when(t_idx == n_tiles - 1)
    def _():
        for h in range(n_heads):
            off = (h % hps) * dk
            s_ref[h] = st_ref[h].T[off:off + dk, :]


def _gla_prompt(kind, z, col0, extra, norm_g, batch, seq, lt, layer):
    dk = GLA_DK if kind == "gla" else HG_D
    n_heads = GLA_H if kind == "gla" else HG_H
    tiles = seq // lt
    mall, pmask, rmask = _level_tables()
    blk = lambda width, cb: pl.BlockSpec((lt, width), lambda b, t: (b * tiles + t, cb))
    if kind == "gla":
        wlr, blr = extra
        ins = [z, z, z, z, wlr, blr]
        specs = [blk(2 * GLA_KW, 0), blk(GLA_VW, 1), blk(GLA_VW, 2), blk(LANES, col0),
                 _const_spec(wlr.shape), _const_spec(blr.shape)]
    else:
        (lg,) = extra
        ins = [z, z, z, z, lg]
        specs = [blk(HG_W, col0), blk(HG_W, col0 + 1), blk(HG_W, col0 + 2), blk(HG_W, col0 + 3), _const_spec(lg.shape)]
    ins += [norm_g, mall, pmask, rmask]
    specs += [_const_spec(norm_g.shape), _const_spec(mall.shape), _const_spec(pmask.shape), _const_spec(rmask.shape)]
    return pl.pallas_call(
        functools.partial(_gla_chunk_kernel, kind, dk, n_heads, layer),
        out_shape=(jax.ShapeDtypeStruct((batch * seq, HALF), f32),
                   jax.ShapeDtypeStruct((batch, n_heads, dk, LANES), f32)),
        grid=(batch, tiles),
        in_specs=specs,
        out_specs=(pl.BlockSpec((lt, HALF), lambda b, t: (b * tiles + t, 0)),
                   pl.BlockSpec((None, n_heads, dk, LANES), lambda b, t: (b, 0, 0, 0))),
        scratch_shapes=[pltpu.VMEM((n_heads, LANES, LANES), f32)],
        compiler_params=_params("parallel", "arbitrary"),
        name=kind + "_prompt",
    )(*ins)


def _lru_gates(xc, wr_ref, wi_ref, br_ref, bi_ref, lam_ref):
    xb = xc.astype(bf16)
    r = _sigmoid(_dot(xb, wr_ref[...]) + br_ref[...])
    i = _sigmoid(_dot(xb, wi_ref[...]) + bi_ref[...])
    log_a = (-RG_C) * r * _softplus(-lam_ref[...])
    return jnp.exp(log_a), jnp.sqrt(-_expm1(2.0 * log_a)) * (i * xc)


def _lru_prompt_kernel(x_ref, gr_ref, cw_ref, cb_ref, wr_ref, wi_ref, br_ref, bi_ref, lam_ref,
                       y_ref, conv_ref, h_ref, xbuf, a_scr, h_scr, car):
    t_idx = pl.program_id(1)
    lt = x_ref.shape[0]

    @pl.when(t_idx == 0)
    def _():
        xbuf[0:SUBLANES, :] = jnp.zeros((SUBLANES, LRU_W), f32)
        car[...] = jnp.zeros(car.shape, f32)

    x = x_ref[...]
    xbuf[SUBLANES:SUBLANES + lt, :] = x
    xc = cb_ref[...] + cw_ref[CONV_W - 1:CONV_W, :] * x
    for j in range(CONV_W - 1):
        off = SUBLANES - (CONV_W - 1) + j
        xc = xc + cw_ref[j:j + 1, :] * xbuf[off:off + lt, :]
    xbuf[0:SUBLANES, :] = xbuf[lt:lt + SUBLANES, :]

    a, hh = _lru_gates(xc, wr_ref, wi_ref, br_ref, bi_ref, lam_ref)
    grp = (lt // SUBLANES, SUBLANES, LRU_W)
    a, hh = a.reshape(grp), hh.reshape(grp)
    sub = lax.broadcasted_iota(jnp.int32, (SUBLANES, LRU_W), 0)
    for d in (1, 2, 4):
        m = sub >= d
        hh = hh + jnp.where(m, a, 0.0) * pltpu.roll(hh, d, 1)
        a = a * jnp.where(m, pltpu.roll(a, d, 1), 1.0)
    a_scr[...] = a.reshape(lt, LRU_W)
    h_scr[...] = hh.reshape(lt, LRU_W)

    def group(g, c):
        rows = pl.ds(pl.multiple_of(g * SUBLANES, SUBLANES), SUBLANES)
        hg = h_scr[rows, :] + a_scr[rows, :] * c
        h_scr[rows, :] = hg
        return jnp.broadcast_to(hg[SUBLANES - 1:SUBLANES, :], (SUBLANES, LRU_W))

    c = lax.fori_loop(0, lt // SUBLANES, group, car[...])
    car[...] = c
    y_ref[...] = h_scr[...] * _gelu(gr_ref[...])
    conv_ref[...] = x[lt - (CONV_W - 1):lt, :]
    h_ref[...] = c[0:1, :]


def _lru_prompt(z, xr_cb, gr_cb, p, batch, seq, lt):
    tiles = seq // lt
    blk = lambda cb: pl.BlockSpec((lt, LRU_W), lambda b, t: (b * tiles + t, cb))
    consts = [p["conv_w"], p["conv_b"], p["w_r"], p["w_i"], p["b_r"], p["b_i"], p["lam"]]
    return pl.pallas_call(
        _lru_prompt_kernel,
        out_shape=(jax.ShapeDtypeStruct((batch * seq, LRU_W), f32),
                   jax.ShapeDtypeStruct((batch, CONV_W - 1, LRU_W), f32),
                   jax.ShapeDtypeStruct((batch, 1, LRU_W), f32)),
        grid=(batch, tiles),
        in_specs=[blk(xr_cb), blk(gr_cb)] + [_const_spec(c.shape) for c in consts],
        out_specs=(pl.BlockSpec((lt, LRU_W), lambda b, t: (b * tiles + t, 0)),
                   pl.BlockSpec((None, CONV_W - 1, LRU_W), lambda b, t: (b, 0, 0)),
                   pl.BlockSpec((None, 1, LRU_W), lambda b, t: (b, 0, 0))),
        scratch_shapes=[pltpu.VMEM((lt + SUBLANES, LRU_W), f32), pltpu.VMEM((lt, LRU_W), f32),
                        pltpu.VMEM((lt, LRU_W), f32), pltpu.VMEM((SUBLANES, LRU_W), f32)],
        compiler_params=_params("parallel", "arbitrary"),
        name="lru_prompt",
    )(z, z, *consts)


S5_R = 4
S5_TAB = 8


def _s5_param_kernel(lr_ref, li_ref, ldt_ref, btr_ref, bti_ref, cr_ref, ci_ref,
                     a1r_ref, a1i_ref, pcr_ref, pci_ref, xr_ref, xi_ref, wcr_ref, wci_ref):
    lr_, li_ = lr_ref[...], li_ref[...]
    dt = jnp.exp(ldt_ref[...])
    mag = jnp.exp(lr_ * dt)
    ar, ai = mag * jnp.cos(li_ * dt), mag * jnp.sin(li_ * dt)
    den = lr_ * lr_ + li_ * li_
    nr, ni = ar - 1.0, ai
    fr = (nr * lr_ + ni * li_) / den
    fi = (ni * lr_ - nr * li_) / den
    bbr = fr[:, None, :] * btr_ref[...] - fi[:, None, :] * bti_ref[...]
    bbi = fr[:, None, :] * bti_ref[...] + fi[:, None, :] * btr_ref[...]
    a1r_ref[...], a1i_ref[...] = ar, ai
    cr, ci = cr_ref[...], ci_ref[...]
    pr, pi = jnp.ones_like(ar), jnp.zeros_like(ai)
    for n in range(S5_R * S5_TAB + 1):
        if n < S5_R:
            xr_ref[n] = pr[:, None, :] * bbr - pi[:, None, :] * bbi
            xi_ref[n] = pr[:, None, :] * bbi + pi[:, None, :] * bbr
        if 1 <= n <= S5_R:
            wcr_ref[n - 1] = cr * pr[:, None, :] - ci * pi[:, None, :]
            wci_ref[n - 1] = -(cr * pi[:, None, :] + ci * pr[:, None, :])
        if n >= S5_R and n % S5_R == 0:
            pcr_ref[n // S5_R - 1], pci_ref[n // S5_R - 1] = pr, pi
        pr, pi = pr * ar - pi * ai, pr * ai + pi * ar


def _toeplitz_kernel(x_ref, c_ref, m_ref):
    for j in range(x_ref.shape[0]):
        m_ref[j] = _dot(x_ref[j], c_ref[j])


def _s5_params(lam_re, lam_im, log_dt, b_re, b_im, c_re, c_im):
    btr = jnp.swapaxes(b_re, 1, 2)
    bti = jnp.swapaxes(b_im, 1, 2)
    shp = jax.ShapeDtypeStruct
    gp = (S5_G, S5_P)
    ghp = (S5_G, S5_GH, S5_P)
    a1r, a1i, pcr, pci, xr, xi, wcr, wci = pl.pallas_call(
        _s5_param_kernel,
        out_shape=(shp(gp, f32), shp(gp, f32), shp((S5_TAB,) + gp, f32), shp((S5_TAB,) + gp, f32),
                   shp((S5_R,) + ghp, f32), shp((S5_R,) + ghp, f32), shp((S5_R,) + ghp, f32), shp((S5_R,) + ghp, f32)),
        name="s5_params",
    )(lam_re, lam_im, log_dt.reshape(S5_G, 1), btr, bti, c_re, c_im)
    gpb = LANES // S5_GH
    nblk = S5_G // gpb
    eye = jnp.eye(gpb, dtype=f32)

    def bdiag_in(bb):
        x = bb.reshape(bb.shape[:-3] + (nblk, gpb, S5_GH, S5_P))
        return jnp.einsum('...jghp,gk->...jghkp', x, eye).reshape(bb.shape[:-3] + (nblk, gpb * S5_GH, gpb * S5_P))

    def bdiag_out(cc):
        x = cc.reshape(cc.shape[:-3] + (nblk, gpb, S5_GH, S5_P))
        return jnp.einsum('...jghp,gk->...jgpkh', x, eye).reshape(cc.shape[:-3] + (nblk, gpb * S5_P, gpb * S5_GH))

    x_bd = jnp.concatenate([bdiag_in(xr), bdiag_in(xi)], axis=-1).astype(bf16)
    c_bd = jnp.concatenate([bdiag_out(c_re), bdiag_out(-c_im)], axis=-2).astype(bf16)
    wc_bd = jnp.concatenate([bdiag_out(wcr), bdiag_out(wci)], axis=-2).astype(bf16)
    lag = pl.pallas_call(
        _toeplitz_kernel,
        out_shape=shp((S5_R, nblk, LANES, LANES), f32),
        grid=(S5_R,),
        in_specs=[pl.BlockSpec((None, nblk, LANES, 2 * gpb * S5_P), lambda d: (d, 0, 0, 0)),
                  _const_spec(c_bd.shape)],
        out_specs=pl.BlockSpec((None, nblk, LANES, LANES), lambda d: (d, 0, 0, 0)),
        compiler_params=_params("parallel"),
        name="s5_toeplitz",
    )(x_bd, c_bd)
    zero = jnp.zeros((nblk, LANES, LANES), f32)
    toep = jnp.concatenate(
        [jnp.concatenate([lag[r - s] if r >= s else zero for r in range(S5_R)], axis=2) for s in range(S5_R)],
        axis=1).astype(bf16)
    w_x = jnp.concatenate([x_bd[S5_R - 1 - s] for s in range(S5_R)], axis=1)
    w_c = jnp.concatenate([wc_bd[r] for r in range(S5_R)], axis=2)
    return dict(a1r=a1r.reshape(1, S5_N), a1i=a1i.reshape(1, S5_N),
                pcr=pcr.reshape(S5_TAB, S5_N), pci=pci.reshape(S5_TAB, S5_N),
                b_bd=x_bd[0], c_bd=c_bd, w_x=w_x, toep=toep, w_c=w_c)


S5_BLK = S5_N // (S5_G // (LANES // S5_GH))


def _s5_input(u, bbd_ref, j):
    bu = _dot(u[:, j * LANES:(j + 1) * LANES].astype(bf16), bbd_ref[j])
    return bu[:, 0:S5_BLK], bu[:, S5_BLK:2 * S5_BLK]


def _s5_output(u, hr, hi, cbd_ref, d_ref, wg_ref, bg_ref):
    ys = []
    for j in range(S5_N // S5_BLK):
        cs = slice(j * S5_BLK, (j + 1) * S5_BLK)
        hcat = jnp.concatenate([hr[:, cs], hi[:, cs]], axis=1).astype(bf16)
        ys.append(_dot(hcat, cbd_ref[j]))
    y = _gelu(jnp.concatenate(ys, axis=1) + d_ref[...] * u)
    return y * _sigmoid(_dot(y.astype(bf16), wg_ref[...]) + bg_ref[...])


def _s5_prompt_kernel(*refs):
    nblk = S5_N // S5_BLK
    u_refs = refs[:nblk]
    wx_ref, tp_ref, wc_ref, pcr_ref, pci_ref, d_ref, wg_ref, bg_ref = refs[nblk:nblk + 8]
    y_refs = refs[nblk + 8:2 * nblk + 8]
    sr_ref, si_ref, er_s, ei_s, y_s = refs[2 * nblk + 8:]
    t_idx = pl.program_id(1)
    cr = u_refs[0].shape[0] // S5_R
    gw = S5_R * LANES

    @pl.when(t_idx == 0)
    def _():
        er_s[0:SUBLANES, :] = jnp.zeros((SUBLANES, S5_N), f32)
        ei_s[0:SUBLANES, :] = jnp.zeros((SUBLANES, S5_N), f32)

    def token(s):
        return pl.ds(s, cr, stride=S5_R)

    sub = lax.broadcasted_iota(jnp.int32, (SUBLANES, S5_BLK), 0)
    for j in range(nblk):
        cs = slice(j * S5_BLK, (j + 1) * S5_BLK)
        uj = jnp.concatenate([u_refs[j][token(s), :] for s in range(S5_R)], axis=1).astype(bf16)
        x = _dot(uj, wx_ref[j])
        y_s[:, j * gw:(j + 1) * gw] = _dot(uj, tp_ref[j])
        hr = x[:, 0:S5_BLK].reshape(cr // SUBLANES, SUBLANES, S5_BLK)
        hi = x[:, S5_BLK:2 * S5_BLK].reshape(cr // SUBLANES, SUBLANES, S5_BLK)
        for d in (1, 2, 4):
            pr = jnp.where(sub >= d, pcr_ref[d - 1:d, cs], 0.0)
            pi = jnp.where(sub >= d, pci_ref[d - 1:d, cs], 0.0)
            sr, si = pltpu.roll(hr, d, 1), pltpu.roll(hi, d, 1)
            hr, hi = hr + (pr * sr - pi * si), hi + (pr * si + pi * sr)
        er_s[SUBLANES:SUBLANES + cr, cs] = hr.reshape(cr, S5_BLK)
        ei_s[SUBLANES:SUBLANES + cr, cs] = hi.reshape(cr, S5_BLK)

    tr, ti = pcr_ref[...], pci_ref[...]
    last = pl.ds(SUBLANES - 1, 1)
    c0 = (jnp.broadcast_to(er_s[last, :], (SUBLANES, S5_N)), jnp.broadcast_to(ei_s[last, :], (SUBLANES, S5_N)))

    def group(g, c):
        c_r, c_i = c
        rows = pl.ds(pl.multiple_of((g + 1) * SUBLANES, SUBLANES), SUBLANES)
        g_r = er_s[rows, :] + tr * c_r - ti * c_i
        g_i = ei_s[rows, :] + tr * c_i + ti * c_r
        er_s[rows, :] = g_r
        ei_s[rows, :] = g_i
        return (jnp.broadcast_to(g_r[SUBLANES - 1:SUBLANES, :], (SUBLANES, S5_N)),
                jnp.broadcast_to(g_i[SUBLANES - 1:SUBLANES, :], (SUBLANES, S5_N)))

    c_r, c_i = lax.fori_loop(0, cr // SUBLANES, group, c0)

    for j in range(nblk):
        cs = slice(j * S5_BLK, (j + 1) * S5_BLK)
        prev = pl.ds(SUBLANES - 1, cr)
        sj = jnp.concatenate([er_s[prev, cs], ei_s[prev, cs]], axis=1).astype(bf16)
        y_s[:, j * gw:(j + 1) * gw] += _dot(sj, wc_ref[j])
    er_s[last, :] = c_r[0:1, :]
    ei_s[last, :] = c_i[0:1, :]
    sr_ref[...] = c_r[0:1, :]
    si_ref[...] = c_i[0:1, :]

    for r in range(S5_R):
        y = jnp.concatenate([y_s[:, j * gw + r * LANES:j * gw + (r + 1) * LANES] for j in range(nblk)], axis=1)
        u_r = jnp.concatenate([u_refs[j][token(r), :] for j in range(nblk)], axis=1)
        y = _gelu(y + d_ref[...] * u_r)
        y = y * _sigmoid(_dot(y.astype(bf16), wg_ref[...]) + bg_ref[...])
        for j in range(nblk):
            y_refs[j][token(r), :] = y[:, j * LANES:(j + 1) * LANES]


def _s5_prompt(us, tabs, p, batch, seq, cr):
    nblk = len(us)
    tiles = seq // (S5_R * cr)
    consts = [tabs["w_x"], tabs["toep"], tabs["w_c"], tabs["pcr"], tabs["pci"], p["d"], p["w_glu"], p["b_glu"]]
    blk = pl.BlockSpec((S5_R * cr, LANES), lambda b, t: (b * tiles + t, 0))
    state = pl.BlockSpec((None, 1, S5_N), lambda b, t: (b, 0, 0))
    outs = pl.pallas_call(
        _s5_prompt_kernel,
        out_shape=tuple(jax.ShapeDtypeStruct((batch * seq, LANES), f32) for _ in range(nblk))
        + (jax.ShapeDtypeStruct((batch, 1, S5_N), f32), jax.ShapeDtypeStruct((batch, 1, S5_N), f32)),
        grid=(batch, tiles),
        in_specs=[blk] * nblk + [_const_spec(c.shape) for c in consts],
        out_specs=(blk,) * nblk + (state, state),
        scratch_shapes=[pltpu.VMEM((cr + SUBLANES, S5_N), f32), pltpu.VMEM((cr + SUBLANES, S5_N), f32),
                        pltpu.VMEM((cr, S5_R * S5_W), f32)],
        compiler_params=_params("parallel", "arbitrary"),
        name="s5_prompt",
    )(*us, *consts)
    return list(outs[:nblk]), outs[nblk], outs[nblk + 1]


def _gla_step_kernel(kind, dk, layer, *refs):
    if kind == "gla":
        q_ref, k_ref, v_ref, gate_ref, lr_ref, wlr_ref, blr_ref, ng_ref, s_ref, y_ref, so_ref, o_scr = refs
        sl = pl.program_id(0)
        q = q_ref[...] * (GLA_DK ** -0.5)
        k = k_ref[...]
        pre_all = _dot(lr_ref[...].astype(bf16), wlr_ref[...]) + blr_ref[...]
        pre = jnp.where(sl == 0, pre_all[:, 0:LANES], pre_all[:, LANES:2 * LANES])
        a = jnp.exp(_log_sigmoid(pre) * (1.0 / GLA_TAU))
    else:
        q_ref, f_ref, v_ref, gate_ref, lg_ref, ng_ref, s_ref, y_ref, so_ref, o_scr = refs
        sl = pl.program_id(0)
        lb_all = _hgrn_lower_bound(lg_ref, layer)
        lb = lb_all[:, 0:LANES]
        for j in range(1, HG_H):
            lb = jnp.where(sl == j, lb_all[:, j * LANES:(j + 1) * LANES], lb)
        a = lb + (1.0 - lb) * _sigmoid(f_ref[...])
        q = _silu(q_ref[...])
        k = 1.0 - a
    hps = LANES // dk
    nb = q.shape[0]
    qat, kt, at = (q * a).T, k.T, a.T
    lane = lax.broadcasted_iota(jnp.int32, (1, LANES), 1)
    v = v_ref[...]
    for hh in range(hps):
        r0 = hh * dk
        msk = (lane >= r0) & (lane < r0 + dk)
        qk = jnp.sum(jnp.where(msk, q * k, 0.0), axis=-1, keepdims=True)
        for b in range(nb):
            s = s_ref[b, hh]
            col = lambda x: jnp.broadcast_to(x[r0:r0 + dk, b:b + 1], (dk, LANES))
            vr = v[b:b + 1, hh * LANES:(hh + 1) * LANES]
            so_ref[b, hh] = col(at) * s + col(kt) * vr
            oi = jnp.sum(col(qat) * s, axis=0, keepdims=True)
            o_scr[b:b + 1, hh * LANES:(hh + 1) * LANES] = jnp.broadcast_to(qk[b:b + 1, :], (1, LANES)) * vr + oi
    for hh in range(hps):
        cs = slice(hh * LANES, (hh + 1) * LANES)
        o = o_scr[:, cs]
        o = o * lax.rsqrt(jnp.mean(o * o, axis=-1, keepdims=True) + RMS_EPS) * ng_ref[...]
        y_ref[:, cs] = o * _silu(gate_ref[:, cs])


def _gla_step(kind, z, extra, norm_g, state, layer):
    nb = z.shape[0]
    dk = GLA_DK if kind == "gla" else HG_D
    n_heads = GLA_H if kind == "gla" else HG_H
    hps = LANES // dk
    n_slabs = n_heads // hps
    vw = hps * LANES
    zb = lambda width, cb0: pl.BlockSpec((nb, width), lambda s: (0, cb0 + s))
    if kind == "gla":
        wlr, blr = extra
        ins = [z, z, z, z, z, wlr, blr]
        specs = [zb(LANES, 0), zb(LANES, GLA_KW // LANES), zb(vw, GLA_VW // vw * 1), zb(vw, GLA_VW // vw * 2),
                 pl.BlockSpec((nb, LANES), lambda s: (0, (EVEN_N - LANES) // LANES)),
                 _const_spec(wlr.shape), _const_spec(blr.shape)]
    else:
        (lg,) = extra
        ins = [z, z, z, z, lg]
        specs = [zb(LANES, 0), zb(LANES, HG_W // LANES), zb(vw, HG_W // vw * 2),
                 zb(vw, HG_W // vw * 3), _const_spec(lg.shape)]
    ins += [norm_g, state]
    specs += [_const_spec(norm_g.shape), pl.BlockSpec((nb, hps, dk, LANES), lambda s: (0, s, 0, 0))]
    return pl.pallas_call(
        functools.partial(_gla_step_kernel, kind, dk, layer),
        out_shape=(jax.ShapeDtypeStruct((nb, HALF), f32), jax.ShapeDtypeStruct(state.shape, f32)),
        grid=(n_slabs,),
        in_specs=specs,
        out_specs=(pl.BlockSpec((nb, vw), lambda s: (0, s)),
                   pl.BlockSpec((nb, hps, dk, LANES), lambda s: (0, s, 0, 0))),
        scratch_shapes=[pltpu.VMEM((nb, vw), f32)],
        compiler_params=_params("parallel"),
        name=kind + "_step",
    )(*ins)


def _lru_step_kernel(x_ref, gr_ref, c0_ref, c1_ref, c2_ref, h0_ref, cw_ref, cb_ref, wr_ref, wi_ref,
                     br_ref, bi_ref, lam_ref, y_ref, h_ref):
    xc = (cb_ref[...] + cw_ref[0:1, :] * c0_ref[...] + cw_ref[1:2, :] * c1_ref[...]
          + cw_ref[2:3, :] * c2_ref[...] + cw_ref[3:4, :] * x_ref[...])
    a, u = _lru_gates(xc, wr_ref, wi_ref, br_ref, bi_ref, lam_ref)
    h = a * h0_ref[...] + u
    h_ref[...] = h
    y_ref[...] = h * _gelu(gr_ref[...])


def _lru_step(z, xr_cb, gr_cb, conv_rows, h0, p):
    nb = z.shape[0]
    full = lambda: pl.BlockSpec((nb, LRU_W), lambda i: (0, 0))
    consts = [p["conv_w"], p["conv_b"], p["w_r"], p["w_i"], p["b_r"], p["b_i"], p["lam"]]
    return pl.pallas_call(
        _lru_step_kernel,
        out_shape=(jax.ShapeDtypeStruct((nb, LRU_W), f32), jax.ShapeDtypeStruct((nb, LRU_W), f32)),
        grid=(1,),
        in_specs=[pl.BlockSpec((nb, LRU_W), lambda i: (0, xr_cb)), pl.BlockSpec((nb, LRU_W), lambda i: (0, gr_cb)),
                  full(), full(), full(), full()] + [_const_spec(c.shape) for c in consts],
        out_specs=(full(), full()),
        compiler_params=_params("arbitrary"),
        name="lru_step",
    )(z, z, *conv_rows, h0, *consts)


def _s5_step_kernel(u_ref, s0r_ref, s0i_ref, a1r_ref, a1i_ref, bbd_ref, cbd_ref, d_ref, wg_ref, bg_ref,
                    y_ref, sr_ref, si_ref):
    u = u_ref[...]
    for j in range(S5_N // S5_BLK):
        cs = slice(j * S5_BLK, (j + 1) * S5_BLK)
        bur, bui = _s5_input(u, bbd_ref, j)
        ar, ai = a1r_ref[:, cs], a1i_ref[:, cs]
        s0r, s0i = s0r_ref[:, cs], s0i_ref[:, cs]
        sr_ref[:, cs] = bur + ar * s0r - ai * s0i
        si_ref[:, cs] = bui + ar * s0i + ai * s0r
    y_ref[...] = _s5_output(u, sr_ref[...], si_ref[...], cbd_ref, d_ref, wg_ref, bg_ref)


def _s5_step(u, s0r, s0i, tabs, p):
    nb = u.shape[0]
    consts = [tabs["a1r"], tabs["a1i"], tabs["b_bd"], tabs["c_bd"], p["d"], p["w_glu"], p["b_glu"]]
    st = lambda: pl.BlockSpec((nb, S5_N), lambda i: (0, 0))
    return pl.pallas_call(
        _s5_step_kernel,
        out_shape=(jax.ShapeDtypeStruct((nb, S5_W), f32), jax.ShapeDtypeStruct((nb, S5_N), f32),
                   jax.ShapeDtypeStruct((nb, S5_N), f32)),
        grid=(1,),
        in_specs=[pl.BlockSpec((nb, S5_W), lambda i: (0, 0)), st(), st()] + [_const_spec(c.shape) for c in consts],
        out_specs=(pl.BlockSpec((nb, S5_W), lambda i: (0, 0)), st(), st()),
        compiler_params=_params("arbitrary"),
        name="s5_step",
    )(u, s0r, s0i, *consts)


def _block_diag(w):
    h, n, _ = w.shape
    return jnp.einsum('hij,hk->hikj', w, jnp.eye(h, dtype=w.dtype)).reshape(h * n, h * n)


def kernel(x_prompt, x_sample, c_prompt, c_sample, state_gla, state_rglru_conv, state_rglru_h, state_s5_re, state_s5_im, state_hgrn, ev_w_in, ev_gla_w_lr, ev_gla_b_lr, ev_gla_norm_g, ev_conv_w, ev_conv_b, ev_lru_w_r, ev_lru_b_r, ev_lru_w_i, ev_lru_b_i, ev_lru_lam, ev_w_out, od_w_in, od_s5_lam_re, od_s5_lam_im, od_s5_log_dt, od_s5_b_re, od_s5_b_im, od_s5_c_re, od_s5_c_im, od_s5_d, od_s5_w_glu, od_s5_b_glu, hg_lb_logits, od_hg_norm_g, od_w_out, w_ada, b_ada, ln_g, ln_b, ffn_w_in, ffn_w_out):
    bp, seq = x_prompt.shape[0], x_prompt.shape[1]
    bs = x_sample.shape[0]
    assert x_sample.shape[1] == 1 and seq % CHUNK == 0
    tp = bp * seq
    tm = min(512, seq)
    lt = min(512, seq)
    lt_gla = min(2048, seq)
    cr_s5 = min(256, seq // S5_R)
    row2 = lambda a: a.reshape(1, -1)

    mod = _modulation(jnp.concatenate([c_prompt, c_sample], axis=0).astype(f32), w_ada, b_ada)
    mod_p = mod[:, :bp].reshape(DEPTH, bp, 1, 6 * D_MODEL)
    mod_s = mod[:, bp:].reshape(DEPTH, 1, bs, 6 * D_MODEL)

    xp = x_prompt.astype(f32).reshape(tp, D_MODEL)
    xs = x_sample.astype(f32).reshape(bs, D_MODEL)
    outs_p = {k: [] for k in ("gla", "conv", "lru", "re", "im", "hg")}
    outs_s = {k: [] for k in ("gla", "conv", "lru", "re", "im", "hg")}

    w_f_in = ffn_w_in.astype(bf16)
    w_f_out = ffn_w_out.astype(bf16)
    for l in range(DEPTH):
        if l % 2 == 0:
            e = l // 2
            w = ev_w_in[e]
            lr0 = 2 * GLA_KW + 2 * GLA_VW
            w_in = jnp.concatenate(
                [w[:, :lr0], w[:, lr0 + GLA_LR:], w[:, lr0:lr0 + GLA_LR],
                 jnp.zeros((D_MODEL, EVEN_N - w.shape[1]), w.dtype)], axis=1).astype(bf16)
            w_out = ev_w_out[e].astype(bf16)
            wlr = jnp.concatenate([ev_gla_w_lr[e], jnp.zeros((LANES - GLA_LR, GLA_KW), f32)], axis=0).astype(bf16)
            gla_extra = (wlr, row2(ev_gla_b_lr[e]))
            ng = row2(ev_gla_norm_g[e])
            lru_p = dict(conv_w=ev_conv_w[e], conv_b=row2(ev_conv_b[e]),
                         w_r=_block_diag(ev_lru_w_r[e]).astype(bf16), w_i=_block_diag(ev_lru_w_i[e]).astype(bf16),
                         b_r=row2(ev_lru_b_r[e]), b_i=row2(ev_lru_b_i[e]), lam=row2(ev_lru_lam[e]))
            xr_cb, gr_cb = lr0 // LRU_W, lr0 // LRU_W + 1
            (z,) = _inproj(xp, mod_p[l], w_in, tm, seq // tm, (EVEN_N,))
            ya, s_gla = _gla_prompt("gla", z, (EVEN_N - LANES) // LANES, gla_extra, ng, bp, seq, lt_gla, l)
            yb, s_conv, s_h = _lru_prompt(z, xr_cb, gr_cb, lru_p, bp, seq, lt)
            xp = _outffn(xp, [ya, yb], mod_p[l], w_out, ln_g[l], ln_b[l], w_f_in, w_f_out, l, tm, seq // tm)
            outs_p["gla"].append(s_gla)
            outs_p["conv"].append(s_conv)
            outs_p["lru"].append(s_h.reshape(bp, LRU_W))
            (z,) = _inproj(xs, mod_s[l], w_in, bs, 1, (EVEN_N,))
            ya, s_gla = _gla_step("gla", z, gla_extra, ng, state_gla[e], l)
            cs = state_rglru_conv[e].astype(f32)
            yb, s_h = _lru_step(z, xr_cb, gr_cb, [cs[:, j] for j in range(CONV_W - 1)],
                                state_rglru_h[e].astype(f32), lru_p)
            xs = _outffn(xs, [ya, yb], mod_s[l], w_out, ln_g[l], ln_b[l], w_f_in, w_f_out, l, bs, 1)
            outs_s["gla"].append(s_gla)
            outs_s["conv"].append(jnp.stack([cs[:, 1], cs[:, 2], z[:, xr_cb * LRU_W:(xr_cb + 1) * LRU_W]], axis=1))
            outs_s["lru"].append(s_h)
        else:
            o = l // 2
            w_in = od_w_in[o].astype(bf16)
            w_out = od_w_out[o].astype(bf16)
            tabs = _s5_params(od_s5_lam_re[o], od_s5_lam_im[o], od_s5_log_dt[o], od_s5_b_re[o], od_s5_b_im[o],
                              od_s5_c_re[o], od_s5_c_im[o])
            s5_p = dict(d=row2(od_s5_d[o]), w_glu=od_s5_w_glu[o].astype(bf16), b_glu=row2(od_s5_b_glu[o]))
            hg_extra = (hg_lb_logits.astype(f32),)
            ng = row2(od_hg_norm_g[o])
            *us, z = _inproj(xp, mod_p[l], w_in, tm, seq // tm, (LANES,) * (S5_W // LANES) + (ODD_N - S5_W,))
            yas, s_re, s_im = _s5_prompt(us, tabs, s5_p, bp, seq, cr_s5)
            yb, s_hg = _gla_prompt("hgrn", z, 0, hg_extra, ng, bp, seq, lt_gla, l)
            xp = _outffn(xp, yas + [yb], mod_p[l], w_out, ln_g[l], ln_b[l], w_f_in, w_f_out, l, tm, seq // tm)
            outs_p["re"].append(s_re.reshape(bp, S5_G, S5_P))
            outs_p["im"].append(s_im.reshape(bp, S5_G, S5_P))
            outs_p["hg"].append(s_hg)
            u, z = _inproj(xs, mod_s[l], w_in, bs, 1, (S5_W, ODD_N - S5_W))
            ya, s_re, s_im = _s5_step(u, state_s5_re[o].astype(f32).reshape(bs, S5_N),
                                      state_s5_im[o].astype(f32).reshape(bs, S5_N), tabs, s5_p)
            yb, s_hg = _gla_step("hgrn", z, hg_extra, ng, state_hgrn[o], l)
            xs = _outffn(xs, [ya, yb], mod_s[l], w_out, ln_g[l], ln_b[l], w_f_in, w_f_out, l, bs, 1)
            outs_s["re"].append(s_re.reshape(bs, S5_G, S5_P))
            outs_s["im"].append(s_im.reshape(bs, S5_G, S5_P))
            outs_s["hg"].append(s_hg)

    st = lambda d, k: d[k][0][None] if len(d[k]) == 1 else jnp.stack(d[k])
    return (xp.reshape(bp, seq, D_MODEL).astype(x_prompt.dtype), xs.reshape(bs, 1, D_MODEL).astype(x_sample.dtype),
            st(outs_p, "gla"), st(outs_s, "gla"), st(outs_p, "conv"), st(outs_s, "conv"),
            st(outs_p, "lru"), st(outs_s, "lru"), st(outs_p, "re"), st(outs_s, "re"),
            st(outs_p, "im"), st(outs_s, "im"), st(outs_p, "hg"), st(outs_s, "hg"))
```

```python
import functools
import math

import numpy as np
import jax
import jax.numpy as jnp
from jax import lax
from jax.experimental import pallas as pl
from jax.experimental.pallas import tpu as pltpu

f32 = jnp.float32
bf16 = jnp.bfloat16

D_MODEL = 1024
DEPTH = 2
HALF = D_MODEL // 2
GLA_H = 4
GLA_DV = HALF // GLA_H
GLA_DK = GLA_DV // 2
GLA_KW = GLA_H * GLA_DK
GLA_VW = GLA_H * GLA_DV
GLA_LR = 16
GLA_TAU = 16.0
LRU_W = HALF
LRU_H = 8
LRU_BW = LRU_W // LRU_H
CONV_W = 4
RG_C = 8.0
S5_W = HALF
S5_GH = 16
S5_G = S5_W // S5_GH
S5_P = 64
S5_N = S5_G * S5_P
HG_H = 4
HG_D = HALF // HG_H
HG_W = HG_H * HG_D
D_FF = ((8 * D_MODEL // 3 + 255) // 256) * 256
ALPHA = (2.0 * DEPTH) ** 0.25
LN_EPS = 1e-5
RMS_EPS = 1e-6

LANES = 128
SUBLANES = 8
VMEM_LIMIT = 56 * 1024 * 1024

CHUNK = 64
N_LEVELS = 6
FINE_LEVELS = 3
GLA_UNROLL = 3
FF_CHUNK = 256
EVEN_Z = 2 * GLA_KW + 2 * GLA_VW + LANES
EVEN_N = EVEN_Z + 2 * LRU_W
GLA_LR_BLOCK = (EVEN_Z - LANES) // LANES
ODD_N = 2560


def _sigmoid(x):
    return 0.5 * jnp.tanh(0.5 * x) + 0.5


def _silu(x):
    return x * _sigmoid(x)


def _gelu(x):
    c = math.sqrt(2.0 / math.pi)
    return x * (0.5 * (1.0 + jnp.tanh(c * (x + 0.044715 * (x * x * x)))))


def _softplus(x):
    return jnp.maximum(x, 0.0) + jnp.log1p(jnp.exp(-jnp.abs(x)))


def _log_sigmoid(x):
    return -_softplus(-x)


def _expm1(x):
    u = jnp.exp(x)
    um1 = u - 1.0
    lg = jnp.log(u)
    safe = jnp.where(lg == 0.0, 1.0, lg)
    return jnp.where(um1 == 0.0, x, jnp.where(um1 == -1.0, -1.0, um1 * x / safe))


def _layer_norm(x, g, b):
    mu = jnp.mean(x, axis=-1, keepdims=True)
    xc = x - mu
    var = jnp.mean(xc * xc, axis=-1, keepdims=True)
    return xc * lax.rsqrt(var + LN_EPS) * g + b


def _dot(a, b):
    return jnp.dot(a, b, preferred_element_type=f32)


def _dot_nt(a, b):
    return lax.dot_general(a, b, (((1,), (1,)), ((), ())), preferred_element_type=f32)


def _dot_tn(a, b):
    return lax.dot_general(a, b, (((0,), (0,)), ((), ())), preferred_element_type=f32)


def _const_spec(shape):
    nd = len(shape)
    return pl.BlockSpec(shape, lambda *_: (0,) * nd, pipeline_mode=pl.Buffered(1))


def _params(*sem):
    return pltpu.CompilerParams(dimension_semantics=sem, vmem_limit_bytes=VMEM_LIMIT)


def _mod_kernel(c_ref, w_ref, b_ref, o_ref):
    cond = _silu(c_ref[...]).astype(bf16)
    o_ref[...] = _dot(cond, w_ref[...].astype(bf16)) + b_ref[...]


def _modulation(c_all, w_ada, b_ada):
    rows = c_all.shape[0]
    tn = 1536
    return pl.pallas_call(
        _mod_kernel,
        out_shape=jax.ShapeDtypeStruct((DEPTH, rows, 6 * D_MODEL), f32),
        grid=(DEPTH, 6 * D_MODEL // tn),
        in_specs=[pl.BlockSpec((rows, D_MODEL), lambda l, j: (0, 0)),
                  pl.BlockSpec((None, D_MODEL, tn), lambda l, j: (l, 0, j)),
                  pl.BlockSpec((None, 1, tn), lambda l, j: (l, 0, j))],
        out_specs=pl.BlockSpec((None, rows, tn), lambda l, j: (l, 0, j)),
        compiler_params=_params("parallel", "parallel"),
        name="adaln_mod",
    )(c_all, w_ada, b_ada.reshape(DEPTH, 1, 6 * D_MODEL))


def _mod_spec(rows_per_block, tiles_per_batch, j):
    return pl.BlockSpec((None, rows_per_block, D_MODEL), lambda i: (i // tiles_per_batch, 0, j))


def _inproj_kernel(x_ref, sc_ref, sh_ref, w_ref, *z_refs):
    h = (x_ref[...] * (1.0 + sc_ref[...]) + sh_ref[...]).astype(bf16)
    lo = 0
    for z_ref in z_refs:
        z_ref[...] = _dot(h, w_ref[:, lo:lo + z_ref.shape[1]])
        lo += z_ref.shape[1]


def _inproj(x, mod, w, tm, tiles_per_batch, widths):
    t, n = x.shape[0], w.shape[1]
    assert sum(widths) == n
    r = mod.shape[1]
    return pl.pallas_call(
        _inproj_kernel,
        out_shape=tuple(jax.ShapeDtypeStruct((t, wd), f32) for wd in widths),
        grid=(t // tm,),
        in_specs=[pl.BlockSpec((tm, D_MODEL), lambda i: (i, 0)),
                  _mod_spec(r, tiles_per_batch, 1), _mod_spec(r, tiles_per_batch, 0),
                  _const_spec((D_MODEL, n))],
        out_specs=tuple(pl.BlockSpec((tm, wd), lambda i: (i, 0)) for wd in widths),
        compiler_params=_params("parallel"),
        name="in_proj",
    )(x, mod, mod, w)


def _outffn_kernel(n_y, x_ref, *refs):
    y_refs = refs[:n_y]
    gtm_ref, shf_ref, scf_ref, gtf_ref, wo_ref, lng_ref, lnb_ref, wi_ref, wf_ref, o_ref = refs[n_y:]
    y = jnp.concatenate([r[...].astype(bf16) for r in y_refs], axis=1)
    mix = _dot(y, wo_ref[...])
    x1 = _layer_norm(ALPHA * x_ref[...] + (1.0 + gtm_ref[...]) * mix, lng_ref[0:1, :], lnb_ref[0:1, :])
    h = (x1 * (1.0 + scf_ref[...]) + shf_ref[...]).astype(bf16)
    acc = jnp.zeros(x1.shape, f32)
    for c in range(D_FF // FF_CHUNK):
        lo = c * FF_CHUNK
        gate = _dot(h, wi_ref[:, lo:lo + FF_CHUNK])
        up = _dot(h, wi_ref[:, D_FF + lo:D_FF + lo + FF_CHUNK])
        act = (_silu(gate) * up).astype(bf16)
        acc = acc + _dot(act, wf_ref[lo:lo + FF_CHUNK, :])
    o_ref[...] = _layer_norm(ALPHA * x1 + (1.0 + gtf_ref[...]) * acc, lng_ref[1:2, :], lnb_ref[1:2, :])


def _outffn(x, ys, mod, w_out, ln_g, ln_b, w_in, w_f, layer, tm, tiles_per_batch):
    t = x.shape[0]
    r = mod.shape[1]
    assert sum(y.shape[1] for y in ys) == D_MODEL
    row = lambda w: pl.BlockSpec((tm, w), lambda i: (i, 0))
    stacked = lambda a: pl.BlockSpec((None,) + a.shape[1:], lambda i: (layer, 0, 0), pipeline_mode=pl.Buffered(1))
    return pl.pallas_call(
        functools.partial(_outffn_kernel, len(ys)),
        out_shape=jax.ShapeDtypeStruct((t, D_MODEL), f32),
        grid=(t // tm,),
        in_specs=[row(D_MODEL)] + [row(y.shape[1]) for y in ys] + [
                  _mod_spec(r, tiles_per_batch, 2), _mod_spec(r, tiles_per_batch, 3),
                  _mod_spec(r, tiles_per_batch, 4), _mod_spec(r, tiles_per_batch, 5),
                  _const_spec((D_MODEL, D_MODEL)), _const_spec((2, D_MODEL)), _const_spec((2, D_MODEL)),
                  stacked(w_in), stacked(w_f)],
        out_specs=row(D_MODEL),
        compiler_params=_params("parallel"),
        name="outproj_ffn",
    )(x, *ys, mod, mod, mod, mod, w_out, ln_g, ln_b, w_in, w_f)


def _level_tables():
    c = CHUNK
    t = np.arange(c)[:, None]
    u = np.arange(c)[None, :]
    pmask, sgn = [], []
    for lvl in range(N_LEVELS):
        m = 1 << lvl
        right = (t % (2 * m)) >= m
        pm = right & ((u % (2 * m)) < m) & ((t // (2 * m)) == (u // (2 * m)))
        pmask.append(pm.astype(np.float32))
        sgn.append(np.broadcast_to(np.where(right, 1.0, -1.0).astype(np.float32), (c, LANES)))
    tri = (u <= t).astype(np.float32)
    tri3 = np.concatenate([tri, tri, tri], axis=1)
    return (jnp.asarray(tri3, dtype=bf16), jnp.asarray(np.stack(pmask), dtype=f32),
            jnp.asarray(np.stack(sgn), dtype=f32))


LOG2E = 1.0 / math.log(2.0)


def _anchor(b, lvl):
    m = 1 << lvl
    b3 = b.reshape(CHUNK // SUBLANES, SUBLANES, LANES)
    if 4 * m == SUBLANES:
        sub = lax.broadcasted_iota(jnp.int32, b3.shape, 1)
        a = jnp.where(sub < 2 * m, jnp.broadcast_to(b3[:, m - 1:m, :], b3.shape),
                      jnp.broadcast_to(b3[:, 3 * m - 1:3 * m, :], b3.shape))
        return a.reshape(CHUNK, LANES)
    assert 2 * m == SUBLANES
    return jnp.broadcast_to(b3[:, m - 1:m, :], b3.shape).reshape(CHUNK, LANES)


def _split3(g):
    g1 = g.astype(bf16)
    r1 = g - g1.astype(f32)
    g2 = r1.astype(bf16)
    r2 = r1 - g2.astype(f32)
    return jnp.concatenate([g1, g2, r2.astype(bf16)], axis=0)


def _hgrn_lower_bound(lg_ref, layer):
    rows = [lg_ref[i:i + 1, :] for i in range(DEPTH)]
    mx = functools.reduce(jnp.maximum, rows)
    ex = [jnp.exp(r - mx) for r in rows]
    den = functools.reduce(lambda a, b: a + b, ex)
    sm = [e / den for e in ex]
    cs = sm[0]
    for i in range(1, layer + 1):
        cs = cs + sm[i]
    return cs - sm[0]


def _gla_prologue(kind, refs, rows):
    if kind == "gla":
        qk_ref, lr_ref, wlr_ref, blr_ref = refs
        q = qk_ref[rows, 0:GLA_KW] * (GLA_DK ** -0.5)
        k = qk_ref[rows, GLA_KW:2 * GLA_KW]
        pre = _dot(lr_ref[rows, :].astype(bf16), wlr_ref[...]) + blr_ref[...]
        g = _log_sigmoid(pre) * (LOG2E / GLA_TAU)
        return q, k, g
    q_ref, f_ref, lb = refs
    forget = lb + (1.0 - lb) * _sigmoid(f_ref[rows, :])
    return _silu(q_ref[rows, :]), 1.0 - forget, jnp.log2(forget)


def _gla_chunk_kernel(kind, dk, n_heads, layer, *refs):
    if kind == "gla":
        (qk_ref, v_ref, gate_ref, lr_ref, wlr_ref, blr_ref, ng_ref, tri_ref, pm_ref, sg_ref,
         y_ref, s_ref, st_ref) = refs
        pro = (qk_ref, lr_ref, wlr_ref, blr_ref)
    else:
        (q_ref, f_ref, v_ref, gate_ref, lg_ref, ng_ref, tri_ref, pm_ref, sg_ref,
         y_ref, s_ref, st_ref) = refs
        pro = (q_ref, f_ref, _hgrn_lower_bound(lg_ref, layer))
    t_idx = pl.program_id(1)
    n_tiles = pl.num_programs(1)
    lt = y_ref.shape[0]
    hps = LANES // dk
    n_slabs = n_heads // hps
    lane = lax.broadcasted_iota(jnp.int32, (1, LANES), 1)
    eye = (lax.broadcasted_iota(jnp.int32, (CHUNK, CHUNK), 0)
           == lax.broadcasted_iota(jnp.int32, (CHUNK, CHUNK), 1))

    @pl.when(t_idx == 0)
    def _():
        st_ref[...] = jnp.zeros(st_ref.shape, f32)

    def head_mask(hh):
        if hps == 1:
            return lambda a: a
        msk = (lane >= hh * dk) & (lane < (hh + 1) * dk)
        return lambda a: jnp.where(msk, a, jnp.zeros_like(a))

    def rows_of(c):
        return pl.ds(pl.multiple_of(c * CHUNK, CHUNK), CHUNK)

    def cumsums(qkg):
        q_all, k_all, g_all = qkg
        out = []
        for sl in range(n_slabs):
            cs = slice(sl * LANES, (sl + 1) * LANES)
            q_s, k_s, g_s = q_all[:, cs], k_all[:, cs], g_all[:, cs]
            out.append((q_s, k_s, g_s, _dot(tri_ref[...], _split3(g_s))))
        return out

    def score_dots(pre):
        raw, qbs, ksts, decs, rds = [], [], [], [], []
        right = [sg_ref[lvl] > 0.0 for lvl in range(FINE_LEVELS)]
        for sl in range(n_slabs):
            q_s, k_s, g_s, b = pre[sl]
            ys = [(jnp.where(right[0], jnp.exp2(g_s) * q_s, k_s)).astype(bf16)]
            for lvl in range(1, FINE_LEVELS):
                x = jnp.exp2((b - _anchor(b, lvl)) * sg_ref[lvl])
                ys.append((x * jnp.where(right[lvl], q_s, k_s)).astype(bf16))
            for lvl in range(FINE_LEVELS, N_LEVELS):
                m = 1 << lvl
                parts = []
                for r0 in range(0, CHUNK, SUBLANES):
                    a0 = (r0 // (2 * m)) * (2 * m) + m - 1
                    anc = jnp.broadcast_to(b[a0:a0 + 1, :], (SUBLANES, LANES))
                    bj = b[r0:r0 + SUBLANES, :]
                    if r0 % (2 * m) >= m:
                        parts.append(jnp.exp2(bj - anc) * q_s[r0:r0 + SUBLANES, :])
                    else:
                        parts.append(jnp.exp2(anc - bj) * k_s[r0:r0 + SUBLANES, :])
                ys.append(jnp.concatenate(parts, axis=0).astype(bf16))
            blast = jnp.broadcast_to(b[CHUNK - 1:CHUNK, :], b.shape)
            qb = (q_s * jnp.exp2(b)).astype(bf16)
            kst = (k_s * jnp.exp2(blast - b)).astype(bf16)
            decs.append(jnp.exp2(b[CHUNK - 1:CHUNK, :]))
            qk_prod = q_s * k_s
            for hh in range(hps):
                mul = head_mask(hh)
                rds.append(jnp.sum(mul(qk_prod), axis=-1, keepdims=True))
                raw.append([_dot_nt(mul(ys[lvl]), ys[lvl]) for lvl in range(N_LEVELS)])
                qbs.append(mul(qb))
                ksts.append(mul(kst))
        return raw, rds, tuple(qbs), tuple(ksts), tuple(decs)

    def combine(raw, rds):
        ps = []
        for h in range(n_heads):
            p = jnp.where(eye, rds[h], 0.0)
            for lvl in range(N_LEVELS):
                p = p + pm_ref[lvl] * raw[h][lvl]
            ps.append(p.astype(bf16))
        return tuple(ps)

    def apply_dots(c, sc):
        ps, qbs, ksts, decs = sc
        outs = []
        for h in range(n_heads):
            vb = v_ref[rows_of(c), h * LANES:(h + 1) * LANES].astype(bf16)
            st = st_ref[h]
            outs.append(_dot(ps[h], vb) + _dot_nt(qbs[h], st.astype(bf16)))
            st_ref[h] = st * decs[h // hps] + _dot_tn(vb, ksts[h])
        return outs

    def finish(c, outs):
        for h in range(n_heads):
            cs = slice(h * LANES, (h + 1) * LANES)
            o = outs[h]
            o = o * lax.rsqrt(jnp.mean(o * o, axis=-1, keepdims=True) + RMS_EPS) * ng_ref[...]
            y_ref[rows_of(c), cs] = o * _silu(gate_ref[rows_of(c), cs])

    def first():
        raw, rds, qbs, ksts, decs = score_dots(cumsums(_gla_prologue(kind, pro, rows_of(0))))
        return combine(raw, rds), qbs, ksts, decs

    def body(c, sc):
        qkg = _gla_prologue(kind, pro, rows_of(c + 1))
        if kind == "gla":
            outs = apply_dots(c, sc)
            pre = cumsums(qkg)
        else:
            pre = cumsums(qkg)
            outs = apply_dots(c, sc)
        raw, rds, qbs, ksts, decs = score_dots(pre)
        finish(c, outs)
        return combine(raw, rds), qbs, ksts, decs

    n_chunks = lt // CHUNK
    last = lax.fori_loop(0, n_chunks - 1, body, first(), unroll=min(GLA_UNROLL, n_chunks - 1))
    finish(n_chunks - 1, apply_dots(n_chunks - 1, last))

    @pl.when(t_idx == n_tiles - 1)
    def _():
        for h in range(n_heads):
            off = (h % hps) * dk
            s_ref[h] = st_ref[h].T[off:off + dk, :]


def _gla_prompt(kind, z, col0, extra, norm_g, batch, seq, lt, layer):
    dk = GLA_DK if kind == "gla" else HG_D
    n_heads = GLA_H if kind == "gla" else HG_H
    tiles = seq // lt
    mall, pmask, rmask = _level_tables()
    blk = lambda width, cb: pl.BlockSpec((lt, width), lambda b, t: (b * tiles + t, cb))
    if kind == "gla":
        wlr, blr = extra
        ins = [z, z, z, z, wlr, blr]
        specs = [blk(2 * GLA_KW, 0), blk(GLA_VW, 1), blk(GLA_VW, 2), blk(LANES, col0),
                 _const_spec(wlr.shape), _const_spec(blr.shape)]
    else:
        (lg,) = extra
        ins = [z, z, z, z, lg]
        specs = [blk(HG_W, col0), blk(HG_W, col0 + 1), blk(HG_W, col0 + 2), blk(HG_W, col0 + 3), _const_spec(lg.shape)]
    ins += [norm_g, mall, pmask, rmask]
    specs += [_const_spec(norm_g.shape), _const_spec(mall.shape), _const_spec(pmask.shape), _const_spec(rmask.shape)]
    return pl.pallas_call(
        functools.partial(_gla_chunk_kernel, kind, dk, n_heads, layer),
        out_shape=(jax.ShapeDtypeStruct((batch * seq, HALF), f32),
                   jax.ShapeDtypeStruct((batch, n_heads, dk, LANES), f32)),
        grid=(batch, tiles),
        in_specs=specs,
        out_specs=(pl.BlockSpec((lt, HALF), lambda b, t: (b * tiles + t, 0)),
                   pl.BlockSpec((None, n_heads, dk, LANES), lambda b, t: (b, 0, 0, 0))),
        scratch_shapes=[pltpu.VMEM((n_heads, LANES, LANES), f32)],
        compiler_params=_params("parallel", "arbitrary"),
        name=kind + "_prompt",
    )(*ins)


def _lru_gates(xc, wr_ref, wi_ref, br_ref, bi_ref, lam_ref):
    xb = xc.astype(bf16)
    r = _sigmoid(_dot(xb, wr_ref[...]) + br_ref[...])
    i = _sigmoid(_dot(xb, wi_ref[...]) + bi_ref[...])
    log_a = (-RG_C) * r * _softplus(-lam_ref[...])
    return jnp.exp(log_a), jnp.sqrt(-_expm1(2.0 * log_a)) * (i * xc)


def _even_front_kernel(xin_ref, sc_ref, sh_ref, w_ref, cw_ref, cb_ref, wr_ref, wi_ref, br_ref, bi_ref, lam_ref,
                       z_ref, y_ref, conv_ref, h_ref, xbuf, a_scr, h_scr, car):
    t_idx = pl.program_id(1)
    lt = xin_ref.shape[0]
    nz = z_ref.shape[1]

    @pl.when(t_idx == 0)
    def _():
        xbuf[0:SUBLANES, :] = jnp.zeros((SUBLANES, LRU_W), f32)
        car[...] = jnp.zeros(car.shape, f32)

    hin = (xin_ref[...] * (1.0 + sc_ref[...]) + sh_ref[...]).astype(bf16)
    x = _dot(hin, w_ref[:, nz:nz + LRU_W])
    gr = _dot(hin, w_ref[:, nz + LRU_W:nz + 2 * LRU_W])
    half = 2 * LANES
    z_ref[:, 0:half] = _dot(hin, w_ref[:, 0:half])
    xbuf[SUBLANES:SUBLANES + lt, :] = x
    xc = cb_ref[...] + cw_ref[CONV_W - 1:CONV_W, :] * x
    for j in range(CONV_W - 1):
        off = SUBLANES - (CONV_W - 1) + j
        xc = xc + cw_ref[j:j + 1, :] * xbuf[off:off + lt, :]
    xbuf[0:SUBLANES, :] = xbuf[lt:lt + SUBLANES, :]

    a, hh = _lru_gates(xc, wr_ref, wi_ref, br_ref, bi_ref, lam_ref)
    z_ref[:, half:nz] = _dot(hin, w_ref[:, half:nz])
    grp = (lt // SUBLANES, SUBLANES, LRU_W)
    a, hh = a.reshape(grp), hh.reshape(grp)
    sub = lax.broadcasted_iota(jnp.int32, (SUBLANES, LRU_W), 0)
    for d in (1, 2, 4):
        m = sub >= d
        hh = hh + jnp.where(m, a, 0.0) * pltpu.roll(hh, d, 1)
        a = a * jnp.where(m, pltpu.roll(a, d, 1), 1.0)
    a_scr[...] = a.reshape(lt, LRU_W)
    h_scr[...] = hh.reshape(lt, LRU_W)

    def group(g, c):
        rows = pl.ds(pl.multiple_of(g * SUBLANES, SUBLANES), SUBLANES)
        hg = h_scr[rows, :] + a_scr[rows, :] * c
        h_scr[rows, :] = hg
        return jnp.broadcast_to(hg[SUBLANES - 1:SUBLANES, :], (SUBLANES, LRU_W))

    c = lax.fori_loop(0, lt // SUBLANES, group, car[...])
    car[...] = c
    y_ref[...] = h_scr[...] * _gelu(gr)
    conv_ref[...] = x[lt - (CONV_W - 1):lt, :]
    h_ref[...] = c[0:1, :]


def _even_front(x, mod, w, p, batch, seq, lt):
    tiles = seq // lt
    row = lambda wd: pl.BlockSpec((lt, wd), lambda b, t: (b * tiles + t, 0))
    modv = lambda j: pl.BlockSpec((None, 1, D_MODEL), lambda b, t: (b, 0, j))
    consts = [p["conv_w"], p["conv_b"], p["w_r"], p["w_i"], p["b_r"], p["b_i"], p["lam"]]
    return pl.pallas_call(
        _even_front_kernel,
        out_shape=(jax.ShapeDtypeStruct((batch * seq, EVEN_Z), f32),
                   jax.ShapeDtypeStruct((batch * seq, LRU_W), f32),
                   jax.ShapeDtypeStruct((batch, CONV_W - 1, LRU_W), f32),
                   jax.ShapeDtypeStruct((batch, 1, LRU_W), f32)),
        grid=(batch, tiles),
        in_specs=[row(D_MODEL), modv(1), modv(0), _const_spec(w.shape)] + [_const_spec(c.shape) for c in consts],
        out_specs=(row(EVEN_Z), row(LRU_W),
                   pl.BlockSpec((None, CONV_W - 1, LRU_W), lambda b, t: (b, 0, 0)),
                   pl.BlockSpec((None, 1, LRU_W), lambda b, t: (b, 0, 0))),
        scratch_shapes=[pltpu.VMEM((lt + SUBLANES, LRU_W), f32), pltpu.VMEM((lt, LRU_W), f32),
                        pltpu.VMEM((lt, LRU_W), f32), pltpu.VMEM((SUBLANES, LRU_W), f32)],
        compiler_params=_params("parallel", "arbitrary"),
        name="even_front",
    )(x, mod, mod, w, *consts)


S5_R = 4
S5_TAB = 8


def _s5_param_kernel(lr_ref, li_ref, ldt_ref, btr_ref, bti_ref, cr_ref, ci_ref,
                     a1r_ref, a1i_ref, pcr_ref, pci_ref, xr_ref, xi_ref, wcr_ref, wci_ref):
    lr_, li_ = lr_ref[...], li_ref[...]
    dt = jnp.exp(ldt_ref[...])
    mag = jnp.exp(lr_ * dt)
    ar, ai = mag * jnp.cos(li_ * dt), mag * jnp.sin(li_ * dt)
    den = lr_ * lr_ + li_ * li_
    nr, ni = ar - 1.0, ai
    fr = (nr * lr_ + ni * li_) / den
    fi = (ni * lr_ - nr * li_) / den
    bbr = fr[:, None, :] * btr_ref[...] - fi[:, None, :] * bti_ref[...]
    bbi = fr[:, None, :] * bti_ref[...] + fi[:, None, :] * btr_ref[...]
    a1r_ref[...], a1i_ref[...] = ar, ai
    cr, ci = cr_ref[...], ci_ref[...]
    pr, pi = jnp.ones_like(ar), jnp.zeros_like(ai)
    for n in range(S5_R * S5_TAB + 1):
        if n < S5_R:
            xr_ref[n] = pr[:, None, :] * bbr - pi[:, None, :] * bbi
            xi_ref[n] = pr[:, None, :] * bbi + pi[:, None, :] * bbr
        if 1 <= n <= S5_R:
            wcr_ref[n - 1] = cr * pr[:, None, :] - ci * pi[:, None, :]
            wci_ref[n - 1] = -(cr * pi[:, None, :] + ci * pr[:, None, :])
        if n >= S5_R and n % S5_R == 0:
            pcr_ref[n // S5_R - 1], pci_ref[n // S5_R - 1] = pr, pi
        pr, pi = pr * ar - pi * ai, pr * ai + pi * ar


def _toeplitz_kernel(x_ref, c_ref, m_ref):
    for j in range(x_ref.shape[0]):
        m_ref[j] = _dot(x_ref[j], c_ref[j])


def _s5_params(lam_re, lam_im, log_dt, b_re, b_im, c_re, c_im):
    btr = jnp.swapaxes(b_re, 1, 2)
    bti = jnp.swapaxes(b_im, 1, 2)
    shp = jax.ShapeDtypeStruct
    gp = (S5_G, S5_P)
    ghp = (S5_G, S5_GH, S5_P)
    a1r, a1i, pcr, pci, xr, xi, wcr, wci = pl.pallas_call(
        _s5_param_kernel,
        out_shape=(shp(gp, f32), shp(gp, f32), shp((S5_TAB,) + gp, f32), shp((S5_TAB,) + gp, f32),
                   shp((S5_R,) + ghp, f32), shp((S5_R,) + ghp, f32), shp((S5_R,) + ghp, f32), shp((S5_R,) + ghp, f32)),
        name="s5_params",
    )(lam_re, lam_im, log_dt.reshape(S5_G, 1), btr, bti, c_re, c_im)
    gpb = LANES // S5_GH
    nblk = S5_G // gpb
    eye = jnp.eye(gpb, dtype=f32)

    def bdiag_in(bb):
        x = bb.reshape(bb.shape[:-3] + (nblk, gpb, S5_GH, S5_P))
        return jnp.einsum('...jghp,gk->...jghkp', x, eye).reshape(bb.shape[:-3] + (nblk, gpb * S5_GH, gpb * S5_P))

    def bdiag_out(cc):
        x = cc.reshape(cc.shape[:-3] + (nblk, gpb, S5_GH, S5_P))
        return jnp.einsum('...jghp,gk->...jgpkh', x, eye).reshape(cc.shape[:-3] + (nblk, gpb * S5_P, gpb * S5_GH))

    x_bd = jnp.concatenate([bdiag_in(xr), bdiag_in(xi)], axis=-1).astype(bf16)
    c_bd = jnp.concatenate([bdiag_out(c_re), bdiag_out(-c_im)], axis=-2).astype(bf16)
    wc_bd = jnp.concatenate([bdiag_out(wcr), bdiag_out(wci)], axis=-2).astype(bf16)
    lag = pl.pallas_call(
        _toeplitz_kernel,
        out_shape=shp((S5_R, nblk, LANES, LANES), f32),
        grid=(S5_R,),
        in_specs=[pl.BlockSpec((None, nblk, LANES, 2 * gpb * S5_P), lambda d: (d, 0, 0, 0)),
                  _const_spec(c_bd.shape)],
        out_specs=pl.BlockSpec((None, nblk, LANES, LANES), lambda d: (d, 0, 0, 0)),
        compiler_params=_params("parallel"),
        name="s5_toeplitz",
    )(x_bd, c_bd)
    zero = jnp.zeros((nblk, LANES, LANES), f32)
    toep = jnp.concatenate(
        [jnp.concatenate([lag[r - s] if r >= s else zero for r in range(S5_R)], axis=2) for s in range(S5_R)],
        axis=1).astype(bf16)
    w_x = jnp.concatenate([x_bd[S5_R - 1 - s] for s in range(S5_R)], axis=1)
    w_c = jnp.concatenate([wc_bd[r] for r in range(S5_R)], axis=2)
    return dict(a1r=a1r.reshape(1, S5_N), a1i=a1i.reshape(1, S5_N),
                pcr=pcr.reshape(S5_TAB, S5_N), pci=pci.reshape(S5_TAB, S5_N),
                b_bd=x_bd[0], c_bd=c_bd, w_x=w_x, toep=toep, w_c=w_c)


S5_BLK = S5_N // (S5_G // (LANES // S5_GH))


def _s5_input(u, bbd_ref, j):
    bu = _dot(u[:, j * LANES:(j + 1) * LANES].astype(bf16), bbd_ref[j])
    return bu[:, 0:S5_BLK], bu[:, S5_BLK:2 * S5_BLK]


def _s5_output(u, hr, hi, cbd_ref, d_ref, wg_ref, bg_ref):
    ys = []
    for j in range(S5_N // S5_BLK):
        cs = slice(j * S5_BLK, (j + 1) * S5_BLK)
        hcat = jnp.concatenate([hr[:, cs], hi[:, cs]], axis=1).astype(bf16)
        ys.append(_dot(hcat, cbd_ref[j]))
    y = _gelu(jnp.concatenate(ys, axis=1) + d_ref[...] * u)
    return y * _sigmoid(_dot(y.astype(bf16), wg_ref[...]) + bg_ref[...])


def _s5_prompt_kernel(*refs):
    nblk = S5_N // S5_BLK
    u_refs = refs[:nblk]
    wx_ref, tp_ref, wc_ref, pcr_ref, pci_ref, d_ref, wg_ref, bg_ref = refs[nblk:nblk + 8]
    y_refs = refs[nblk + 8:2 * nblk + 8]
    sr_ref, si_ref, er_s, ei_s, y_s = refs[2 * nblk + 8:]
    t_idx = pl.program_id(1)
    cr = u_refs[0].shape[0] // S5_R
    gw = S5_R * LANES

    @pl.when(t_idx == 0)
    def _():
        er_s[0:SUBLANES, :] = jnp.zeros((SUBLANES, S5_N), f32)
        ei_s[0:SUBLANES, :] = jnp.zeros((SUBLANES, S5_N), f32)

    def token(s):
        return pl.ds(s, cr, stride=S5_R)

    sub = lax.broadcasted_iota(jnp.int32, (SUBLANES, S5_BLK), 0)
    for j in range(nblk):
        cs = slice(j * S5_BLK, (j + 1) * S5_BLK)
        uj = jnp.concatenate([u_refs[j][token(s), :] for s in range(S5_R)], axis=1).astype(bf16)
        x = _dot(uj, wx_ref[j])
        y_s[:, j * gw:(j + 1) * gw] = _dot(uj, tp_ref[j])
        hr = x[:, 0:S5_BLK].reshape(cr // SUBLANES, SUBLANES, S5_BLK)
        hi = x[:, S5_BLK:2 * S5_BLK].reshape(cr // SUBLANES, SUBLANES, S5_BLK)
        for d in (1, 2, 4):
            pr = jnp.where(sub >= d, pcr_ref[d - 1:d, cs], 0.0)
            pi = jnp.where(sub >= d, pci_ref[d - 1:d, cs], 0.0)
            sr, si = pltpu.roll(hr, d, 1), pltpu.roll(hi, d, 1)
            hr, hi = hr + (pr * sr - pi * si), hi + (pr * si + pi * sr)
        er_s[SUBLANES:SUBLANES + cr, cs] = hr.reshape(cr, S5_BLK)
        ei_s[SUBLANES:SUBLANES + cr, cs] = hi.reshape(cr, S5_BLK)

    tr, ti = pcr_ref[...], pci_ref[...]
    last = pl.ds(SUBLANES - 1, 1)
    c0 = (jnp.broadcast_to(er_s[last, :], (SUBLANES, S5_N)), jnp.broadcast_to(ei_s[last, :], (SUBLANES, S5_N)))

    def group(g, c):
        c_r, c_i = c
        rows = pl.ds(pl.multiple_of((g + 1) * SUBLANES, SUBLANES), SUBLANES)
        g_r = er_s[rows, :] + tr * c_r - ti * c_i
        g_i = ei_s[rows, :] + tr * c_i + ti * c_r
        er_s[rows, :] = g_r
        ei_s[rows, :] = g_i
        return (jnp.broadcast_to(g_r[SUBLANES - 1:SUBLANES, :], (SUBLANES, S5_N)),
                jnp.broadcast_to(g_i[SUBLANES - 1:SUBLANES, :], (SUBLANES, S5_N)))

    c_r, c_i = lax.fori_loop(0, cr // SUBLANES, group, c0)

    for j in range(nblk):
        cs = slice(j * S5_BLK, (j + 1) * S5_BLK)
        prev = pl.ds(SUBLANES - 1, cr)
        sj = jnp.concatenate([er_s[prev, cs], ei_s[prev, cs]], axis=1).astype(bf16)
        y_s[:, j * gw:(j + 1) * gw] += _dot(sj, wc_ref[j])
    er_s[last, :] = c_r[0:1, :]
    ei_s[last, :] = c_i[0:1, :]
    sr_ref[...] = c_r[0:1, :]
    si_ref[...] = c_i[0:1, :]

    for r in range(S5_R):
        y = jnp.concatenate([y_s[:, j * gw + r * LANES:j * gw + (r + 1) * LANES] for j in range(nblk)], axis=1)
        u_r = jnp.concatenate([u_refs[j][token(r), :] for j in range(nblk)], axis=1)
        y = _gelu(y + d_ref[...] * u_r)
        y = y * _sigmoid(_dot(y.astype(bf16), wg_ref[...]) + bg_ref[...])
        for j in range(nblk):
            y_refs[j][token(r), :] = y[:, j * LANES:(j + 1) * LANES]


def _s5_prompt(us, tabs, p, batch, seq, cr):
    nblk = len(us)
    tiles = seq // (S5_R * cr)
    consts = [tabs["w_x"], tabs["toep"], tabs["w_c"], tabs["pcr"], tabs["pci"], p["d"], p["w_glu"], p["b_glu"]]
    blk = pl.BlockSpec((S5_R * cr, LANES), lambda b, t: (b * tiles + t, 0))
    state = pl.BlockSpec((None, 1, S5_N), lambda b, t: (b, 0, 0))
    outs = pl.pallas_call(
        _s5_prompt_kernel,
        out_shape=tuple(jax.ShapeDtypeStruct((batch * seq, LANES), f32) for _ in range(nblk))
        + (jax.ShapeDtypeStruct((batch, 1, S5_N), f32), jax.ShapeDtypeStruct((batch, 1, S5_N), f32)),
        grid=(batch, tiles),
        in_specs=[blk] * nblk + [_const_spec(c.shape) for c in consts],
        out_specs=(blk,) * nblk + (state, state),
        scratch_shapes=[pltpu.VMEM((cr + SUBLANES, S5_N), f32), pltpu.VMEM((cr + SUBLANES, S5_N), f32),
                        pltpu.VMEM((cr, S5_R * S5_W), f32)],
        compiler_params=_params("parallel", "arbitrary"),
        name="s5_prompt",
    )(*us, *consts)
    return list(outs[:nblk]), outs[nblk], outs[nblk + 1]


def _gla_step_kernel(kind, dk, layer, *refs):
    if kind == "gla":
        q_ref, k_ref, v_ref, gate_ref, lr_ref, wlr_ref, blr_ref, ng_ref, s_ref, y_ref, so_ref, o_scr = refs
        sl = pl.program_id(0)
        q = q_ref[...] * (GLA_DK ** -0.5)
        k = k_ref[...]
        pre_all = _dot(lr_ref[...].astype(bf16), wlr_ref[...]) + blr_ref[...]
        pre = jnp.where(sl == 0, pre_all[:, 0:LANES], pre_all[:, LANES:2 * LANES])
        a = jnp.exp(_log_sigmoid(pre) * (1.0 / GLA_TAU))
    else:
        q_ref, f_ref, v_ref, gate_ref, lg_ref, ng_ref, s_ref, y_ref, so_ref, o_scr = refs
        sl = pl.program_id(0)
        lb_all = _hgrn_lower_bound(lg_ref, layer)
        lb = lb_all[:, 0:LANES]
        for j in range(1, HG_H):
            lb = jnp.where(sl == j, lb_all[:, j * LANES:(j + 1) * LANES], lb)
        a = lb + (1.0 - lb) * _sigmoid(f_ref[...])
        q = _silu(q_ref[...])
        k = 1.0 - a
    hps = LANES // dk
    nb = q.shape[0]
    qat, kt, at = (q * a).T, k.T, a.T
    lane = lax.broadcasted_iota(jnp.int32, (1, LANES), 1)
    v = v_ref[...]
    for hh in range(hps):
        r0 = hh * dk
        msk = (lane >= r0) & (lane < r0 + dk)
        qk = jnp.sum(jnp.where(msk, q * k, 0.0), axis=-1, keepdims=True)
        for b in range(nb):
            s = s_ref[b, hh]
            col = lambda x: jnp.broadcast_to(x[r0:r0 + dk, b:b + 1], (dk, LANES))
            vr = v[b:b + 1, hh * LANES:(hh + 1) * LANES]
            so_ref[b, hh] = col(at) * s + col(kt) * vr
            oi = jnp.sum(col(qat) * s, axis=0, keepdims=True)
            o_scr[b:b + 1, hh * LANES:(hh + 1) * LANES] = jnp.broadcast_to(qk[b:b + 1, :], (1, LANES)) * vr + oi
    for hh in range(hps):
        cs = slice(hh * LANES, (hh + 1) * LANES)
        o = o_scr[:, cs]
        o = o * lax.rsqrt(jnp.mean(o * o, axis=-1, keepdims=True) + RMS_EPS) * ng_ref[...]
        y_ref[:, cs] = o * _silu(gate_ref[:, cs])


def _gla_step(kind, z, extra, norm_g, state, layer):
    nb = z.shape[0]
    dk = GLA_DK if kind == "gla" else HG_D
    n_heads = GLA_H if kind == "gla" else HG_H
    hps = LANES // dk
    n_slabs = n_heads // hps
    vw = hps * LANES
    zb = lambda width, cb0: pl.BlockSpec((nb, width), lambda s: (0, cb0 + s))
    if kind == "gla":
        wlr, blr = extra
        ins = [z, z, z, z, z, wlr, blr]
        specs = [zb(LANES, 0), zb(LANES, GLA_KW // LANES), zb(vw, GLA_VW // vw * 1), zb(vw, GLA_VW // vw * 2),
                 pl.BlockSpec((nb, LANES), lambda s: (0, GLA_LR_BLOCK)),
                 _const_spec(wlr.shape), _const_spec(blr.shape)]
    else:
        (lg,) = extra
        ins = [z, z, z, z, lg]
        specs = [zb(LANES, 0), zb(LANES, HG_W // LANES), zb(vw, HG_W // vw * 2),
                 zb(vw, HG_W // vw * 3), _const_spec(lg.shape)]
    ins += [norm_g, state]
    specs += [_const_spec(norm_g.shape), pl.BlockSpec((nb, hps, dk, LANES), lambda s: (0, s, 0, 0))]
    return pl.pallas_call(
        functools.partial(_gla_step_kernel, kind, dk, layer),
        out_shape=(jax.ShapeDtypeStruct((nb, HALF), f32), jax.ShapeDtypeStruct(state.shape, f32)),
        grid=(n_slabs,),
        in_specs=specs,
        out_specs=(pl.BlockSpec((nb, vw), lambda s: (0, s)),
                   pl.BlockSpec((nb, hps, dk, LANES), lambda s: (0, s, 0, 0))),
        scratch_shapes=[pltpu.VMEM((nb, vw), f32)],
        compiler_params=_params("parallel"),
        name=kind + "_step",
    )(*ins)


def _lru_step_kernel(x_ref, gr_ref, c0_ref, c1_ref, c2_ref, h0_ref, cw_ref, cb_ref, wr_ref, wi_ref,
                     br_ref, bi_ref, lam_ref, y_ref, h_ref):
    xc = (cb_ref[...] + cw_ref[0:1, :] * c0_ref[...] + cw_ref[1:2, :] * c1_ref[...]
          + cw_ref[2:3, :] * c2_ref[...] + cw_ref[3:4, :] * x_ref[...])
    a, u = _lru_gates(xc, wr_ref, wi_ref, br_ref, bi_ref, lam_ref)
    h = a * h0_ref[...] + u
    h_ref[...] = h
    y_ref[...] = h * _gelu(gr_ref[...])


def _lru_step(xr, gr, conv_rows, h0, p):
    nb = xr.shape[0]
    full = lambda: pl.BlockSpec((nb, LRU_W), lambda i: (0, 0))
    consts = [p["conv_w"], p["conv_b"], p["w_r"], p["w_i"], p["b_r"], p["b_i"], p["lam"]]
    return pl.pallas_call(
        _lru_step_kernel,
        out_shape=(jax.ShapeDtypeStruct((nb, LRU_W), f32), jax.ShapeDtypeStruct((nb, LRU_W), f32)),
        grid=(1,),
        in_specs=[full() for _ in range(CONV_W + 2)] + [_const_spec(c.shape) for c in consts],
        out_specs=(full(), full()),
        compiler_params=_params("arbitrary"),
        name="lru_step",
    )(xr, gr, *conv_rows, h0, *consts)


def _s5_step_kernel(u_ref, s0r_ref, s0i_ref, a1r_ref, a1i_ref, bbd_ref, cbd_ref, d_ref, wg_ref, bg_ref,
                    y_ref, sr_ref, si_ref):
    u = u_ref[...]
    for j in range(S5_N // S5_BLK):
        cs = slice(j * S5_BLK, (j + 1) * S5_BLK)
        bur, bui = _s5_input(u, bbd_ref, j)
        ar, ai = a1r_ref[:, cs], a1i_ref[:, cs]
        s0r, s0i = s0r_ref[:, cs], s0i_ref[:, cs]
        sr_ref[:, cs] = bur + ar * s0r - ai * s0i
        si_ref[:, cs] = bui + ar * s0i + ai * s0r
    y_ref[...] = _s5_output(u, sr_ref[...], si_ref[...], cbd_ref, d_ref, wg_ref, bg_ref)


def _s5_step(u, s0r, s0i, tabs, p):
    nb = u.shape[0]
    consts = [tabs["a1r"], tabs["a1i"], tabs["b_bd"], tabs["c_bd"], p["d"], p["w_glu"], p["b_glu"]]
    st = lambda: pl.BlockSpec((nb, S5_N), lambda i: (0, 0))
    return pl.pallas_call(
        _s5_step_kernel,
        out_shape=(jax.ShapeDtypeStruct((nb, S5_W), f32), jax.ShapeDtypeStruct((nb, S5_N), f32),
                   jax.ShapeDtypeStruct((nb, S5_N), f32)),
        grid=(1,),
        in_specs=[pl.BlockSpec((nb, S5_W), lambda i: (0, 0)), st(), st()] + [_const_spec(c.shape) for c in consts],
        out_specs=(pl.BlockSpec((nb, S5_W), lambda i: (0, 0)), st(), st()),
        compiler_params=_params("arbitrary"),
        name="s5_step",
    )(u, s0r, s0i, *consts)


def _block_diag(w):
    h, n, _ = w.shape
    return jnp.einsum('hij,hk->hikj', w, jnp.eye(h, dtype=w.dtype)).reshape(h * n, h * n)


def kernel(x_prompt, x_sample, c_prompt, c_sample, state_gla, state_rglru_conv, state_rglru_h, state_s5_re, state_s5_im, state_hgrn, ev_w_in, ev_gla_w_lr, ev_gla_b_lr, ev_gla_norm_g, ev_conv_w, ev_conv_b, ev_lru_w_r, ev_lru_b_r, ev_lru_w_i, ev_lru_b_i, ev_lru_lam, ev_w_out, od_w_in, od_s5_lam_re, od_s5_lam_im, od_s5_log_dt, od_s5_b_re, od_s5_b_im, od_s5_c_re, od_s5_c_im, od_s5_d, od_s5_w_glu, od_s5_b_glu, hg_lb_logits, od_hg_norm_g, od_w_out, w_ada, b_ada, ln_g, ln_b, ffn_w_in, ffn_w_out):
    bp, seq = x_prompt.shape[0], x_prompt.shape[1]
    bs = x_sample.shape[0]
    assert x_sample.shape[1] == 1 and seq % CHUNK == 0
    tp = bp * seq
    tm = min(512, seq)
    lt = min(512, seq)
    lt_gla = min(2048, seq)
    cr_s5 = min(256, seq // S5_R)
    row2 = lambda a: a.reshape(1, -1)

    mod = _modulation(jnp.concatenate([c_prompt, c_sample], axis=0).astype(f32), w_ada, b_ada)
    mod_p = mod[:, :bp].reshape(DEPTH, bp, 1, 6 * D_MODEL)
    mod_s = mod[:, bp:].reshape(DEPTH, 1, bs, 6 * D_MODEL)

    xp = x_prompt.astype(f32).reshape(tp, D_MODEL)
    xs = x_sample.astype(f32).reshape(bs, D_MODEL)
    outs_p = {k: [] for k in ("gla", "conv", "lru", "re", "im", "hg")}
    outs_s = {k: [] for k in ("gla", "conv", "lru", "re", "im", "hg")}

    w_f_in = ffn_w_in.astype(bf16)
    w_f_out = ffn_w_out.astype(bf16)
    for l in range(DEPTH):
        if l % 2 == 0:
            e = l // 2
            w = ev_w_in[e]
            lr0 = 2 * GLA_KW + 2 * GLA_VW
            w_in = jnp.concatenate(
                [w[:, :lr0 + GLA_LR], jnp.zeros((D_MODEL, LANES - GLA_LR), w.dtype), w[:, lr0 + GLA_LR:]],
                axis=1).astype(bf16)
            w_out = ev_w_out[e].astype(bf16)
            wlr = jnp.concatenate([ev_gla_w_lr[e], jnp.zeros((LANES - GLA_LR, GLA_KW), f32)], axis=0).astype(bf16)
            gla_extra = (wlr, row2(ev_gla_b_lr[e]))
            ng = row2(ev_gla_norm_g[e])
            lru_p = dict(conv_w=ev_conv_w[e], conv_b=row2(ev_conv_b[e]),
                         w_r=_block_diag(ev_lru_w_r[e]).astype(bf16), w_i=_block_diag(ev_lru_w_i[e]).astype(bf16),
                         b_r=row2(ev_lru_b_r[e]), b_i=row2(ev_lru_b_i[e]), lam=row2(ev_lru_lam[e]))
            z, yb, s_conv, s_h = _even_front(xp, mod_p[l], w_in, lru_p, bp, seq, lt)
            ya, s_gla = _gla_prompt("gla", z, GLA_LR_BLOCK, gla_extra, ng, bp, seq, lt_gla, l)
            xp = _outffn(xp, [ya, yb], mod_p[l], w_out, ln_g[l], ln_b[l], w_f_in, w_f_out, l, tm, seq // tm)
            outs_p["gla"].append(s_gla)
            outs_p["conv"].append(s_conv)
            outs_p["lru"].append(s_h.reshape(bp, LRU_W))
            z, xr, gr = _inproj(xs, mod_s[l], w_in, bs, 1, (EVEN_Z, LRU_W, LRU_W))
            ya, s_gla = _gla_step("gla", z, gla_extra, ng, state_gla[e], l)
            cs = state_rglru_conv[e].astype(f32)
            yb, s_h = _lru_step(xr, gr, [cs[:, j] for j in range(CONV_W - 1)], state_rglru_h[e].astype(f32), lru_p)
            xs = _outffn(xs, [ya, yb], mod_s[l], w_out, ln_g[l], ln_b[l], w_f_in, w_f_out, l, bs, 1)
            outs_s["gla"].append(s_gla)
            outs_s["conv"].append(jnp.stack([cs[:, 1], cs[:, 2], xr], axis=1))
            outs_s["lru"].append(s_h)
        else:
            o = l // 2
            w_in = od_w_in[o].astype(bf16)
            w_out = od_w_out[o].astype(bf16)
            tabs = _s5_params(od_s5_lam_re[o], od_s5_lam_im[o], od_s5_log_dt[o], od_s5_b_re[o], od_s5_b_im[o],
                              od_s5_c_re[o], od_s5_c_im[o])
            s5_p = dict(d=row2(od_s5_d[o]), w_glu=od_s5_w_glu[o].astype(bf16), b_glu=row2(od_s5_b_glu[o]))
            hg_extra = (hg_lb_logits.astype(f32),)
            ng = row2(od_hg_norm_g[o])
            *us, z = _inproj(xp, mod_p[l], w_in, tm, seq // tm, (LANES,) * (S5_W // LANES) + (ODD_N - S5_W,))
            yas, s_re, s_im = _s5_prompt(us, tabs, s5_p, bp, seq, cr_s5)
            yb, s_hg = _gla_prompt("hgrn", z, 0, hg_extra, ng, bp, seq, lt_gla, l)
            xp = _outffn(xp, yas + [yb], mod_p[l], w_out, ln_g[l], ln_b[l], w_f_in, w_f_out, l, tm, seq // tm)
            outs_p["re"].append(s_re.reshape(bp, S5_G, S5_P))
            outs_p["im"].append(s_im.reshape(bp, S5_G, S5_P))
            outs_p["hg"].append(s_hg)
            u, z = _inproj(xs, mod_s[l], w_in, bs, 1, (S5_W, ODD_N - S5_W))
            ya, s_re, s_im = _s5_step(u, state_s5_re[o].astype(f32).reshape(bs, S5_N),
                                      state_s5_im[o].astype(f32).reshape(bs, S5_N), tabs, s5_p)
            yb, s_hg = _gla_step("hgrn", z, hg_extra, ng, state_hgrn[o], l)
            xs = _outffn(xs, [ya, yb], mod_s[l], w_out, ln_g[l], ln_b[l], w_f_in, w_f_out, l, bs, 1)
            outs_s["re"].append(s_re.reshape(bs, S5_G, S5_P))
            outs_s["im"].append(s_im.reshape(bs, S5_G, S5_P))
            outs_s["hg"].append(s_hg)

    st = lambda d, k: d[k][0][None] if len(d[k]) == 1 else jnp.stack(d[k])
    return (xp.reshape(bp, seq, D_MODEL).astype(x_prompt.dtype), xs.reshape(bs, 1, D_MODEL).astype(x_sample.dtype),
            st(outs_p, "gla"), st(outs_s, "gla"), st(outs_p, "conv"), st(outs_s, "conv"),
            st(outs_p, "lru"), st(outs_s, "lru"), st(outs_p, "re"), st(outs_s, "re"),
            st(outs_p, "im"), st(outs_s, "im"), st(outs_p, "hg"), st(outs_s, "hg"))
```

```python
import functools
import math

import numpy as np
import jax
import jax.numpy as jnp
from jax import lax
from jax.experimental import pallas as pl
from jax.experimental.pallas import tpu as pltpu

f32 = jnp.float32
bf16 = jnp.bfloat16

D_MODEL = 1024
DEPTH = 2
HALF = D_MODEL // 2
GLA_H = 4
GLA_DV = HALF // GLA_H
GLA_DK = GLA_DV // 2
GLA_KW = GLA_H * GLA_DK
GLA_VW = GLA_H * GLA_DV
GLA_LR = 16
GLA_TAU = 16.0
LRU_W = HALF
LRU_H = 8
LRU_BW = LRU_W // LRU_H
CONV_W = 4
RG_C = 8.0
S5_W = HALF
S5_GH = 16
S5_G = S5_W // S5_GH
S5_P = 64
S5_N = S5_G * S5_P
HG_H = 4
HG_D = HALF // HG_H
HG_W = HG_H * HG_D
D_FF = ((8 * D_MODEL // 3 + 255) // 256) * 256
ALPHA = (2.0 * DEPTH) ** 0.25
LN_EPS = 1e-5
RMS_EPS = 1e-6

LANES = 128
SUBLANES = 8
VMEM_LIMIT = 56 * 1024 * 1024

CHUNK = 64
N_LEVELS = 6
FINE_LEVELS = 3
GLA_UNROLL = 3
FF_CHUNK = 256
EVEN_Z = 2 * GLA_KW + 2 * GLA_VW + LANES
EVEN_N = EVEN_Z + 2 * LRU_W
GLA_LR_BLOCK = (EVEN_Z - LANES) // LANES
ODD_N = 2560


def _sigmoid(x):
    return 0.5 * jnp.tanh(0.5 * x) + 0.5


def _silu(x):
    return x * _sigmoid(x)


def _gelu(x):
    c = math.sqrt(2.0 / math.pi)
    return x * (0.5 * (1.0 + jnp.tanh(c * (x + 0.044715 * (x * x * x)))))


def _softplus(x):
    return jnp.maximum(x, 0.0) + jnp.log1p(jnp.exp(-jnp.abs(x)))


def _log_sigmoid(x):
    return -_softplus(-x)


def _expm1(x):
    u = jnp.exp(x)
    um1 = u - 1.0
    lg = jnp.log(u)
    safe = jnp.where(lg == 0.0, 1.0, lg)
    return jnp.where(um1 == 0.0, x, jnp.where(um1 == -1.0, -1.0, um1 * x / safe))


def _layer_norm(x, g, b):
    mu = jnp.mean(x, axis=-1, keepdims=True)
    xc = x - mu
    var = jnp.mean(xc * xc, axis=-1, keepdims=True)
    return xc * lax.rsqrt(var + LN_EPS) * g + b


def _dot(a, b):
    return jnp.dot(a, b, preferred_element_type=f32)


def _dot_nt(a, b):
    return lax.dot_general(a, b, (((1,), (1,)), ((), ())), preferred_element_type=f32)


def _dot_tn(a, b):
    return lax.dot_general(a, b, (((0,), (0,)), ((), ())), preferred_element_type=f32)


def _const_spec(shape):
    nd = len(shape)
    return pl.BlockSpec(shape, lambda *_: (0,) * nd, pipeline_mode=pl.Buffered(1))


def _params(*sem):
    return pltpu.CompilerParams(dimension_semantics=sem, vmem_limit_bytes=VMEM_LIMIT)


def _mod_kernel(c_ref, w_ref, b_ref, o_ref):
    cond = _silu(c_ref[...]).astype(bf16)
    o_ref[...] = _dot(cond, w_ref[...].astype(bf16)) + b_ref[...]


def _modulation(c_all, w_ada, b_ada):
    rows = c_all.shape[0]
    tn = 1536
    return pl.pallas_call(
        _mod_kernel,
        out_shape=jax.ShapeDtypeStruct((DEPTH, rows, 6 * D_MODEL), f32),
        grid=(DEPTH, 6 * D_MODEL // tn),
        in_specs=[pl.BlockSpec((rows, D_MODEL), lambda l, j: (0, 0)),
                  pl.BlockSpec((None, D_MODEL, tn), lambda l, j: (l, 0, j)),
                  pl.BlockSpec((None, 1, tn), lambda l, j: (l, 0, j))],
        out_specs=pl.BlockSpec((None, rows, tn), lambda l, j: (l, 0, j)),
        compiler_params=_params("parallel", "parallel"),
        name="adaln_mod",
    )(c_all, w_ada, b_ada.reshape(DEPTH, 1, 6 * D_MODEL))


def _mod_spec(rows_per_block, tiles_per_batch, j):
    return pl.BlockSpec((None, rows_per_block, D_MODEL), lambda i: (i // tiles_per_batch, 0, j))


def _inproj_kernel(x_ref, sc_ref, sh_ref, w_ref, *z_refs):
    h = (x_ref[...] * (1.0 + sc_ref[...]) + sh_ref[...]).astype(bf16)
    lo = 0
    for z_ref in z_refs:
        z_ref[...] = _dot(h, w_ref[:, lo:lo + z_ref.shape[1]])
        lo += z_ref.shape[1]


def _inproj(x, mod, w, tm, tiles_per_batch, widths):
    t, n = x.shape[0], w.shape[1]
    assert sum(widths) == n
    r = mod.shape[1]
    return pl.pallas_call(
        _inproj_kernel,
        out_shape=tuple(jax.ShapeDtypeStruct((t, wd), f32) for wd in widths),
        grid=(t // tm,),
        in_specs=[pl.BlockSpec((tm, D_MODEL), lambda i: (i, 0)),
                  _mod_spec(r, tiles_per_batch, 1), _mod_spec(r, tiles_per_batch, 0),
                  _const_spec((D_MODEL, n))],
        out_specs=tuple(pl.BlockSpec((tm, wd), lambda i: (i, 0)) for wd in widths),
        compiler_params=_params("parallel"),
        name="in_proj",
    )(x, mod, mod, w)


def _outffn_kernel(n_y, x_ref, *refs):
    y_refs = refs[:n_y]
    gtm_ref, shf_ref, scf_ref, gtf_ref, wo_ref, lng_ref, lnb_ref, wi_ref, wf_ref, o_ref = refs[n_y:]
    y = jnp.concatenate([r[...].astype(bf16) for r in y_refs], axis=1)
    mix = _dot(y, wo_ref[...])
    x1 = _layer_norm(ALPHA * x_ref[...] + (1.0 + gtm_ref[...]) * mix, lng_ref[0:1, :], lnb_ref[0:1, :])
    h = (x1 * (1.0 + scf_ref[...]) + shf_ref[...]).astype(bf16)
    acc = jnp.zeros(x1.shape, f32)
    for c in range(D_FF // FF_CHUNK):
        lo = c * FF_CHUNK
        gate = _dot(h, wi_ref[:, lo:lo + FF_CHUNK])
        up = _dot(h, wi_ref[:, D_FF + lo:D_FF + lo + FF_CHUNK])
        act = (_silu(gate) * up).astype(bf16)
        acc = acc + _dot(act, wf_ref[lo:lo + FF_CHUNK, :])
    o_ref[...] = _layer_norm(ALPHA * x1 + (1.0 + gtf_ref[...]) * acc, lng_ref[1:2, :], lnb_ref[1:2, :])


def _outffn(x, ys, mod, w_out, ln_g, ln_b, w_in, w_f, layer, tm, tiles_per_batch):
    t = x.shape[0]
    r = mod.shape[1]
    assert sum(y.shape[1] for y in ys) == D_MODEL
    row = lambda w: pl.BlockSpec((tm, w), lambda i: (i, 0))
    stacked = lambda a: pl.BlockSpec((None,) + a.shape[1:], lambda i: (layer, 0, 0), pipeline_mode=pl.Buffered(1))
    return pl.pallas_call(
        functools.partial(_outffn_kernel, len(ys)),
        out_shape=jax.ShapeDtypeStruct((t, D_MODEL), f32),
        grid=(t // tm,),
        in_specs=[row(D_MODEL)] + [row(y.shape[1]) for y in ys] + [
                  _mod_spec(r, tiles_per_batch, 2), _mod_spec(r, tiles_per_batch, 3),
                  _mod_spec(r, tiles_per_batch, 4), _mod_spec(r, tiles_per_batch, 5),
                  _const_spec((D_MODEL, D_MODEL)), _const_spec((2, D_MODEL)), _const_spec((2, D_MODEL)),
                  stacked(w_in), stacked(w_f)],
        out_specs=row(D_MODEL),
        compiler_params=_params("parallel"),
        name="outproj_ffn",
    )(x, *ys, mod, mod, mod, mod, w_out, ln_g, ln_b, w_in, w_f)


def _level_tables():
    c = CHUNK
    t = np.arange(c)[:, None]
    u = np.arange(c)[None, :]
    pmask, sgn = [], []
    for lvl in range(N_LEVELS):
        m = 1 << lvl
        right = (t % (2 * m)) >= m
        pm = right & ((u % (2 * m)) < m) & ((t // (2 * m)) == (u // (2 * m)))
        pmask.append(pm.astype(np.float32))
        sgn.append(np.broadcast_to(np.where(right, 1.0, -1.0).astype(np.float32), (c, LANES)))
    tri = (u <= t).astype(np.float32)
    tri3 = np.concatenate([tri, tri, tri], axis=1)
    return (jnp.asarray(tri3, dtype=bf16), jnp.asarray(np.stack(pmask), dtype=f32),
            jnp.asarray(np.stack(sgn), dtype=f32))


LOG2E = 1.0 / math.log(2.0)


def _anchor(b, lvl):
    m = 1 << lvl
    b3 = b.reshape(CHUNK // SUBLANES, SUBLANES, LANES)
    if 4 * m == SUBLANES:
        sub = lax.broadcasted_iota(jnp.int32, b3.shape, 1)
        a = jnp.where(sub < 2 * m, jnp.broadcast_to(b3[:, m - 1:m, :], b3.shape),
                      jnp.broadcast_to(b3[:, 3 * m - 1:3 * m, :], b3.shape))
        return a.reshape(CHUNK, LANES)
    assert 2 * m == SUBLANES
    return jnp.broadcast_to(b3[:, m - 1:m, :], b3.shape).reshape(CHUNK, LANES)


def _split3(g):
    g1 = g.astype(bf16)
    r1 = g - g1.astype(f32)
    g2 = r1.astype(bf16)
    r2 = r1 - g2.astype(f32)
    return jnp.concatenate([g1, g2, r2.astype(bf16)], axis=0)


def _hgrn_lower_bound(lg_ref, layer):
    rows = [lg_ref[i:i + 1, :] for i in range(DEPTH)]
    mx = functools.reduce(jnp.maximum, rows)
    ex = [jnp.exp(r - mx) for r in rows]
    den = functools.reduce(lambda a, b: a + b, ex)
    sm = [e / den for e in ex]
    cs = sm[0]
    for i in range(1, layer + 1):
        cs = cs + sm[i]
    return cs - sm[0]


def _gla_prologue(kind, refs, rows):
    if kind == "gla":
        qk_ref, lr_ref, wlr_ref, blr_ref = refs
        q = qk_ref[rows, 0:GLA_KW] * (GLA_DK ** -0.5)
        k = qk_ref[rows, GLA_KW:2 * GLA_KW]
        pre = _dot(lr_ref[rows, :].astype(bf16), wlr_ref[...]) + blr_ref[...]
        g = _log_sigmoid(pre) * (LOG2E / GLA_TAU)
        return q, k, g
    q_ref, f_ref, lb = refs
    forget = lb + (1.0 - lb) * _sigmoid(f_ref[rows, :])
    return _silu(q_ref[rows, :]), 1.0 - forget, jnp.log2(forget)


def _gla_chunk_kernel(kind, dk, n_heads, layer, *refs):
    if kind == "gla":
        (qk_ref, v_ref, gate_ref, lr_ref, wlr_ref, blr_ref, ng_ref, tri_ref, pm_ref, sg_ref,
         y_ref, s_ref, st_ref) = refs
        pro = (qk_ref, lr_ref, wlr_ref, blr_ref)
    else:
        (q_ref, f_ref, v_ref, gate_ref, lg_ref, ng_ref, tri_ref, pm_ref, sg_ref,
         y_ref, s_ref, st_ref) = refs
        pro = (q_ref, f_ref, _hgrn_lower_bound(lg_ref, layer))
    t_idx = pl.program_id(1)
    n_tiles = pl.num_programs(1)
    lt = y_ref.shape[0]
    hps = LANES // dk
    n_slabs = n_heads // hps
    lane = lax.broadcasted_iota(jnp.int32, (1, LANES), 1)
    eye = (lax.broadcasted_iota(jnp.int32, (CHUNK, CHUNK), 0)
           == lax.broadcasted_iota(jnp.int32, (CHUNK, CHUNK), 1))

    @pl.when(t_idx == 0)
    def _():
        st_ref[...] = jnp.zeros(st_ref.shape, f32)

    def head_mask(hh):
        if hps == 1:
            return lambda a: a
        msk = (lane >= hh * dk) & (lane < (hh + 1) * dk)
        return lambda a: jnp.where(msk, a, jnp.zeros_like(a))

    def rows_of(c):
        return pl.ds(pl.multiple_of(c * CHUNK, CHUNK), CHUNK)

    def cumsums(qkg):
        q_all, k_all, g_all = qkg
        out = []
        for sl in range(n_slabs):
            cs = slice(sl * LANES, (sl + 1) * LANES)
            q_s, k_s, g_s = q_all[:, cs], k_all[:, cs], g_all[:, cs]
            out.append((q_s, k_s, g_s, _dot(tri_ref[...], _split3(g_s))))
        return out

    def score_dots(pre):
        raw, qbs, ksts, decs, rds = [], [], [], [], []
        right = [sg_ref[lvl] > 0.0 for lvl in range(FINE_LEVELS)]
        for sl in range(n_slabs):
            q_s, k_s, g_s, b = pre[sl]
            ys = [(jnp.where(right[0], jnp.exp2(g_s) * q_s, k_s)).astype(bf16)]
            for lvl in range(1, FINE_LEVELS):
                x = jnp.exp2((b - _anchor(b, lvl)) * sg_ref[lvl])
                ys.append((x * jnp.where(right[lvl], q_s, k_s)).astype(bf16))
            for lvl in range(FINE_LEVELS, N_LEVELS):
                m = 1 << lvl
                parts = []
                for r0 in range(0, CHUNK, SUBLANES):
                    a0 = (r0 // (2 * m)) * (2 * m) + m - 1
                    anc = jnp.broadcast_to(b[a0:a0 + 1, :], (SUBLANES, LANES))
                    bj = b[r0:r0 + SUBLANES, :]
                    if r0 % (2 * m) >= m:
                        parts.append(jnp.exp2(bj - anc) * q_s[r0:r0 + SUBLANES, :])
                    else:
                        parts.append(jnp.exp2(anc - bj) * k_s[r0:r0 + SUBLANES, :])
                ys.append(jnp.concatenate(parts, axis=0).astype(bf16))
            blast = jnp.broadcast_to(b[CHUNK - 1:CHUNK, :], b.shape)
            qb = (q_s * jnp.exp2(b)).astype(bf16)
            kst = (k_s * jnp.exp2(blast - b)).astype(bf16)
            decs.append(jnp.exp2(b[CHUNK - 1:CHUNK, :]))
            qk_prod = q_s * k_s
            for hh in range(hps):
                mul = head_mask(hh)
                rds.append(jnp.sum(mul(qk_prod), axis=-1, keepdims=True))
                raw.append([_dot_nt(mul(ys[lvl]), ys[lvl]) for lvl in range(N_LEVELS)])
                qbs.append(mul(qb))
                ksts.append(mul(kst))
        return raw, rds, tuple(qbs), tuple(ksts), tuple(decs)

    def combine(raw, rds):
        ps = []
        for h in range(n_heads):
            p = jnp.where(eye, rds[h], 0.0)
            for lvl in range(N_LEVELS):
                p = p + pm_ref[lvl] * raw[h][lvl]
            ps.append(p.astype(bf16))
        return tuple(ps)

    def apply_dots(c, sc):
        ps, qbs, ksts, decs = sc
        outs = []
        for h in range(n_heads):
            vb = v_ref[rows_of(c), h * LANES:(h + 1) * LANES].astype(bf16)
            st = st_ref[h]
            outs.append(_dot(ps[h], vb) + _dot_nt(qbs[h], st.astype(bf16)))
            st_ref[h] = st * decs[h // hps] + _dot_tn(vb, ksts[h])
        return outs

    def finish(c, outs):
        for h in range(n_heads):
            cs = slice(h * LANES, (h + 1) * LANES)
            o = outs[h]
            o = o * lax.rsqrt(jnp.mean(o * o, axis=-1, keepdims=True) + RMS_EPS) * ng_ref[...]
            y_ref[rows_of(c), cs] = o * _silu(gate_ref[rows_of(c), cs])

    def first():
        raw, rds, qbs, ksts, decs = score_dots(cumsums(_gla_prologue(kind, pro, rows_of(0))))
        return combine(raw, rds), qbs, ksts, decs

    def body(c, sc):
        qkg = _gla_prologue(kind, pro, rows_of(c + 1))
        if kind == "gla":
            outs = apply_dots(c, sc)
            pre = cumsums(qkg)
        else:
            pre = cumsums(qkg)
            outs = apply_dots(c, sc)
        raw, rds, qbs, ksts, decs = score_dots(pre)
        finish(c, outs)
        return combine(raw, rds), qbs, ksts, decs

    n_chunks = lt // CHUNK
    last = lax.fori_loop(0, n_chunks - 1, body, first(), unroll=min(GLA_UNROLL, n_chunks - 1))
    finish(n_chunks - 1, apply_dots(n_chunks - 1, last))

    @pl.when(t_idx == n_tiles - 1)
    def _():
        for h in range(n_heads):
            off = (h % hps) * dk
            s_ref[h] = st_ref[h].T[off:off + dk, :]


def _gla_prompt(kind, z, col0, extra, norm_g, batch, seq, lt, layer):
    dk = GLA_DK if kind == "gla" else HG_D
    n_heads = GLA_H if kind == "gla" else HG_H
    tiles = seq // lt
    mall, pmask, rmask = _level_tables()
    blk = lambda width, cb: pl.BlockSpec((lt, width), lambda b, t: (b * tiles + t, cb))
    if kind == "gla":
        wlr, blr = extra
        ins = [z, z, z, z, wlr, blr]
        specs = [blk(2 * GLA_KW, 0), blk(GLA_VW, 1), blk(GLA_VW, 2), blk(LANES, col0),
                 _const_spec(wlr.shape), _const_spec(blr.shape)]
    else:
        (lg,) = extra
        ins = [z, z, z, z, lg]
        specs = [blk(HG_W, col0), blk(HG_W, col0 + 1), blk(HG_W, col0 + 2), blk(HG_W, col0 + 3), _const_spec(lg.shape)]
    ins += [norm_g, mall, pmask, rmask]
    specs += [_const_spec(norm_g.shape), _const_spec(mall.shape), _const_spec(pmask.shape), _const_spec(rmask.shape)]
    return pl.pallas_call(
        functools.partial(_gla_chunk_kernel, kind, dk, n_heads, layer),
        out_shape=(jax.ShapeDtypeStruct((batch * seq, HALF), f32),
                   jax.ShapeDtypeStruct((batch, n_heads, dk, LANES), f32)),
        grid=(batch, tiles),
        in_specs=specs,
        out_specs=(pl.BlockSpec((lt, HALF), lambda b, t: (b * tiles + t, 0)),
                   pl.BlockSpec((None, n_heads, dk, LANES), lambda b, t: (b, 0, 0, 0))),
        scratch_shapes=[pltpu.VMEM((n_heads, LANES, LANES), f32)],
        compiler_params=_params("parallel", "arbitrary"),
        name=kind + "_prompt",
    )(*ins)


def _lru_gates(xc, wr_ref, wi_ref, br_ref, bi_ref, lam_ref):
    xb = xc.astype(bf16)
    r = _sigmoid(_dot(xb, wr_ref[...]) + br_ref[...])
    i = _sigmoid(_dot(xb, wi_ref[...]) + bi_ref[...])
    log_a = (-RG_C) * r * _softplus(-lam_ref[...])
    return jnp.exp(log_a), jnp.sqrt(-_expm1(2.0 * log_a)) * (i * xc)


def _even_front_kernel(xin_ref, sc_ref, sh_ref, w_ref, cw_ref, cb_ref, wr_ref, wi_ref, br_ref, bi_ref, lam_ref,
                       z_ref, y_ref, conv_ref, h_ref, xbuf, a_scr, h_scr, car):
    t_idx = pl.program_id(1)
    lt = xin_ref.shape[0]
    nz = z_ref.shape[1]

    @pl.when(t_idx == 0)
    def _():
        xbuf[0:SUBLANES, :] = jnp.zeros((SUBLANES, LRU_W), f32)
        car[...] = jnp.zeros(car.shape, f32)

    hin = (xin_ref[...] * (1.0 + sc_ref[...]) + sh_ref[...]).astype(bf16)
    x = _dot(hin, w_ref[:, nz:nz + LRU_W])
    gr = _dot(hin, w_ref[:, nz + LRU_W:nz + 2 * LRU_W])
    half = 2 * LANES
    z_ref[:, 0:half] = _dot(hin, w_ref[:, 0:half])
    xbuf[SUBLANES:SUBLANES + lt, :] = x
    xc = cb_ref[...] + cw_ref[CONV_W - 1:CONV_W, :] * x
    for j in range(CONV_W - 1):
        off = SUBLANES - (CONV_W - 1) + j
        xc = xc + cw_ref[j:j + 1, :] * xbuf[off:off + lt, :]
    xbuf[0:SUBLANES, :] = xbuf[lt:lt + SUBLANES, :]

    a, hh = _lru_gates(xc, wr_ref, wi_ref, br_ref, bi_ref, lam_ref)
    z_ref[:, half:nz] = _dot(hin, w_ref[:, half:nz])
    grp = (lt // SUBLANES, SUBLANES, LRU_W)
    a, hh = a.reshape(grp), hh.reshape(grp)
    sub = lax.broadcasted_iota(jnp.int32, (SUBLANES, LRU_W), 0)
    for d in (1, 2, 4):
        m = sub >= d
        hh = hh + jnp.where(m, a, 0.0) * pltpu.roll(hh, d, 1)
        a = a * jnp.where(m, pltpu.roll(a, d, 1), 1.0)
    a_scr[...] = a.reshape(lt, LRU_W)
    h_scr[...] = hh.reshape(lt, LRU_W)

    def group(g, c):
        rows = pl.ds(pl.multiple_of(g * SUBLANES, SUBLANES), SUBLANES)
        hg = h_scr[rows, :] + a_scr[rows, :] * c
        h_scr[rows, :] = hg
        return jnp.broadcast_to(hg[SUBLANES - 1:SUBLANES, :], (SUBLANES, LRU_W))

    c = lax.fori_loop(0, lt // SUBLANES, group, car[...])
    car[...] = c
    y_ref[...] = h_scr[...] * _gelu(gr)
    conv_ref[...] = x[lt - (CONV_W - 1):lt, :]
    h_ref[...] = c[0:1, :]


def _even_front(x, mod, w, p, batch, seq, lt):
    tiles = seq // lt
    row = lambda wd: pl.BlockSpec((lt, wd), lambda b, t: (b * tiles + t, 0))
    modv = lambda j: pl.BlockSpec((None, 1, D_MODEL), lambda b, t: (b, 0, j))
    consts = [p["conv_w"], p["conv_b"], p["w_r"], p["w_i"], p["b_r"], p["b_i"], p["lam"]]
    return pl.pallas_call(
        _even_front_kernel,
        out_shape=(jax.ShapeDtypeStruct((batch * seq, EVEN_Z), f32),
                   jax.ShapeDtypeStruct((batch * seq, LRU_W), f32),
                   jax.ShapeDtypeStruct((batch, CONV_W - 1, LRU_W), f32),
                   jax.ShapeDtypeStruct((batch, 1, LRU_W), f32)),
        grid=(batch, tiles),
        in_specs=[row(D_MODEL), modv(1), modv(0), _const_spec(w.shape)] + [_const_spec(c.shape) for c in consts],
        out_specs=(row(EVEN_Z), row(LRU_W),
                   pl.BlockSpec((None, CONV_W - 1, LRU_W), lambda b, t: (b, 0, 0)),
                   pl.BlockSpec((None, 1, LRU_W), lambda b, t: (b, 0, 0))),
        scratch_shapes=[pltpu.VMEM((lt + SUBLANES, LRU_W), f32), pltpu.VMEM((lt, LRU_W), f32),
                        pltpu.VMEM((lt, LRU_W), f32), pltpu.VMEM((SUBLANES, LRU_W), f32)],
        compiler_params=_params("parallel", "arbitrary"),
        name="even_front",
    )(x, mod, mod, w, *consts)


S5_R = 4
S5_TAB = 8


S5_GPB = LANES // S5_GH
S5_NBLK = S5_G // S5_GPB


def _s5_zoh(lr_, li_, ldt):
    dt = jnp.exp(ldt)
    mag = jnp.exp(lr_ * dt)
    ar, ai = mag * jnp.cos(li_ * dt), mag * jnp.sin(li_ * dt)
    den = lr_ * lr_ + li_ * li_
    nr, ni = ar - 1.0, ai
    return ar, ai, (nr * lr_ + ni * li_) / den, (ni * lr_ - nr * li_) / den


def _s5_param_kernel(lr_ref, li_ref, ldt_ref, lrf_ref, lif_ref, ldtf_ref, btr_ref, bti_ref, cr_ref, ci_ref,
                     a1r_ref, a1i_ref, pcr_ref, pci_ref, wx_ref, wct_ref, ct_ref, bd_s):
    ar, ai, fr, fi = _s5_zoh(lr_ref[...], li_ref[...], ldt_ref[...])
    bbr = fr[:, None, :] * btr_ref[...] - fi[:, None, :] * bti_ref[...]
    bbi = fr[:, None, :] * bti_ref[...] + fi[:, None, :] * btr_ref[...]
    cr, ci = cr_ref[...], ci_ref[...]

    def put(dst_ref, row0, re, im):
        for j in range(S5_NBLK):
            bd_s[...] = jnp.zeros(bd_s.shape, f32)
            for g in range(S5_GPB):
                rows = slice(g * S5_GH, (g + 1) * S5_GH)
                bd_s[rows, g * S5_P:(g + 1) * S5_P] = re[j * S5_GPB + g]
                bd_s[rows, S5_BLK + g * S5_P:S5_BLK + (g + 1) * S5_P] = im[j * S5_GPB + g]
            dst_ref[j, row0:row0 + LANES, :] = bd_s[...].astype(bf16)

    put(ct_ref, 0, cr, -ci)
    pr, pi = jnp.ones_like(ar), jnp.zeros_like(ai)
    for n in range(S5_R + 1):
        if n < S5_R:
            put(wx_ref, (S5_R - 1 - n) * LANES,
                pr[:, None, :] * bbr - pi[:, None, :] * bbi, pr[:, None, :] * bbi + pi[:, None, :] * bbr)
        if n >= 1:
            put(wct_ref, (n - 1) * LANES,
                cr * pr[:, None, :] - ci * pi[:, None, :], -(cr * pi[:, None, :] + ci * pr[:, None, :]))
        pr, pi = pr * ar - pi * ai, pr * ai + pi * ar

    ar, ai, _, _ = _s5_zoh(lrf_ref[...], lif_ref[...], ldtf_ref[...])
    a1r_ref[...], a1i_ref[...] = ar, ai
    pr, pi = ar, ai
    for n in range(1, S5_R * S5_TAB + 1):
        if n % S5_R == 0:
            pcr_ref[n // S5_R - 1:n // S5_R, :] = pr
            pci_ref[n // S5_R - 1:n // S5_R, :] = pi
        pr, pi = pr * ar - pi * ai, pr * ai + pi * ar


def _toeplitz_kernel(wx_ref, ct_ref, tp_ref):
    lag = [_dot_nt(wx_ref[(S5_R - 1 - d) * LANES:(S5_R - d) * LANES, :], ct_ref[...]).astype(bf16)
           for d in range(S5_R)]
    tp_ref[...] = jnp.zeros(tp_ref.shape, bf16)
    for s in range(S5_R):
        for r in range(s, S5_R):
            tp_ref[s * LANES:(s + 1) * LANES, r * LANES:(r + 1) * LANES] = lag[r - s]


def _s5_params(lam_re, lam_im, log_dt, b_re, b_im, c_re, c_im):
    btr = jnp.swapaxes(b_re, 1, 2)
    bti = jnp.swapaxes(b_im, 1, 2)
    shp = jax.ShapeDtypeStruct
    flat = lambda a: a.reshape(1, S5_N)
    wide = (S5_NBLK, S5_R * LANES, 2 * S5_BLK)
    a1r, a1i, pcr, pci, w_x, w_ct, c_t = pl.pallas_call(
        _s5_param_kernel,
        out_shape=(shp((1, S5_N), f32), shp((1, S5_N), f32), shp((S5_TAB, S5_N), f32), shp((S5_TAB, S5_N), f32),
                   shp(wide, bf16), shp(wide, bf16), shp((S5_NBLK, LANES, 2 * S5_BLK), bf16)),
        scratch_shapes=[pltpu.VMEM((LANES, 2 * S5_BLK), f32)],
        name="s5_params",
    )(lam_re, lam_im, log_dt.reshape(S5_G, 1), flat(lam_re), flat(lam_im), flat(jnp.repeat(log_dt, S5_P)),
      btr, bti, c_re, c_im)
    toep = pl.pallas_call(
        _toeplitz_kernel,
        out_shape=shp((S5_NBLK, S5_R * LANES, S5_R * LANES), bf16),
        grid=(S5_NBLK,),
        in_specs=[pl.BlockSpec((None,) + wide[1:], lambda j: (j, 0, 0)),
                  pl.BlockSpec((None, LANES, 2 * S5_BLK), lambda j: (j, 0, 0))],
        out_specs=pl.BlockSpec((None, S5_R * LANES, S5_R * LANES), lambda j: (j, 0, 0)),
        compiler_params=_params("parallel"),
        name="s5_toeplitz",
    )(w_x, c_t)
    return dict(a1r=a1r, a1i=a1i, pcr=pcr, pci=pci, c_t=c_t, w_x=w_x, toep=toep, w_ct=w_ct)


S5_BLK = S5_N // (S5_G // (LANES // S5_GH))


def _s5_input(u, wx_ref, j):
    b_bar = wx_ref[j, (S5_R - 1) * LANES:S5_R * LANES, :]
    bu = _dot(u[:, j * LANES:(j + 1) * LANES].astype(bf16), b_bar)
    return bu[:, 0:S5_BLK], bu[:, S5_BLK:2 * S5_BLK]


def _s5_output(u, hr, hi, ct_ref, d_ref, wg_ref, bg_ref):
    ys = []
    for j in range(S5_N // S5_BLK):
        cs = slice(j * S5_BLK, (j + 1) * S5_BLK)
        hcat = jnp.concatenate([hr[:, cs], hi[:, cs]], axis=1).astype(bf16)
        ys.append(_dot_nt(hcat, ct_ref[j]))
    y = _gelu(jnp.concatenate(ys, axis=1) + d_ref[...] * u)
    return y * _sigmoid(_dot(y.astype(bf16), wg_ref[...]) + bg_ref[...])


def _s5_prompt_kernel(*refs):
    nblk = S5_N // S5_BLK
    u_refs = refs[:nblk]
    wx_ref, tp_ref, wct_ref, pcr_ref, pci_ref, d_ref, wg_ref, bg_ref = refs[nblk:nblk + 8]
    y_refs = refs[nblk + 8:2 * nblk + 8]
    sr_ref, si_ref, er_s, ei_s, y_s = refs[2 * nblk + 8:]
    t_idx = pl.program_id(1)
    cr = u_refs[0].shape[0] // S5_R
    gw = S5_R * LANES

    @pl.when(t_idx == 0)
    def _():
        er_s[0:SUBLANES, :] = jnp.zeros((SUBLANES, S5_N), f32)
        ei_s[0:SUBLANES, :] = jnp.zeros((SUBLANES, S5_N), f32)

    def token(s):
        return pl.ds(s, cr, stride=S5_R)

    sub = lax.broadcasted_iota(jnp.int32, (SUBLANES, S5_BLK), 0)
    for j in range(nblk):
        cs = slice(j * S5_BLK, (j + 1) * S5_BLK)
        uj = jnp.concatenate([u_refs[j][token(s), :] for s in range(S5_R)], axis=1).astype(bf16)
        x = _dot(uj, wx_ref[j])
        y_s[:, j * gw:(j + 1) * gw] = _dot(uj, tp_ref[j])
        hr = x[:, 0:S5_BLK].reshape(cr // SUBLANES, SUBLANES, S5_BLK)
        hi = x[:, S5_BLK:2 * S5_BLK].reshape(cr // SUBLANES, SUBLANES, S5_BLK)
        for d in (1, 2, 4):
            pr = jnp.where(sub >= d, pcr_ref[d - 1:d, cs], 0.0)
            pi = jnp.where(sub >= d, pci_ref[d - 1:d, cs], 0.0)
            sr, si = pltpu.roll(hr, d, 1), pltpu.roll(hi, d, 1)
            hr, hi = hr + (pr * sr - pi * si), hi + (pr * si + pi * sr)
        er_s[SUBLANES:SUBLANES + cr, cs] = hr.reshape(cr, S5_BLK)
        ei_s[SUBLANES:SUBLANES + cr, cs] = hi.reshape(cr, S5_BLK)

    tr, ti = pcr_ref[...], pci_ref[...]
    last = pl.ds(SUBLANES - 1, 1)
    c0 = (jnp.broadcast_to(er_s[last, :], (SUBLANES, S5_N)), jnp.broadcast_to(ei_s[last, :], (SUBLANES, S5_N)))

    def group(g, c):
        c_r, c_i = c
        rows = pl.ds(pl.multiple_of((g + 1) * SUBLANES, SUBLANES), SUBLANES)
        g_r = er_s[rows, :] + tr * c_r - ti * c_i
        g_i = ei_s[rows, :] + tr * c_i + ti * c_r
        er_s[rows, :] = g_r
        ei_s[rows, :] = g_i
        return (jnp.broadcast_to(g_r[SUBLANES - 1:SUBLANES, :], (SUBLANES, S5_N)),
                jnp.broadcast_to(g_i[SUBLANES - 1:SUBLANES, :], (SUBLANES, S5_N)))

    c_r, c_i = lax.fori_loop(0, cr // SUBLANES, group, c0)

    for j in range(nblk):
        cs = slice(j * S5_BLK, (j + 1) * S5_BLK)
        prev = pl.ds(SUBLANES - 1, cr)
        sj = jnp.concatenate([er_s[prev, cs], ei_s[prev, cs]], axis=1).astype(bf16)
        y_s[:, j * gw:(j + 1) * gw] += _dot_nt(sj, wct_ref[j])
    er_s[last, :] = c_r[0:1, :]
    ei_s[last, :] = c_i[0:1, :]
    sr_ref[...] = c_r[0:1, :]
    si_ref[...] = c_i[0:1, :]

    for r in range(S5_R):
        y = jnp.concatenate([y_s[:, j * gw + r * LANES:j * gw + (r + 1) * LANES] for j in range(nblk)], axis=1)
        u_r = jnp.concatenate([u_refs[j][token(r), :] for j in range(nblk)], axis=1)
        y = _gelu(y + d_ref[...] * u_r)
        y = y * _sigmoid(_dot(y.astype(bf16), wg_ref[...]) + bg_ref[...])
        for j in range(nblk):
            y_refs[j][token(r), :] = y[:, j * LANES:(j + 1) * LANES]


def _s5_prompt(us, tabs, p, batch, seq, cr):
    nblk = len(us)
    tiles = seq // (S5_R * cr)
    consts = [tabs["w_x"], tabs["toep"], tabs["w_ct"], tabs["pcr"], tabs["pci"], p["d"], p["w_glu"], p["b_glu"]]
    blk = pl.BlockSpec((S5_R * cr, LANES), lambda b, t: (b * tiles + t, 0))
    state = pl.BlockSpec((None, 1, S5_N), lambda b, t: (b, 0, 0))
    outs = pl.pallas_call(
        _s5_prompt_kernel,
        out_shape=tuple(jax.ShapeDtypeStruct((batch * seq, LANES), f32) for _ in range(nblk))
        + (jax.ShapeDtypeStruct((batch, 1, S5_N), f32), jax.ShapeDtypeStruct((batch, 1, S5_N), f32)),
        grid=(batch, tiles),
        in_specs=[blk] * nblk + [_const_spec(c.shape) for c in consts],
        out_specs=(blk,) * nblk + (state, state),
        scratch_shapes=[pltpu.VMEM((cr + SUBLANES, S5_N), f32), pltpu.VMEM((cr + SUBLANES, S5_N), f32),
                        pltpu.VMEM((cr, S5_R * S5_W), f32)],
        compiler_params=_params("parallel", "arbitrary"),
        name="s5_prompt",
    )(*us, *consts)
    return list(outs[:nblk]), outs[nblk], outs[nblk + 1]


def _gla_step_kernel(kind, dk, layer, *refs):
    if kind == "gla":
        q_ref, k_ref, v_ref, gate_ref, lr_ref, wlr_ref, blr_ref, ng_ref, s_ref, y_ref, so_ref, o_scr = refs
        sl = pl.program_id(0)
        q = q_ref[...] * (GLA_DK ** -0.5)
        k = k_ref[...]
        pre_all = _dot(lr_ref[...].astype(bf16), wlr_ref[...]) + blr_ref[...]
        pre = jnp.where(sl == 0, pre_all[:, 0:LANES], pre_all[:, LANES:2 * LANES])
        a = jnp.exp(_log_sigmoid(pre) * (1.0 / GLA_TAU))
    else:
        q_ref, f_ref, v_ref, gate_ref, lg_ref, ng_ref, s_ref, y_ref, so_ref, o_scr = refs
        sl = pl.program_id(0)
        lb_all = _hgrn_lower_bound(lg_ref, layer)
        lb = lb_all[:, 0:LANES]
        for j in range(1, HG_H):
            lb = jnp.where(sl == j, lb_all[:, j * LANES:(j + 1) * LANES], lb)
        a = lb + (1.0 - lb) * _sigmoid(f_ref[...])
        q = _silu(q_ref[...])
        k = 1.0 - a
    hps = LANES // dk
    nb = q.shape[0]
    qat, kt, at = (q * a).T, k.T, a.T
    lane = lax.broadcasted_iota(jnp.int32, (1, LANES), 1)
    v = v_ref[...]
    for hh in range(hps):
        r0 = hh * dk
        msk = (lane >= r0) & (lane < r0 + dk)
        qk = jnp.sum(jnp.where(msk, q * k, 0.0), axis=-1, keepdims=True)
        for b in range(nb):
            s = s_ref[b, hh]
            col = lambda x: jnp.broadcast_to(x[r0:r0 + dk, b:b + 1], (dk, LANES))
            vr = v[b:b + 1, hh * LANES:(hh + 1) * LANES]
            so_ref[b, hh] = col(at) * s + col(kt) * vr
            oi = jnp.sum(col(qat) * s, axis=0, keepdims=True)
            o_scr[b:b + 1, hh * LANES:(hh + 1) * LANES] = jnp.broadcast_to(qk[b:b + 1, :], (1, LANES)) * vr + oi
    for hh in range(hps):
        cs = slice(hh * LANES, (hh + 1) * LANES)
        o = o_scr[:, cs]
        o = o * lax.rsqrt(jnp.mean(o * o, axis=-1, keepdims=True) + RMS_EPS) * ng_ref[...]
        y_ref[:, cs] = o * _silu(gate_ref[:, cs])


def _gla_step(kind, z, extra, norm_g, state, layer):
    nb = z.shape[0]
    dk = GLA_DK if kind == "gla" else HG_D
    n_heads = GLA_H if kind == "gla" else HG_H
    hps = LANES // dk
    n_slabs = n_heads // hps
    vw = hps * LANES
    zb = lambda width, cb0: pl.BlockSpec((nb, width), lambda s: (0, cb0 + s))
    if kind == "gla":
        wlr, blr = extra
        ins = [z, z, z, z, z, wlr, blr]
        specs = [zb(LANES, 0), zb(LANES, GLA_KW // LANES), zb(vw, GLA_VW // vw * 1), zb(vw, GLA_VW // vw * 2),
                 pl.BlockSpec((nb, LANES), lambda s: (0, GLA_LR_BLOCK)),
                 _const_spec(wlr.shape), _const_spec(blr.shape)]
    else:
        (lg,) = extra
        ins = [z, z, z, z, lg]
        specs = [zb(LANES, 0), zb(LANES, HG_W // LANES), zb(vw, HG_W // vw * 2),
                 zb(vw, HG_W // vw * 3), _const_spec(lg.shape)]
    ins += [norm_g, state]
    specs += [_const_spec(norm_g.shape), pl.BlockSpec((nb, hps, dk, LANES), lambda s: (0, s, 0, 0))]
    return pl.pallas_call(
        functools.partial(_gla_step_kernel, kind, dk, layer),
        out_shape=(jax.ShapeDtypeStruct((nb, HALF), f32), jax.ShapeDtypeStruct(state.shape, f32)),
        grid=(n_slabs,),
        in_specs=specs,
        out_specs=(pl.BlockSpec((nb, vw), lambda s: (0, s)),
                   pl.BlockSpec((nb, hps, dk, LANES), lambda s: (0, s, 0, 0))),
        scratch_shapes=[pltpu.VMEM((nb, vw), f32)],
        compiler_params=_params("parallel"),
        name=kind + "_step",
    )(*ins)


def _lru_step_kernel(x_ref, gr_ref, c0_ref, c1_ref, c2_ref, h0_ref, cw_ref, cb_ref, wr_ref, wi_ref,
                     br_ref, bi_ref, lam_ref, y_ref, h_ref):
    xc = (cb_ref[...] + cw_ref[0:1, :] * c0_ref[...] + cw_ref[1:2, :] * c1_ref[...]
          + cw_ref[2:3, :] * c2_ref[...] + cw_ref[3:4, :] * x_ref[...])
    a, u = _lru_gates(xc, wr_ref, wi_ref, br_ref, bi_ref, lam_ref)
    h = a * h0_ref[...] + u
    h_ref[...] = h
    y_ref[...] = h * _gelu(gr_ref[...])


def _lru_step(xr, gr, conv_rows, h0, p):
    nb = xr.shape[0]
    full = lambda: pl.BlockSpec((nb, LRU_W), lambda i: (0, 0))
    consts = [p["conv_w"], p["conv_b"], p["w_r"], p["w_i"], p["b_r"], p["b_i"], p["lam"]]
    return pl.pallas_call(
        _lru_step_kernel,
        out_shape=(jax.ShapeDtypeStruct((nb, LRU_W), f32), jax.ShapeDtypeStruct((nb, LRU_W), f32)),
        grid=(1,),
        in_specs=[full() for _ in range(CONV_W + 2)] + [_const_spec(c.shape) for c in consts],
        out_specs=(full(), full()),
        compiler_params=_params("arbitrary"),
        name="lru_step",
    )(xr, gr, *conv_rows, h0, *consts)


def _s5_step_kernel(u_ref, s0r_ref, s0i_ref, a1r_ref, a1i_ref, wx_ref, ct_ref, d_ref, wg_ref, bg_ref,
                    y_ref, sr_ref, si_ref):
    u = u_ref[...]
    for j in range(S5_N // S5_BLK):
        cs = slice(j * S5_BLK, (j + 1) * S5_BLK)
        bur, bui = _s5_input(u, wx_ref, j)
        ar, ai = a1r_ref[:, cs], a1i_ref[:, cs]
        s0r, s0i = s0r_ref[:, cs], s0i_ref[:, cs]
        sr_ref[:, cs] = bur + ar * s0r - ai * s0i
        si_ref[:, cs] = bui + ar * s0i + ai * s0r
    y_ref[...] = _s5_output(u, sr_ref[...], si_ref[...], ct_ref, d_ref, wg_ref, bg_ref)


def _s5_step(u, s0r, s0i, tabs, p):
    nb = u.shape[0]
    consts = [tabs["a1r"], tabs["a1i"], tabs["w_x"], tabs["c_t"], p["d"], p["w_glu"], p["b_glu"]]
    st = lambda: pl.BlockSpec((nb, S5_N), lambda i: (0, 0))
    return pl.pallas_call(
        _s5_step_kernel,
        out_shape=(jax.ShapeDtypeStruct((nb, S5_W), f32), jax.ShapeDtypeStruct((nb, S5_N), f32),
                   jax.ShapeDtypeStruct((nb, S5_N), f32)),
        grid=(1,),
        in_specs=[pl.BlockSpec((nb, S5_W), lambda i: (0, 0)), st(), st()] + [_const_spec(c.shape) for c in consts],
        out_specs=(pl.BlockSpec((nb, S5_W), lambda i: (0, 0)), st(), st()),
        compiler_params=_params("arbitrary"),
        name="s5_step",
    )(u, s0r, s0i, *consts)


def _block_diag(w):
    h, n, _ = w.shape
    return jnp.einsum('hij,hk->hikj', w, jnp.eye(h, dtype=w.dtype)).reshape(h * n, h * n)


def kernel(x_prompt, x_sample, c_prompt, c_sample, state_gla, state_rglru_conv, state_rglru_h, state_s5_re, state_s5_im, state_hgrn, ev_w_in, ev_gla_w_lr, ev_gla_b_lr, ev_gla_norm_g, ev_conv_w, ev_conv_b, ev_lru_w_r, ev_lru_b_r, ev_lru_w_i, ev_lru_b_i, ev_lru_lam, ev_w_out, od_w_in, od_s5_lam_re, od_s5_lam_im, od_s5_log_dt, od_s5_b_re, od_s5_b_im, od_s5_c_re, od_s5_c_im, od_s5_d, od_s5_w_glu, od_s5_b_glu, hg_lb_logits, od_hg_norm_g, od_w_out, w_ada, b_ada, ln_g, ln_b, ffn_w_in, ffn_w_out):
    bp, seq = x_prompt.shape[0], x_prompt.shape[1]
    bs = x_sample.shape[0]
    assert x_sample.shape[1] == 1 and seq % CHUNK == 0
    tp = bp * seq
    tm = min(512, seq)
    lt = min(512, seq)
    lt_gla = min(2048, seq)
    cr_s5 = min(256, seq // S5_R)
    row2 = lambda a: a.reshape(1, -1)

    mod = _modulation(jnp.concatenate([c_prompt, c_sample], axis=0).astype(f32), w_ada, b_ada)
    mod_p = mod[:, :bp].reshape(DEPTH, bp, 1, 6 * D_MODEL)
    mod_s = mod[:, bp:].reshape(DEPTH, 1, bs, 6 * D_MODEL)

    xp = x_prompt.astype(f32).reshape(tp, D_MODEL)
    xs = x_sample.astype(f32).reshape(bs, D_MODEL)
    outs_p = {k: [] for k in ("gla", "conv", "lru", "re", "im", "hg")}
    outs_s = {k: [] for k in ("gla", "conv", "lru", "re", "im", "hg")}

    w_f_in = ffn_w_in.astype(bf16)
    w_f_out = ffn_w_out.astype(bf16)
    for l in range(DEPTH):
        if l % 2 == 0:
            e = l // 2
            w = ev_w_in[e]
            lr0 = 2 * GLA_KW + 2 * GLA_VW
            w_in = jnp.concatenate(
                [w[:, :lr0 + GLA_LR], jnp.zeros((D_MODEL, LANES - GLA_LR), w.dtype), w[:, lr0 + GLA_LR:]],
                axis=1).astype(bf16)
            w_out = ev_w_out[e].astype(bf16)
            wlr = jnp.concatenate([ev_gla_w_lr[e], jnp.zeros((LANES - GLA_LR, GLA_KW), f32)], axis=0).astype(bf16)
            gla_extra = (wlr, row2(ev_gla_b_lr[e]))
            ng = row2(ev_gla_norm_g[e])
            lru_p = dict(conv_w=ev_conv_w[e], conv_b=row2(ev_conv_b[e]),
                         w_r=_block_diag(ev_lru_w_r[e]).astype(bf16), w_i=_block_diag(ev_lru_w_i[e]).astype(bf16),
                         b_r=row2(ev_lru_b_r[e]), b_i=row2(ev_lru_b_i[e]), lam=row2(ev_lru_lam[e]))
            z, yb, s_conv, s_h = _even_front(xp, mod_p[l], w_in, lru_p, bp, seq, lt)
            ya, s_gla = _gla_prompt("gla", z, GLA_LR_BLOCK, gla_extra, ng, bp, seq, lt_gla, l)
            xp = _outffn(xp, [ya, yb], mod_p[l], w_out, ln_g[l], ln_b[l], w_f_in, w_f_out, l, tm, seq // tm)
            outs_p["gla"].append(s_gla)
            outs_p["conv"].append(s_conv)
            outs_p["lru"].append(s_h.reshape(bp, LRU_W))
            z, xr, gr = _inproj(xs, mod_s[l], w_in, bs, 1, (EVEN_Z, LRU_W, LRU_W))
            ya, s_gla = _gla_step("gla", z, gla_extra, ng, state_gla[e], l)
            cs = state_rglru_conv[e].astype(f32)
            yb, s_h = _lru_step(xr, gr, [cs[:, j] for j in range(CONV_W - 1)], state_rglru_h[e].astype(f32), lru_p)
            xs = _outffn(xs, [ya, yb], mod_s[l], w_out, ln_g[l], ln_b[l], w_f_in, w_f_out, l, bs, 1)
            outs_s["gla"].append(s_gla)
            outs_s["conv"].append(jnp.stack([cs[:, 1], cs[:, 2], xr], axis=1))
            outs_s["lru"].append(s_h)
        else:
            o = l // 2
            w_in = od_w_in[o].astype(bf16)
            w_out = od_w_out[o].astype(bf16)
            tabs = _s5_params(od_s5_lam_re[o], od_s5_lam_im[o], od_s5_log_dt[o], od_s5_b_re[o], od_s5_b_im[o],
                              od_s5_c_re[o], od_s5_c_im[o])
            s5_p = dict(d=row2(od_s5_d[o]), w_glu=od_s5_w_glu[o].astype(bf16), b_glu=row2(od_s5_b_glu[o]))
            hg_extra = (hg_lb_logits.astype(f32),)
            ng = row2(od_hg_norm_g[o])
            *us, z = _inproj(xp, mod_p[l], w_in, tm, seq // tm, (LANES,) * (S5_W // LANES) + (ODD_N - S5_W,))
            yas, s_re, s_im = _s5_prompt(us, tabs, s5_p, bp, seq, cr_s5)
            yb, s_hg = _gla_prompt("hgrn", z, 0, hg_extra, ng, bp, seq, lt_gla, l)
            xp = _outffn(xp, yas + [yb], mod_p[l], w_out, ln_g[l], ln_b[l], w_f_in, w_f_out, l, tm, seq // tm)
            outs_p["re"].append(s_re.reshape(bp, S5_G, S5_P))
            outs_p["im"].append(s_im.reshape(bp, S5_G, S5_P))
            outs_p["hg"].append(s_hg)
            u, z = _inproj(xs, mod_s[l], w_in, bs, 1, (S5_W, ODD_N - S5_W))
            ya, s_re, s_im = _s5_step(u, state_s5_re[o].astype(f32).reshape(bs, S5_N),
                                      state_s5_im[o].astype(f32).reshape(bs, S5_N), tabs, s5_p)
            yb, s_hg = _gla_step("hgrn", z, hg_extra, ng, state_hgrn[o], l)
            xs = _outffn(xs, [ya, yb], mod_s[l], w_out, ln_g[l], ln_b[l], w_f_in, w_f_out, l, bs, 1)
            outs_s["re"].append(s_re.reshape(bs, S5_G, S5_P))
            outs_s["im"].append(s_im.reshape(bs, S5_G, S5_P))
            outs_s["hg"].append(s_hg)

    st = lambda d, k: d[k][0][None] if len(d[k]) == 1 else jnp.stack(d[k])
    return (xp.reshape(bp, seq, D_MODEL).astype(x_prompt.dtype), xs.reshape(bs, 1, D_MODEL).astype(x_sample.dtype),
            st(outs_p, "gla"), st(outs_s, "gla"), st(outs_p, "conv"), st(outs_s, "conv"),
            st(outs_p, "lru"), st(outs_s, "lru"), st(outs_p, "re"), st(outs_s, "re"),
            st(outs_p, "im"), st(outs_s, "im"), st(outs_p, "hg"), st(outs_s, "hg"))
```

```python
import functools
import math

import numpy as np
import jax
import jax.numpy as jnp
from jax import lax
from jax.experimental import pallas as pl
from jax.experimental.pallas import tpu as pltpu

f32 = jnp.float32
bf16 = jnp.bfloat16

D_MODEL = 1024
DEPTH = 2
HALF = D_MODEL // 2
GLA_H = 4
GLA_DV = HALF // GLA_H
GLA_DK = GLA_DV // 2
GLA_KW = GLA_H * GLA_DK
GLA_VW = GLA_H * GLA_DV
GLA_LR = 16
GLA_TAU = 16.0
LRU_W = HALF
LRU_H = 8
LRU_BW = LRU_W // LRU_H
CONV_W = 4
RG_C = 8.0
S5_W = HALF
S5_GH = 16
S5_G = S5_W // S5_GH
S5_P = 64
S5_N = S5_G * S5_P
HG_H = 4
HG_D = HALF // HG_H
HG_W = HG_H * HG_D
D_FF = ((8 * D_MODEL // 3 + 255) // 256) * 256
ALPHA = (2.0 * DEPTH) ** 0.25
LN_EPS = 1e-5
RMS_EPS = 1e-6

LANES = 128
SUBLANES = 8
VMEM_LIMIT = 56 * 1024 * 1024

CHUNK = 64
N_LEVELS = 6
FINE_LEVELS = 3
GLA_UNROLL = 3
FF_CHUNK = 256
FFN_SUB_ROWS = 512
EVEN_Z = 2 * GLA_KW + 2 * GLA_VW + LANES
EVEN_N = EVEN_Z + 2 * LRU_W
GLA_LR_BLOCK = (EVEN_Z - LANES) // LANES
ODD_N = 2560


def _sigmoid(x):
    return 0.5 * jnp.tanh(0.5 * x) + 0.5


def _silu(x):
    return x * _sigmoid(x)


def _gelu(x):
    c = math.sqrt(2.0 / math.pi)
    return x * (0.5 * (1.0 + jnp.tanh(c * (x + 0.044715 * (x * x * x)))))


def _softplus(x):
    return jnp.maximum(x, 0.0) + jnp.log1p(jnp.exp(-jnp.abs(x)))


def _log_sigmoid(x):
    return -_softplus(-x)


def _layer_norm(x, g, b):
    mu = jnp.mean(x, axis=-1, keepdims=True)
    xc = x - mu
    var = jnp.mean(xc * xc, axis=-1, keepdims=True)
    return xc * lax.rsqrt(var + LN_EPS) * g + b


def _dot(a, b):
    return jnp.dot(a, b, preferred_element_type=f32)


def _dot_nt(a, b):
    return lax.dot_general(a, b, (((1,), (1,)), ((), ())), preferred_element_type=f32)


def _dot_tn(a, b):
    return lax.dot_general(a, b, (((0,), (0,)), ((), ())), preferred_element_type=f32)


def _const_spec(shape):
    nd = len(shape)
    return pl.BlockSpec(shape, lambda *_: (0,) * nd, pipeline_mode=pl.Buffered(1))


def _params(*sem):
    return pltpu.CompilerParams(dimension_semantics=sem, vmem_limit_bytes=VMEM_LIMIT)


def _mod_kernel(c_ref, w_ref, b_ref, o_ref):
    cond = _silu(c_ref[...]).astype(bf16)
    o_ref[...] = _dot(cond, w_ref[...].astype(bf16)) + b_ref[...]


def _modulation(c_all, w_ada, b_ada):
    rows = c_all.shape[0]
    tn = 1536
    return pl.pallas_call(
        _mod_kernel,
        out_shape=jax.ShapeDtypeStruct((DEPTH, rows, 6 * D_MODEL), f32),
        grid=(DEPTH, 6 * D_MODEL // tn),
        in_specs=[pl.BlockSpec((rows, D_MODEL), lambda l, j: (0, 0)),
                  pl.BlockSpec((None, D_MODEL, tn), lambda l, j: (l, 0, j)),
                  pl.BlockSpec((None, 1, tn), lambda l, j: (l, 0, j))],
        out_specs=pl.BlockSpec((None, rows, tn), lambda l, j: (l, 0, j)),
        compiler_params=_params("parallel", "parallel"),
        name="adaln_mod",
    )(c_all, w_ada, b_ada.reshape(DEPTH, 1, 6 * D_MODEL))


def _mod_spec(rows_per_block, tiles_per_batch, j):
    return pl.BlockSpec((None, rows_per_block, D_MODEL), lambda i: (i // tiles_per_batch, 0, j))


def _inproj_kernel(x_ref, sc_ref, sh_ref, w_ref, *z_refs):
    h = (x_ref[...] * (1.0 + sc_ref[...]) + sh_ref[...]).astype(bf16)
    lo = 0
    for z_ref in z_refs:
        z_ref[...] = _dot(h, w_ref[:, lo:lo + z_ref.shape[1]])
        lo += z_ref.shape[1]


def _inproj(x, mod, w, tm, tiles_per_batch, widths):
    t, n = x.shape[0], w.shape[1]
    assert sum(widths) == n
    r = mod.shape[1]
    return pl.pallas_call(
        _inproj_kernel,
        out_shape=tuple(jax.ShapeDtypeStruct((t, wd), f32) for wd in widths),
        grid=(t // tm,),
        in_specs=[pl.BlockSpec((tm, D_MODEL), lambda i: (i, 0)),
                  _mod_spec(r, tiles_per_batch, 1), _mod_spec(r, tiles_per_batch, 0),
                  _const_spec((D_MODEL, n))],
        out_specs=tuple(pl.BlockSpec((tm, wd), lambda i: (i, 0)) for wd in widths),
        compiler_params=_params("parallel"),
        name="in_proj",
    )(x, mod, mod, w)


def _outffn_kernel(n_y, x_ref, *refs):
    y_refs = refs[:n_y]
    gtm_ref, shf_ref, scf_ref, gtf_ref, wo_ref, lng_ref, lnb_ref, wi_ref, wf_ref, o_ref = refs[n_y:]
    tm = x_ref.shape[0]
    n_sub = 2 if tm % (2 * FFN_SUB_ROWS) == 0 else 1
    rs = [slice(i * (tm // n_sub), (i + 1) * (tm // n_sub)) for i in range(n_sub)]
    mrows = lambda ref, rows: ref[...] if ref.shape[0] == 1 else ref[rows, :]
    n_ff = D_FF // FF_CHUNK

    def ln1(i, mix):
        rows = rs[i]
        x1 = _layer_norm(ALPHA * x_ref[rows, :] + (1.0 + mrows(gtm_ref, rows)) * mix, lng_ref[0:1, :], lnb_ref[0:1, :])
        return x1, (x1 * (1.0 + mrows(scf_ref, rows)) + mrows(shf_ref, rows)).astype(bf16)

    def ln2(i, x1, acc):
        rows = rs[i]
        o_ref[rows, :] = _layer_norm(ALPHA * x1 + (1.0 + mrows(gtf_ref, rows)) * acc, lng_ref[1:2, :], lnb_ref[1:2, :])

    mixes = [_dot(jnp.concatenate([r[rows, :].astype(bf16) for r in y_refs], axis=1), wo_ref[...]) for rows in rs]
    cur = ln1(0, mixes[0])
    pending = None
    for i in range(n_sub):
        x1, h = cur
        acc = jnp.zeros(x1.shape, f32)
        for c in range(n_ff):
            lo = c * FF_CHUNK
            gate = _dot(h, wi_ref[:, lo:lo + FF_CHUNK])
            up = _dot(h, wi_ref[:, D_FF + lo:D_FF + lo + FF_CHUNK])
            act = (_silu(gate) * up).astype(bf16)
            acc = acc + _dot(act, wf_ref[lo:lo + FF_CHUNK, :])
            if c == 1:
                if pending is not None:
                    ln2(*pending)
                    pending = None
                if i + 1 < n_sub:
                    cur = ln1(i + 1, mixes[i + 1])
        pending = (i, x1, acc)
    ln2(*pending)


def _outffn(x, ys, mod, w_out, ln_g, ln_b, w_in, w_f, layer, tm, tiles_per_batch):
    t = x.shape[0]
    r = mod.shape[1]
    assert sum(y.shape[1] for y in ys) == D_MODEL
    row = lambda w: pl.BlockSpec((tm, w), lambda i: (i, 0))
    stacked = lambda a: pl.BlockSpec((None,) + a.shape[1:], lambda i: (layer, 0, 0), pipeline_mode=pl.Buffered(1))
    return pl.pallas_call(
        functools.partial(_outffn_kernel, len(ys)),
        out_shape=jax.ShapeDtypeStruct((t, D_MODEL), f32),
        grid=(t // tm,),
        in_specs=[row(D_MODEL)] + [row(y.shape[1]) for y in ys] + [
                  _mod_spec(r, tiles_per_batch, 2), _mod_spec(r, tiles_per_batch, 3),
                  _mod_spec(r, tiles_per_batch, 4), _mod_spec(r, tiles_per_batch, 5),
                  _const_spec((D_MODEL, D_MODEL)), _const_spec((2, D_MODEL)), _const_spec((2, D_MODEL)),
                  stacked(w_in), stacked(w_f)],
        out_specs=row(D_MODEL),
        compiler_params=_params("parallel"),
        name="outproj_ffn",
    )(x, *ys, mod, mod, mod, mod, w_out, ln_g, ln_b, w_in, w_f)


def _level_tables():
    c = CHUNK
    t = np.arange(c)[:, None]
    u = np.arange(c)[None, :]
    pmask, sgn = [], []
    for lvl in range(N_LEVELS):
        m = 1 << lvl
        right = (t % (2 * m)) >= m
        pm = right & ((u % (2 * m)) < m) & ((t // (2 * m)) == (u // (2 * m)))
        pmask.append(pm.astype(np.float32))
        sgn.append(np.broadcast_to(np.where(right, 1.0, -1.0).astype(np.float32), (c, LANES)))
    tri = (u <= t).astype(np.float32)
    tri3 = np.concatenate([tri, tri, tri], axis=1)
    return (jnp.asarray(tri3, dtype=bf16), jnp.asarray(np.stack(pmask), dtype=f32),
            jnp.asarray(np.stack(sgn), dtype=f32))


LOG2E = 1.0 / math.log(2.0)


def _anchor(b, lvl):
    m = 1 << lvl
    b3 = b.reshape(CHUNK // SUBLANES, SUBLANES, LANES)
    if 4 * m == SUBLANES:
        sub = lax.broadcasted_iota(jnp.int32, b3.shape, 1)
        a = jnp.where(sub < 2 * m, jnp.broadcast_to(b3[:, m - 1:m, :], b3.shape),
                      jnp.broadcast_to(b3[:, 3 * m - 1:3 * m, :], b3.shape))
        return a.reshape(CHUNK, LANES)
    assert 2 * m == SUBLANES
    return jnp.broadcast_to(b3[:, m - 1:m, :], b3.shape).reshape(CHUNK, LANES)


def _split3(g):
    g1 = g.astype(bf16)
    r1 = g - g1.astype(f32)
    g2 = r1.astype(bf16)
    r2 = r1 - g2.astype(f32)
    return jnp.concatenate([g1, g2, r2.astype(bf16)], axis=0)


def _hgrn_lower_bound(lg_ref, layer):
    rows = [lg_ref[i:i + 1, :] for i in range(DEPTH)]
    mx = functools.reduce(jnp.maximum, rows)
    ex = [jnp.exp(r - mx) for r in rows]
    den = functools.reduce(lambda a, b: a + b, ex)
    sm = [e / den for e in ex]
    cs = sm[0]
    for i in range(1, layer + 1):
        cs = cs + sm[i]
    return cs - sm[0]


def _gla_prologue(kind, refs, rows):
    if kind == "gla":
        qk_ref, lr_ref, wlr_ref, blr_ref = refs
        q = qk_ref[rows, 0:GLA_KW] * (GLA_DK ** -0.5)
        k = qk_ref[rows, GLA_KW:2 * GLA_KW]
        pre = _dot(lr_ref[rows, :].astype(bf16), wlr_ref[...]) + blr_ref[...]
        g = _log_sigmoid(pre) * (LOG2E / GLA_TAU)
        return q, k, g
    q_ref, f_ref, lb = refs
    forget = lb + (1.0 - lb) * _sigmoid(f_ref[rows, :])
    return _silu(q_ref[rows, :]), 1.0 - forget, jnp.log2(forget)


def _gla_chunk_kernel(kind, dk, n_heads, layer, *refs):
    if kind == "gla":
        (qk_ref, v_ref, gate_ref, lr_ref, wlr_ref, blr_ref, ng_ref, tri_ref, pm_ref, sg_ref,
         y_ref, s_ref, st_ref) = refs
        pro = (qk_ref, lr_ref, wlr_ref, blr_ref)
    else:
        (q_ref, f_ref, v_ref, gate_ref, lg_ref, ng_ref, tri_ref, pm_ref, sg_ref,
         y_ref, s_ref, st_ref) = refs
        pro = (q_ref, f_ref, _hgrn_lower_bound(lg_ref, layer))
    t_idx = pl.program_id(1)
    n_tiles = pl.num_programs(1)
    lt = y_ref.shape[0]
    hps = LANES // dk
    n_slabs = n_heads // hps
    lane = lax.broadcasted_iota(jnp.int32, (1, LANES), 1)
    eye = (lax.broadcasted_iota(jnp.int32, (CHUNK, CHUNK), 0)
           == lax.broadcasted_iota(jnp.int32, (CHUNK, CHUNK), 1))

    @pl.when(t_idx == 0)
    def _():
        st_ref[...] = jnp.zeros(st_ref.shape, f32)

    def head_mask(hh):
        if hps == 1:
            return lambda a: a
        msk = (lane >= hh * dk) & (lane < (hh + 1) * dk)
        return lambda a: jnp.where(msk, a, jnp.zeros_like(a))

    def rows_of(c):
        return pl.ds(pl.multiple_of(c * CHUNK, CHUNK), CHUNK)

    def cumsums(qkg):
        q_all, k_all, g_all = qkg
        out = []
        for sl in range(n_slabs):
            cs = slice(sl * LANES, (sl + 1) * LANES)
            q_s, k_s, g_s = q_all[:, cs], k_all[:, cs], g_all[:, cs]
            out.append((q_s, k_s, g_s, _dot(tri_ref[...], _split3(g_s))))
        return out

    def score_dots(pre):
        raw, qbs, ksts, decs, rds = [], [], [], [], []
        right = [sg_ref[lvl] > 0.0 for lvl in range(FINE_LEVELS)]
        for sl in range(n_slabs):
            q_s, k_s, g_s, b = pre[sl]
            ys = [(jnp.where(right[0], jnp.exp2(g_s) * q_s, k_s)).astype(bf16)]
            for lvl in range(1, FINE_LEVELS):
                x = jnp.exp2((b - _anchor(b, lvl)) * sg_ref[lvl])
                ys.append((x * jnp.where(right[lvl], q_s, k_s)).astype(bf16))
            for lvl in range(FINE_LEVELS, N_LEVELS):
                m = 1 << lvl
                parts = []
                for r0 in range(0, CHUNK, SUBLANES):
                    a0 = (r0 // (2 * m)) * (2 * m) + m - 1
                    anc = jnp.broadcast_to(b[a0:a0 + 1, :], (SUBLANES, LANES))
                    bj = b[r0:r0 + SUBLANES, :]
                    if r0 % (2 * m) >= m:
                        parts.append(jnp.exp2(bj - anc) * q_s[r0:r0 + SUBLANES, :])
                    else:
                        parts.append(jnp.exp2(anc - bj) * k_s[r0:r0 + SUBLANES, :])
                ys.append(jnp.concatenate(parts, axis=0).astype(bf16))
            blast = jnp.broadcast_to(b[CHUNK - 1:CHUNK, :], b.shape)
            qb = (q_s * jnp.exp2(b)).astype(bf16)
            kst = (k_s * jnp.exp2(blast - b)).astype(bf16)
            decs.append(jnp.exp2(b[CHUNK - 1:CHUNK, :]))
            qk_prod = q_s * k_s
            for hh in range(hps):
                mul = head_mask(hh)
                rds.append(jnp.sum(mul(qk_prod), axis=-1, keepdims=True))
                raw.append([_dot_nt(mul(ys[lvl]), ys[lvl]) for lvl in range(N_LEVELS)])
                qbs.append(mul(qb))
                ksts.append(mul(kst))
        return raw, rds, tuple(qbs), tuple(ksts), tuple(decs)

    def combine(raw, rds):
        ps = []
        for h in range(n_heads):
            p = jnp.where(eye, rds[h], 0.0)
            for lvl in range(N_LEVELS):
                p = p + pm_ref[lvl] * raw[h][lvl]
            ps.append(p.astype(bf16))
        return tuple(ps)

    def apply_dots(c, sc):
        ps, qbs, ksts, decs = sc
        outs = []
        for h in range(n_heads):
            vb = v_ref[rows_of(c), h * LANES:(h + 1) * LANES].astype(bf16)
            st = st_ref[h]
            outs.append(_dot(ps[h], vb) + _dot_nt(qbs[h], st.astype(bf16)))
            st_ref[h] = st * decs[h // hps] + _dot_tn(vb, ksts[h])
        return outs

    def finish(c, outs):
        for h in range(n_heads):
            cs = slice(h * LANES, (h + 1) * LANES)
            o = outs[h]
            o = o * lax.rsqrt(jnp.mean(o * o, axis=-1, keepdims=True) + RMS_EPS) * ng_ref[...]
            y_ref[rows_of(c), cs] = o * _silu(gate_ref[rows_of(c), cs])

    def first():
        raw, rds, qbs, ksts, decs = score_dots(cumsums(_gla_prologue(kind, pro, rows_of(0))))
        return combine(raw, rds), qbs, ksts, decs

    def body(c, sc):
        qkg = _gla_prologue(kind, pro, rows_of(c + 1))
        if kind == "gla":
            outs = apply_dots(c, sc)
            pre = cumsums(qkg)
        else:
            pre = cumsums(qkg)
            outs = apply_dots(c, sc)
        raw, rds, qbs, ksts, decs = score_dots(pre)
        finish(c, outs)
        return combine(raw, rds), qbs, ksts, decs

    n_chunks = lt // CHUNK
    last = lax.fori_loop(0, n_chunks - 1, body, first(), unroll=min(GLA_UNROLL, n_chunks - 1))
    finish(n_chunks - 1, apply_dots(n_chunks - 1, last))

    @pl.when(t_idx == n_tiles - 1)
    def _():
        for h in range(n_heads):
            off = (h % hps) * dk
            s_ref[h] = st_ref[h].T[off:off + dk, :]


def _gla_prompt(kind, z, col0, extra, norm_g, batch, seq, lt, layer):
    dk = GLA_DK if kind == "gla" else HG_D
    n_heads = GLA_H if kind == "gla" else HG_H
    tiles = seq // lt
    mall, pmask, rmask = _level_tables()
    blk = lambda width, cb: pl.BlockSpec((lt, width), lambda b, t: (b * tiles + t, cb))
    if kind == "gla":
        wlr, blr = extra
        ins = [z, z, z, z, wlr, blr]
        specs = [blk(2 * GLA_KW, 0), blk(GLA_VW, 1), blk(GLA_VW, 2), blk(LANES, col0),
                 _const_spec(wlr.shape), _const_spec(blr.shape)]
    else:
        (lg,) = extra
        ins = [z, z, z, z, lg]
        specs = [blk(HG_W, col0), blk(HG_W, col0 + 1), blk(HG_W, col0 + 2), blk(HG_W, col0 + 3), _const_spec(lg.shape)]
    ins += [norm_g, mall, pmask, rmask]
    specs += [_const_spec(norm_g.shape), _const_spec(mall.shape), _const_spec(pmask.shape), _const_spec(rmask.shape)]
    return pl.pallas_call(
        functools.partial(_gla_chunk_kernel, kind, dk, n_heads, layer),
        out_shape=(jax.ShapeDtypeStruct((batch * seq, HALF), f32),
                   jax.ShapeDtypeStruct((batch, n_heads, dk, LANES), f32)),
        grid=(batch, tiles),
        in_specs=specs,
        out_specs=(pl.BlockSpec((lt, HALF), lambda b, t: (b * tiles + t, 0)),
                   pl.BlockSpec((None, n_heads, dk, LANES), lambda b, t: (b, 0, 0, 0))),
        scratch_shapes=[pltpu.VMEM((n_heads, LANES, LANES), f32)],
        compiler_params=_params("parallel", "arbitrary"),
        name=kind + "_prompt",
    )(*ins)


def _lru_gates(xc, wr_ref, wi_ref, br_ref, bi_ref, lam_ref):
    xb = xc.astype(bf16)
    r = _sigmoid(_dot(xb, wr_ref[...]) + br_ref[...])
    i = _sigmoid(_dot(xb, wi_ref[...]) + bi_ref[...])
    log_a = (-RG_C) * r * _softplus(-lam_ref[...])
    t = jnp.tanh(log_a)
    one_minus_a2 = (-2.0 * t) / (1.0 - t)
    return jnp.exp(log_a), jnp.sqrt(one_minus_a2) * (i * xc)


def _even_front_kernel(xin_ref, sc_ref, sh_ref, w_ref, cw_ref, cb_ref, wr_ref, wi_ref, br_ref, bi_ref, lam_ref,
                       z_ref, y_ref, conv_ref, h_ref, xbuf, a_scr, h_scr, car):
    t_idx = pl.program_id(1)
    lt = xin_ref.shape[0]
    nz = z_ref.shape[1]

    @pl.when(t_idx == 0)
    def _():
        xbuf[0:SUBLANES, :] = jnp.zeros((SUBLANES, LRU_W), f32)
        car[...] = jnp.zeros(car.shape, f32)

    hin = (xin_ref[...] * (1.0 + sc_ref[...]) + sh_ref[...]).astype(bf16)
    x = _dot(hin, w_ref[:, nz:nz + LRU_W])
    gr = _dot(hin, w_ref[:, nz + LRU_W:nz + 2 * LRU_W])
    half = 2 * LANES
    z_ref[:, 0:half] = _dot(hin, w_ref[:, 0:half])
    xbuf[SUBLANES:SUBLANES + lt, :] = x
    xc = cb_ref[...] + cw_ref[CONV_W - 1:CONV_W, :] * x
    for j in range(CONV_W - 1):
        off = SUBLANES - (CONV_W - 1) + j
        xc = xc + cw_ref[j:j + 1, :] * xbuf[off:off + lt, :]
    xbuf[0:SUBLANES, :] = xbuf[lt:lt + SUBLANES, :]

    a, hh = _lru_gates(xc, wr_ref, wi_ref, br_ref, bi_ref, lam_ref)
    z_ref[:, half:nz] = _dot(hin, w_ref[:, half:nz])
    grp = (lt // SUBLANES, SUBLANES, LRU_W)
    a, hh = a.reshape(grp), hh.reshape(grp)
    sub = lax.broadcasted_iota(jnp.int32, (SUBLANES, LRU_W), 0)
    for d in (1, 2, 4):
        m = sub >= d
        hh = hh + jnp.where(m, a, 0.0) * pltpu.roll(hh, d, 1)
        a = a * jnp.where(m, pltpu.roll(a, d, 1), 1.0)
    a_scr[...] = a.reshape(lt, LRU_W)
    h_scr[...] = hh.reshape(lt, LRU_W)

    def group(g, c):
        rows = pl.ds(pl.multiple_of(g * SUBLANES, SUBLANES), SUBLANES)
        hg = h_scr[rows, :] + a_scr[rows, :] * c
        h_scr[rows, :] = hg
        return jnp.broadcast_to(hg[SUBLANES - 1:SUBLANES, :], (SUBLANES, LRU_W))

    c = lax.fori_loop(0, lt // SUBLANES, group, car[...])
    car[...] = c
    y_ref[...] = h_scr[...] * _gelu(gr)
    conv_ref[...] = x[lt - (CONV_W - 1):lt, :]
    h_ref[...] = c[0:1, :]


def _even_front(x, mod, w, p, batch, seq, lt):
    tiles = seq // lt
    row = lambda wd: pl.BlockSpec((lt, wd), lambda b, t: (b * tiles + t, 0))
    modv = lambda j: pl.BlockSpec((None, 1, D_MODEL), lambda b, t: (b, 0, j))
    consts = [p["conv_w"], p["conv_b"], p["w_r"], p["w_i"], p["b_r"], p["b_i"], p["lam"]]
    return pl.pallas_call(
        _even_front_kernel,
        out_shape=(jax.ShapeDtypeStruct((batch * seq, EVEN_Z), f32),
                   jax.ShapeDtypeStruct((batch * seq, LRU_W), f32),
                   jax.ShapeDtypeStruct((batch, CONV_W - 1, LRU_W), f32),
                   jax.ShapeDtypeStruct((batch, 1, LRU_W), f32)),
        grid=(batch, tiles),
        in_specs=[row(D_MODEL), modv(1), modv(0), _const_spec(w.shape)] + [_const_spec(c.shape) for c in consts],
        out_specs=(row(EVEN_Z), row(LRU_W),
                   pl.BlockSpec((None, CONV_W - 1, LRU_W), lambda b, t: (b, 0, 0)),
                   pl.BlockSpec((None, 1, LRU_W), lambda b, t: (b, 0, 0))),
        scratch_shapes=[pltpu.VMEM((lt + SUBLANES, LRU_W), f32), pltpu.VMEM((lt, LRU_W), f32),
                        pltpu.VMEM((lt, LRU_W), f32), pltpu.VMEM((SUBLANES, LRU_W), f32)],
        compiler_params=_params("parallel", "arbitrary"),
        name="even_front",
    )(x, mod, mod, w, *consts)


S5_R = 4
S5_TAB = 8


S5_GPB = LANES // S5_GH
S5_NBLK = S5_G // S5_GPB


def _s5_zoh(lr_, li_, ldt):
    dt = jnp.exp(ldt)
    mag = jnp.exp(lr_ * dt)
    ar, ai = mag * jnp.cos(li_ * dt), mag * jnp.sin(li_ * dt)
    den = lr_ * lr_ + li_ * li_
    nr, ni = ar - 1.0, ai
    return ar, ai, (nr * lr_ + ni * li_) / den, (ni * lr_ - nr * li_) / den


def _s5_param_kernel(lr_ref, li_ref, ldt_ref, lrf_ref, lif_ref, ldtf_ref, btr_ref, bti_ref, cr_ref, ci_ref,
                     a1r_ref, a1i_ref, pcr_ref, pci_ref, wx_ref, wct_ref, ct_ref, bd_s):
    ar, ai, fr, fi = _s5_zoh(lr_ref[...], li_ref[...], ldt_ref[...])
    bbr = fr[:, None, :] * btr_ref[...] - fi[:, None, :] * bti_ref[...]
    bbi = fr[:, None, :] * bti_ref[...] + fi[:, None, :] * btr_ref[...]
    cr, ci = cr_ref[...], ci_ref[...]

    def put(dst_ref, row0, re, im):
        for j in range(S5_NBLK):
            bd_s[...] = jnp.zeros(bd_s.shape, f32)
            for g in range(S5_GPB):
                rows = slice(g * S5_GH, (g + 1) * S5_GH)
                bd_s[rows, g * S5_P:(g + 1) * S5_P] = re[j * S5_GPB + g]
                bd_s[rows, S5_BLK + g * S5_P:S5_BLK + (g + 1) * S5_P] = im[j * S5_GPB + g]
            dst_ref[j, row0:row0 + LANES, :] = bd_s[...].astype(bf16)

    put(ct_ref, 0, cr, -ci)
    pr, pi = jnp.ones_like(ar), jnp.zeros_like(ai)
    for n in range(S5_R + 1):
        if n < S5_R:
            put(wx_ref, (S5_R - 1 - n) * LANES,
                pr[:, None, :] * bbr - pi[:, None, :] * bbi, pr[:, None, :] * bbi + pi[:, None, :] * bbr)
        if n >= 1:
            put(wct_ref, (n - 1) * LANES,
                cr * pr[:, None, :] - ci * pi[:, None, :], -(cr * pi[:, None, :] + ci * pr[:, None, :]))
        pr, pi = pr * ar - pi * ai, pr * ai + pi * ar

    ar, ai, _, _ = _s5_zoh(lrf_ref[...], lif_ref[...], ldtf_ref[...])
    a1r_ref[...], a1i_ref[...] = ar, ai
    pr, pi = ar, ai
    for n in range(1, S5_R * S5_TAB + 1):
        if n % S5_R == 0:
            pcr_ref[n // S5_R - 1:n // S5_R, :] = pr
            pci_ref[n // S5_R - 1:n // S5_R, :] = pi
        pr, pi = pr * ar - pi * ai, pr * ai + pi * ar


def _toeplitz_kernel(wx_ref, ct_ref, tp_ref):
    lag = [_dot_nt(wx_ref[(S5_R - 1 - d) * LANES:(S5_R - d) * LANES, :], ct_ref[...]).astype(bf16)
           for d in range(S5_R)]
    tp_ref[...] = jnp.zeros(tp_ref.shape, bf16)
    for s in range(S5_R):
        for r in range(s, S5_R):
            tp_ref[s * LANES:(s + 1) * LANES, r * LANES:(r + 1) * LANES] = lag[r - s]


def _s5_params(lam_re, lam_im, log_dt, b_re, b_im, c_re, c_im):
    btr = jnp.swapaxes(b_re, 1, 2)
    bti = jnp.swapaxes(b_im, 1, 2)
    shp = jax.ShapeDtypeStruct
    flat = lambda a: a.reshape(1, S5_N)
    wide = (S5_NBLK, S5_R * LANES, 2 * S5_BLK)
    a1r, a1i, pcr, pci, w_x, w_ct, c_t = pl.pallas_call(
        _s5_param_kernel,
        out_shape=(shp((1, S5_N), f32), shp((1, S5_N), f32), shp((S5_TAB, S5_N), f32), shp((S5_TAB, S5_N), f32),
                   shp(wide, bf16), shp(wide, bf16), shp((S5_NBLK, LANES, 2 * S5_BLK), bf16)),
        scratch_shapes=[pltpu.VMEM((LANES, 2 * S5_BLK), f32)],
        name="s5_params",
    )(lam_re, lam_im, log_dt.reshape(S5_G, 1), flat(lam_re), flat(lam_im), flat(jnp.repeat(log_dt, S5_P)),
      btr, bti, c_re, c_im)
    toep = pl.pallas_call(
        _toeplitz_kernel,
        out_shape=shp((S5_NBLK, S5_R * LANES, S5_R * LANES), bf16),
        grid=(S5_NBLK,),
        in_specs=[pl.BlockSpec((None,) + wide[1:], lambda j: (j, 0, 0)),
                  pl.BlockSpec((None, LANES, 2 * S5_BLK), lambda j: (j, 0, 0))],
        out_specs=pl.BlockSpec((None, S5_R * LANES, S5_R * LANES), lambda j: (j, 0, 0)),
        compiler_params=_params("parallel"),
        name="s5_toeplitz",
    )(w_x, c_t)
    return dict(a1r=a1r, a1i=a1i, pcr=pcr, pci=pci, c_t=c_t, w_x=w_x, toep=toep, w_ct=w_ct)


S5_BLK = S5_N // (S5_G // (LANES // S5_GH))


def _s5_input(u, wx_ref, j):
    b_bar = wx_ref[j, (S5_R - 1) * LANES:S5_R * LANES, :]
    bu = _dot(u[:, j * LANES:(j + 1) * LANES].astype(bf16), b_bar)
    return bu[:, 0:S5_BLK], bu[:, S5_BLK:2 * S5_BLK]


def _s5_output(u, hr, hi, ct_ref, d_ref, wg_ref, bg_ref):
    ys = []
    for j in range(S5_N // S5_BLK):
        cs = slice(j * S5_BLK, (j + 1) * S5_BLK)
        hcat = jnp.concatenate([hr[:, cs], hi[:, cs]], axis=1).astype(bf16)
        ys.append(_dot_nt(hcat, ct_ref[j]))
    y = _gelu(jnp.concatenate(ys, axis=1) + d_ref[...] * u)
    return y * _sigmoid(_dot(y.astype(bf16), wg_ref[...]) + bg_ref[...])


def _s5_prompt_kernel(*refs):
    nblk = S5_N // S5_BLK
    u_refs = refs[:nblk]
    wx_ref, tp_ref, wct_ref, pcr_ref, pci_ref, d_ref, wg_ref, bg_ref = refs[nblk:nblk + 8]
    y_refs = refs[nblk + 8:2 * nblk + 8]
    sr_ref, si_ref, er_s, ei_s, y_s = refs[2 * nblk + 8:]
    t_idx = pl.program_id(1)
    cr = u_refs[0].shape[0] // S5_R
    gw = S5_R * LANES

    @pl.when(t_idx == 0)
    def _():
        er_s[0:SUBLANES, :] = jnp.zeros((SUBLANES, S5_N), f32)
        ei_s[0:SUBLANES, :] = jnp.zeros((SUBLANES, S5_N), f32)

    def token(s):
        return pl.ds(s, cr, stride=S5_R)

    sub = lax.broadcasted_iota(jnp.int32, (SUBLANES, S5_BLK), 0)
    for j in range(nblk):
        cs = slice(j * S5_BLK, (j + 1) * S5_BLK)
        uj = jnp.concatenate([u_refs[j][token(s), :] for s in range(S5_R)], axis=1).astype(bf16)
        x = _dot(uj, wx_ref[j])
        y_s[:, j * gw:(j + 1) * gw] = _dot(uj, tp_ref[j])
        hr = x[:, 0:S5_BLK].reshape(cr // SUBLANES, SUBLANES, S5_BLK)
        hi = x[:, S5_BLK:2 * S5_BLK].reshape(cr // SUBLANES, SUBLANES, S5_BLK)
        for d in (1, 2, 4):
            pr = jnp.where(sub >= d, pcr_ref[d - 1:d, cs], 0.0)
            pi = jnp.where(sub >= d, pci_ref[d - 1:d, cs], 0.0)
            sr, si = pltpu.roll(hr, d, 1), pltpu.roll(hi, d, 1)
            hr, hi = hr + (pr * sr - pi * si), hi + (pr * si + pi * sr)
        er_s[SUBLANES:SUBLANES + cr, cs] = hr.reshape(cr, S5_BLK)
        ei_s[SUBLANES:SUBLANES + cr, cs] = hi.reshape(cr, S5_BLK)

    tr, ti = pcr_ref[...], pci_ref[...]
    last = pl.ds(SUBLANES - 1, 1)
    c0 = (jnp.broadcast_to(er_s[last, :], (SUBLANES, S5_N)), jnp.broadcast_to(ei_s[last, :], (SUBLANES, S5_N)))

    def group(g, c):
        c_r, c_i = c
        rows = pl.ds(pl.multiple_of((g + 1) * SUBLANES, SUBLANES), SUBLANES)
        g_r = er_s[rows, :] + tr * c_r - ti * c_i
        g_i = ei_s[rows, :] + tr * c_i + ti * c_r
        er_s[rows, :] = g_r
        ei_s[rows, :] = g_i
        return (jnp.broadcast_to(g_r[SUBLANES - 1:SUBLANES, :], (SUBLANES, S5_N)),
                jnp.broadcast_to(g_i[SUBLANES - 1:SUBLANES, :], (SUBLANES, S5_N)))

    c_r, c_i = lax.fori_loop(0, cr // SUBLANES, group, c0)

    for j in range(nblk):
        cs = slice(j * S5_BLK, (j + 1) * S5_BLK)
        prev = pl.ds(SUBLANES - 1, cr)
        sj = jnp.concatenate([er_s[prev, cs], ei_s[prev, cs]], axis=1).astype(bf16)
        y_s[:, j * gw:(j + 1) * gw] += _dot_nt(sj, wct_ref[j])
    er_s[last, :] = c_r[0:1, :]
    ei_s[last, :] = c_i[0:1, :]
    sr_ref[...] = c_r[0:1, :]
    si_ref[...] = c_i[0:1, :]

    for r in range(S5_R):
        y = jnp.concatenate([y_s[:, j * gw + r * LANES:j * gw + (r + 1) * LANES] for j in range(nblk)], axis=1)
        u_r = jnp.concatenate([u_refs[j][token(r), :] for j in range(nblk)], axis=1)
        y = _gelu(y + d_ref[...] * u_r)
        y = y * _sigmoid(_dot(y.astype(bf16), wg_ref[...]) + bg_ref[...])
        for j in range(nblk):
            y_refs[j][token(r), :] = y[:, j * LANES:(j + 1) * LANES]


def _s5_prompt(us, tabs, p, batch, seq, cr):
    nblk = len(us)
    tiles = seq // (S5_R * cr)
    consts = [tabs["w_x"], tabs["toep"], tabs["w_ct"], tabs["pcr"], tabs["pci"], p["d"], p["w_glu"], p["b_glu"]]
    blk = pl.BlockSpec((S5_R * cr, LANES), lambda b, t: (b * tiles + t, 0))
    state = pl.BlockSpec((None, 1, S5_N), lambda b, t: (b, 0, 0))
    outs = pl.pallas_call(
        _s5_prompt_kernel,
        out_shape=tuple(jax.ShapeDtypeStruct((batch * seq, LANES), f32) for _ in range(nblk))
        + (jax.ShapeDtypeStruct((batch, 1, S5_N), f32), jax.ShapeDtypeStruct((batch, 1, S5_N), f32)),
        grid=(batch, tiles),
        in_specs=[blk] * nblk + [_const_spec(c.shape) for c in consts],
        out_specs=(blk,) * nblk + (state, state),
        scratch_shapes=[pltpu.VMEM((cr + SUBLANES, S5_N), f32), pltpu.VMEM((cr + SUBLANES, S5_N), f32),
                        pltpu.VMEM((cr, S5_R * S5_W), f32)],
        compiler_params=_params("parallel", "arbitrary"),
        name="s5_prompt",
    )(*us, *consts)
    return list(outs[:nblk]), outs[nblk], outs[nblk + 1]


def _gla_step_kernel(kind, dk, layer, *refs):
    if kind == "gla":
        q_ref, k_ref, v_ref, gate_ref, lr_ref, wlr_ref, blr_ref, ng_ref, s_ref, y_ref, so_ref, o_scr = refs
        sl = pl.program_id(0)
        q = q_ref[...] * (GLA_DK ** -0.5)
        k = k_ref[...]
        pre_all = _dot(lr_ref[...].astype(bf16), wlr_ref[...]) + blr_ref[...]
        pre = jnp.where(sl == 0, pre_all[:, 0:LANES], pre_all[:, LANES:2 * LANES])
        a = jnp.exp(_log_sigmoid(pre) * (1.0 / GLA_TAU))
    else:
        q_ref, f_ref, v_ref, gate_ref, lg_ref, ng_ref, s_ref, y_ref, so_ref, o_scr = refs
        sl = pl.program_id(0)
        lb_all = _hgrn_lower_bound(lg_ref, layer)
        lb = lb_all[:, 0:LANES]
        for j in range(1, HG_H):
            lb = jnp.where(sl == j, lb_all[:, j * LANES:(j + 1) * LANES], lb)
        a = lb + (1.0 - lb) * _sigmoid(f_ref[...])
        q = _silu(q_ref[...])
        k = 1.0 - a
    hps = LANES // dk
    nb = q.shape[0]
    qat, kt, at = (q * a).T, k.T, a.T
    lane = lax.broadcasted_iota(jnp.int32, (1, LANES), 1)
    v = v_ref[...]
    for hh in range(hps):
        r0 = hh * dk
        msk = (lane >= r0) & (lane < r0 + dk)
        qk = jnp.sum(jnp.where(msk, q * k, 0.0), axis=-1, keepdims=True)
        for b in range(nb):
            s = s_ref[b, hh]
            col = lambda x: jnp.broadcast_to(x[r0:r0 + dk, b:b + 1], (dk, LANES))
            vr = v[b:b + 1, hh * LANES:(hh + 1) * LANES]
            so_ref[b, hh] = col(at) * s + col(kt) * vr
            oi = jnp.sum(col(qat) * s, axis=0, keepdims=True)
            o_scr[b:b + 1, hh * LANES:(hh + 1) * LANES] = jnp.broadcast_to(qk[b:b + 1, :], (1, LANES)) * vr + oi
    for hh in range(hps):
        cs = slice(hh * LANES, (hh + 1) * LANES)
        o = o_scr[:, cs]
        o = o * lax.rsqrt(jnp.mean(o * o, axis=-1, keepdims=True) + RMS_EPS) * ng_ref[...]
        y_ref[:, cs] = o * _silu(gate_ref[:, cs])


def _gla_step(kind, z, extra, norm_g, state, layer):
    nb = z.shape[0]
    dk = GLA_DK if kind == "gla" else HG_D
    n_heads = GLA_H if kind == "gla" else HG_H
    hps = LANES // dk
    n_slabs = n_heads // hps
    vw = hps * LANES
    zb = lambda width, cb0: pl.BlockSpec((nb, width), lambda s: (0, cb0 + s))
    if kind == "gla":
        wlr, blr = extra
        ins = [z, z, z, z, z, wlr, blr]
        specs = [zb(LANES, 0), zb(LANES, GLA_KW // LANES), zb(vw, GLA_VW // vw * 1), zb(vw, GLA_VW // vw * 2),
                 pl.BlockSpec((nb, LANES), lambda s: (0, GLA_LR_BLOCK)),
                 _const_spec(wlr.shape), _const_spec(blr.shape)]
    else:
        (lg,) = extra
        ins = [z, z, z, z, lg]
        specs = [zb(LANES, 0), zb(LANES, HG_W // LANES), zb(vw, HG_W // vw * 2),
                 zb(vw, HG_W // vw * 3), _const_spec(lg.shape)]
    ins += [norm_g, state]
    specs += [_const_spec(norm_g.shape), pl.BlockSpec((nb, hps, dk, LANES), lambda s: (0, s, 0, 0))]
    return pl.pallas_call(
        functools.partial(_gla_step_kernel, kind, dk, layer),
        out_shape=(jax.ShapeDtypeStruct((nb, HALF), f32), jax.ShapeDtypeStruct(state.shape, f32)),
        grid=(n_slabs,),
        in_specs=specs,
        out_specs=(pl.BlockSpec((nb, vw), lambda s: (0, s)),
                   pl.BlockSpec((nb, hps, dk, LANES), lambda s: (0, s, 0, 0))),
        scratch_shapes=[pltpu.VMEM((nb, vw), f32)],
        compiler_params=_params("parallel"),
        name=kind + "_step",
    )(*ins)


def _lru_step_kernel(x_ref, gr_ref, c0_ref, c1_ref, c2_ref, h0_ref, cw_ref, cb_ref, wr_ref, wi_ref,
                     br_ref, bi_ref, lam_ref, y_ref, h_ref):
    xc = (cb_ref[...] + cw_ref[0:1, :] * c0_ref[...] + cw_ref[1:2, :] * c1_ref[...]
          + cw_ref[2:3, :] * c2_ref[...] + cw_ref[3:4, :] * x_ref[...])
    a, u = _lru_gates(xc, wr_ref, wi_ref, br_ref, bi_ref, lam_ref)
    h = a * h0_ref[...] + u
    h_ref[...] = h
    y_ref[...] = h * _gelu(gr_ref[...])


def _lru_step(xr, gr, conv_rows, h0, p):
    nb = xr.shape[0]
    full = lambda: pl.BlockSpec((nb, LRU_W), lambda i: (0, 0))
    consts = [p["conv_w"], p["conv_b"], p["w_r"], p["w_i"], p["b_r"], p["b_i"], p["lam"]]
    return pl.pallas_call(
        _lru_step_kernel,
        out_shape=(jax.ShapeDtypeStruct((nb, LRU_W), f32), jax.ShapeDtypeStruct((nb, LRU_W), f32)),
        grid=(1,),
        in_specs=[full() for _ in range(CONV_W + 2)] + [_const_spec(c.shape) for c in consts],
        out_specs=(full(), full()),
        compiler_params=_params("arbitrary"),
        name="lru_step",
    )(xr, gr, *conv_rows, h0, *consts)


def _s5_step_kernel(u_ref, s0r_ref, s0i_ref, a1r_ref, a1i_ref, wx_ref, ct_ref, d_ref, wg_ref, bg_ref,
                    y_ref, sr_ref, si_ref):
    u = u_ref[...]
    for j in range(S5_N // S5_BLK):
        cs = slice(j * S5_BLK, (j + 1) * S5_BLK)
        bur, bui = _s5_input(u, wx_ref, j)
        ar, ai = a1r_ref[:, cs], a1i_ref[:, cs]
        s0r, s0i = s0r_ref[:, cs], s0i_ref[:, cs]
        sr_ref[:, cs] = bur + ar * s0r - ai * s0i
        si_ref[:, cs] = bui + ar * s0i + ai * s0r
    y_ref[...] = _s5_output(u, sr_ref[...], si_ref[...], ct_ref, d_ref, wg_ref, bg_ref)


def _s5_step(u, s0r, s0i, tabs, p):
    nb = u.shape[0]
    consts = [tabs["a1r"], tabs["a1i"], tabs["w_x"], tabs["c_t"], p["d"], p["w_glu"], p["b_glu"]]
    st = lambda: pl.BlockSpec((nb, S5_N), lambda i: (0, 0))
    return pl.pallas_call(
        _s5_step_kernel,
        out_shape=(jax.ShapeDtypeStruct((nb, S5_W), f32), jax.ShapeDtypeStruct((nb, S5_N), f32),
                   jax.ShapeDtypeStruct((nb, S5_N), f32)),
        grid=(1,),
        in_specs=[pl.BlockSpec((nb, S5_W), lambda i: (0, 0)), st(), st()] + [_const_spec(c.shape) for c in consts],
        out_specs=(pl.BlockSpec((nb, S5_W), lambda i: (0, 0)), st(), st()),
        compiler_params=_params("arbitrary"),
        name="s5_step",
    )(u, s0r, s0i, *consts)


def _block_diag(w):
    h, n, _ = w.shape
    return jnp.einsum('hij,hk->hikj', w, jnp.eye(h, dtype=w.dtype)).reshape(h * n, h * n)


def kernel(x_prompt, x_sample, c_prompt, c_sample, state_gla, state_rglru_conv, state_rglru_h, state_s5_re, state_s5_im, state_hgrn, ev_w_in, ev_gla_w_lr, ev_gla_b_lr, ev_gla_norm_g, ev_conv_w, ev_conv_b, ev_lru_w_r, ev_lru_b_r, ev_lru_w_i, ev_lru_b_i, ev_lru_lam, ev_w_out, od_w_in, od_s5_lam_re, od_s5_lam_im, od_s5_log_dt, od_s5_b_re, od_s5_b_im, od_s5_c_re, od_s5_c_im, od_s5_d, od_s5_w_glu, od_s5_b_glu, hg_lb_logits, od_hg_norm_g, od_w_out, w_ada, b_ada, ln_g, ln_b, ffn_w_in, ffn_w_out):
    bp, seq = x_prompt.shape[0], x_prompt.shape[1]
    bs = x_sample.shape[0]
    assert x_sample.shape[1] == 1 and seq % CHUNK == 0
    tp = bp * seq
    tm = min(512, seq)
    tm_ffn = min(2 * FFN_SUB_ROWS, seq)
    lt = min(512, seq)
    lt_gla = min(2048, seq)
    cr_s5 = min(256, seq // S5_R)
    row2 = lambda a: a.reshape(1, -1)

    mod = _modulation(jnp.concatenate([c_prompt, c_sample], axis=0).astype(f32), w_ada, b_ada)
    mod_p = mod[:, :bp].reshape(DEPTH, bp, 1, 6 * D_MODEL)
    mod_s = mod[:, bp:].reshape(DEPTH, 1, bs, 6 * D_MODEL)

    xp = x_prompt.astype(f32).reshape(tp, D_MODEL)
    xs = x_sample.astype(f32).reshape(bs, D_MODEL)
    outs_p = {k: [] for k in ("gla", "conv", "lru", "re", "im", "hg")}
    outs_s = {k: [] for k in ("gla", "conv", "lru", "re", "im", "hg")}

    w_f_in = ffn_w_in.astype(bf16)
    w_f_out = ffn_w_out.astype(bf16)
    for l in range(DEPTH):
        if l % 2 == 0:
            e = l // 2
            w = ev_w_in[e]
            lr0 = 2 * GLA_KW + 2 * GLA_VW
            w_in = jnp.concatenate(
                [w[:, :lr0 + GLA_LR], jnp.zeros((D_MODEL, LANES - GLA_LR), w.dtype), w[:, lr0 + GLA_LR:]],
                axis=1).astype(bf16)
            w_out = ev_w_out[e].astype(bf16)
            wlr = jnp.concatenate([ev_gla_w_lr[e], jnp.zeros((LANES - GLA_LR, GLA_KW), f32)], axis=0).astype(bf16)
            gla_extra = (wlr, row2(ev_gla_b_lr[e]))
            ng = row2(ev_gla_norm_g[e])
            lru_p = dict(conv_w=ev_conv_w[e], conv_b=row2(ev_conv_b[e]),
                         w_r=_block_diag(ev_lru_w_r[e]).astype(bf16), w_i=_block_diag(ev_lru_w_i[e]).astype(bf16),
                         b_r=row2(ev_lru_b_r[e]), b_i=row2(ev_lru_b_i[e]), lam=row2(ev_lru_lam[e]))
            z, yb, s_conv, s_h = _even_front(xp, mod_p[l], w_in, lru_p, bp, seq, lt)
            ya, s_gla = _gla_prompt("gla", z, GLA_LR_BLOCK, gla_extra, ng, bp, seq, lt_gla, l)
            xp = _outffn(xp, [ya, yb], mod_p[l], w_out, ln_g[l], ln_b[l], w_f_in, w_f_out, l, tm_ffn, seq // tm_ffn)
            outs_p["gla"].append(s_gla)
            outs_p["conv"].append(s_conv)
            outs_p["lru"].append(s_h.reshape(bp, LRU_W))
            z, xr, gr = _inproj(xs, mod_s[l], w_in, bs, 1, (EVEN_Z, LRU_W, LRU_W))
            ya, s_gla = _gla_step("gla", z, gla_extra, ng, state_gla[e], l)
            cs = state_rglru_conv[e].astype(f32)
            yb, s_h = _lru_step(xr, gr, [cs[:, j] for j in range(CONV_W - 1)], state_rglru_h[e].astype(f32), lru_p)
            xs = _outffn(xs, [ya, yb], mod_s[l], w_out, ln_g[l], ln_b[l], w_f_in, w_f_out, l, bs, 1)
            outs_s["gla"].append(s_gla)
            outs_s["conv"].append(jnp.stack([cs[:, 1], cs[:, 2], xr], axis=1))
            outs_s["lru"].append(s_h)
        else:
            o = l // 2
            w_in = od_w_in[o].astype(bf16)
            w_out = od_w_out[o].astype(bf16)
            tabs = _s5_params(od_s5_lam_re[o], od_s5_lam_im[o], od_s5_log_dt[o], od_s5_b_re[o], od_s5_b_im[o],
                              od_s5_c_re[o], od_s5_c_im[o])
            s5_p = dict(d=row2(od_s5_d[o]), w_glu=od_s5_w_glu[o].astype(bf16), b_glu=row2(od_s5_b_glu[o]))
            hg_extra = (hg_lb_logits.astype(f32),)
            ng = row2(od_hg_norm_g[o])
            *us, z = _inproj(xp, mod_p[l], w_in, tm, seq // tm, (LANES,) * (S5_W // LANES) + (ODD_N - S5_W,))
            yas, s_re, s_im = _s5_prompt(us, tabs, s5_p, bp, seq, cr_s5)
            yb, s_hg = _gla_prompt("hgrn", z, 0, hg_extra, ng, bp, seq, lt_gla, l)
            xp = _outffn(xp, yas + [yb], mod_p[l], w_out, ln_g[l], ln_b[l], w_f_in, w_f_out, l, tm_ffn, seq // tm_ffn)
            outs_p["re"].append(s_re.reshape(bp, S5_G, S5_P))
            outs_p["im"].append(s_im.reshape(bp, S5_G, S5_P))
            outs_p["hg"].append(s_hg)
            u, z = _inproj(xs, mod_s[l], w_in, bs, 1, (S5_W, ODD_N - S5_W))
            ya, s_re, s_im = _s5_step(u, state_s5_re[o].astype(f32).reshape(bs, S5_N),
                                      state_s5_im[o].astype(f32).reshape(bs, S5_N), tabs, s5_p)
            yb, s_hg = _gla_step("hgrn", z, hg_extra, ng, state_hgrn[o], l)
            xs = _outffn(xs, [ya, yb], mod_s[l], w_out, ln_g[l], ln_b[l], w_f_in, w_f_out, l, bs, 1)
            outs_s["re"].append(s_re.reshape(bs, S5_G, S5_P))
            outs_s["im"].append(s_im.reshape(bs, S5_G, S5_P))
            outs_s["hg"].append(s_hg)

    st = lambda d, k: d[k][0][None] if len(d[k]) == 1 else jnp.stack(d[k])
    return (xp.reshape(bp, seq, D_MODEL).astype(x_prompt.dtype), xs.reshape(bs, 1, D_MODEL).astype(x_sample.dtype),
            st(outs_p, "gla"), st(outs_s, "gla"), st(outs_p, "conv"), st(outs_s, "conv"),
            st(outs_p, "lru"), st(outs_s, "lru"), st(outs_p, "re"), st(outs_s, "re"),
            st(outs_p, "im"), st(outs_s, "im"), st(outs_p, "hg"), st(outs_s, "hg"))
```

```python
import functools
import math

import numpy as np
import jax
import jax.numpy as jnp
from jax import lax
from jax.experimental import pallas as pl
from jax.experimental.pallas import tpu as pltpu

f32 = jnp.float32
bf16 = jnp.bfloat16

D_MODEL = 1024
DEPTH = 2
HALF = D_MODEL // 2
GLA_H = 4
GLA_DV = HALF // GLA_H
GLA_DK = GLA_DV // 2
GLA_KW = GLA_H * GLA_DK
GLA_VW = GLA_H * GLA_DV
GLA_LR = 16
GLA_TAU = 16.0
LRU_W = HALF
LRU_H = 8
LRU_BW = LRU_W // LRU_H
CONV_W = 4
RG_C = 8.0
S5_W = HALF
S5_GH = 16
S5_G = S5_W // S5_GH
S5_P = 64
S5_N = S5_G * S5_P
HG_H = 4
HG_D = HALF // HG_H
HG_W = HG_H * HG_D
D_FF = ((8 * D_MODEL // 3 + 255) // 256) * 256
ALPHA = (2.0 * DEPTH) ** 0.25
LN_EPS = 1e-5
RMS_EPS = 1e-6

LANES = 128
SUBLANES = 8
MXU_WIDTH = 256
LRU_ROWS = 128
VMEM_LIMIT = 56 * 1024 * 1024

CHUNK = 64
N_LEVELS = 6
FINE_LEVELS = 3
GLA_UNROLL = 3
FF_CHUNK = 256
FFN_SUB_ROWS = 512
EVEN_Z = 2 * GLA_KW + 2 * GLA_VW + LANES
EVEN_N = EVEN_Z + 2 * LRU_W
GLA_LR_BLOCK = (EVEN_Z - LANES) // LANES
ODD_N = 2560


def _sigmoid(x):
    return 0.5 * jnp.tanh(0.5 * x) + 0.5


def _silu(x):
    return x * _sigmoid(x)


def _gelu(x):
    c = math.sqrt(2.0 / math.pi)
    return x * (0.5 * (1.0 + jnp.tanh(c * (x + 0.044715 * (x * x * x)))))


def _softplus(x):
    return jnp.maximum(x, 0.0) + jnp.log1p(jnp.exp(-jnp.abs(x)))


def _log_sigmoid(x):
    return -_softplus(-x)


def _layer_norm(x, g, b):
    mu = jnp.mean(x, axis=-1, keepdims=True)
    xc = x - mu
    var = jnp.mean(xc * xc, axis=-1, keepdims=True)
    return xc * lax.rsqrt(var + LN_EPS) * g + b


def _dot(a, b):
    return jnp.dot(a, b, preferred_element_type=f32)


def _dot_nt(a, b):
    return lax.dot_general(a, b, (((1,), (1,)), ((), ())), preferred_element_type=f32)


def _dot_tn(a, b):
    return lax.dot_general(a, b, (((0,), (0,)), ((), ())), preferred_element_type=f32)


def _const_spec(shape):
    nd = len(shape)
    return pl.BlockSpec(shape, lambda *_: (0,) * nd, pipeline_mode=pl.Buffered(1))


def _params(*sem):
    return pltpu.CompilerParams(dimension_semantics=sem, vmem_limit_bytes=VMEM_LIMIT)


def _mod_kernel(c_ref, w_ref, b_ref, o_ref):
    cond = _silu(c_ref[...]).astype(bf16)
    o_ref[...] = _dot(cond, w_ref[...].astype(bf16)) + b_ref[...]


def _modulation(c_all, w_ada, b_ada):
    rows = c_all.shape[0]
    tn = 1536
    return pl.pallas_call(
        _mod_kernel,
        out_shape=jax.ShapeDtypeStruct((DEPTH, rows, 6 * D_MODEL), f32),
        grid=(DEPTH, 6 * D_MODEL // tn),
        in_specs=[pl.BlockSpec((rows, D_MODEL), lambda l, j: (0, 0)),
                  pl.BlockSpec((None, D_MODEL, tn), lambda l, j: (l, 0, j)),
                  pl.BlockSpec((None, 1, tn), lambda l, j: (l, 0, j))],
        out_specs=pl.BlockSpec((None, rows, tn), lambda l, j: (l, 0, j)),
        compiler_params=_params("parallel", "parallel"),
        name="adaln_mod",
    )(c_all, w_ada, b_ada.reshape(DEPTH, 1, 6 * D_MODEL))


def _mod_spec(rows_per_block, tiles_per_batch, j):
    return pl.BlockSpec((None, rows_per_block, D_MODEL), lambda i: (i // tiles_per_batch, 0, j))


def _inproj_kernel(x_ref, sc_ref, sh_ref, w_ref, *z_refs):
    h = (x_ref[...] * (1.0 + sc_ref[...]) + sh_ref[...]).astype(bf16)
    lo = 0
    i = 0
    while i < len(z_refs):
        j = i + 1
        while z_refs[i].shape[1] < MXU_WIDTH and j < len(z_refs) and z_refs[j].shape[1] == z_refs[i].shape[1]:
            j += 1
        width = sum(r.shape[1] for r in z_refs[i:j])
        z = _dot(h, w_ref[:, lo:lo + width])
        off = 0
        for r in z_refs[i:j]:
            r[...] = z[:, off:off + r.shape[1]]
            off += r.shape[1]
        lo += width
        i = j


def _inproj(x, mod, w, tm, tiles_per_batch, widths):
    t, n = x.shape[0], w.shape[1]
    assert sum(widths) == n
    r = mod.shape[1]
    return pl.pallas_call(
        _inproj_kernel,
        out_shape=tuple(jax.ShapeDtypeStruct((t, wd), f32) for wd in widths),
        grid=(t // tm,),
        in_specs=[pl.BlockSpec((tm, D_MODEL), lambda i: (i, 0)),
                  _mod_spec(r, tiles_per_batch, 1), _mod_spec(r, tiles_per_batch, 0),
                  _const_spec((D_MODEL, n))],
        out_specs=tuple(pl.BlockSpec((tm, wd), lambda i: (i, 0)) for wd in widths),
        compiler_params=_params("parallel"),
        name="in_proj",
    )(x, mod, mod, w)


def _outffn_kernel(n_y, x_ref, *refs):
    y_refs = refs[:n_y]
    gtm_ref, shf_ref, scf_ref, gtf_ref, wo_ref, lng_ref, lnb_ref, wi_ref, wf_ref, o_ref = refs[n_y:]
    tm = x_ref.shape[0]
    n_sub = 2 if tm % (2 * FFN_SUB_ROWS) == 0 else 1
    rs = [slice(i * (tm // n_sub), (i + 1) * (tm // n_sub)) for i in range(n_sub)]
    mrows = lambda ref, rows: ref[...] if ref.shape[0] == 1 else ref[rows, :]
    n_ff = D_FF // FF_CHUNK

    def ln1(i, mix):
        rows = rs[i]
        x1 = _layer_norm(ALPHA * x_ref[rows, :] + (1.0 + mrows(gtm_ref, rows)) * mix, lng_ref[0:1, :], lnb_ref[0:1, :])
        return x1, (x1 * (1.0 + mrows(scf_ref, rows)) + mrows(shf_ref, rows)).astype(bf16)

    def ln2(i, x1, acc):
        rows = rs[i]
        o_ref[rows, :] = _layer_norm(ALPHA * x1 + (1.0 + mrows(gtf_ref, rows)) * acc, lng_ref[1:2, :], lnb_ref[1:2, :])

    mixes = [_dot(jnp.concatenate([r[rows, :].astype(bf16) for r in y_refs], axis=1), wo_ref[...]) for rows in rs]
    cur = ln1(0, mixes[0])
    pending = None
    for i in range(n_sub):
        x1, h = cur
        acc = jnp.zeros(x1.shape, f32)
        for c in range(n_ff):
            lo = c * FF_CHUNK
            gate = _dot(h, wi_ref[:, lo:lo + FF_CHUNK])
            up = _dot(h, wi_ref[:, D_FF + lo:D_FF + lo + FF_CHUNK])
            act = (_silu(gate) * up).astype(bf16)
            acc = acc + _dot(act, wf_ref[lo:lo + FF_CHUNK, :])
            if c == 1:
                if pending is not None:
                    ln2(*pending)
                    pending = None
                if i + 1 < n_sub:
                    cur = ln1(i + 1, mixes[i + 1])
        pending = (i, x1, acc)
    ln2(*pending)


def _outffn(x, ys, mod, w_out, ln_g, ln_b, w_in, w_f, layer, tm, tiles_per_batch):
    t = x.shape[0]
    r = mod.shape[1]
    assert sum(y.shape[1] for y in ys) == D_MODEL
    row = lambda w: pl.BlockSpec((tm, w), lambda i: (i, 0))
    stacked = lambda a: pl.BlockSpec((None,) + a.shape[1:], lambda i: (layer, 0, 0), pipeline_mode=pl.Buffered(1))
    return pl.pallas_call(
        functools.partial(_outffn_kernel, len(ys)),
        out_shape=jax.ShapeDtypeStruct((t, D_MODEL), f32),
        grid=(t // tm,),
        in_specs=[row(D_MODEL)] + [row(y.shape[1]) for y in ys] + [
                  _mod_spec(r, tiles_per_batch, 2), _mod_spec(r, tiles_per_batch, 3),
                  _mod_spec(r, tiles_per_batch, 4), _mod_spec(r, tiles_per_batch, 5),
                  _const_spec((D_MODEL, D_MODEL)), _const_spec((2, D_MODEL)), _const_spec((2, D_MODEL)),
                  stacked(w_in), stacked(w_f)],
        out_specs=row(D_MODEL),
        compiler_params=_params("parallel"),
        name="outproj_ffn",
    )(x, *ys, mod, mod, mod, mod, w_out, ln_g, ln_b, w_in, w_f)


def _level_tables():
    c = CHUNK
    t = np.arange(c)[:, None]
    u = np.arange(c)[None, :]
    pmask, sgn = [], []
    for lvl in range(N_LEVELS):
        m = 1 << lvl
        right = (t % (2 * m)) >= m
        pm = right & ((u % (2 * m)) < m) & ((t // (2 * m)) == (u // (2 * m)))
        pmask.append(pm.astype(np.float32))
        sgn.append(np.broadcast_to(np.where(right, 1.0, -1.0).astype(np.float32), (c, LANES)))
    tri = (u <= t).astype(np.float32)
    tri3 = np.concatenate([tri, tri, tri], axis=1)
    return (jnp.asarray(tri3, dtype=bf16), jnp.asarray(np.stack(pmask), dtype=f32),
            jnp.asarray(np.stack(sgn), dtype=f32))


LOG2E = 1.0 / math.log(2.0)


def _anchor(b, lvl):
    m = 1 << lvl
    b3 = b.reshape(CHUNK // SUBLANES, SUBLANES, LANES)
    if 4 * m == SUBLANES:
        sub = lax.broadcasted_iota(jnp.int32, b3.shape, 1)
        a = jnp.where(sub < 2 * m, jnp.broadcast_to(b3[:, m - 1:m, :], b3.shape),
                      jnp.broadcast_to(b3[:, 3 * m - 1:3 * m, :], b3.shape))
        return a.reshape(CHUNK, LANES)
    assert 2 * m == SUBLANES
    return jnp.broadcast_to(b3[:, m - 1:m, :], b3.shape).reshape(CHUNK, LANES)


def _split3(g):
    g1 = g.astype(bf16)
    r1 = g - g1.astype(f32)
    g2 = r1.astype(bf16)
    r2 = r1 - g2.astype(f32)
    return jnp.concatenate([g1, g2, r2.astype(bf16)], axis=0)


def _hgrn_lower_bound(lg_ref, layer):
    rows = [lg_ref[i:i + 1, :] for i in range(DEPTH)]
    mx = functools.reduce(jnp.maximum, rows)
    ex = [jnp.exp(r - mx) for r in rows]
    den = functools.reduce(lambda a, b: a + b, ex)
    sm = [e / den for e in ex]
    cs = sm[0]
    for i in range(1, layer + 1):
        cs = cs + sm[i]
    return cs - sm[0]


def _gla_prologue(kind, refs, rows):
    if kind == "gla":
        qk_ref, lr_ref, wlr_ref, blr_ref = refs
        q = qk_ref[rows, 0:GLA_KW] * (GLA_DK ** -0.5)
        k = qk_ref[rows, GLA_KW:2 * GLA_KW]
        pre = _dot(lr_ref[rows, :].astype(bf16), wlr_ref[...]) + blr_ref[...]
        g = _log_sigmoid(pre) * (LOG2E / GLA_TAU)
        return q, k, g
    q_ref, f_ref, lb = refs
    forget = lb + (1.0 - lb) * _sigmoid(f_ref[rows, :])
    return _silu(q_ref[rows, :]), 1.0 - forget, jnp.log2(forget)


def _gla_chunk_kernel(kind, dk, n_heads, layer, *refs):
    if kind == "gla":
        (qk_ref, v_ref, gate_ref, lr_ref, wlr_ref, blr_ref, ng_ref, tri_ref, pm_ref, sg_ref,
         y_ref, s_ref, st_ref) = refs
        pro = (qk_ref, lr_ref, wlr_ref, blr_ref)
    else:
        (q_ref, f_ref, v_ref, gate_ref, lg_ref, ng_ref, tri_ref, pm_ref, sg_ref,
         y_ref, s_ref, st_ref) = refs
        pro = (q_ref, f_ref, _hgrn_lower_bound(lg_ref, layer))
    t_idx = pl.program_id(1)
    n_tiles = pl.num_programs(1)
    lt = y_ref.shape[0]
    hps = LANES // dk
    n_slabs = n_heads // hps
    lane = lax.broadcasted_iota(jnp.int32, (1, LANES), 1)
    eye = (lax.broadcasted_iota(jnp.int32, (CHUNK, CHUNK), 0)
           == lax.broadcasted_iota(jnp.int32, (CHUNK, CHUNK), 1))

    @pl.when(t_idx == 0)
    def _():
        st_ref[...] = jnp.zeros(st_ref.shape, f32)

    def head_mask(hh):
        if hps == 1:
            return lambda a: a
        msk = (lane >= hh * dk) & (lane < (hh + 1) * dk)
        return lambda a: jnp.where(msk, a, jnp.zeros_like(a))

    def rows_of(c):
        return pl.ds(pl.multiple_of(c * CHUNK, CHUNK), CHUNK)

    def cumsums(qkg):
        q_all, k_all, g_all = qkg
        out = []
        for sl in range(n_slabs):
            cs = slice(sl * LANES, (sl + 1) * LANES)
            q_s, k_s, g_s = q_all[:, cs], k_all[:, cs], g_all[:, cs]
            out.append((q_s, k_s, g_s, _dot(tri_ref[...], _split3(g_s))))
        return out

    def score_dots(pre):
        raw, qbs, ksts, decs, rds = [], [], [], [], []
        right = [sg_ref[lvl] > 0.0 for lvl in range(FINE_LEVELS)]
        for sl in range(n_slabs):
            q_s, k_s, g_s, b = pre[sl]
            ys = [(jnp.where(right[0], jnp.exp2(g_s) * q_s, k_s)).astype(bf16)]
            for lvl in range(1, FINE_LEVELS):
                x = jnp.exp2((b - _anchor(b, lvl)) * sg_ref[lvl])
                ys.append((x * jnp.where(right[lvl], q_s, k_s)).astype(bf16))
            for lvl in range(FINE_LEVELS, N_LEVELS):
                m = 1 << lvl
                parts = []
                for r0 in range(0, CHUNK, SUBLANES):
                    a0 = (r0 // (2 * m)) * (2 * m) + m - 1
                    anc = jnp.broadcast_to(b[a0:a0 + 1, :], (SUBLANES, LANES))
                    bj = b[r0:r0 + SUBLANES, :]
                    if r0 % (2 * m) >= m:
                        parts.append(jnp.exp2(bj - anc) * q_s[r0:r0 + SUBLANES, :])
                    else:
                        parts.append(jnp.exp2(anc - bj) * k_s[r0:r0 + SUBLANES, :])
                ys.append(jnp.concatenate(parts, axis=0).astype(bf16))
            blast = jnp.broadcast_to(b[CHUNK - 1:CHUNK, :], b.shape)
            qb = (q_s * jnp.exp2(b)).astype(bf16)
            kst = (k_s * jnp.exp2(blast - b)).astype(bf16)
            decs.append(jnp.exp2(b[CHUNK - 1:CHUNK, :]))
            qk_prod = q_s * k_s
            for hh in range(hps):
                mul = head_mask(hh)
                rds.append(jnp.sum(mul(qk_prod), axis=-1, keepdims=True))
                raw.append([_dot_nt(mul(ys[lvl]), ys[lvl]) for lvl in range(N_LEVELS)])
                qbs.append(mul(qb))
                ksts.append(mul(kst))
        return raw, rds, tuple(qbs), tuple(ksts), tuple(decs)

    def combine(raw, rds):
        ps = []
        for h in range(n_heads):
            p = jnp.where(eye, rds[h], 0.0)
            for lvl in range(N_LEVELS):
                p = p + pm_ref[lvl] * raw[h][lvl]
            ps.append(p.astype(bf16))
        return tuple(ps)

    def apply_dots(c, sc):
        ps, qbs, ksts, decs = sc
        outs = []
        for h in range(n_heads):
            vb = v_ref[rows_of(c), h * LANES:(h + 1) * LANES].astype(bf16)
            st = st_ref[h]
            outs.append(_dot(ps[h], vb) + _dot_nt(qbs[h], st.astype(bf16)))
            st_ref[h] = st * decs[h // hps] + _dot_tn(vb, ksts[h])
        return outs

    def finish(c, outs):
        for h in range(n_heads):
            cs = slice(h * LANES, (h + 1) * LANES)
            o = outs[h]
            o = o * lax.rsqrt(jnp.mean(o * o, axis=-1, keepdims=True) + RMS_EPS) * ng_ref[...]
            y_ref[rows_of(c), cs] = o * _silu(gate_ref[rows_of(c), cs])

    def first():
        raw, rds, qbs, ksts, decs = score_dots(cumsums(_gla_prologue(kind, pro, rows_of(0))))
        return combine(raw, rds), qbs, ksts, decs

    def body(c, sc):
        qkg = _gla_prologue(kind, pro, rows_of(c + 1))
        if kind == "gla":
            outs = apply_dots(c, sc)
            pre = cumsums(qkg)
        else:
            pre = cumsums(qkg)
            outs = apply_dots(c, sc)
        raw, rds, qbs, ksts, decs = score_dots(pre)
        finish(c, outs)
        return combine(raw, rds), qbs, ksts, decs

    n_chunks = lt // CHUNK
    last = lax.fori_loop(0, n_chunks - 1, body, first(), unroll=min(GLA_UNROLL, n_chunks - 1))
    finish(n_chunks - 1, apply_dots(n_chunks - 1, last))

    @pl.when(t_idx == n_tiles - 1)
    def _():
        for h in range(n_heads):
            off = (h % hps) * dk
            s_ref[h] = st_ref[h].T[off:off + dk, :]


def _gla_prompt(kind, z, col0, extra, norm_g, batch, seq, lt, layer):
    dk = GLA_DK if kind == "gla" else HG_D
    n_heads = GLA_H if kind == "gla" else HG_H
    tiles = seq // lt
    mall, pmask, rmask = _level_tables()
    blk = lambda width, cb: pl.BlockSpec((lt, width), lambda b, t: (b * tiles + t, cb))
    if kind == "gla":
        wlr, blr = extra
        ins = [z, z, z, z, wlr, blr]
        specs = [blk(2 * GLA_KW, 0), blk(GLA_VW, 1), blk(GLA_VW, 2), blk(LANES, col0),
                 _const_spec(wlr.shape), _const_spec(blr.shape)]
    else:
        (lg,) = extra
        ins = [z, z, z, z, lg]
        specs = [blk(HG_W, col0), blk(HG_W, col0 + 1), blk(HG_W, col0 + 2), blk(HG_W, col0 + 3), _const_spec(lg.shape)]
    ins += [norm_g, mall, pmask, rmask]
    specs += [_const_spec(norm_g.shape), _const_spec(mall.shape), _const_spec(pmask.shape), _const_spec(rmask.shape)]
    return pl.pallas_call(
        functools.partial(_gla_chunk_kernel, kind, dk, n_heads, layer),
        out_shape=(jax.ShapeDtypeStruct((batch * seq, HALF), f32),
                   jax.ShapeDtypeStruct((batch, n_heads, dk, LANES), f32)),
        grid=(batch, tiles),
        in_specs=specs,
        out_specs=(pl.BlockSpec((lt, HALF), lambda b, t: (b * tiles + t, 0)),
                   pl.BlockSpec((None, n_heads, dk, LANES), lambda b, t: (b, 0, 0, 0))),
        scratch_shapes=[pltpu.VMEM((n_heads, LANES, LANES), f32)],
        compiler_params=_params("parallel", "arbitrary"),
        name=kind + "_prompt",
    )(*ins)


def _lru_gates(xc, wr_ref, wi_ref, br_ref, bi_ref, lam_ref):
    xb = xc.astype(bf16)
    r = _sigmoid(_dot(xb, wr_ref[...]) + br_ref[...])
    i = _sigmoid(_dot(xb, wi_ref[...]) + bi_ref[...])
    log_a = (-RG_C) * r * _softplus(-lam_ref[...])
    t = jnp.tanh(log_a)
    one_minus_a2 = (-2.0 * t) / (1.0 - t)
    return jnp.exp(log_a), jnp.sqrt(one_minus_a2) * (i * xc)


def _even_front_kernel(xin_ref, sc_ref, sh_ref, w_ref, cw_ref, cb_ref, wr_ref, wi_ref, br_ref, bi_ref, lam_ref,
                       z_ref, y_ref, conv_ref, h_ref, xbuf, a_scr, h_scr, car):
    t_idx = pl.program_id(1)
    lt = xin_ref.shape[0]
    nz = z_ref.shape[1]

    @pl.when(t_idx == 0)
    def _():
        xbuf[0:SUBLANES, :] = jnp.zeros((SUBLANES, LRU_W), f32)
        car[...] = jnp.zeros(car.shape, f32)

    hin = (xin_ref[...] * (1.0 + sc_ref[...]) + sh_ref[...]).astype(bf16)
    x = _dot(hin, w_ref[:, nz:nz + LRU_W])
    xbuf[SUBLANES:SUBLANES + lt, :] = x
    tiles = [(lo, min(lo + MXU_WIDTH, nz)) for lo in range(0, nz, MXU_WIDTH)]
    tiles += [(nz + LRU_W + lo, nz + LRU_W + lo + MXU_WIDTH) for lo in range(0, LRU_W, MXU_WIDTH)]

    def col_tile(lo, hi):
        zt = _dot(hin, w_ref[:, lo:hi])
        if lo < nz:
            z_ref[:, lo:hi] = zt
        else:
            y_ref[:, lo - nz - LRU_W:hi - nz - LRU_W] = _gelu(zt)

    blocks = [(r0, j) for r0 in range(0, lt, LRU_ROWS) for j in range(LRU_W // LANES)]
    sub = lax.broadcasted_iota(jnp.int32, (SUBLANES, LANES), 0)
    grp = (LRU_ROWS // SUBLANES, SUBLANES, LANES)
    done = 0
    for n, (r0, j) in enumerate(blocks):
        while done < len(tiles) and done * len(blocks) <= n * len(tiles):
            col_tile(*tiles[done])
            done += 1
        cs = slice(j * LANES, (j + 1) * LANES)
        base = SUBLANES + r0
        xc = cb_ref[:, cs] + cw_ref[CONV_W - 1:CONV_W, cs] * xbuf[base:base + LRU_ROWS, cs]
        for jj in range(CONV_W - 1):
            off = base - (CONV_W - 1) + jj
            xc = xc + cw_ref[jj:jj + 1, cs] * xbuf[off:off + LRU_ROWS, cs]
        a, hh = _lru_gates(xc, wr_ref.at[cs, cs], wi_ref.at[cs, cs], br_ref.at[:, cs], bi_ref.at[:, cs],
                           lam_ref.at[:, cs])
        a, hh = a.reshape(grp), hh.reshape(grp)
        for d in (1, 2, 4):
            m = sub >= d
            hh = hh + jnp.where(m, a, 0.0) * pltpu.roll(hh, d, 1)
            a = a * jnp.where(m, pltpu.roll(a, d, 1), 1.0)
        a_scr[r0:r0 + LRU_ROWS, cs] = a.reshape(LRU_ROWS, LANES)
        h_scr[r0:r0 + LRU_ROWS, cs] = hh.reshape(LRU_ROWS, LANES)
    while done < len(tiles):
        col_tile(*tiles[done])
        done += 1
    xbuf[0:SUBLANES, :] = xbuf[lt:lt + SUBLANES, :]

    def group(g, c):
        rows = pl.ds(pl.multiple_of(g * SUBLANES, SUBLANES), SUBLANES)
        hg = h_scr[rows, :] + a_scr[rows, :] * c
        h_scr[rows, :] = hg
        return jnp.broadcast_to(hg[SUBLANES - 1:SUBLANES, :], (SUBLANES, LRU_W))

    c = lax.fori_loop(0, lt // SUBLANES, group, car[...])
    car[...] = c
    y_ref[...] = h_scr[...] * y_ref[...]
    conv_ref[...] = x[lt - (CONV_W - 1):lt, :]
    h_ref[...] = c[0:1, :]


def _even_front(x, mod, w, p, batch, seq, lt):
    tiles = seq // lt
    row = lambda wd: pl.BlockSpec((lt, wd), lambda b, t: (b * tiles + t, 0))
    modv = lambda j: pl.BlockSpec((None, 1, D_MODEL), lambda b, t: (b, 0, j))
    consts = [p["conv_w"], p["conv_b"], p["w_r"], p["w_i"], p["b_r"], p["b_i"], p["lam"]]
    return pl.pallas_call(
        _even_front_kernel,
        out_shape=(jax.ShapeDtypeStruct((batch * seq, EVEN_Z), f32),
                   jax.ShapeDtypeStruct((batch * seq, LRU_W), f32),
                   jax.ShapeDtypeStruct((batch, CONV_W - 1, LRU_W), f32),
                   jax.ShapeDtypeStruct((batch, 1, LRU_W), f32)),
        grid=(batch, tiles),
        in_specs=[row(D_MODEL), modv(1), modv(0), _const_spec(w.shape)] + [_const_spec(c.shape) for c in consts],
        out_specs=(row(EVEN_Z), row(LRU_W),
                   pl.BlockSpec((None, CONV_W - 1, LRU_W), lambda b, t: (b, 0, 0)),
                   pl.BlockSpec((None, 1, LRU_W), lambda b, t: (b, 0, 0))),
        scratch_shapes=[pltpu.VMEM((lt + SUBLANES, LRU_W), f32), pltpu.VMEM((lt, LRU_W), f32),
                        pltpu.VMEM((lt, LRU_W), f32), pltpu.VMEM((SUBLANES, LRU_W), f32)],
        compiler_params=_params("parallel", "arbitrary"),
        name="even_front",
    )(x, mod, mod, w, *consts)


S5_R = 4
S5_TAB = 8


S5_GPB = LANES // S5_GH
S5_NBLK = S5_G // S5_GPB


def _s5_zoh(lr_, li_, ldt):
    dt = jnp.exp(ldt)
    mag = jnp.exp(lr_ * dt)
    ar, ai = mag * jnp.cos(li_ * dt), mag * jnp.sin(li_ * dt)
    den = lr_ * lr_ + li_ * li_
    nr, ni = ar - 1.0, ai
    return ar, ai, (nr * lr_ + ni * li_) / den, (ni * lr_ - nr * li_) / den


def _s5_param_kernel(lr_ref, li_ref, ldt_ref, lrf_ref, lif_ref, ldtf_ref, btr_ref, bti_ref, cr_ref, ci_ref,
                     a1r_ref, a1i_ref, pcr_ref, pci_ref, wx_ref, wct_ref, ct_ref, bd_s):
    ar, ai, fr, fi = _s5_zoh(lr_ref[...], li_ref[...], ldt_ref[...])
    bbr = fr[:, None, :] * btr_ref[...] - fi[:, None, :] * bti_ref[...]
    bbi = fr[:, None, :] * bti_ref[...] + fi[:, None, :] * btr_ref[...]
    cr, ci = cr_ref[...], ci_ref[...]

    def put(dst_ref, row0, re, im):
        for j in range(S5_NBLK):
            bd_s[...] = jnp.zeros(bd_s.shape, f32)
            for g in range(S5_GPB):
                rows = slice(g * S5_GH, (g + 1) * S5_GH)
                bd_s[rows, g * S5_P:(g + 1) * S5_P] = re[j * S5_GPB + g]
                bd_s[rows, S5_BLK + g * S5_P:S5_BLK + (g + 1) * S5_P] = im[j * S5_GPB + g]
            dst_ref[j, row0:row0 + LANES, :] = bd_s[...].astype(bf16)

    put(ct_ref, 0, cr, -ci)
    pr, pi = jnp.ones_like(ar), jnp.zeros_like(ai)
    for n in range(S5_R + 1):
        if n < S5_R:
            put(wx_ref, (S5_R - 1 - n) * LANES,
                pr[:, None, :] * bbr - pi[:, None, :] * bbi, pr[:, None, :] * bbi + pi[:, None, :] * bbr)
        if n >= 1:
            put(wct_ref, (n - 1) * LANES,
                cr * pr[:, None, :] - ci * pi[:, None, :], -(cr * pi[:, None, :] + ci * pr[:, None, :]))
        pr, pi = pr * ar - pi * ai, pr * ai + pi * ar

    ar, ai, _, _ = _s5_zoh(lrf_ref[...], lif_ref[...], ldtf_ref[...])
    a1r_ref[...], a1i_ref[...] = ar, ai
    pr, pi = ar, ai
    for n in range(1, S5_R * S5_TAB + 1):
        if n % S5_R == 0:
            pcr_ref[n // S5_R - 1:n // S5_R, :] = pr
            pci_ref[n // S5_R - 1:n // S5_R, :] = pi
        pr, pi = pr * ar - pi * ai, pr * ai + pi * ar


def _toeplitz_kernel(wx_ref, ct_ref, tp_ref):
    lag = [_dot_nt(wx_ref[(S5_R - 1 - d) * LANES:(S5_R - d) * LANES, :], ct_ref[...]).astype(bf16)
           for d in range(S5_R)]
    tp_ref[...] = jnp.zeros(tp_ref.shape, bf16)
    for s in range(S5_R):
        for r in range(s, S5_R):
            tp_ref[s * LANES:(s + 1) * LANES, r * LANES:(r + 1) * LANES] = lag[r - s]


def _s5_params(lam_re, lam_im, log_dt, b_re, b_im, c_re, c_im):
    btr = jnp.swapaxes(b_re, 1, 2)
    bti = jnp.swapaxes(b_im, 1, 2)
    shp = jax.ShapeDtypeStruct
    flat = lambda a: a.reshape(1, S5_N)
    wide = (S5_NBLK, S5_R * LANES, 2 * S5_BLK)
    a1r, a1i, pcr, pci, w_x, w_ct, c_t = pl.pallas_call(
        _s5_param_kernel,
        out_shape=(shp((1, S5_N), f32), shp((1, S5_N), f32), shp((S5_TAB, S5_N), f32), shp((S5_TAB, S5_N), f32),
                   shp(wide, bf16), shp(wide, bf16), shp((S5_NBLK, LANES, 2 * S5_BLK), bf16)),
        scratch_shapes=[pltpu.VMEM((LANES, 2 * S5_BLK), f32)],
        name="s5_params",
    )(lam_re, lam_im, log_dt.reshape(S5_G, 1), flat(lam_re), flat(lam_im), flat(jnp.repeat(log_dt, S5_P)),
      btr, bti, c_re, c_im)
    toep = pl.pallas_call(
        _toeplitz_kernel,
        out_shape=shp((S5_NBLK, S5_R * LANES, S5_R * LANES), bf16),
        grid=(S5_NBLK,),
        in_specs=[pl.BlockSpec((None,) + wide[1:], lambda j: (j, 0, 0)),
                  pl.BlockSpec((None, LANES, 2 * S5_BLK), lambda j: (j, 0, 0))],
        out_specs=pl.BlockSpec((None, S5_R * LANES, S5_R * LANES), lambda j: (j, 0, 0)),
        compiler_params=_params("parallel"),
        name="s5_toeplitz",
    )(w_x, c_t)
    return dict(a1r=a1r, a1i=a1i, pcr=pcr, pci=pci, c_t=c_t, w_x=w_x, toep=toep, w_ct=w_ct)


S5_BLK = S5_N // (S5_G // (LANES // S5_GH))


def _s5_input(u, wx_ref, j):
    b_bar = wx_ref[j, (S5_R - 1) * LANES:S5_R * LANES, :]
    bu = _dot(u[:, j * LANES:(j + 1) * LANES].astype(bf16), b_bar)
    return bu[:, 0:S5_BLK], bu[:, S5_BLK:2 * S5_BLK]


def _s5_output(u, hr, hi, ct_ref, d_ref, wg_ref, bg_ref):
    ys = []
    for j in range(S5_N // S5_BLK):
        cs = slice(j * S5_BLK, (j + 1) * S5_BLK)
        hcat = jnp.concatenate([hr[:, cs], hi[:, cs]], axis=1).astype(bf16)
        ys.append(_dot_nt(hcat, ct_ref[j]))
    y = _gelu(jnp.concatenate(ys, axis=1) + d_ref[...] * u)
    return y * _sigmoid(_dot(y.astype(bf16), wg_ref[...]) + bg_ref[...])


def _s5_prompt_kernel(*refs):
    nblk = S5_N // S5_BLK
    u_refs = refs[:nblk]
    wx_ref, tp_ref, wct_ref, pcr_ref, pci_ref, d_ref, wg_ref, bg_ref = refs[nblk:nblk + 8]
    y_refs = refs[nblk + 8:2 * nblk + 8]
    sr_ref, si_ref, er_s, ei_s, y_s = refs[2 * nblk + 8:]
    t_idx = pl.program_id(1)
    cr = u_refs[0].shape[0] // S5_R
    gw = S5_R * LANES

    @pl.when(t_idx == 0)
    def _():
        er_s[0:SUBLANES, :] = jnp.zeros((SUBLANES, S5_N), f32)
        ei_s[0:SUBLANES, :] = jnp.zeros((SUBLANES, S5_N), f32)

    def token(s):
        return pl.ds(s, cr, stride=S5_R)

    sub = lax.broadcasted_iota(jnp.int32, (SUBLANES, S5_BLK), 0)
    for j in range(nblk):
        cs = slice(j * S5_BLK, (j + 1) * S5_BLK)
        uj = jnp.concatenate([u_refs[j][token(s), :] for s in range(S5_R)], axis=1).astype(bf16)
        x = _dot(uj, wx_ref[j])
        y_s[:, j * gw:(j + 1) * gw] = _dot(uj, tp_ref[j])
        hr = x[:, 0:S5_BLK].reshape(cr // SUBLANES, SUBLANES, S5_BLK)
        hi = x[:, S5_BLK:2 * S5_BLK].reshape(cr // SUBLANES, SUBLANES, S5_BLK)
        for d in (1, 2, 4):
            pr = jnp.where(sub >= d, pcr_ref[d - 1:d, cs], 0.0)
            pi = jnp.where(sub >= d, pci_ref[d - 1:d, cs], 0.0)
            sr, si = pltpu.roll(hr, d, 1), pltpu.roll(hi, d, 1)
            hr, hi = hr + (pr * sr - pi * si), hi + (pr * si + pi * sr)
        er_s[SUBLANES:SUBLANES + cr, cs] = hr.reshape(cr, S5_BLK)
        ei_s[SUBLANES:SUBLANES + cr, cs] = hi.reshape(cr, S5_BLK)

    tr, ti = pcr_ref[...], pci_ref[...]
    last = pl.ds(SUBLANES - 1, 1)
    c0 = (jnp.broadcast_to(er_s[last, :], (SUBLANES, S5_N)), jnp.broadcast_to(ei_s[last, :], (SUBLANES, S5_N)))

    def group(g, c):
        c_r, c_i = c
        rows = pl.ds(pl.multiple_of((g + 1) * SUBLANES, SUBLANES), SUBLANES)
        g_r = er_s[rows, :] + tr * c_r - ti * c_i
        g_i = ei_s[rows, :] + tr * c_i + ti * c_r
        er_s[rows, :] = g_r
        ei_s[rows, :] = g_i
        return (jnp.broadcast_to(g_r[SUBLANES - 1:SUBLANES, :], (SUBLANES, S5_N)),
                jnp.broadcast_to(g_i[SUBLANES - 1:SUBLANES, :], (SUBLANES, S5_N)))

    c_r, c_i = lax.fori_loop(0, cr // SUBLANES, group, c0)

    for j in range(nblk):
        cs = slice(j * S5_BLK, (j + 1) * S5_BLK)
        prev = pl.ds(SUBLANES - 1, cr)
        sj = jnp.concatenate([er_s[prev, cs], ei_s[prev, cs]], axis=1).astype(bf16)
        y_s[:, j * gw:(j + 1) * gw] += _dot_nt(sj, wct_ref[j])
    er_s[last, :] = c_r[0:1, :]
    ei_s[last, :] = c_i[0:1, :]
    sr_ref[...] = c_r[0:1, :]
    si_ref[...] = c_i[0:1, :]

    for r in range(S5_R):
        y = jnp.concatenate([y_s[:, j * gw + r * LANES:j * gw + (r + 1) * LANES] for j in range(nblk)], axis=1)
        u_r = jnp.concatenate([u_refs[j][token(r), :] for j in range(nblk)], axis=1)
        y = _gelu(y + d_ref[...] * u_r)
        y = y * _sigmoid(_dot(y.astype(bf16), wg_ref[...]) + bg_ref[...])
        for j in range(nblk):
            y_refs[j][token(r), :] = y[:, j * LANES:(j + 1) * LANES]


def _s5_prompt(us, tabs, p, batch, seq, cr):
    nblk = len(us)
    tiles = seq // (S5_R * cr)
    consts = [tabs["w_x"], tabs["toep"], tabs["w_ct"], tabs["pcr"], tabs["pci"], p["d"], p["w_glu"], p["b_glu"]]
    blk = pl.BlockSpec((S5_R * cr, LANES), lambda b, t: (b * tiles + t, 0))
    state = pl.BlockSpec((None, 1, S5_N), lambda b, t: (b, 0, 0))
    outs = pl.pallas_call(
        _s5_prompt_kernel,
        out_shape=tuple(jax.ShapeDtypeStruct((batch * seq, LANES), f32) for _ in range(nblk))
        + (jax.ShapeDtypeStruct((batch, 1, S5_N), f32), jax.ShapeDtypeStruct((batch, 1, S5_N), f32)),
        grid=(batch, tiles),
        in_specs=[blk] * nblk + [_const_spec(c.shape) for c in consts],
        out_specs=(blk,) * nblk + (state, state),
        scratch_shapes=[pltpu.VMEM((cr + SUBLANES, S5_N), f32), pltpu.VMEM((cr + SUBLANES, S5_N), f32),
                        pltpu.VMEM((cr, S5_R * S5_W), f32)],
        compiler_params=_params("parallel", "arbitrary"),
        name="s5_prompt",
    )(*us, *consts)
    return list(outs[:nblk]), outs[nblk], outs[nblk + 1]


def _gla_step_kernel(kind, dk, layer, *refs):
    if kind == "gla":
        q_ref, k_ref, v_ref, gate_ref, lr_ref, wlr_ref, blr_ref, ng_ref, s_ref, y_ref, so_ref, o_scr = refs
        sl = pl.program_id(0)
        q = q_ref[...] * (GLA_DK ** -0.5)
        k = k_ref[...]
        pre_all = _dot(lr_ref[...].astype(bf16), wlr_ref[...]) + blr_ref[...]
        pre = jnp.where(sl == 0, pre_all[:, 0:LANES], pre_all[:, LANES:2 * LANES])
        a = jnp.exp(_log_sigmoid(pre) * (1.0 / GLA_TAU))
    else:
        q_ref, f_ref, v_ref, gate_ref, lg_ref, ng_ref, s_ref, y_ref, so_ref, o_scr = refs
        sl = pl.program_id(0)
        lb_all = _hgrn_lower_bound(lg_ref, layer)
        lb = lb_all[:, 0:LANES]
        for j in range(1, HG_H):
            lb = jnp.where(sl == j, lb_all[:, j * LANES:(j + 1) * LANES], lb)
        a = lb + (1.0 - lb) * _sigmoid(f_ref[...])
        q = _silu(q_ref[...])
        k = 1.0 - a
    hps = LANES // dk
    nb = q.shape[0]
    qat, kt, at = (q * a).T, k.T, a.T
    lane = lax.broadcasted_iota(jnp.int32, (1, LANES), 1)
    v = v_ref[...]
    for hh in range(hps):
        r0 = hh * dk
        msk = (lane >= r0) & (lane < r0 + dk)
        qk = jnp.sum(jnp.where(msk, q * k, 0.0), axis=-1, keepdims=True)
        for b in range(nb):
            s = s_ref[b, hh]
            col = lambda x: jnp.broadcast_to(x[r0:r0 + dk, b:b + 1], (dk, LANES))
            vr = v[b:b + 1, hh * LANES:(hh + 1) * LANES]
            so_ref[b, hh] = col(at) * s + col(kt) * vr
            oi = jnp.sum(col(qat) * s, axis=0, keepdims=True)
            o_scr[b:b + 1, hh * LANES:(hh + 1) * LANES] = jnp.broadcast_to(qk[b:b + 1, :], (1, LANES)) * vr + oi
    for hh in range(hps):
        cs = slice(hh * LANES, (hh + 1) * LANES)
        o = o_scr[:, cs]
        o = o * lax.rsqrt(jnp.mean(o * o, axis=-1, keepdims=True) + RMS_EPS) * ng_ref[...]
        y_ref[:, cs] = o * _silu(gate_ref[:, cs])


def _gla_step(kind, z, extra, norm_g, state, layer):
    nb = z.shape[0]
    dk = GLA_DK if kind == "gla" else HG_D
    n_heads = GLA_H if kind == "gla" else HG_H
    hps = LANES // dk
    n_slabs = n_heads // hps
    vw = hps * LANES
    zb = lambda width, cb0: pl.BlockSpec((nb, width), lambda s: (0, cb0 + s))
    if kind == "gla":
        wlr, blr = extra
        ins = [z, z, z, z, z, wlr, blr]
        specs = [zb(LANES, 0), zb(LANES, GLA_KW // LANES), zb(vw, GLA_VW // vw * 1), zb(vw, GLA_VW // vw * 2),
                 pl.BlockSpec((nb, LANES), lambda s: (0, GLA_LR_BLOCK)),
                 _const_spec(wlr.shape), _const_spec(blr.shape)]
    else:
        (lg,) = extra
        ins = [z, z, z, z, lg]
        specs = [zb(LANES, 0), zb(LANES, HG_W // LANES), zb(vw, HG_W // vw * 2),
                 zb(vw, HG_W // vw * 3), _const_spec(lg.shape)]
    ins += [norm_g, state]
    specs += [_const_spec(norm_g.shape), pl.BlockSpec((nb, hps, dk, LANES), lambda s: (0, s, 0, 0))]
    return pl.pallas_call(
        functools.partial(_gla_step_kernel, kind, dk, layer),
        out_shape=(jax.ShapeDtypeStruct((nb, HALF), f32), jax.ShapeDtypeStruct(state.shape, f32)),
        grid=(n_slabs,),
        in_specs=specs,
        out_specs=(pl.BlockSpec((nb, vw), lambda s: (0, s)),
                   pl.BlockSpec((nb, hps, dk, LANES), lambda s: (0, s, 0, 0))),
        scratch_shapes=[pltpu.VMEM((nb, vw), f32)],
        compiler_params=_params("parallel"),
        name=kind + "_step",
    )(*ins)


def _lru_step_kernel(x_ref, gr_ref, c0_ref, c1_ref, c2_ref, h0_ref, cw_ref, cb_ref, wr_ref, wi_ref,
                     br_ref, bi_ref, lam_ref, y_ref, h_ref):
    xc = (cb_ref[...] + cw_ref[0:1, :] * c0_ref[...] + cw_ref[1:2, :] * c1_ref[...]
          + cw_ref[2:3, :] * c2_ref[...] + cw_ref[3:4, :] * x_ref[...])
    a, u = _lru_gates(xc, wr_ref, wi_ref, br_ref, bi_ref, lam_ref)
    h = a * h0_ref[...] + u
    h_ref[...] = h
    y_ref[...] = h * _gelu(gr_ref[...])


def _lru_step(xr, gr, conv_rows, h0, p):
    nb = xr.shape[0]
    full = lambda: pl.BlockSpec((nb, LRU_W), lambda i: (0, 0))
    consts = [p["conv_w"], p["conv_b"], p["w_r"], p["w_i"], p["b_r"], p["b_i"], p["lam"]]
    return pl.pallas_call(
        _lru_step_kernel,
        out_shape=(jax.ShapeDtypeStruct((nb, LRU_W), f32), jax.ShapeDtypeStruct((nb, LRU_W), f32)),
        grid=(1,),
        in_specs=[full() for _ in range(CONV_W + 2)] + [_const_spec(c.shape) for c in consts],
        out_specs=(full(), full()),
        compiler_params=_params("arbitrary"),
        name="lru_step",
    )(xr, gr, *conv_rows, h0, *consts)


def _s5_step_kernel(u_ref, s0r_ref, s0i_ref, a1r_ref, a1i_ref, wx_ref, ct_ref, d_ref, wg_ref, bg_ref,
                    y_ref, sr_ref, si_ref):
    u = u_ref[...]
    for j in range(S5_N // S5_BLK):
        cs = slice(j * S5_BLK, (j + 1) * S5_BLK)
        bur, bui = _s5_input(u, wx_ref, j)
        ar, ai = a1r_ref[:, cs], a1i_ref[:, cs]
        s0r, s0i = s0r_ref[:, cs], s0i_ref[:, cs]
        sr_ref[:, cs] = bur + ar * s0r - ai * s0i
        si_ref[:, cs] = bui + ar * s0i + ai * s0r
    y_ref[...] = _s5_output(u, sr_ref[...], si_ref[...], ct_ref, d_ref, wg_ref, bg_ref)


def _s5_step(u, s0r, s0i, tabs, p):
    nb = u.shape[0]
    consts = [tabs["a1r"], tabs["a1i"], tabs["w_x"], tabs["c_t"], p["d"], p["w_glu"], p["b_glu"]]
    st = lambda: pl.BlockSpec((nb, S5_N), lambda i: (0, 0))
    return pl.pallas_call(
        _s5_step_kernel,
        out_shape=(jax.ShapeDtypeStruct((nb, S5_W), f32), jax.ShapeDtypeStruct((nb, S5_N), f32),
                   jax.ShapeDtypeStruct((nb, S5_N), f32)),
        grid=(1,),
        in_specs=[pl.BlockSpec((nb, S5_W), lambda i: (0, 0)), st(), st()] + [_const_spec(c.shape) for c in consts],
        out_specs=(pl.BlockSpec((nb, S5_W), lambda i: (0, 0)), st(), st()),
        compiler_params=_params("arbitrary"),
        name="s5_step",
    )(u, s0r, s0i, *consts)


def _block_diag(w):
    h, n, _ = w.shape
    return jnp.einsum('hij,hk->hikj', w, jnp.eye(h, dtype=w.dtype)).reshape(h * n, h * n)


def kernel(x_prompt, x_sample, c_prompt, c_sample, state_gla, state_rglru_conv, state_rglru_h, state_s5_re, state_s5_im, state_hgrn, ev_w_in, ev_gla_w_lr, ev_gla_b_lr, ev_gla_norm_g, ev_conv_w, ev_conv_b, ev_lru_w_r, ev_lru_b_r, ev_lru_w_i, ev_lru_b_i, ev_lru_lam, ev_w_out, od_w_in, od_s5_lam_re, od_s5_lam_im, od_s5_log_dt, od_s5_b_re, od_s5_b_im, od_s5_c_re, od_s5_c_im, od_s5_d, od_s5_w_glu, od_s5_b_glu, hg_lb_logits, od_hg_norm_g, od_w_out, w_ada, b_ada, ln_g, ln_b, ffn_w_in, ffn_w_out):
    bp, seq = x_prompt.shape[0], x_prompt.shape[1]
    bs = x_sample.shape[0]
    assert x_sample.shape[1] == 1 and seq % CHUNK == 0
    tp = bp * seq
    tm = min(512, seq)
    tm_ffn = min(2 * FFN_SUB_ROWS, seq)
    lt = min(512, seq)
    lt_gla = min(2048, seq)
    cr_s5 = min(256, seq // S5_R)
    row2 = lambda a: a.reshape(1, -1)

    mod = _modulation(jnp.concatenate([c_prompt, c_sample], axis=0).astype(f32), w_ada, b_ada)
    mod_p = mod[:, :bp].reshape(DEPTH, bp, 1, 6 * D_MODEL)
    mod_s = mod[:, bp:].reshape(DEPTH, 1, bs, 6 * D_MODEL)

    xp = x_prompt.astype(f32).reshape(tp, D_MODEL)
    xs = x_sample.astype(f32).reshape(bs, D_MODEL)
    outs_p = {k: [] for k in ("gla", "conv", "lru", "re", "im", "hg")}
    outs_s = {k: [] for k in ("gla", "conv", "lru", "re", "im", "hg")}

    w_f_in = ffn_w_in.astype(bf16)
    w_f_out = ffn_w_out.astype(bf16)
    for l in range(DEPTH):
        if l % 2 == 0:
            e = l // 2
            w = ev_w_in[e]
            lr0 = 2 * GLA_KW + 2 * GLA_VW
            w_in = jnp.concatenate(
                [w[:, :lr0 + GLA_LR], jnp.zeros((D_MODEL, LANES - GLA_LR), w.dtype), w[:, lr0 + GLA_LR:]],
                axis=1).astype(bf16)
            w_out = ev_w_out[e].astype(bf16)
            wlr = jnp.concatenate([ev_gla_w_lr[e], jnp.zeros((LANES - GLA_LR, GLA_KW), f32)], axis=0).astype(bf16)
            gla_extra = (wlr, row2(ev_gla_b_lr[e]))
            ng = row2(ev_gla_norm_g[e])
            lru_p = dict(conv_w=ev_conv_w[e], conv_b=row2(ev_conv_b[e]),
                         w_r=_block_diag(ev_lru_w_r[e]).astype(bf16), w_i=_block_diag(ev_lru_w_i[e]).astype(bf16),
                         b_r=row2(ev_lru_b_r[e]), b_i=row2(ev_lru_b_i[e]), lam=row2(ev_lru_lam[e]))
            z, yb, s_conv, s_h = _even_front(xp, mod_p[l], w_in, lru_p, bp, seq, lt)
            ya, s_gla = _gla_prompt("gla", z, GLA_LR_BLOCK, gla_extra, ng, bp, seq, lt_gla, l)
            xp = _outffn(xp, [ya, yb], mod_p[l], w_out, ln_g[l], ln_b[l], w_f_in, w_f_out, l, tm_ffn, seq // tm_ffn)
            outs_p["gla"].append(s_gla)
            outs_p["conv"].append(s_conv)
            outs_p["lru"].append(s_h.reshape(bp, LRU_W))
            z, xr, gr = _inproj(xs, mod_s[l], w_in, bs, 1, (EVEN_Z, LRU_W, LRU_W))
            ya, s_gla = _gla_step("gla", z, gla_extra, ng, state_gla[e], l)
            cs = state_rglru_conv[e].astype(f32)
            yb, s_h = _lru_step(xr, gr, [cs[:, j] for j in range(CONV_W - 1)], state_rglru_h[e].astype(f32), lru_p)
            xs = _outffn(xs, [ya, yb], mod_s[l], w_out, ln_g[l], ln_b[l], w_f_in, w_f_out, l, bs, 1)
            outs_s["gla"].append(s_gla)
            outs_s["conv"].append(jnp.stack([cs[:, 1], cs[:, 2], xr], axis=1))
            outs_s["lru"].append(s_h)
        else:
            o = l // 2
            w_in = od_w_in[o].astype(bf16)
            w_out = od_w_out[o].astype(bf16)
            tabs = _s5_params(od_s5_lam_re[o], od_s5_lam_im[o], od_s5_log_dt[o], od_s5_b_re[o], od_s5_b_im[o],
                              od_s5_c_re[o], od_s5_c_im[o])
            s5_p = dict(d=row2(od_s5_d[o]), w_glu=od_s5_w_glu[o].astype(bf16), b_glu=row2(od_s5_b_glu[o]))
            hg_extra = (hg_lb_logits.astype(f32),)
            ng = row2(od_hg_norm_g[o])
            *us, z = _inproj(xp, mod_p[l], w_in, tm, seq // tm, (LANES,) * (S5_W // LANES) + (ODD_N - S5_W,))
            yas, s_re, s_im = _s5_prompt(us, tabs, s5_p, bp, seq, cr_s5)
            yb, s_hg = _gla_prompt("hgrn", z, 0, hg_extra, ng, bp, seq, lt_gla, l)
            xp = _outffn(xp, yas + [yb], mod_p[l], w_out, ln_g[l], ln_b[l], w_f_in, w_f_out, l, tm_ffn, seq // tm_ffn)
            outs_p["re"].append(s_re.reshape(bp, S5_G, S5_P))
            outs_p["im"].append(s_im.reshape(bp, S5_G, S5_P))
            outs_p["hg"].append(s_hg)
            u, z = _inproj(xs, mod_s[l], w_in, bs, 1, (S5_W, ODD_N - S5_W))
            ya, s_re, s_im = _s5_step(u, state_s5_re[o].astype(f32).reshape(bs, S5_N),
                                      state_s5_im[o].astype(f32).reshape(bs, S5_N), tabs, s5_p)
            yb, s_hg = _gla_step("hgrn", z, hg_extra, ng, state_hgrn[o], l)
            xs = _outffn(xs, [ya, yb], mod_s[l], w_out, ln_g[l], ln_b[l], w_f_in, w_f_out, l, bs, 1)
            outs_s["re"].append(s_re.reshape(bs, S5_G, S5_P))
            outs_s["im"].append(s_im.reshape(bs, S5_G, S5_P))
            outs_s["hg"].append(s_hg)

    st = lambda d, k: d[k][0][None] if len(d[k]) == 1 else jnp.stack(d[k])
    return (xp.reshape(bp, seq, D_MODEL).astype(x_prompt.dtype), xs.reshape(bs, 1, D_MODEL).astype(x_sample.dtype),
            st(outs_p, "gla"), st(outs_s, "gla"), st(outs_p, "conv"), st(outs_s, "conv"),
            st(outs_p, "lru"), st(outs_s, "lru"), st(outs_p, "re"), st(outs_s, "re"),
            st(outs_p, "im"), st(outs_s, "im"), st(outs_p, "hg"), st(outs_s, "hg"))
```

```python
import functools
import math

import numpy as np
import jax
import jax.numpy as jnp
from jax import lax
from jax.experimental import pallas as pl
from jax.experimental.pallas import tpu as pltpu

f32 = jnp.float32
bf16 = jnp.bfloat16

D_MODEL = 1024
DEPTH = 2
HALF = D_MODEL // 2
GLA_H = 4
GLA_DV = HALF // GLA_H
GLA_DK = GLA_DV // 2
GLA_KW = GLA_H * GLA_DK
GLA_VW = GLA_H * GLA_DV
GLA_LR = 16
GLA_TAU = 16.0
LRU_W = HALF
LRU_H = 8
LRU_BW = LRU_W // LRU_H
CONV_W = 4
RG_C = 8.0
S5_W = HALF
S5_GH = 16
S5_G = S5_W // S5_GH
S5_P = 64
S5_N = S5_G * S5_P
HG_H = 4
HG_D = HALF // HG_H
HG_W = HG_H * HG_D
D_FF = ((8 * D_MODEL // 3 + 255) // 256) * 256
ALPHA = (2.0 * DEPTH) ** 0.25
LN_EPS = 1e-5
RMS_EPS = 1e-6

LANES = 128
SUBLANES = 8
MXU_WIDTH = 256
LRU_ROWS = 128
VMEM_LIMIT = 56 * 1024 * 1024

CHUNK = 64
N_LEVELS = 6
FINE_LEVELS = 3
GLA_UNROLL = 3
GATE_ROWS = 512
FF_CHUNK = 256
FFN_SUB_ROWS = 512
EVEN_Z = 2 * GLA_KW + 2 * GLA_VW + LANES
EVEN_N = EVEN_Z + 2 * LRU_W
GLA_LR_BLOCK = (EVEN_Z - LANES) // LANES
ODD_N = 2560


def _sigmoid(x):
    return 0.5 * jnp.tanh(0.5 * x) + 0.5


def _silu(x):
    return x * _sigmoid(x)


def _gelu(x):
    c = math.sqrt(2.0 / math.pi)
    return x * (0.5 * (1.0 + jnp.tanh(c * (x + 0.044715 * (x * x * x)))))


def _softplus(x):
    return jnp.maximum(x, 0.0) + jnp.log1p(jnp.exp(-jnp.abs(x)))


def _log_sigmoid(x):
    return -_softplus(-x)


def _layer_norm(x, g, b):
    mu = jnp.mean(x, axis=-1, keepdims=True)
    xc = x - mu
    var = jnp.mean(xc * xc, axis=-1, keepdims=True)
    return xc * lax.rsqrt(var + LN_EPS) * g + b


def _dot(a, b):
    return jnp.dot(a, b, preferred_element_type=f32)


def _dot_nt(a, b):
    return lax.dot_general(a, b, (((1,), (1,)), ((), ())), preferred_element_type=f32)


def _dot_tn(a, b):
    return lax.dot_general(a, b, (((0,), (0,)), ((), ())), preferred_element_type=f32)


def _const_spec(shape):
    nd = len(shape)
    return pl.BlockSpec(shape, lambda *_: (0,) * nd, pipeline_mode=pl.Buffered(1))


def _params(*sem):
    return pltpu.CompilerParams(dimension_semantics=sem, vmem_limit_bytes=VMEM_LIMIT)


def _mod_kernel(c_ref, w_ref, b_ref, o_ref):
    cond = _silu(c_ref[...]).astype(bf16)
    o_ref[...] = _dot(cond, w_ref[...].astype(bf16)) + b_ref[...]


def _modulation(c_all, w_ada, b_ada):
    rows = c_all.shape[0]
    tn = 1536
    return pl.pallas_call(
        _mod_kernel,
        out_shape=jax.ShapeDtypeStruct((DEPTH, rows, 6 * D_MODEL), f32),
        grid=(DEPTH, 6 * D_MODEL // tn),
        in_specs=[pl.BlockSpec((rows, D_MODEL), lambda l, j: (0, 0)),
                  pl.BlockSpec((None, D_MODEL, tn), lambda l, j: (l, 0, j)),
                  pl.BlockSpec((None, 1, tn), lambda l, j: (l, 0, j))],
        out_specs=pl.BlockSpec((None, rows, tn), lambda l, j: (l, 0, j)),
        compiler_params=_params("parallel", "parallel"),
        name="adaln_mod",
    )(c_all, w_ada, b_ada.reshape(DEPTH, 1, 6 * D_MODEL))


def _mod_spec(rows_per_block, tiles_per_batch, j):
    return pl.BlockSpec((None, rows_per_block, D_MODEL), lambda i: (i // tiles_per_batch, 0, j))


def _inproj_kernel(x_ref, sc_ref, sh_ref, w_ref, *z_refs):
    h = (x_ref[...] * (1.0 + sc_ref[...]) + sh_ref[...]).astype(bf16)
    lo = 0
    i = 0
    while i < len(z_refs):
        j = i + 1
        while z_refs[i].shape[1] < MXU_WIDTH and j < len(z_refs) and z_refs[j].shape[1] == z_refs[i].shape[1]:
            j += 1
        width = sum(r.shape[1] for r in z_refs[i:j])
        z = _dot(h, w_ref[:, lo:lo + width])
        off = 0
        for r in z_refs[i:j]:
            r[...] = z[:, off:off + r.shape[1]]
            off += r.shape[1]
        lo += width
        i = j


def _inproj(x, mod, w, tm, tiles_per_batch, widths):
    t, n = x.shape[0], w.shape[1]
    assert sum(widths) == n
    r = mod.shape[1]
    return pl.pallas_call(
        _inproj_kernel,
        out_shape=tuple(jax.ShapeDtypeStruct((t, wd), f32) for wd in widths),
        grid=(t // tm,),
        in_specs=[pl.BlockSpec((tm, D_MODEL), lambda i: (i, 0)),
                  _mod_spec(r, tiles_per_batch, 1), _mod_spec(r, tiles_per_batch, 0),
                  _const_spec((D_MODEL, n))],
        out_specs=tuple(pl.BlockSpec((tm, wd), lambda i: (i, 0)) for wd in widths),
        compiler_params=_params("parallel"),
        name="in_proj",
    )(x, mod, mod, w)


def _outffn_kernel(n_y, x_ref, *refs):
    y_refs = refs[:n_y]
    gtm_ref, shf_ref, scf_ref, gtf_ref, wo_ref, lng_ref, lnb_ref, wi_ref, wf_ref, o_ref = refs[n_y:]
    tm = x_ref.shape[0]
    n_sub = 2 if tm % (2 * FFN_SUB_ROWS) == 0 else 1
    rs = [slice(i * (tm // n_sub), (i + 1) * (tm // n_sub)) for i in range(n_sub)]
    mrows = lambda ref, rows: ref[...] if ref.shape[0] == 1 else ref[rows, :]
    n_ff = D_FF // FF_CHUNK

    def ln1(i, mix):
        rows = rs[i]
        x1 = _layer_norm(ALPHA * x_ref[rows, :] + (1.0 + mrows(gtm_ref, rows)) * mix, lng_ref[0:1, :], lnb_ref[0:1, :])
        return x1, (x1 * (1.0 + mrows(scf_ref, rows)) + mrows(shf_ref, rows)).astype(bf16)

    def ln2(i, x1, acc):
        rows = rs[i]
        o_ref[rows, :] = _layer_norm(ALPHA * x1 + (1.0 + mrows(gtf_ref, rows)) * acc, lng_ref[1:2, :], lnb_ref[1:2, :])

    mixes = [_dot(jnp.concatenate([r[rows, :].astype(bf16) for r in y_refs], axis=1), wo_ref[...]) for rows in rs]
    cur = ln1(0, mixes[0])
    pending = None
    for i in range(n_sub):
        x1, h = cur
        acc = jnp.zeros(x1.shape, f32)
        for c in range(n_ff):
            lo = c * FF_CHUNK
            gate = _dot(h, wi_ref[:, lo:lo + FF_CHUNK])
            up = _dot(h, wi_ref[:, D_FF + lo:D_FF + lo + FF_CHUNK])
            act = (_silu(gate) * up).astype(bf16)
            acc = acc + _dot(act, wf_ref[lo:lo + FF_CHUNK, :])
            if c == 1:
                if pending is not None:
                    ln2(*pending)
                    pending = None
                if i + 1 < n_sub:
                    cur = ln1(i + 1, mixes[i + 1])
        pending = (i, x1, acc)
    ln2(*pending)


def _outffn(x, ys, mod, w_out, ln_g, ln_b, w_in, w_f, layer, tm, tiles_per_batch):
    t = x.shape[0]
    r = mod.shape[1]
    assert sum(y.shape[1] for y in ys) == D_MODEL
    row = lambda w: pl.BlockSpec((tm, w), lambda i: (i, 0))
    stacked = lambda a: pl.BlockSpec((None,) + a.shape[1:], lambda i: (layer, 0, 0), pipeline_mode=pl.Buffered(1))
    return pl.pallas_call(
        functools.partial(_outffn_kernel, len(ys)),
        out_shape=jax.ShapeDtypeStruct((t, D_MODEL), f32),
        grid=(t // tm,),
        in_specs=[row(D_MODEL)] + [row(y.shape[1]) for y in ys] + [
                  _mod_spec(r, tiles_per_batch, 2), _mod_spec(r, tiles_per_batch, 3),
                  _mod_spec(r, tiles_per_batch, 4), _mod_spec(r, tiles_per_batch, 5),
                  _const_spec((D_MODEL, D_MODEL)), _const_spec((2, D_MODEL)), _const_spec((2, D_MODEL)),
                  stacked(w_in), stacked(w_f)],
        out_specs=row(D_MODEL),
        compiler_params=_params("parallel"),
        name="outproj_ffn",
    )(x, *ys, mod, mod, mod, mod, w_out, ln_g, ln_b, w_in, w_f)


def _level_tables():
    c = CHUNK
    t = np.arange(c)[:, None]
    u = np.arange(c)[None, :]
    pmask, sgn = [], []
    for lvl in range(N_LEVELS):
        m = 1 << lvl
        right = (t % (2 * m)) >= m
        pm = right & ((u % (2 * m)) < m) & ((t // (2 * m)) == (u // (2 * m)))
        pmask.append(pm.astype(np.float32))
        sgn.append(np.broadcast_to(np.where(right, 1.0, -1.0).astype(np.float32), (c, LANES)))
    tri = (u <= t).astype(np.float32)
    tri3 = np.concatenate([tri, tri, tri], axis=1)
    return (jnp.asarray(tri3, dtype=bf16), jnp.asarray(np.stack(pmask), dtype=f32),
            jnp.asarray(np.stack(sgn), dtype=f32))


LOG2E = 1.0 / math.log(2.0)


def _anchor(b, lvl):
    m = 1 << lvl
    b3 = b.reshape(CHUNK // SUBLANES, SUBLANES, LANES)
    if 4 * m == SUBLANES:
        sub = lax.broadcasted_iota(jnp.int32, b3.shape, 1)
        a = jnp.where(sub < 2 * m, jnp.broadcast_to(b3[:, m - 1:m, :], b3.shape),
                      jnp.broadcast_to(b3[:, 3 * m - 1:3 * m, :], b3.shape))
        return a.reshape(CHUNK, LANES)
    assert 2 * m == SUBLANES
    return jnp.broadcast_to(b3[:, m - 1:m, :], b3.shape).reshape(CHUNK, LANES)


def _split3(g):
    g1 = g.astype(bf16)
    r1 = g - g1.astype(f32)
    g2 = r1.astype(bf16)
    r2 = r1 - g2.astype(f32)
    return jnp.concatenate([g1, g2, r2.astype(bf16)], axis=0)


def _hgrn_lower_bound(lg_ref, layer):
    rows = [lg_ref[i:i + 1, :] for i in range(DEPTH)]
    mx = functools.reduce(jnp.maximum, rows)
    ex = [jnp.exp(r - mx) for r in rows]
    den = functools.reduce(lambda a, b: a + b, ex)
    sm = [e / den for e in ex]
    cs = sm[0]
    for i in range(1, layer + 1):
        cs = cs + sm[i]
    return cs - sm[0]


def _gla_prologue(kind, refs, rows):
    if kind == "gla":
        qk_ref, g_scr = refs
        q = qk_ref[rows, 0:GLA_KW] * (GLA_DK ** -0.5)
        k = qk_ref[rows, GLA_KW:2 * GLA_KW]
        return q, k, g_scr[rows, :]
    q_ref, f_ref, lb = refs
    forget = lb + (1.0 - lb) * _sigmoid(f_ref[rows, :])
    return _silu(q_ref[rows, :]), 1.0 - forget, jnp.log2(forget)


def _gla_chunk_kernel(kind, dk, n_heads, layer, *refs):
    if kind == "gla":
        (qk_ref, v_ref, gate_ref, lr_ref, wlr_ref, blr_ref, ng_ref, tri_ref, pm_ref, sg_ref,
         y_ref, s_ref, st_ref, g_scr) = refs
        pro = (qk_ref, g_scr)
        for r0 in range(0, y_ref.shape[0], GATE_ROWS):
            rs = slice(r0, min(r0 + GATE_ROWS, y_ref.shape[0]))
            pre = _dot(lr_ref[rs, :].astype(bf16), wlr_ref[...]) + blr_ref[...]
            g_scr[rs, :] = _log_sigmoid(pre) * (LOG2E / GLA_TAU)
    else:
        (q_ref, f_ref, v_ref, gate_ref, lg_ref, ng_ref, tri_ref, pm_ref, sg_ref,
         y_ref, s_ref, st_ref) = refs
        pro = (q_ref, f_ref, _hgrn_lower_bound(lg_ref, layer))
    t_idx = pl.program_id(1)
    n_tiles = pl.num_programs(1)
    lt = y_ref.shape[0]
    hps = LANES // dk
    n_slabs = n_heads // hps
    lane = lax.broadcasted_iota(jnp.int32, (1, LANES), 1)
    eye = (lax.broadcasted_iota(jnp.int32, (CHUNK, CHUNK), 0)
           == lax.broadcasted_iota(jnp.int32, (CHUNK, CHUNK), 1))

    @pl.when(t_idx == 0)
    def _():
        st_ref[...] = jnp.zeros(st_ref.shape, f32)

    def head_mask(hh):
        if hps == 1:
            return lambda a: a
        msk = (lane >= hh * dk) & (lane < (hh + 1) * dk)
        return lambda a: jnp.where(msk, a, jnp.zeros_like(a))

    def rows_of(c):
        return pl.ds(pl.multiple_of(c * CHUNK, CHUNK), CHUNK)

    def cumsums(qkg):
        q_all, k_all, g_all = qkg
        out = []
        for sl in range(n_slabs):
            cs = slice(sl * LANES, (sl + 1) * LANES)
            q_s, k_s, g_s = q_all[:, cs], k_all[:, cs], g_all[:, cs]
            out.append((q_s, k_s, g_s, _dot(tri_ref[...], _split3(g_s))))
        return out

    def score_dots(pre):
        raw, qbs, ksts, decs, rds = [], [], [], [], []
        right = [sg_ref[lvl] > 0.0 for lvl in range(FINE_LEVELS)]
        for sl in range(n_slabs):
            q_s, k_s, g_s, b = pre[sl]
            ys = [(jnp.where(right[0], jnp.exp2(g_s) * q_s, k_s)).astype(bf16)]
            for lvl in range(1, FINE_LEVELS):
                x = jnp.exp2((b - _anchor(b, lvl)) * sg_ref[lvl])
                ys.append((x * jnp.where(right[lvl], q_s, k_s)).astype(bf16))
            for lvl in range(FINE_LEVELS, N_LEVELS):
                m = 1 << lvl
                parts = []
                for r0 in range(0, CHUNK, SUBLANES):
                    a0 = (r0 // (2 * m)) * (2 * m) + m - 1
                    anc = jnp.broadcast_to(b[a0:a0 + 1, :], (SUBLANES, LANES))
                    bj = b[r0:r0 + SUBLANES, :]
                    if r0 % (2 * m) >= m:
                        parts.append(jnp.exp2(bj - anc) * q_s[r0:r0 + SUBLANES, :])
                    else:
                        parts.append(jnp.exp2(anc - bj) * k_s[r0:r0 + SUBLANES, :])
                ys.append(jnp.concatenate(parts, axis=0).astype(bf16))
            blast = jnp.broadcast_to(b[CHUNK - 1:CHUNK, :], b.shape)
            qb = (q_s * jnp.exp2(b)).astype(bf16)
            kst = (k_s * jnp.exp2(blast - b)).astype(bf16)
            decs.append(jnp.exp2(b[CHUNK - 1:CHUNK, :]))
            qk_prod = q_s * k_s
            for hh in range(hps):
                mul = head_mask(hh)
                rds.append(jnp.sum(mul(qk_prod), axis=-1, keepdims=True))
                raw.append([_dot_nt(mul(ys[lvl]), ys[lvl]) for lvl in range(N_LEVELS)])
                qbs.append(mul(qb))
                ksts.append(mul(kst))
        return raw, rds, tuple(qbs), tuple(ksts), tuple(decs)

    def combine(raw, rds):
        ps = []
        for h in range(n_heads):
            p = jnp.where(eye, rds[h], 0.0)
            for lvl in range(N_LEVELS):
                p = p + pm_ref[lvl] * raw[h][lvl]
            ps.append(p.astype(bf16))
        return tuple(ps)

    def apply_dots(c, sc):
        ps, qbs, ksts, decs = sc
        outs = []
        for h in range(n_heads):
            vb = v_ref[rows_of(c), h * LANES:(h + 1) * LANES].astype(bf16)
            st = st_ref[h]
            outs.append(_dot(ps[h], vb) + _dot_nt(qbs[h], st.astype(bf16)))
            st_ref[h] = st * decs[h // hps] + _dot_tn(vb, ksts[h])
        return outs

    def finish(c, outs):
        for h in range(n_heads):
            cs = slice(h * LANES, (h + 1) * LANES)
            o = outs[h]
            o = o * lax.rsqrt(jnp.mean(o * o, axis=-1, keepdims=True) + RMS_EPS) * ng_ref[...]
            y_ref[rows_of(c), cs] = o * _silu(gate_ref[rows_of(c), cs])

    def first():
        raw, rds, qbs, ksts, decs = score_dots(cumsums(_gla_prologue(kind, pro, rows_of(0))))
        return combine(raw, rds), qbs, ksts, decs

    def body(c, sc):
        pre = cumsums(_gla_prologue(kind, pro, rows_of(c + 1)))
        outs = apply_dots(c, sc)
        raw, rds, qbs, ksts, decs = score_dots(pre)
        finish(c, outs)
        return combine(raw, rds), qbs, ksts, decs

    n_chunks = lt // CHUNK
    last = lax.fori_loop(0, n_chunks - 1, body, first(), unroll=min(GLA_UNROLL, n_chunks - 1))
    finish(n_chunks - 1, apply_dots(n_chunks - 1, last))

    @pl.when(t_idx == n_tiles - 1)
    def _():
        for h in range(n_heads):
            off = (h % hps) * dk
            s_ref[h] = st_ref[h].T[off:off + dk, :]


def _gla_prompt(kind, z, col0, extra, norm_g, batch, seq, lt, layer):
    dk = GLA_DK if kind == "gla" else HG_D
    n_heads = GLA_H if kind == "gla" else HG_H
    tiles = seq // lt
    mall, pmask, rmask = _level_tables()
    blk = lambda width, cb: pl.BlockSpec((lt, width), lambda b, t: (b * tiles + t, cb))
    if kind == "gla":
        wlr, blr = extra
        ins = [z, z, z, z, wlr, blr]
        specs = [blk(2 * GLA_KW, 0), blk(GLA_VW, 1), blk(GLA_VW, 2), blk(LANES, col0),
                 _const_spec(wlr.shape), _const_spec(blr.shape)]
    else:
        (lg,) = extra
        ins = [z, z, z, z, lg]
        specs = [blk(HG_W, col0), blk(HG_W, col0 + 1), blk(HG_W, col0 + 2), blk(HG_W, col0 + 3), _const_spec(lg.shape)]
    ins += [norm_g, mall, pmask, rmask]
    specs += [_const_spec(norm_g.shape), _const_spec(mall.shape), _const_spec(pmask.shape), _const_spec(rmask.shape)]
    return pl.pallas_call(
        functools.partial(_gla_chunk_kernel, kind, dk, n_heads, layer),
        out_shape=(jax.ShapeDtypeStruct((batch * seq, HALF), f32),
                   jax.ShapeDtypeStruct((batch, n_heads, dk, LANES), f32)),
        grid=(batch, tiles),
        in_specs=specs,
        out_specs=(pl.BlockSpec((lt, HALF), lambda b, t: (b * tiles + t, 0)),
                   pl.BlockSpec((None, n_heads, dk, LANES), lambda b, t: (b, 0, 0, 0))),
        scratch_shapes=[pltpu.VMEM((n_heads, LANES, LANES), f32)]
        + ([pltpu.VMEM((lt, GLA_KW), f32)] if kind == "gla" else []),
        compiler_params=_params("parallel", "arbitrary"),
        name=kind + "_prompt",
    )(*ins)


def _lru_gates(xc, wr_ref, wi_ref, br_ref, bi_ref, lam_ref):
    xb = xc.astype(bf16)
    r = _sigmoid(_dot(xb, wr_ref[...]) + br_ref[...])
    i = _sigmoid(_dot(xb, wi_ref[...]) + bi_ref[...])
    log_a = (-RG_C) * r * _softplus(-lam_ref[...])
    t = jnp.tanh(log_a)
    one_minus_a2 = (-2.0 * t) / (1.0 - t)
    return jnp.exp(log_a), jnp.sqrt(one_minus_a2) * (i * xc)


def _even_front_kernel(xin_ref, sc_ref, sh_ref, w_ref, cw_ref, cb_ref, wr_ref, wi_ref, br_ref, bi_ref, lam_ref,
                       z_ref, y_ref, conv_ref, h_ref, xbuf, a_scr, h_scr, car):
    t_idx = pl.program_id(1)
    lt = xin_ref.shape[0]
    nz = z_ref.shape[1]

    @pl.when(t_idx == 0)
    def _():
        xbuf[0:SUBLANES, :] = jnp.zeros((SUBLANES, LRU_W), f32)
        car[...] = jnp.zeros(car.shape, f32)

    hin = (xin_ref[...] * (1.0 + sc_ref[...]) + sh_ref[...]).astype(bf16)
    x = _dot(hin, w_ref[:, nz:nz + LRU_W])
    xbuf[SUBLANES:SUBLANES + lt, :] = x
    tiles = [(lo, min(lo + MXU_WIDTH, nz)) for lo in range(0, nz, MXU_WIDTH)]
    tiles += [(nz + LRU_W + lo, nz + LRU_W + lo + MXU_WIDTH) for lo in range(0, LRU_W, MXU_WIDTH)]

    def col_tile(lo, hi):
        zt = _dot(hin, w_ref[:, lo:hi])
        if lo < nz:
            z_ref[:, lo:hi] = zt
        else:
            y_ref[:, lo - nz - LRU_W:hi - nz - LRU_W] = _gelu(zt)

    blocks = [(r0, j) for r0 in range(0, lt, LRU_ROWS) for j in range(LRU_W // LANES)]
    sub = lax.broadcasted_iota(jnp.int32, (SUBLANES, LANES), 0)
    grp = (LRU_ROWS // SUBLANES, SUBLANES, LANES)
    done = 0
    for n, (r0, j) in enumerate(blocks):
        while done < len(tiles) and done * len(blocks) <= n * len(tiles):
            col_tile(*tiles[done])
            done += 1
        cs = slice(j * LANES, (j + 1) * LANES)
        base = SUBLANES + r0
        xc = cb_ref[:, cs] + cw_ref[CONV_W - 1:CONV_W, cs] * xbuf[base:base + LRU_ROWS, cs]
        for jj in range(CONV_W - 1):
            off = base - (CONV_W - 1) + jj
            xc = xc + cw_ref[jj:jj + 1, cs] * xbuf[off:off + LRU_ROWS, cs]
        a, hh = _lru_gates(xc, wr_ref.at[cs, cs], wi_ref.at[cs, cs], br_ref.at[:, cs], bi_ref.at[:, cs],
                           lam_ref.at[:, cs])
        a, hh = a.reshape(grp), hh.reshape(grp)
        for d in (1, 2, 4):
            m = sub >= d
            hh = hh + jnp.where(m, a, 0.0) * pltpu.roll(hh, d, 1)
            a = a * jnp.where(m, pltpu.roll(a, d, 1), 1.0)
        a_scr[r0:r0 + LRU_ROWS, cs] = a.reshape(LRU_ROWS, LANES)
        h_scr[r0:r0 + LRU_ROWS, cs] = hh.reshape(LRU_ROWS, LANES)
    while done < len(tiles):
        col_tile(*tiles[done])
        done += 1
    xbuf[0:SUBLANES, :] = xbuf[lt:lt + SUBLANES, :]

    def group(g, c):
        rows = pl.ds(pl.multiple_of(g * SUBLANES, SUBLANES), SUBLANES)
        hg = h_scr[rows, :] + a_scr[rows, :] * c
        h_scr[rows, :] = hg
        return jnp.broadcast_to(hg[SUBLANES - 1:SUBLANES, :], (SUBLANES, LRU_W))

    c = lax.fori_loop(0, lt // SUBLANES, group, car[...])
    car[...] = c
    y_ref[...] = h_scr[...] * y_ref[...]
    conv_ref[...] = x[lt - (CONV_W - 1):lt, :]
    h_ref[...] = c[0:1, :]


def _even_front(x, mod, w, p, batch, seq, lt):
    tiles = seq // lt
    row = lambda wd: pl.BlockSpec((lt, wd), lambda b, t: (b * tiles + t, 0))
    modv = lambda j: pl.BlockSpec((None, 1, D_MODEL), lambda b, t: (b, 0, j))
    consts = [p["conv_w"], p["conv_b"], p["w_r"], p["w_i"], p["b_r"], p["b_i"], p["lam"]]
    return pl.pallas_call(
        _even_front_kernel,
        out_shape=(jax.ShapeDtypeStruct((batch * seq, EVEN_Z), f32),
                   jax.ShapeDtypeStruct((batch * seq, LRU_W), f32),
                   jax.ShapeDtypeStruct((batch, CONV_W - 1, LRU_W), f32),
                   jax.ShapeDtypeStruct((batch, 1, LRU_W), f32)),
        grid=(batch, tiles),
        in_specs=[row(D_MODEL), modv(1), modv(0), _const_spec(w.shape)] + [_const_spec(c.shape) for c in consts],
        out_specs=(row(EVEN_Z), row(LRU_W),
                   pl.BlockSpec((None, CONV_W - 1, LRU_W), lambda b, t: (b, 0, 0)),
                   pl.BlockSpec((None, 1, LRU_W), lambda b, t: (b, 0, 0))),
        scratch_shapes=[pltpu.VMEM((lt + SUBLANES, LRU_W), f32), pltpu.VMEM((lt, LRU_W), f32),
                        pltpu.VMEM((lt, LRU_W), f32), pltpu.VMEM((SUBLANES, LRU_W), f32)],
        compiler_params=_params("parallel", "arbitrary"),
        name="even_front",
    )(x, mod, mod, w, *consts)


S5_R = 4
S5_TAB = 8


S5_GPB = LANES // S5_GH
S5_NBLK = S5_G // S5_GPB


def _s5_zoh(lr_, li_, ldt):
    dt = jnp.exp(ldt)
    mag = jnp.exp(lr_ * dt)
    ar, ai = mag * jnp.cos(li_ * dt), mag * jnp.sin(li_ * dt)
    den = lr_ * lr_ + li_ * li_
    nr, ni = ar - 1.0, ai
    return ar, ai, (nr * lr_ + ni * li_) / den, (ni * lr_ - nr * li_) / den


def _s5_param_kernel(lr_ref, li_ref, ldt_ref, lrf_ref, lif_ref, ldtf_ref, btr_ref, bti_ref, cr_ref, ci_ref,
                     a1r_ref, a1i_ref, pcr_ref, pci_ref, wx_ref, wct_ref, ct_ref, bd_s):
    ar, ai, fr, fi = _s5_zoh(lr_ref[...], li_ref[...], ldt_ref[...])
    bbr = fr[:, None, :] * btr_ref[...] - fi[:, None, :] * bti_ref[...]
    bbi = fr[:, None, :] * bti_ref[...] + fi[:, None, :] * btr_ref[...]
    cr, ci = cr_ref[...], ci_ref[...]

    def put(dst_ref, row0, re, im):
        for j in range(S5_NBLK):
            bd_s[...] = jnp.zeros(bd_s.shape, f32)
            for g in range(S5_GPB):
                rows = slice(g * S5_GH, (g + 1) * S5_GH)
                bd_s[rows, g * S5_P:(g + 1) * S5_P] = re[j * S5_GPB + g]
                bd_s[rows, S5_BLK + g * S5_P:S5_BLK + (g + 1) * S5_P] = im[j * S5_GPB + g]
            dst_ref[j, row0:row0 + LANES, :] = bd_s[...].astype(bf16)

    put(ct_ref, 0, cr, -ci)
    pr, pi = jnp.ones_like(ar), jnp.zeros_like(ai)
    for n in range(S5_R + 1):
        if n < S5_R:
            put(wx_ref, (S5_R - 1 - n) * LANES,
                pr[:, None, :] * bbr - pi[:, None, :] * bbi, pr[:, None, :] * bbi + pi[:, None, :] * bbr)
        if n >= 1:
            put(wct_ref, (n - 1) * LANES,
                cr * pr[:, None, :] - ci * pi[:, None, :], -(cr * pi[:, None, :] + ci * pr[:, None, :]))
        pr, pi = pr * ar - pi * ai, pr * ai + pi * ar

    ar, ai, _, _ = _s5_zoh(lrf_ref[...], lif_ref[...], ldtf_ref[...])
    a1r_ref[...], a1i_ref[...] = ar, ai
    pr, pi = ar, ai
    for n in range(1, S5_R * S5_TAB + 1):
        if n % S5_R == 0:
            pcr_ref[n // S5_R - 1:n // S5_R, :] = pr
            pci_ref[n // S5_R - 1:n // S5_R, :] = pi
        pr, pi = pr * ar - pi * ai, pr * ai + pi * ar


def _toeplitz_kernel(wx_ref, ct_ref, tp_ref):
    lag = [_dot_nt(wx_ref[(S5_R - 1 - d) * LANES:(S5_R - d) * LANES, :], ct_ref[...]).astype(bf16)
           for d in range(S5_R)]
    tp_ref[...] = jnp.zeros(tp_ref.shape, bf16)
    for s in range(S5_R):
        for r in range(s, S5_R):
            tp_ref[s * LANES:(s + 1) * LANES, r * LANES:(r + 1) * LANES] = lag[r - s]


def _s5_params(lam_re, lam_im, log_dt, b_re, b_im, c_re, c_im):
    btr = jnp.swapaxes(b_re, 1, 2)
    bti = jnp.swapaxes(b_im, 1, 2)
    shp = jax.ShapeDtypeStruct
    flat = lambda a: a.reshape(1, S5_N)
    wide = (S5_NBLK, S5_R * LANES, 2 * S5_BLK)
    a1r, a1i, pcr, pci, w_x, w_ct, c_t = pl.pallas_call(
        _s5_param_kernel,
        out_shape=(shp((1, S5_N), f32), shp((1, S5_N), f32), shp((S5_TAB, S5_N), f32), shp((S5_TAB, S5_N), f32),
                   shp(wide, bf16), shp(wide, bf16), shp((S5_NBLK, LANES, 2 * S5_BLK), bf16)),
        scratch_shapes=[pltpu.VMEM((LANES, 2 * S5_BLK), f32)],
        name="s5_params",
    )(lam_re, lam_im, log_dt.reshape(S5_G, 1), flat(lam_re), flat(lam_im), flat(jnp.repeat(log_dt, S5_P)),
      btr, bti, c_re, c_im)
    toep = pl.pallas_call(
        _toeplitz_kernel,
        out_shape=shp((S5_NBLK, S5_R * LANES, S5_R * LANES), bf16),
        grid=(S5_NBLK,),
        in_specs=[pl.BlockSpec((None,) + wide[1:], lambda j: (j, 0, 0)),
                  pl.BlockSpec((None, LANES, 2 * S5_BLK), lambda j: (j, 0, 0))],
        out_specs=pl.BlockSpec((None, S5_R * LANES, S5_R * LANES), lambda j: (j, 0, 0)),
        compiler_params=_params("parallel"),
        name="s5_toeplitz",
    )(w_x, c_t)
    return dict(a1r=a1r, a1i=a1i, pcr=pcr, pci=pci, c_t=c_t, w_x=w_x, toep=toep, w_ct=w_ct)


S5_BLK = S5_N // (S5_G // (LANES // S5_GH))


def _s5_input(u, wx_ref, j):
    b_bar = wx_ref[j, (S5_R - 1) * LANES:S5_R * LANES, :]
    bu = _dot(u[:, j * LANES:(j + 1) * LANES].astype(bf16), b_bar)
    return bu[:, 0:S5_BLK], bu[:, S5_BLK:2 * S5_BLK]


def _s5_output(u, hr, hi, ct_ref, d_ref, wg_ref, bg_ref):
    ys = []
    for j in range(S5_N // S5_BLK):
        cs = slice(j * S5_BLK, (j + 1) * S5_BLK)
        hcat = jnp.concatenate([hr[:, cs], hi[:, cs]], axis=1).astype(bf16)
        ys.append(_dot_nt(hcat, ct_ref[j]))
    y = _gelu(jnp.concatenate(ys, axis=1) + d_ref[...] * u)
    return y * _sigmoid(_dot(y.astype(bf16), wg_ref[...]) + bg_ref[...])


def _s5_prompt_kernel(*refs):
    nblk = S5_N // S5_BLK
    u_refs = refs[:nblk]
    wx_ref, tp_ref, wct_ref, pcr_ref, pci_ref, d_ref, wg_ref, bg_ref = refs[nblk:nblk + 8]
    y_refs = refs[nblk + 8:2 * nblk + 8]
    sr_ref, si_ref, er_s, ei_s, y_s = refs[2 * nblk + 8:]
    t_idx = pl.program_id(1)
    cr = u_refs[0].shape[0] // S5_R
    gw = S5_R * LANES

    @pl.when(t_idx == 0)
    def _():
        er_s[0:SUBLANES, :] = jnp.zeros((SUBLANES, S5_N), f32)
        ei_s[0:SUBLANES, :] = jnp.zeros((SUBLANES, S5_N), f32)

    def token(s):
        return pl.ds(s, cr, stride=S5_R)

    sub = lax.broadcasted_iota(jnp.int32, (SUBLANES, S5_BLK), 0)
    for j in range(nblk):
        cs = slice(j * S5_BLK, (j + 1) * S5_BLK)
        uj = jnp.concatenate([u_refs[j][token(s), :] for s in range(S5_R)], axis=1).astype(bf16)
        x = _dot(uj, wx_ref[j])
        y_s[:, j * gw:(j + 1) * gw] = _dot(uj, tp_ref[j])
        hr = x[:, 0:S5_BLK].reshape(cr // SUBLANES, SUBLANES, S5_BLK)
        hi = x[:, S5_BLK:2 * S5_BLK].reshape(cr // SUBLANES, SUBLANES, S5_BLK)
        for d in (1, 2, 4):
            pr = jnp.where(sub >= d, pcr_ref[d - 1:d, cs], 0.0)
            pi = jnp.where(sub >= d, pci_ref[d - 1:d, cs], 0.0)
            sr, si = pltpu.roll(hr, d, 1), pltpu.roll(hi, d, 1)
            hr, hi = hr + (pr * sr - pi * si), hi + (pr * si + pi * sr)
        er_s[SUBLANES:SUBLANES + cr, cs] = hr.reshape(cr, S5_BLK)
        ei_s[SUBLANES:SUBLANES + cr, cs] = hi.reshape(cr, S5_BLK)

    tr, ti = pcr_ref[...], pci_ref[...]
    last = pl.ds(SUBLANES - 1, 1)
    c0 = (jnp.broadcast_to(er_s[last, :], (SUBLANES, S5_N)), jnp.broadcast_to(ei_s[last, :], (SUBLANES, S5_N)))

    def group(g, c):
        c_r, c_i = c
        rows = pl.ds(pl.multiple_of((g + 1) * SUBLANES, SUBLANES), SUBLANES)
        g_r = er_s[rows, :] + tr * c_r - ti * c_i
        g_i = ei_s[rows, :] + tr * c_i + ti * c_r
        er_s[rows, :] = g_r
        ei_s[rows, :] = g_i
        return (jnp.broadcast_to(g_r[SUBLANES - 1:SUBLANES, :], (SUBLANES, S5_N)),
                jnp.broadcast_to(g_i[SUBLANES - 1:SUBLANES, :], (SUBLANES, S5_N)))

    c_r, c_i = lax.fori_loop(0, cr // SUBLANES, group, c0)

    for j in range(nblk):
        cs = slice(j * S5_BLK, (j + 1) * S5_BLK)
        prev = pl.ds(SUBLANES - 1, cr)
        sj = jnp.concatenate([er_s[prev, cs], ei_s[prev, cs]], axis=1).astype(bf16)
        y_s[:, j * gw:(j + 1) * gw] += _dot_nt(sj, wct_ref[j])
    er_s[last, :] = c_r[0:1, :]
    ei_s[last, :] = c_i[0:1, :]
    sr_ref[...] = c_r[0:1, :]
    si_ref[...] = c_i[0:1, :]

    for r in range(S5_R):
        y = jnp.concatenate([y_s[:, j * gw + r * LANES:j * gw + (r + 1) * LANES] for j in range(nblk)], axis=1)
        u_r = jnp.concatenate([u_refs[j][token(r), :] for j in range(nblk)], axis=1)
        y = _gelu(y + d_ref[...] * u_r)
        y = y * _sigmoid(_dot(y.astype(bf16), wg_ref[...]) + bg_ref[...])
        for j in range(nblk):
            y_refs[j][token(r), :] = y[:, j * LANES:(j + 1) * LANES]


def _s5_prompt(us, tabs, p, batch, seq, cr):
    nblk = len(us)
    tiles = seq // (S5_R * cr)
    consts = [tabs["w_x"], tabs["toep"], tabs["w_ct"], tabs["pcr"], tabs["pci"], p["d"], p["w_glu"], p["b_glu"]]
    blk = pl.BlockSpec((S5_R * cr, LANES), lambda b, t: (b * tiles + t, 0))
    state = pl.BlockSpec((None, 1, S5_N), lambda b, t: (b, 0, 0))
    outs = pl.pallas_call(
        _s5_prompt_kernel,
        out_shape=tuple(jax.ShapeDtypeStruct((batch * seq, LANES), f32) for _ in range(nblk))
        + (jax.ShapeDtypeStruct((batch, 1, S5_N), f32), jax.ShapeDtypeStruct((batch, 1, S5_N), f32)),
        grid=(batch, tiles),
        in_specs=[blk] * nblk + [_const_spec(c.shape) for c in consts],
        out_specs=(blk,) * nblk + (state, state),
        scratch_shapes=[pltpu.VMEM((cr + SUBLANES, S5_N), f32), pltpu.VMEM((cr + SUBLANES, S5_N), f32),
                        pltpu.VMEM((cr, S5_R * S5_W), f32)],
        compiler_params=_params("parallel", "arbitrary"),
        name="s5_prompt",
    )(*us, *consts)
    return list(outs[:nblk]), outs[nblk], outs[nblk + 1]


def _gla_step_kernel(kind, dk, layer, *refs):
    if kind == "gla":
        q_ref, k_ref, v_ref, gate_ref, lr_ref, wlr_ref, blr_ref, ng_ref, s_ref, y_ref, so_ref, o_scr = refs
        sl = pl.program_id(0)
        q = q_ref[...] * (GLA_DK ** -0.5)
        k = k_ref[...]
        pre_all = _dot(lr_ref[...].astype(bf16), wlr_ref[...]) + blr_ref[...]
        pre = jnp.where(sl == 0, pre_all[:, 0:LANES], pre_all[:, LANES:2 * LANES])
        a = jnp.exp(_log_sigmoid(pre) * (1.0 / GLA_TAU))
    else:
        q_ref, f_ref, v_ref, gate_ref, lg_ref, ng_ref, s_ref, y_ref, so_ref, o_scr = refs
        sl = pl.program_id(0)
        lb_all = _hgrn_lower_bound(lg_ref, layer)
        lb = lb_all[:, 0:LANES]
        for j in range(1, HG_H):
            lb = jnp.where(sl == j, lb_all[:, j * LANES:(j + 1) * LANES], lb)
        a = lb + (1.0 - lb) * _sigmoid(f_ref[...])
        q = _silu(q_ref[...])
        k = 1.0 - a
    hps = LANES // dk
    nb = q.shape[0]
    qat, kt, at = (q * a).T, k.T, a.T
    lane = lax.broadcasted_iota(jnp.int32, (1, LANES), 1)
    v = v_ref[...]
    for hh in range(hps):
        r0 = hh * dk
        msk = (lane >= r0) & (lane < r0 + dk)
        qk = jnp.sum(jnp.where(msk, q * k, 0.0), axis=-1, keepdims=True)
        for b in range(nb):
            s = s_ref[b, hh]
            col = lambda x: jnp.broadcast_to(x[r0:r0 + dk, b:b + 1], (dk, LANES))
            vr = v[b:b + 1, hh * LANES:(hh + 1) * LANES]
            so_ref[b, hh] = col(at) * s + col(kt) * vr
            oi = jnp.sum(col(qat) * s, axis=0, keepdims=True)
            o_scr[b:b + 1, hh * LANES:(hh + 1) * LANES] = jnp.broadcast_to(qk[b:b + 1, :], (1, LANES)) * vr + oi
    for hh in range(hps):
        cs = slice(hh * LANES, (hh + 1) * LANES)
        o = o_scr[:, cs]
        o = o * lax.rsqrt(jnp.mean(o * o, axis=-1, keepdims=True) + RMS_EPS) * ng_ref[...]
        y_ref[:, cs] = o * _silu(gate_ref[:, cs])


def _gla_step(kind, z, extra, norm_g, state, layer):
    nb = z.shape[0]
    dk = GLA_DK if kind == "gla" else HG_D
    n_heads = GLA_H if kind == "gla" else HG_H
    hps = LANES // dk
    n_slabs = n_heads // hps
    vw = hps * LANES
    zb = lambda width, cb0: pl.BlockSpec((nb, width), lambda s: (0, cb0 + s))
    if kind == "gla":
        wlr, blr = extra
        ins = [z, z, z, z, z, wlr, blr]
        specs = [zb(LANES, 0), zb(LANES, GLA_KW // LANES), zb(vw, GLA_VW // vw * 1), zb(vw, GLA_VW // vw * 2),
                 pl.BlockSpec((nb, LANES), lambda s: (0, GLA_LR_BLOCK)),
                 _const_spec(wlr.shape), _const_spec(blr.shape)]
    else:
        (lg,) = extra
        ins = [z, z, z, z, lg]
        specs = [zb(LANES, 0), zb(LANES, HG_W // LANES), zb(vw, HG_W // vw * 2),
                 zb(vw, HG_W // vw * 3), _const_spec(lg.shape)]
    ins += [norm_g, state]
    specs += [_const_spec(norm_g.shape), pl.BlockSpec((nb, hps, dk, LANES), lambda s: (0, s, 0, 0))]
    return pl.pallas_call(
        functools.partial(_gla_step_kernel, kind, dk, layer),
        out_shape=(jax.ShapeDtypeStruct((nb, HALF), f32), jax.ShapeDtypeStruct(state.shape, f32)),
        grid=(n_slabs,),
        in_specs=specs,
        out_specs=(pl.BlockSpec((nb, vw), lambda s: (0, s)),
                   pl.BlockSpec((nb, hps, dk, LANES), lambda s: (0, s, 0, 0))),
        scratch_shapes=[pltpu.VMEM((nb, vw), f32)],
        compiler_params=_params("parallel"),
        name=kind + "_step",
    )(*ins)


def _lru_step_kernel(x_ref, gr_ref, c0_ref, c1_ref, c2_ref, h0_ref, cw_ref, cb_ref, wr_ref, wi_ref,
                     br_ref, bi_ref, lam_ref, y_ref, h_ref):
    xc = (cb_ref[...] + cw_ref[0:1, :] * c0_ref[...] + cw_ref[1:2, :] * c1_ref[...]
          + cw_ref[2:3, :] * c2_ref[...] + cw_ref[3:4, :] * x_ref[...])
    a, u = _lru_gates(xc, wr_ref, wi_ref, br_ref, bi_ref, lam_ref)
    h = a * h0_ref[...] + u
    h_ref[...] = h
    y_ref[...] = h * _gelu(gr_ref[...])


def _lru_step(xr, gr, conv_rows, h0, p):
    nb = xr.shape[0]
    full = lambda: pl.BlockSpec((nb, LRU_W), lambda i: (0, 0))
    consts = [p["conv_w"], p["conv_b"], p["w_r"], p["w_i"], p["b_r"], p["b_i"], p["lam"]]
    return pl.pallas_call(
        _lru_step_kernel,
        out_shape=(jax.ShapeDtypeStruct((nb, LRU_W), f32), jax.ShapeDtypeStruct((nb, LRU_W), f32)),
        grid=(1,),
        in_specs=[full() for _ in range(CONV_W + 2)] + [_const_spec(c.shape) for c in consts],
        out_specs=(full(), full()),
        compiler_params=_params("arbitrary"),
        name="lru_step",
    )(xr, gr, *conv_rows, h0, *consts)


def _s5_step_kernel(u_ref, s0r_ref, s0i_ref, a1r_ref, a1i_ref, wx_ref, ct_ref, d_ref, wg_ref, bg_ref,
                    y_ref, sr_ref, si_ref):
    u = u_ref[...]
    for j in range(S5_N // S5_BLK):
        cs = slice(j * S5_BLK, (j + 1) * S5_BLK)
        bur, bui = _s5_input(u, wx_ref, j)
        ar, ai = a1r_ref[:, cs], a1i_ref[:, cs]
        s0r, s0i = s0r_ref[:, cs], s0i_ref[:, cs]
        sr_ref[:, cs] = bur + ar * s0r - ai * s0i
        si_ref[:, cs] = bui + ar * s0i + ai * s0r
    y_ref[...] = _s5_output(u, sr_ref[...], si_ref[...], ct_ref, d_ref, wg_ref, bg_ref)


def _s5_step(u, s0r, s0i, tabs, p):
    nb = u.shape[0]
    consts = [tabs["a1r"], tabs["a1i"], tabs["w_x"], tabs["c_t"], p["d"], p["w_glu"], p["b_glu"]]
    st = lambda: pl.BlockSpec((nb, S5_N), lambda i: (0, 0))
    return pl.pallas_call(
        _s5_step_kernel,
        out_shape=(jax.ShapeDtypeStruct((nb, S5_W), f32), jax.ShapeDtypeStruct((nb, S5_N), f32),
                   jax.ShapeDtypeStruct((nb, S5_N), f32)),
        grid=(1,),
        in_specs=[pl.BlockSpec((nb, S5_W), lambda i: (0, 0)), st(), st()] + [_const_spec(c.shape) for c in consts],
        out_specs=(pl.BlockSpec((nb, S5_W), lambda i: (0, 0)), st(), st()),
        compiler_params=_params("arbitrary"),
        name="s5_step",
    )(u, s0r, s0i, *consts)


def _block_diag(w):
    h, n, _ = w.shape
    return jnp.einsum('hij,hk->hikj', w, jnp.eye(h, dtype=w.dtype)).reshape(h * n, h * n)


def kernel(x_prompt, x_sample, c_prompt, c_sample, state_gla, state_rglru_conv, state_rglru_h, state_s5_re, state_s5_im, state_hgrn, ev_w_in, ev_gla_w_lr, ev_gla_b_lr, ev_gla_norm_g, ev_conv_w, ev_conv_b, ev_lru_w_r, ev_lru_b_r, ev_lru_w_i, ev_lru_b_i, ev_lru_lam, ev_w_out, od_w_in, od_s5_lam_re, od_s5_lam_im, od_s5_log_dt, od_s5_b_re, od_s5_b_im, od_s5_c_re, od_s5_c_im, od_s5_d, od_s5_w_glu, od_s5_b_glu, hg_lb_logits, od_hg_norm_g, od_w_out, w_ada, b_ada, ln_g, ln_b, ffn_w_in, ffn_w_out):
    bp, seq = x_prompt.shape[0], x_prompt.shape[1]
    bs = x_sample.shape[0]
    assert x_sample.shape[1] == 1 and seq % CHUNK == 0
    tp = bp * seq
    tm = min(512, seq)
    tm_ffn = min(2 * FFN_SUB_ROWS, seq)
    lt = min(512, seq)
    lt_gla = min(2048, seq)
    cr_s5 = min(256, seq // S5_R)
    row2 = lambda a: a.reshape(1, -1)

    mod = _modulation(jnp.concatenate([c_prompt, c_sample], axis=0).astype(f32), w_ada, b_ada)
    mod_p = mod[:, :bp].reshape(DEPTH, bp, 1, 6 * D_MODEL)
    mod_s = mod[:, bp:].reshape(DEPTH, 1, bs, 6 * D_MODEL)

    xp = x_prompt.astype(f32).reshape(tp, D_MODEL)
    xs = x_sample.astype(f32).reshape(bs, D_MODEL)
    outs_p = {k: [] for k in ("gla", "conv", "lru", "re", "im", "hg")}
    outs_s = {k: [] for k in ("gla", "conv", "lru", "re", "im", "hg")}

    w_f_in = ffn_w_in.astype(bf16)
    w_f_out = ffn_w_out.astype(bf16)
    for l in range(DEPTH):
        if l % 2 == 0:
            e = l // 2
            w = ev_w_in[e]
            lr0 = 2 * GLA_KW + 2 * GLA_VW
            w_in = jnp.concatenate(
                [w[:, :lr0 + GLA_LR], jnp.zeros((D_MODEL, LANES - GLA_LR), w.dtype), w[:, lr0 + GLA_LR:]],
                axis=1).astype(bf16)
            w_out = ev_w_out[e].astype(bf16)
            wlr = jnp.concatenate([ev_gla_w_lr[e], jnp.zeros((LANES - GLA_LR, GLA_KW), f32)], axis=0).astype(bf16)
            gla_extra = (wlr, row2(ev_gla_b_lr[e]))
            ng = row2(ev_gla_norm_g[e])
            lru_p = dict(conv_w=ev_conv_w[e], conv_b=row2(ev_conv_b[e]),
                         w_r=_block_diag(ev_lru_w_r[e]).astype(bf16), w_i=_block_diag(ev_lru_w_i[e]).astype(bf16),
                         b_r=row2(ev_lru_b_r[e]), b_i=row2(ev_lru_b_i[e]), lam=row2(ev_lru_lam[e]))
            z, yb, s_conv, s_h = _even_front(xp, mod_p[l], w_in, lru_p, bp, seq, lt)
            ya, s_gla = _gla_prompt("gla", z, GLA_LR_BLOCK, gla_extra, ng, bp, seq, lt_gla, l)
            xp = _outffn(xp, [ya, yb], mod_p[l], w_out, ln_g[l], ln_b[l], w_f_in, w_f_out, l, tm_ffn, seq // tm_ffn)
            outs_p["gla"].append(s_gla)
            outs_p["conv"].append(s_conv)
            outs_p["lru"].append(s_h.reshape(bp, LRU_W))
            z, xr, gr = _inproj(xs, mod_s[l], w_in, bs, 1, (EVEN_Z, LRU_W, LRU_W))
            ya, s_gla = _gla_step("gla", z, gla_extra, ng, state_gla[e], l)
            cs = state_rglru_conv[e].astype(f32)
            yb, s_h = _lru_step(xr, gr, [cs[:, j] for j in range(CONV_W - 1)], state_rglru_h[e].astype(f32), lru_p)
            xs = _outffn(xs, [ya, yb], mod_s[l], w_out, ln_g[l], ln_b[l], w_f_in, w_f_out, l, bs, 1)
            outs_s["gla"].append(s_gla)
            outs_s["conv"].append(jnp.stack([cs[:, 1], cs[:, 2], xr], axis=1))
            outs_s["lru"].append(s_h)
        else:
            o = l // 2
            w_in = od_w_in[o].astype(bf16)
            w_out = od_w_out[o].astype(bf16)
            tabs = _s5_params(od_s5_lam_re[o], od_s5_lam_im[o], od_s5_log_dt[o], od_s5_b_re[o], od_s5_b_im[o],
                              od_s5_c_re[o], od_s5_c_im[o])
            s5_p = dict(d=row2(od_s5_d[o]), w_glu=od_s5_w_glu[o].astype(bf16), b_glu=row2(od_s5_b_glu[o]))
            hg_extra = (hg_lb_logits.astype(f32),)
            ng = row2(od_hg_norm_g[o])
            *us, z = _inproj(xp, mod_p[l], w_in, tm, seq // tm, (LANES,) * (S5_W // LANES) + (ODD_N - S5_W,))
            yas, s_re, s_im = _s5_prompt(us, tabs, s5_p, bp, seq, cr_s5)
            yb, s_hg = _gla_prompt("hgrn", z, 0, hg_extra, ng, bp, seq, lt_gla, l)
            xp = _outffn(xp, yas + [yb], mod_p[l], w_out, ln_g[l], ln_b[l], w_f_in, w_f_out, l, tm_ffn, seq // tm_ffn)
            outs_p["re"].append(s_re.reshape(bp, S5_G, S5_P))
            outs_p["im"].append(s_im.reshape(bp, S5_G, S5_P))
            outs_p["hg"].append(s_hg)
            u, z = _inproj(xs, mod_s[l], w_in, bs, 1, (S5_W, ODD_N - S5_W))
            ya, s_re, s_im = _s5_step(u, state_s5_re[o].astype(f32).reshape(bs, S5_N),
                                      state_s5_im[o].astype(f32).reshape(bs, S5_N), tabs, s5_p)
            yb, s_hg = _gla_step("hgrn", z, hg_extra, ng, state_hgrn[o], l)
            xs = _outffn(xs, [ya, yb], mod_s[l], w_out, ln_g[l], ln_b[l], w_f_in, w_f_out, l, bs, 1)
            outs_s["re"].append(s_re.reshape(bs, S5_G, S5_P))
            outs_s["im"].append(s_im.reshape(bs, S5_G, S5_P))
            outs_s["hg"].append(s_hg)

    st = lambda d, k: d[k][0][None] if len(d[k]) == 1 else jnp.stack(d[k])
    return (xp.reshape(bp, seq, D_MODEL).astype(x_prompt.dtype), xs.reshape(bs, 1, D_MODEL).astype(x_sample.dtype),
            st(outs_p, "gla"), st(outs_s, "gla"), st(outs_p, "conv"), st(outs_s, "conv"),
            st(outs_p, "lru"), st(outs_s, "lru"), st(outs_p, "re"), st(outs_s, "re"),
            st(outs_p, "im"), st(outs_s, "im"), st(outs_p, "hg"), st(outs_s, "hg"))
```

```python
import functools
import math

import numpy as np
import jax
import jax.numpy as jnp
from jax import lax
from jax.experimental import pallas as pl
from jax.experimental.pallas import tpu as pltpu

f32 = jnp.float32
bf16 = jnp.bfloat16

D_MODEL = 1024
DEPTH = 2
HALF = D_MODEL // 2
GLA_H = 4
GLA_DV = HALF // GLA_H
GLA_DK = GLA_DV // 2
GLA_KW = GLA_H * GLA_DK
GLA_VW = GLA_H * GLA_DV
GLA_LR = 16
GLA_TAU = 16.0
LRU_W = HALF
LRU_H = 8
LRU_BW = LRU_W // LRU_H
CONV_W = 4
RG_C = 8.0
S5_W = HALF
S5_GH = 16
S5_G = S5_W // S5_GH
S5_P = 64
S5_N = S5_G * S5_P
HG_H = 4
HG_D = HALF // HG_H
HG_W = HG_H * HG_D
D_FF = ((8 * D_MODEL // 3 + 255) // 256) * 256
ALPHA = (2.0 * DEPTH) ** 0.25
LN_EPS = 1e-5
RMS_EPS = 1e-6

LANES = 128
SUBLANES = 8
MXU_WIDTH = 256
LRU_ROWS = 128
VMEM_LIMIT = 56 * 1024 * 1024

CHUNK = 64
N_LEVELS = 6
FINE_LEVELS = 3
GLA_UNROLL = 3
GATE_ROWS = 512
FF_CHUNK = 256
FFN_SUB_ROWS = 512
EVEN_Z = 2 * GLA_KW + 2 * GLA_VW + LANES
EVEN_N = EVEN_Z + 2 * LRU_W
GLA_LR_BLOCK = (EVEN_Z - LANES) // LANES
ODD_N = 2560


def _sigmoid(x):
    return 0.5 * jnp.tanh(0.5 * x) + 0.5


def _silu(x):
    return x * _sigmoid(x)


def _gelu(x):
    c = math.sqrt(2.0 / math.pi)
    return x * (0.5 * (1.0 + jnp.tanh(c * (x + 0.044715 * (x * x * x)))))


def _softplus(x):
    return jnp.maximum(x, 0.0) + jnp.log1p(jnp.exp(-jnp.abs(x)))


def _log_sigmoid(x):
    return -_softplus(-x)


def _layer_norm(x, g, b):
    mu = jnp.mean(x, axis=-1, keepdims=True)
    xc = x - mu
    var = jnp.mean(xc * xc, axis=-1, keepdims=True)
    return xc * lax.rsqrt(var + LN_EPS) * g + b


def _dot(a, b):
    return jnp.dot(a, b, preferred_element_type=f32)


def _dot_nt(a, b):
    return lax.dot_general(a, b, (((1,), (1,)), ((), ())), preferred_element_type=f32)


def _dot_tn(a, b):
    return lax.dot_general(a, b, (((0,), (0,)), ((), ())), preferred_element_type=f32)


def _const_spec(shape):
    nd = len(shape)
    return pl.BlockSpec(shape, lambda *_: (0,) * nd, pipeline_mode=pl.Buffered(1))


def _params(*sem):
    return pltpu.CompilerParams(dimension_semantics=sem, vmem_limit_bytes=VMEM_LIMIT)


def _mod_kernel(c_ref, w_ref, b_ref, o_ref):
    cond = _silu(c_ref[...]).astype(bf16)
    o_ref[...] = _dot(cond, w_ref[...].astype(bf16)) + b_ref[...]


def _modulation(c_all, w_ada, b_ada):
    rows = c_all.shape[0]
    tn = 1536
    return pl.pallas_call(
        _mod_kernel,
        out_shape=jax.ShapeDtypeStruct((DEPTH, rows, 6 * D_MODEL), f32),
        grid=(DEPTH, 6 * D_MODEL // tn),
        in_specs=[pl.BlockSpec((rows, D_MODEL), lambda l, j: (0, 0)),
                  pl.BlockSpec((None, D_MODEL, tn), lambda l, j: (l, 0, j)),
                  pl.BlockSpec((None, 1, tn), lambda l, j: (l, 0, j))],
        out_specs=pl.BlockSpec((None, rows, tn), lambda l, j: (l, 0, j)),
        compiler_params=_params("parallel", "parallel"),
        name="adaln_mod",
    )(c_all, w_ada, b_ada.reshape(DEPTH, 1, 6 * D_MODEL))


def _mod_spec(rows_per_block, tiles_per_batch, j):
    return pl.BlockSpec((None, rows_per_block, D_MODEL), lambda i: (i // tiles_per_batch, 0, j))


def _inproj_kernel(x_ref, sc_ref, sh_ref, w_ref, *z_refs):
    h = (x_ref[...] * (1.0 + sc_ref[...]) + sh_ref[...]).astype(bf16)
    lo = 0
    i = 0
    while i < len(z_refs):
        j = i + 1
        while z_refs[i].shape[1] < MXU_WIDTH and j < len(z_refs) and z_refs[j].shape[1] == z_refs[i].shape[1]:
            j += 1
        width = sum(r.shape[1] for r in z_refs[i:j])
        z = _dot(h, w_ref[:, lo:lo + width])
        off = 0
        for r in z_refs[i:j]:
            r[...] = z[:, off:off + r.shape[1]]
            off += r.shape[1]
        lo += width
        i = j


def _inproj(x, mod, w, tm, tiles_per_batch, widths):
    t, n = x.shape[0], w.shape[1]
    assert sum(widths) == n
    r = mod.shape[1]
    return pl.pallas_call(
        _inproj_kernel,
        out_shape=tuple(jax.ShapeDtypeStruct((t, wd), f32) for wd in widths),
        grid=(t // tm,),
        in_specs=[pl.BlockSpec((tm, D_MODEL), lambda i: (i, 0)),
                  _mod_spec(r, tiles_per_batch, 1), _mod_spec(r, tiles_per_batch, 0),
                  _const_spec((D_MODEL, n))],
        out_specs=tuple(pl.BlockSpec((tm, wd), lambda i: (i, 0)) for wd in widths),
        compiler_params=_params("parallel"),
        name="in_proj",
    )(x, mod, mod, w)


def _outffn_kernel(n_y, x_ref, *refs):
    y_refs = refs[:n_y]
    gtm_ref, shf_ref, scf_ref, gtf_ref, wo_ref, lng_ref, lnb_ref, wi_ref, wf_ref, o_ref = refs[n_y:]
    tm = x_ref.shape[0]
    n_sub = 2 if tm % (2 * FFN_SUB_ROWS) == 0 else 1
    rs = [slice(i * (tm // n_sub), (i + 1) * (tm // n_sub)) for i in range(n_sub)]
    mrows = lambda ref, rows: ref[...] if ref.shape[0] == 1 else ref[rows, :]
    n_ff = D_FF // FF_CHUNK

    def ln1(i, mix):
        rows = rs[i]
        x1 = _layer_norm(ALPHA * x_ref[rows, :] + (1.0 + mrows(gtm_ref, rows)) * mix, lng_ref[0:1, :], lnb_ref[0:1, :])
        return x1, (x1 * (1.0 + mrows(scf_ref, rows)) + mrows(shf_ref, rows)).astype(bf16)

    def ln2(i, x1, acc):
        rows = rs[i]
        o_ref[rows, :] = _layer_norm(ALPHA * x1 + (1.0 + mrows(gtf_ref, rows)) * acc, lng_ref[1:2, :], lnb_ref[1:2, :])

    mixes = [_dot(jnp.concatenate([r[rows, :].astype(bf16) for r in y_refs], axis=1), wo_ref[...]) for rows in rs]
    cur = ln1(0, mixes[0])
    pending = None
    for i in range(n_sub):
        x1, h = cur
        acc = jnp.zeros(x1.shape, f32)
        for c in range(n_ff):
            lo = c * FF_CHUNK
            gate = _dot(h, wi_ref[:, lo:lo + FF_CHUNK])
            up = _dot(h, wi_ref[:, D_FF + lo:D_FF + lo + FF_CHUNK])
            act = (_silu(gate) * up).astype(bf16)
            acc = acc + _dot(act, wf_ref[lo:lo + FF_CHUNK, :])
            if c == 1:
                if pending is not None:
                    ln2(*pending)
                    pending = None
                if i + 1 < n_sub:
                    cur = ln1(i + 1, mixes[i + 1])
        pending = (i, x1, acc)
    ln2(*pending)


def _outffn(x, ys, mod, w_out, ln_g, ln_b, w_in, w_f, layer, tm, tiles_per_batch):
    t = x.shape[0]
    r = mod.shape[1]
    assert sum(y.shape[1] for y in ys) == D_MODEL
    row = lambda w: pl.BlockSpec((tm, w), lambda i: (i, 0))
    stacked = lambda a: pl.BlockSpec((None,) + a.shape[1:], lambda i: (layer, 0, 0), pipeline_mode=pl.Buffered(1))
    return pl.pallas_call(
        functools.partial(_outffn_kernel, len(ys)),
        out_shape=jax.ShapeDtypeStruct((t, D_MODEL), f32),
        grid=(t // tm,),
        in_specs=[row(D_MODEL)] + [row(y.shape[1]) for y in ys] + [
                  _mod_spec(r, tiles_per_batch, 2), _mod_spec(r, tiles_per_batch, 3),
                  _mod_spec(r, tiles_per_batch, 4), _mod_spec(r, tiles_per_batch, 5),
                  _const_spec((D_MODEL, D_MODEL)), _const_spec((2, D_MODEL)), _const_spec((2, D_MODEL)),
                  stacked(w_in), stacked(w_f)],
        out_specs=row(D_MODEL),
        compiler_params=_params("parallel"),
        name="outproj_ffn",
    )(x, *ys, mod, mod, mod, mod, w_out, ln_g, ln_b, w_in, w_f)


def _level_tables():
    c = CHUNK
    t = np.arange(c)[:, None]
    u = np.arange(c)[None, :]
    pmask, sgn = [], []
    for lvl in range(N_LEVELS):
        m = 1 << lvl
        right = (t % (2 * m)) >= m
        pm = right & ((u % (2 * m)) < m) & ((t // (2 * m)) == (u // (2 * m)))
        pmask.append(pm.astype(np.float32))
        sgn.append(np.broadcast_to(np.where(right, 1.0, -1.0).astype(np.float32), (c, LANES)))
    tri = (u <= t).astype(np.float32)
    tri3 = np.concatenate([tri, tri, tri], axis=1)
    return (jnp.asarray(tri3, dtype=bf16), jnp.asarray(np.stack(pmask), dtype=f32),
            jnp.asarray(np.stack(sgn), dtype=f32))


LOG2E = 1.0 / math.log(2.0)


def _anchor(b, lvl):
    m = 1 << lvl
    b3 = b.reshape(CHUNK // SUBLANES, SUBLANES, LANES)
    if 4 * m == SUBLANES:
        sub = lax.broadcasted_iota(jnp.int32, b3.shape, 1)
        a = jnp.where(sub < 2 * m, jnp.broadcast_to(b3[:, m - 1:m, :], b3.shape),
                      jnp.broadcast_to(b3[:, 3 * m - 1:3 * m, :], b3.shape))
        return a.reshape(CHUNK, LANES)
    assert 2 * m == SUBLANES
    return jnp.broadcast_to(b3[:, m - 1:m, :], b3.shape).reshape(CHUNK, LANES)


def _split3(g):
    g1 = g.astype(bf16)
    r1 = g - g1.astype(f32)
    g2 = r1.astype(bf16)
    r2 = r1 - g2.astype(f32)
    return jnp.concatenate([g1, g2, r2.astype(bf16)], axis=0)


def _hgrn_lower_bound(lg_ref, layer):
    rows = [lg_ref[i:i + 1, :] for i in range(DEPTH)]
    mx = functools.reduce(jnp.maximum, rows)
    ex = [jnp.exp(r - mx) for r in rows]
    den = functools.reduce(lambda a, b: a + b, ex)
    sm = [e / den for e in ex]
    cs = sm[0]
    for i in range(1, layer + 1):
        cs = cs + sm[i]
    return cs - sm[0]


def _gla_prologue(kind, refs, rows):
    if kind == "gla":
        qk_ref, g_scr = refs
        q = qk_ref[rows, 0:GLA_KW] * (GLA_DK ** -0.5)
        k = qk_ref[rows, GLA_KW:2 * GLA_KW]
        return q, k, g_scr[rows, :]
    q_ref, f_ref, lb = refs
    forget = lb + (1.0 - lb) * _sigmoid(f_ref[rows, :])
    return _silu(q_ref[rows, :]), 1.0 - forget, jnp.log2(forget)


def _gla_chunk_kernel(kind, dk, n_heads, layer, *refs):
    if kind == "gla":
        (qk_ref, v_ref, gate_ref, lr_ref, wlr_ref, blr_ref, ng_ref, tri_ref, pm_ref, sg_ref,
         y_ref, s_ref, st_ref, g_scr) = refs
        pro = (qk_ref, g_scr)
        for r0 in range(0, y_ref.shape[0], GATE_ROWS):
            rs = slice(r0, min(r0 + GATE_ROWS, y_ref.shape[0]))
            pre = _dot(lr_ref[rs, :].astype(bf16), wlr_ref[...]) + blr_ref[...]
            g_scr[rs, :] = _log_sigmoid(pre) * (LOG2E / GLA_TAU)
    else:
        (q_ref, f_ref, v_ref, gate_ref, lg_ref, ng_ref, tri_ref, pm_ref, sg_ref,
         y_ref, s_ref, st_ref) = refs
        pro = (q_ref, f_ref, _hgrn_lower_bound(lg_ref, layer))
    t_idx = pl.program_id(1)
    n_tiles = pl.num_programs(1)
    lt = y_ref.shape[0]
    hps = LANES // dk
    n_slabs = n_heads // hps
    lane = lax.broadcasted_iota(jnp.int32, (1, LANES), 1)
    eye = (lax.broadcasted_iota(jnp.int32, (CHUNK, CHUNK), 0)
           == lax.broadcasted_iota(jnp.int32, (CHUNK, CHUNK), 1))

    @pl.when(t_idx == 0)
    def _():
        st_ref[...] = jnp.zeros(st_ref.shape, f32)

    def head_mask(hh):
        if hps == 1:
            return lambda a: a
        msk = (lane >= hh * dk) & (lane < (hh + 1) * dk)
        return lambda a: jnp.where(msk, a, jnp.zeros_like(a))

    def rows_of(c):
        return pl.ds(pl.multiple_of(c * CHUNK, CHUNK), CHUNK)

    def cumsums(qkg):
        q_all, k_all, g_all = qkg
        out = []
        for sl in range(n_slabs):
            cs = slice(sl * LANES, (sl + 1) * LANES)
            q_s, k_s, g_s = q_all[:, cs], k_all[:, cs], g_all[:, cs]
            out.append((q_s, k_s, g_s, _dot(tri_ref[...], _split3(g_s))))
        return out

    def score_dots(pre):
        raw, qbs, ksts, decs, rds = [], [], [], [], []
        right = [sg_ref[lvl] > 0.0 for lvl in range(FINE_LEVELS)]
        for sl in range(n_slabs):
            q_s, k_s, g_s, b = pre[sl]
            ys = [(jnp.where(right[0], jnp.exp2(g_s) * q_s, k_s)).astype(bf16)]
            for lvl in range(1, FINE_LEVELS):
                x = jnp.exp2((b - _anchor(b, lvl)) * sg_ref[lvl])
                ys.append((x * jnp.where(right[lvl], q_s, k_s)).astype(bf16))
            for lvl in range(FINE_LEVELS, N_LEVELS):
                m = 1 << lvl
                parts = []
                for r0 in range(0, CHUNK, SUBLANES):
                    a0 = (r0 // (2 * m)) * (2 * m) + m - 1
                    anc = jnp.broadcast_to(b[a0:a0 + 1, :], (SUBLANES, LANES))
                    bj = b[r0:r0 + SUBLANES, :]
                    if r0 % (2 * m) >= m:
                        parts.append(jnp.exp2(bj - anc) * q_s[r0:r0 + SUBLANES, :])
                    else:
                        parts.append(jnp.exp2(anc - bj) * k_s[r0:r0 + SUBLANES, :])
                ys.append(jnp.concatenate(parts, axis=0).astype(bf16))
            blast = jnp.broadcast_to(b[CHUNK - 1:CHUNK, :], b.shape)
            qb = (q_s * jnp.exp2(b)).astype(bf16)
            kst = (k_s * jnp.exp2(blast - b)).astype(bf16)
            decs.append(jnp.exp2(b[CHUNK - 1:CHUNK, :]))
            qk_prod = q_s * k_s
            for hh in range(hps):
                mul = head_mask(hh)
                rds.append(jnp.sum(mul(qk_prod), axis=-1, keepdims=True))
                raw.append([_dot_nt(mul(ys[lvl]), ys[lvl]) for lvl in range(N_LEVELS)])
                qbs.append(mul(qb))
                ksts.append(mul(kst))
        return raw, rds, tuple(qbs), tuple(ksts), tuple(decs)

    def combine(raw, rds):
        ps = []
        for h in range(n_heads):
            p = jnp.where(eye, rds[h], 0.0)
            for lvl in range(N_LEVELS):
                p = p + pm_ref[lvl] * raw[h][lvl]
            ps.append(p.astype(bf16))
        return tuple(ps)

    def apply_dots(c, sc):
        ps, qbs, ksts, decs = sc
        outs = []
        for h in range(n_heads):
            vb = v_ref[rows_of(c), h * LANES:(h + 1) * LANES].astype(bf16)
            st = st_ref[h]
            outs.append(_dot(ps[h], vb) + _dot_nt(qbs[h], st.astype(bf16)))
            st_ref[h] = st * decs[h // hps] + _dot_tn(vb, ksts[h])
        return outs

    def finish(c, outs):
        for h in range(n_heads):
            cs = slice(h * LANES, (h + 1) * LANES)
            o = outs[h]
            o = o * lax.rsqrt(jnp.mean(o * o, axis=-1, keepdims=True) + RMS_EPS) * ng_ref[...]
            y_ref[rows_of(c), cs] = o * _silu(gate_ref[rows_of(c), cs])

    def first():
        raw, rds, qbs, ksts, decs = score_dots(cumsums(_gla_prologue(kind, pro, rows_of(0))))
        return combine(raw, rds), qbs, ksts, decs

    def body(c, sc):
        pre = cumsums(_gla_prologue(kind, pro, rows_of(c + 1)))
        outs = apply_dots(c, sc)
        raw, rds, qbs, ksts, decs = score_dots(pre)
        finish(c, outs)
        return combine(raw, rds), qbs, ksts, decs

    n_chunks = lt // CHUNK
    last = lax.fori_loop(0, n_chunks - 1, body, first(), unroll=min(GLA_UNROLL, n_chunks - 1))
    finish(n_chunks - 1, apply_dots(n_chunks - 1, last))

    @pl.when(t_idx == n_tiles - 1)
    def _():
        for h in range(n_heads):
            off = (h % hps) * dk
            s_ref[h] = st_ref[h].T[off:off + dk, :]


def _gla_prompt(kind, z, col0, extra, norm_g, batch, seq, lt, layer):
    dk = GLA_DK if kind == "gla" else HG_D
    n_heads = GLA_H if kind == "gla" else HG_H
    tiles = seq // lt
    mall, pmask, rmask = _level_tables()
    blk = lambda width, cb: pl.BlockSpec((lt, width), lambda b, t: (b * tiles + t, cb))
    if kind == "gla":
        wlr, blr = extra
        ins = [z, z, z, z, wlr, blr]
        specs = [blk(2 * GLA_KW, 0), blk(GLA_VW, 1), blk(GLA_VW, 2), blk(LANES, col0),
                 _const_spec(wlr.shape), _const_spec(blr.shape)]
    else:
        (lg,) = extra
        ins = [z, z, z, z, lg]
        specs = [blk(HG_W, col0), blk(HG_W, col0 + 1), blk(HG_W, col0 + 2), blk(HG_W, col0 + 3), _const_spec(lg.shape)]
    ins += [norm_g, mall, pmask, rmask]
    specs += [_const_spec(norm_g.shape), _const_spec(mall.shape), _const_spec(pmask.shape), _const_spec(rmask.shape)]
    return pl.pallas_call(
        functools.partial(_gla_chunk_kernel, kind, dk, n_heads, layer),
        out_shape=(jax.ShapeDtypeStruct((batch * seq, HALF), f32),
                   jax.ShapeDtypeStruct((batch, n_heads, dk, LANES), f32)),
        grid=(batch, tiles),
        in_specs=specs,
        out_specs=(pl.BlockSpec((lt, HALF), lambda b, t: (b * tiles + t, 0)),
                   pl.BlockSpec((None, n_heads, dk, LANES), lambda b, t: (b, 0, 0, 0))),
        scratch_shapes=[pltpu.VMEM((n_heads, LANES, LANES), f32)]
        + ([pltpu.VMEM((lt, GLA_KW), f32)] if kind == "gla" else []),
        compiler_params=_params("parallel", "arbitrary"),
        name=kind + "_prompt",
    )(*ins)


def _lru_gates(xc, wr_ref, wi_ref, br_ref, bi_ref, lam_ref):
    xb = xc.astype(bf16)
    r = _sigmoid(_dot(xb, wr_ref[...]) + br_ref[...])
    i = _sigmoid(_dot(xb, wi_ref[...]) + bi_ref[...])
    log_a = (-RG_C) * r * _softplus(-lam_ref[...])
    t = jnp.tanh(log_a)
    one_minus_a2 = (-2.0 * t) / (1.0 - t)
    return jnp.exp(log_a), jnp.sqrt(one_minus_a2) * (i * xc)


def _even_front_kernel(xin_ref, sc_ref, sh_ref, w_ref, cw_ref, cb_ref, wr_ref, wi_ref, br_ref, bi_ref, lam_ref,
                       z_ref, y_ref, conv_ref, h_ref, xbuf, a_scr, h_scr, car):
    t_idx = pl.program_id(1)
    lt = xin_ref.shape[0]
    nz = z_ref.shape[1]

    @pl.when(t_idx == 0)
    def _():
        xbuf[0:SUBLANES, :] = jnp.zeros((SUBLANES, LRU_W), f32)
        car[...] = jnp.zeros(car.shape, f32)

    hin = (xin_ref[...] * (1.0 + sc_ref[...]) + sh_ref[...]).astype(bf16)
    x = _dot(hin, w_ref[:, nz:nz + LRU_W])
    xbuf[SUBLANES:SUBLANES + lt, :] = x
    tiles = [(lo, min(lo + MXU_WIDTH, nz)) for lo in range(0, nz, MXU_WIDTH)]
    tiles += [(nz + LRU_W + lo, nz + LRU_W + lo + MXU_WIDTH) for lo in range(0, LRU_W, MXU_WIDTH)]

    def col_tile(lo, hi):
        zt = _dot(hin, w_ref[:, lo:hi])
        if lo < nz:
            z_ref[:, lo:hi] = zt
        else:
            y_ref[:, lo - nz - LRU_W:hi - nz - LRU_W] = _gelu(zt)

    blocks = [(r0, j) for r0 in range(0, lt, LRU_ROWS) for j in range(LRU_W // LANES)]
    sub = lax.broadcasted_iota(jnp.int32, (SUBLANES, LANES), 0)
    grp = (LRU_ROWS // SUBLANES, SUBLANES, LANES)
    done = 0
    for n, (r0, j) in enumerate(blocks):
        while done < len(tiles) and done * len(blocks) <= n * len(tiles):
            col_tile(*tiles[done])
            done += 1
        cs = slice(j * LANES, (j + 1) * LANES)
        base = SUBLANES + r0
        xc = cb_ref[:, cs] + cw_ref[CONV_W - 1:CONV_W, cs] * xbuf[base:base + LRU_ROWS, cs]
        for jj in range(CONV_W - 1):
            off = base - (CONV_W - 1) + jj
            xc = xc + cw_ref[jj:jj + 1, cs] * xbuf[off:off + LRU_ROWS, cs]
        a, hh = _lru_gates(xc, wr_ref.at[cs, cs], wi_ref.at[cs, cs], br_ref.at[:, cs], bi_ref.at[:, cs],
                           lam_ref.at[:, cs])
        a, hh = a.reshape(grp), hh.reshape(grp)
        for d in (1, 2, 4):
            m = sub >= d
            hh = hh + jnp.where(m, a, 0.0) * pltpu.roll(hh, d, 1)
            a = a * jnp.where(m, pltpu.roll(a, d, 1), 1.0)
        a_scr[r0:r0 + LRU_ROWS, cs] = a.reshape(LRU_ROWS, LANES)
        h_scr[r0:r0 + LRU_ROWS, cs] = hh.reshape(LRU_ROWS, LANES)
    while done < len(tiles):
        col_tile(*tiles[done])
        done += 1
    xbuf[0:SUBLANES, :] = xbuf[lt:lt + SUBLANES, :]

    def group(g, c):
        rows = pl.ds(pl.multiple_of(g * SUBLANES, SUBLANES), SUBLANES)
        hg = h_scr[rows, :] + a_scr[rows, :] * c
        h_scr[rows, :] = hg
        return jnp.broadcast_to(hg[SUBLANES - 1:SUBLANES, :], (SUBLANES, LRU_W))

    c = lax.fori_loop(0, lt // SUBLANES, group, car[...])
    car[...] = c
    y_ref[...] = h_scr[...] * y_ref[...]
    conv_ref[...] = x[lt - (CONV_W - 1):lt, :]
    h_ref[...] = c[0:1, :]


def _even_front(x, mod, w, p, batch, seq, lt):
    tiles = seq // lt
    row = lambda wd: pl.BlockSpec((lt, wd), lambda b, t: (b * tiles + t, 0))
    modv = lambda j: pl.BlockSpec((None, 1, D_MODEL), lambda b, t: (b, 0, j))
    consts = [p["conv_w"], p["conv_b"], p["w_r"], p["w_i"], p["b_r"], p["b_i"], p["lam"]]
    return pl.pallas_call(
        _even_front_kernel,
        out_shape=(jax.ShapeDtypeStruct((batch * seq, EVEN_Z), f32),
                   jax.ShapeDtypeStruct((batch * seq, LRU_W), f32),
                   jax.ShapeDtypeStruct((batch, CONV_W - 1, LRU_W), f32),
                   jax.ShapeDtypeStruct((batch, 1, LRU_W), f32)),
        grid=(batch, tiles),
        in_specs=[row(D_MODEL), modv(1), modv(0), _const_spec(w.shape)] + [_const_spec(c.shape) for c in consts],
        out_specs=(row(EVEN_Z), row(LRU_W),
                   pl.BlockSpec((None, CONV_W - 1, LRU_W), lambda b, t: (b, 0, 0)),
                   pl.BlockSpec((None, 1, LRU_W), lambda b, t: (b, 0, 0))),
        scratch_shapes=[pltpu.VMEM((lt + SUBLANES, LRU_W), f32), pltpu.VMEM((lt, LRU_W), f32),
                        pltpu.VMEM((lt, LRU_W), f32), pltpu.VMEM((SUBLANES, LRU_W), f32)],
        compiler_params=_params("parallel", "arbitrary"),
        name="even_front",
    )(x, mod, mod, w, *consts)


S5_R = 4
S5_TAB = 8


S5_GPB = LANES // S5_GH
S5_NBLK = S5_G // S5_GPB


def _s5_zoh(lr_, li_, ldt):
    dt = jnp.exp(ldt)
    mag = jnp.exp(lr_ * dt)
    ar, ai = mag * jnp.cos(li_ * dt), mag * jnp.sin(li_ * dt)
    den = lr_ * lr_ + li_ * li_
    nr, ni = ar - 1.0, ai
    return ar, ai, (nr * lr_ + ni * li_) / den, (ni * lr_ - nr * li_) / den


def _s5_param_kernel(lr_ref, li_ref, ldt_ref, lrf_ref, lif_ref, ldtf_ref, btr_ref, bti_ref, cr_ref, ci_ref,
                     a1r_ref, a1i_ref, pcr_ref, pci_ref, wx_ref, wct_ref, ct_ref, bd_s):
    ar, ai, fr, fi = _s5_zoh(lr_ref[...], li_ref[...], ldt_ref[...])
    bbr = fr[:, None, :] * btr_ref[...] - fi[:, None, :] * bti_ref[...]
    bbi = fr[:, None, :] * bti_ref[...] + fi[:, None, :] * btr_ref[...]
    cr, ci = cr_ref[...], ci_ref[...]

    def put(dst_ref, row0, re, im):
        for j in range(S5_NBLK):
            bd_s[...] = jnp.zeros(bd_s.shape, f32)
            for g in range(S5_GPB):
                rows = slice(g * S5_GH, (g + 1) * S5_GH)
                bd_s[rows, g * S5_P:(g + 1) * S5_P] = re[j * S5_GPB + g]
                bd_s[rows, S5_BLK + g * S5_P:S5_BLK + (g + 1) * S5_P] = im[j * S5_GPB + g]
            dst_ref[j, row0:row0 + LANES, :] = bd_s[...].astype(bf16)

    put(ct_ref, 0, cr, -ci)
    pr, pi = jnp.ones_like(ar), jnp.zeros_like(ai)
    for n in range(S5_R + 1):
        if n < S5_R:
            put(wx_ref, (S5_R - 1 - n) * LANES,
                pr[:, None, :] * bbr - pi[:, None, :] * bbi, pr[:, None, :] * bbi + pi[:, None, :] * bbr)
        if n >= 1:
            put(wct_ref, (n - 1) * LANES,
                cr * pr[:, None, :] - ci * pi[:, None, :], -(cr * pi[:, None, :] + ci * pr[:, None, :]))
        pr, pi = pr * ar - pi * ai, pr * ai + pi * ar

    ar, ai, _, _ = _s5_zoh(lrf_ref[...], lif_ref[...], ldtf_ref[...])
    a1r_ref[...], a1i_ref[...] = ar, ai
    pr, pi = ar, ai
    for n in range(1, S5_R * S5_TAB + 1):
        if n % S5_R == 0:
            pcr_ref[n // S5_R - 1:n // S5_R, :] = pr
            pci_ref[n // S5_R - 1:n // S5_R, :] = pi
        pr, pi = pr * ar - pi * ai, pr * ai + pi * ar


def _toeplitz_kernel(wx_ref, ct_ref, tp_ref):
    lag = [_dot_nt(wx_ref[(S5_R - 1 - d) * LANES:(S5_R - d) * LANES, :], ct_ref[...]).astype(bf16)
           for d in range(S5_R)]
    tp_ref[...] = jnp.zeros(tp_ref.shape, bf16)
    for s in range(S5_R):
        for r in range(s, S5_R):
            tp_ref[s * LANES:(s + 1) * LANES, r * LANES:(r + 1) * LANES] = lag[r - s]


def _s5_params(lam_re, lam_im, log_dt, b_re, b_im, c_re, c_im):
    btr = jnp.swapaxes(b_re, 1, 2)
    bti = jnp.swapaxes(b_im, 1, 2)
    shp = jax.ShapeDtypeStruct
    flat = lambda a: a.reshape(1, S5_N)
    wide = (S5_NBLK, S5_R * LANES, 2 * S5_BLK)
    a1r, a1i, pcr, pci, w_x, w_ct, c_t = pl.pallas_call(
        _s5_param_kernel,
        out_shape=(shp((1, S5_N), f32), shp((1, S5_N), f32), shp((S5_TAB, S5_N), f32), shp((S5_TAB, S5_N), f32),
                   shp(wide, bf16), shp(wide, bf16), shp((S5_NBLK, LANES, 2 * S5_BLK), bf16)),
        scratch_shapes=[pltpu.VMEM((LANES, 2 * S5_BLK), f32)],
        name="s5_params",
    )(lam_re, lam_im, log_dt.reshape(S5_G, 1), flat(lam_re), flat(lam_im), flat(jnp.repeat(log_dt, S5_P)),
      btr, bti, c_re, c_im)
    toep = pl.pallas_call(
        _toeplitz_kernel,
        out_shape=shp((S5_NBLK, S5_R * LANES, S5_R * LANES), bf16),
        grid=(S5_NBLK,),
        in_specs=[pl.BlockSpec((None,) + wide[1:], lambda j: (j, 0, 0)),
                  pl.BlockSpec((None, LANES, 2 * S5_BLK), lambda j: (j, 0, 0))],
        out_specs=pl.BlockSpec((None, S5_R * LANES, S5_R * LANES), lambda j: (j, 0, 0)),
        compiler_params=_params("parallel"),
        name="s5_toeplitz",
    )(w_x, c_t)
    return dict(a1r=a1r, a1i=a1i, pcr=pcr, pci=pci, c_t=c_t, w_x=w_x, toep=toep, w_ct=w_ct)


S5_BLK = S5_N // (S5_G // (LANES // S5_GH))


def _s5_input(u, wx_ref, j):
    b_bar = wx_ref[j, (S5_R - 1) * LANES:S5_R * LANES, :]
    bu = _dot(u[:, j * LANES:(j + 1) * LANES].astype(bf16), b_bar)
    return bu[:, 0:S5_BLK], bu[:, S5_BLK:2 * S5_BLK]


def _s5_output(u, hr, hi, ct_ref, d_ref, wg_ref, bg_ref):
    ys = []
    for j in range(S5_N // S5_BLK):
        cs = slice(j * S5_BLK, (j + 1) * S5_BLK)
        hcat = jnp.concatenate([hr[:, cs], hi[:, cs]], axis=1).astype(bf16)
        ys.append(_dot_nt(hcat, ct_ref[j]))
    y = _gelu(jnp.concatenate(ys, axis=1) + d_ref[...] * u)
    return y * _sigmoid(_dot(y.astype(bf16), wg_ref[...]) + bg_ref[...])


def _s5_prompt_kernel(*refs):
    nblk = S5_N // S5_BLK
    u_refs = refs[:nblk]
    wx_ref, tp_ref, wct_ref, pcr_ref, pci_ref, d_ref, wg_ref, bg_ref = refs[nblk:nblk + 8]
    y_refs = refs[nblk + 8:2 * nblk + 8]
    sr_ref, si_ref, er_s, ei_s, y_s = refs[2 * nblk + 8:]
    t_idx = pl.program_id(1)
    cr = u_refs[0].shape[0] // S5_R
    gw = S5_R * LANES

    @pl.when(t_idx == 0)
    def _():
        er_s[0:SUBLANES, :] = jnp.zeros((SUBLANES, S5_N), f32)
        ei_s[0:SUBLANES, :] = jnp.zeros((SUBLANES, S5_N), f32)

    def token(s):
        return pl.ds(s, cr, stride=S5_R)

    sub = lax.broadcasted_iota(jnp.int32, (SUBLANES, S5_BLK), 0)
    for j in range(nblk):
        cs = slice(j * S5_BLK, (j + 1) * S5_BLK)
        uj = jnp.concatenate([u_refs[j][token(s), :] for s in range(S5_R)], axis=1).astype(bf16)
        x = _dot(uj, wx_ref[j])
        y_s[:, j * gw:(j + 1) * gw] = _dot(uj, tp_ref[j])
        hr = x[:, 0:S5_BLK].reshape(cr // SUBLANES, SUBLANES, S5_BLK)
        hi = x[:, S5_BLK:2 * S5_BLK].reshape(cr // SUBLANES, SUBLANES, S5_BLK)
        for d in (1, 2, 4):
            pr = jnp.where(sub >= d, pcr_ref[d - 1:d, cs], 0.0)
            pi = jnp.where(sub >= d, pci_ref[d - 1:d, cs], 0.0)
            sr, si = pltpu.roll(hr, d, 1), pltpu.roll(hi, d, 1)
            hr, hi = hr + (pr * sr - pi * si), hi + (pr * si + pi * sr)
        er_s[SUBLANES:SUBLANES + cr, cs] = hr.reshape(cr, S5_BLK)
        ei_s[SUBLANES:SUBLANES + cr, cs] = hi.reshape(cr, S5_BLK)

    tr, ti = pcr_ref[...], pci_ref[...]
    last = pl.ds(SUBLANES - 1, 1)
    c0 = (jnp.broadcast_to(er_s[last, :], (SUBLANES, S5_N)), jnp.broadcast_to(ei_s[last, :], (SUBLANES, S5_N)))

    def group(g, c):
        c_r, c_i = c
        rows = pl.ds(pl.multiple_of((g + 1) * SUBLANES, SUBLANES), SUBLANES)
        g_r = er_s[rows, :] + tr * c_r - ti * c_i
        g_i = ei_s[rows, :] + tr * c_i + ti * c_r
        er_s[rows, :] = g_r
        ei_s[rows, :] = g_i
        return (jnp.broadcast_to(g_r[SUBLANES - 1:SUBLANES, :], (SUBLANES, S5_N)),
                jnp.broadcast_to(g_i[SUBLANES - 1:SUBLANES, :], (SUBLANES, S5_N)))

    c_r, c_i = lax.fori_loop(0, cr // SUBLANES, group, c0)

    for j in range(nblk):
        cs = slice(j * S5_BLK, (j + 1) * S5_BLK)
        prev = pl.ds(SUBLANES - 1, cr)
        sj = jnp.concatenate([er_s[prev, cs], ei_s[prev, cs]], axis=1).astype(bf16)
        y_s[:, j * gw:(j + 1) * gw] += _dot_nt(sj, wct_ref[j])
    er_s[last, :] = c_r[0:1, :]
    ei_s[last, :] = c_i[0:1, :]
    sr_ref[...] = c_r[0:1, :]
    si_ref[...] = c_i[0:1, :]

    for r in range(S5_R):
        y = jnp.concatenate([y_s[:, j * gw + r * LANES:j * gw + (r + 1) * LANES] for j in range(nblk)], axis=1)
        u_r = jnp.concatenate([u_refs[j][token(r), :] for j in range(nblk)], axis=1)
        y = _gelu(y + d_ref[...] * u_r)
        y = y * _sigmoid(_dot(y.astype(bf16), wg_ref[...]) + bg_ref[...])
        for j in range(nblk):
            y_refs[j][token(r), :] = y[:, j * LANES:(j + 1) * LANES]


def _s5_prompt(us, tabs, p, batch, seq, cr):
    nblk = len(us)
    tiles = seq // (S5_R * cr)
    consts = [tabs["w_x"], tabs["toep"], tabs["w_ct"], tabs["pcr"], tabs["pci"], p["d"], p["w_glu"], p["b_glu"]]
    blk = pl.BlockSpec((S5_R * cr, LANES), lambda b, t: (b * tiles + t, 0))
    state = pl.BlockSpec((None, 1, S5_N), lambda b, t: (b, 0, 0))
    outs = pl.pallas_call(
        _s5_prompt_kernel,
        out_shape=tuple(jax.ShapeDtypeStruct((batch * seq, LANES), f32) for _ in range(nblk))
        + (jax.ShapeDtypeStruct((batch, 1, S5_N), f32), jax.ShapeDtypeStruct((batch, 1, S5_N), f32)),
        grid=(batch, tiles),
        in_specs=[blk] * nblk + [_const_spec(c.shape) for c in consts],
        out_specs=(blk,) * nblk + (state, state),
        scratch_shapes=[pltpu.VMEM((cr + SUBLANES, S5_N), f32), pltpu.VMEM((cr + SUBLANES, S5_N), f32),
                        pltpu.VMEM((cr, S5_R * S5_W), f32)],
        compiler_params=_params("parallel", "arbitrary"),
        name="s5_prompt",
    )(*us, *consts)
    return list(outs[:nblk]), outs[nblk], outs[nblk + 1]


def _gla_step_kernel(kind, dk, layer, *refs):
    if kind == "gla":
        q_ref, k_ref, v_ref, gate_ref, lr_ref, wlr_ref, blr_ref, ng_ref, s_ref, y_ref, so_ref, o_scr = refs
        sl = pl.program_id(0)
        q = q_ref[...] * (GLA_DK ** -0.5)
        k = k_ref[...]
        pre_all = _dot(lr_ref[...].astype(bf16), wlr_ref[...]) + blr_ref[...]
        pre = jnp.where(sl == 0, pre_all[:, 0:LANES], pre_all[:, LANES:2 * LANES])
        a = jnp.exp(_log_sigmoid(pre) * (1.0 / GLA_TAU))
    else:
        q_ref, f_ref, v_ref, gate_ref, lg_ref, ng_ref, s_ref, y_ref, so_ref, o_scr = refs
        sl = pl.program_id(0)
        lb_all = _hgrn_lower_bound(lg_ref, layer)
        lb = lb_all[:, 0:LANES]
        for j in range(1, HG_H):
            lb = jnp.where(sl == j, lb_all[:, j * LANES:(j + 1) * LANES], lb)
        a = lb + (1.0 - lb) * _sigmoid(f_ref[...])
        q = _silu(q_ref[...])
        k = 1.0 - a
    hps = LANES // dk
    nb = q.shape[0]
    kt, at = k.T, a.T
    lane = lax.broadcasted_iota(jnp.int32, (1, LANES), 1)
    v = v_ref[...]
    q_heads = [jnp.where((lane >= hh * dk) & (lane < (hh + 1) * dk), q, 0.0) for hh in range(hps)]
    for b in range(nb):
        new = []
        for hh in range(hps):
            r0 = hh * dk
            col = lambda x: jnp.broadcast_to(x[r0:r0 + dk, b:b + 1], (dk, LANES))
            a_col = col(at)
            k_col = col(kt) if kind == "gla" else 1.0 - a_col
            s_new = a_col * s_ref[b, hh] + k_col * v[b:b + 1, hh * LANES:(hh + 1) * LANES]
            so_ref[b, hh] = s_new
            new.append(s_new)
        b0 = b // SUBLANES * SUBLANES
        lhs = jnp.concatenate([qh[b0:b0 + SUBLANES, :] for qh in q_heads], axis=0).astype(bf16)
        res = _dot(lhs, jnp.concatenate(new, axis=0).astype(bf16))
        for hh in range(hps):
            r = hh * SUBLANES + b - b0
            o_scr[b:b + 1, hh * LANES:(hh + 1) * LANES] = res[r:r + 1, :]
    for hh in range(hps):
        cs = slice(hh * LANES, (hh + 1) * LANES)
        o = o_scr[:, cs]
        o = o * lax.rsqrt(jnp.mean(o * o, axis=-1, keepdims=True) + RMS_EPS) * ng_ref[...]
        y_ref[:, cs] = o * _silu(gate_ref[:, cs])


def _gla_step(kind, z, extra, norm_g, state, layer):
    nb = z.shape[0]
    dk = GLA_DK if kind == "gla" else HG_D
    n_heads = GLA_H if kind == "gla" else HG_H
    hps = LANES // dk
    n_slabs = n_heads // hps
    vw = hps * LANES
    zb = lambda width, cb0: pl.BlockSpec((nb, width), lambda s: (0, cb0 + s))
    if kind == "gla":
        wlr, blr = extra
        ins = [z, z, z, z, z, wlr, blr]
        specs = [zb(LANES, 0), zb(LANES, GLA_KW // LANES), zb(vw, GLA_VW // vw * 1), zb(vw, GLA_VW // vw * 2),
                 pl.BlockSpec((nb, LANES), lambda s: (0, GLA_LR_BLOCK)),
                 _const_spec(wlr.shape), _const_spec(blr.shape)]
    else:
        (lg,) = extra
        ins = [z, z, z, z, lg]
        specs = [zb(LANES, 0), zb(LANES, HG_W // LANES), zb(vw, HG_W // vw * 2),
                 zb(vw, HG_W // vw * 3), _const_spec(lg.shape)]
    ins += [norm_g, state]
    specs += [_const_spec(norm_g.shape), pl.BlockSpec((nb, hps, dk, LANES), lambda s: (0, s, 0, 0))]
    return pl.pallas_call(
        functools.partial(_gla_step_kernel, kind, dk, layer),
        out_shape=(jax.ShapeDtypeStruct((nb, HALF), f32), jax.ShapeDtypeStruct(state.shape, f32)),
        grid=(n_slabs,),
        in_specs=specs,
        out_specs=(pl.BlockSpec((nb, vw), lambda s: (0, s)),
                   pl.BlockSpec((nb, hps, dk, LANES), lambda s: (0, s, 0, 0))),
        scratch_shapes=[pltpu.VMEM((nb, vw), f32)],
        compiler_params=_params("parallel"),
        name=kind + "_step",
    )(*ins)


def _lru_step_kernel(x_ref, gr_ref, c0_ref, c1_ref, c2_ref, h0_ref, cw_ref, cb_ref, wr_ref, wi_ref,
                     br_ref, bi_ref, lam_ref, y_ref, h_ref):
    xc = (cb_ref[...] + cw_ref[0:1, :] * c0_ref[...] + cw_ref[1:2, :] * c1_ref[...]
          + cw_ref[2:3, :] * c2_ref[...] + cw_ref[3:4, :] * x_ref[...])
    a, u = _lru_gates(xc, wr_ref, wi_ref, br_ref, bi_ref, lam_ref)
    h = a * h0_ref[...] + u
    h_ref[...] = h
    y_ref[...] = h * _gelu(gr_ref[...])


def _lru_step(xr, gr, conv_rows, h0, p):
    nb = xr.shape[0]
    full = lambda: pl.BlockSpec((nb, LRU_W), lambda i: (0, 0))
    consts = [p["conv_w"], p["conv_b"], p["w_r"], p["w_i"], p["b_r"], p["b_i"], p["lam"]]
    return pl.pallas_call(
        _lru_step_kernel,
        out_shape=(jax.ShapeDtypeStruct((nb, LRU_W), f32), jax.ShapeDtypeStruct((nb, LRU_W), f32)),
        grid=(1,),
        in_specs=[full() for _ in range(CONV_W + 2)] + [_const_spec(c.shape) for c in consts],
        out_specs=(full(), full()),
        compiler_params=_params("arbitrary"),
        name="lru_step",
    )(xr, gr, *conv_rows, h0, *consts)


def _s5_step_kernel(u_ref, s0r_ref, s0i_ref, a1r_ref, a1i_ref, wx_ref, ct_ref, d_ref, wg_ref, bg_ref,
                    y_ref, sr_ref, si_ref):
    u = u_ref[...]
    for j in range(S5_N // S5_BLK):
        cs = slice(j * S5_BLK, (j + 1) * S5_BLK)
        bur, bui = _s5_input(u, wx_ref, j)
        ar, ai = a1r_ref[:, cs], a1i_ref[:, cs]
        s0r, s0i = s0r_ref[:, cs], s0i_ref[:, cs]
        sr_ref[:, cs] = bur + ar * s0r - ai * s0i
        si_ref[:, cs] = bui + ar * s0i + ai * s0r
    y_ref[...] = _s5_output(u, sr_ref[...], si_ref[...], ct_ref, d_ref, wg_ref, bg_ref)


def _s5_step(u, s0r, s0i, tabs, p):
    nb = u.shape[0]
    consts = [tabs["a1r"], tabs["a1i"], tabs["w_x"], tabs["c_t"], p["d"], p["w_glu"], p["b_glu"]]
    st = lambda: pl.BlockSpec((nb, S5_N), lambda i: (0, 0))
    return pl.pallas_call(
        _s5_step_kernel,
        out_shape=(jax.ShapeDtypeStruct((nb, S5_W), f32), jax.ShapeDtypeStruct((nb, S5_N), f32),
                   jax.ShapeDtypeStruct((nb, S5_N), f32)),
        grid=(1,),
        in_specs=[pl.BlockSpec((nb, S5_W), lambda i: (0, 0)), st(), st()] + [_const_spec(c.shape) for c in consts],
        out_specs=(pl.BlockSpec((nb, S5_W), lambda i: (0, 0)), st(), st()),
        compiler_params=_params("arbitrary"),
        name="s5_step",
    )(u, s0r, s0i, *consts)


def _block_diag(w):
    h, n, _ = w.shape
    return jnp.einsum('hij,hk->hikj', w, jnp.eye(h, dtype=w.dtype)).reshape(h * n, h * n)


def kernel(x_prompt, x_sample, c_prompt, c_sample, state_gla, state_rglru_conv, state_rglru_h, state_s5_re, state_s5_im, state_hgrn, ev_w_in, ev_gla_w_lr, ev_gla_b_lr, ev_gla_norm_g, ev_conv_w, ev_conv_b, ev_lru_w_r, ev_lru_b_r, ev_lru_w_i, ev_lru_b_i, ev_lru_lam, ev_w_out, od_w_in, od_s5_lam_re, od_s5_lam_im, od_s5_log_dt, od_s5_b_re, od_s5_b_im, od_s5_c_re, od_s5_c_im, od_s5_d, od_s5_w_glu, od_s5_b_glu, hg_lb_logits, od_hg_norm_g, od_w_out, w_ada, b_ada, ln_g, ln_b, ffn_w_in, ffn_w_out):
    bp, seq = x_prompt.shape[0], x_prompt.shape[1]
    bs = x_sample.shape[0]
    assert x_sample.shape[1] == 1 and seq % CHUNK == 0
    tp = bp * seq
    tm = min(512, seq)
    tm_ffn = min(2 * FFN_SUB_ROWS, seq)
    lt = min(512, seq)
    lt_gla = min(2048, seq)
    cr_s5 = min(256, seq // S5_R)
    row2 = lambda a: a.reshape(1, -1)

    mod = _modulation(jnp.concatenate([c_prompt, c_sample], axis=0).astype(f32), w_ada, b_ada)
    mod_p = mod[:, :bp].reshape(DEPTH, bp, 1, 6 * D_MODEL)
    mod_s = mod[:, bp:].reshape(DEPTH, 1, bs, 6 * D_MODEL)

    xp = x_prompt.astype(f32).reshape(tp, D_MODEL)
    xs = x_sample.astype(f32).reshape(bs, D_MODEL)
    outs_p = {k: [] for k in ("gla", "conv", "lru", "re", "im", "hg")}
    outs_s = {k: [] for k in ("gla", "conv", "lru", "re", "im", "hg")}

    w_f_in = ffn_w_in.astype(bf16)
    w_f_out = ffn_w_out.astype(bf16)
    for l in range(DEPTH):
        if l % 2 == 0:
            e = l // 2
            w = ev_w_in[e]
            lr0 = 2 * GLA_KW + 2 * GLA_VW
            w_in = jnp.concatenate(
                [w[:, :lr0 + GLA_LR], jnp.zeros((D_MODEL, LANES - GLA_LR), w.dtype), w[:, lr0 + GLA_LR:]],
                axis=1).astype(bf16)
            w_out = ev_w_out[e].astype(bf16)
            wlr = jnp.concatenate([ev_gla_w_lr[e], jnp.zeros((LANES - GLA_LR, GLA_KW), f32)], axis=0).astype(bf16)
            gla_extra = (wlr, row2(ev_gla_b_lr[e]))
            ng = row2(ev_gla_norm_g[e])
            lru_p = dict(conv_w=ev_conv_w[e], conv_b=row2(ev_conv_b[e]),
                         w_r=_block_diag(ev_lru_w_r[e]).astype(bf16), w_i=_block_diag(ev_lru_w_i[e]).astype(bf16),
                         b_r=row2(ev_lru_b_r[e]), b_i=row2(ev_lru_b_i[e]), lam=row2(ev_lru_lam[e]))
            z, yb, s_conv, s_h = _even_front(xp, mod_p[l], w_in, lru_p, bp, seq, lt)
            ya, s_gla = _gla_prompt("gla", z, GLA_LR_BLOCK, gla_extra, ng, bp, seq, lt_gla, l)
            xp = _outffn(xp, [ya, yb], mod_p[l], w_out, ln_g[l], ln_b[l], w_f_in, w_f_out, l, tm_ffn, seq // tm_ffn)
            outs_p["gla"].append(s_gla)
            outs_p["conv"].append(s_conv)
            outs_p["lru"].append(s_h.reshape(bp, LRU_W))
            z, xr, gr = _inproj(xs, mod_s[l], w_in, bs, 1, (EVEN_Z, LRU_W, LRU_W))
            ya, s_gla = _gla_step("gla", z, gla_extra, ng, state_gla[e], l)
            cs = state_rglru_conv[e].astype(f32)
            yb, s_h = _lru_step(xr, gr, [cs[:, j] for j in range(CONV_W - 1)], state_rglru_h[e].astype(f32), lru_p)
            xs = _outffn(xs, [ya, yb], mod_s[l], w_out, ln_g[l], ln_b[l], w_f_in, w_f_out, l, bs, 1)
            outs_s["gla"].append(s_gla)
            outs_s["conv"].append(jnp.stack([cs[:, 1], cs[:, 2], xr], axis=1))
            outs_s["lru"].append(s_h)
        else:
            o = l // 2
            w_in = od_w_in[o].astype(bf16)
            w_out = od_w_out[o].astype(bf16)
            tabs = _s5_params(od_s5_lam_re[o], od_s5_lam_im[o], od_s5_log_dt[o], od_s5_b_re[o], od_s5_b_im[o],
                              od_s5_c_re[o], od_s5_c_im[o])
            s5_p = dict(d=row2(od_s5_d[o]), w_glu=od_s5_w_glu[o].astype(bf16), b_glu=row2(od_s5_b_glu[o]))
            hg_extra = (hg_lb_logits.astype(f32),)
            ng = row2(od_hg_norm_g[o])
            *us, z = _inproj(xp, mod_p[l], w_in, tm, seq // tm, (LANES,) * (S5_W // LANES) + (ODD_N - S5_W,))
            yas, s_re, s_im = _s5_prompt(us, tabs, s5_p, bp, seq, cr_s5)
            yb, s_hg = _gla_prompt("hgrn", z, 0, hg_extra, ng, bp, seq, lt_gla, l)
            xp = _outffn(xp, yas + [yb], mod_p[l], w_out, ln_g[l], ln_b[l], w_f_in, w_f_out, l, tm_ffn, seq // tm_ffn)
            outs_p["re"].append(s_re.reshape(bp, S5_G, S5_P))
            outs_p["im"].append(s_im.reshape(bp, S5_G, S5_P))
            outs_p["hg"].append(s_hg)
            u, z = _inproj(xs, mod_s[l], w_in, bs, 1, (S5_W, ODD_N - S5_W))
            ya, s_re, s_im = _s5_step(u, state_s5_re[o].astype(f32).reshape(bs, S5_N),
                                      state_s5_im[o].astype(f32).reshape(bs, S5_N), tabs, s5_p)
            yb, s_hg = _gla_step("hgrn", z, hg_extra, ng, state_hgrn[o], l)
            xs = _outffn(xs, [ya, yb], mod_s[l], w_out, ln_g[l], ln_b[l], w_f_in, w_f_out, l, bs, 1)
            outs_s["re"].append(s_re.reshape(bs, S5_G, S5_P))
            outs_s["im"].append(s_im.reshape(bs, S5_G, S5_P))
            outs_s["hg"].append(s_hg)

    st = lambda d, k: d[k][0][None] if len(d[k]) == 1 else jnp.stack(d[k])
    return (xp.reshape(bp, seq, D_MODEL).astype(x_prompt.dtype), xs.reshape(bs, 1, D_MODEL).astype(x_sample.dtype),
            st(outs_p, "gla"), st(outs_s, "gla"), st(outs_p, "conv"), st(outs_s, "conv"),
            st(outs_p, "lru"), st(outs_s, "lru"), st(outs_p, "re"), st(outs_s, "re"),
            st(outs_p, "im"), st(outs_s, "im"), st(outs_p, "hg"), st(outs_s, "hg"))
```

```python
import functools
import math

import numpy as np
import jax
import jax.numpy as jnp
from jax import lax
from jax.experimental import pallas as pl
from jax.experimental.pallas import tpu as pltpu

f32 = jnp.float32
bf16 = jnp.bfloat16

D_MODEL = 1024
DEPTH = 2
HALF = D_MODEL // 2
GLA_H = 4
GLA_DV = HALF // GLA_H
GLA_DK = GLA_DV // 2
GLA_KW = GLA_H * GLA_DK
GLA_VW = GLA_H * GLA_DV
GLA_LR = 16
GLA_TAU = 16.0
LRU_W = HALF
LRU_H = 8
LRU_BW = LRU_W // LRU_H
CONV_W = 4
RG_C = 8.0
S5_W = HALF
S5_GH = 16
S5_G = S5_W // S5_GH
S5_P = 64
S5_N = S5_G * S5_P
HG_H = 4
HG_D = HALF // HG_H
HG_W = HG_H * HG_D
D_FF = ((8 * D_MODEL // 3 + 255) // 256) * 256
ALPHA = (2.0 * DEPTH) ** 0.25
LN_EPS = 1e-5
RMS_EPS = 1e-6

LANES = 128
SUBLANES = 8
MXU_WIDTH = 256
LRU_ROWS = 128
VMEM_LIMIT = 56 * 1024 * 1024

CHUNK = 64
N_LEVELS = 6
FINE_LEVELS = 3
GLA_UNROLL = 3
GATE_ROWS = 512
FF_CHUNK = 256
FFN_SUB_ROWS = 512
EVEN_Z = 2 * GLA_KW + 2 * GLA_VW + LANES
EVEN_N = EVEN_Z + 2 * LRU_W
GLA_LR_BLOCK = (EVEN_Z - LANES) // LANES
ODD_N = 2560


def _sigmoid(x):
    return 0.5 * jnp.tanh(0.5 * x) + 0.5


def _silu(x):
    return x * _sigmoid(x)


def _gelu(x):
    c = math.sqrt(2.0 / math.pi)
    return x * (0.5 * (1.0 + jnp.tanh(c * (x + 0.044715 * (x * x * x)))))


def _softplus(x):
    return jnp.maximum(x, 0.0) + jnp.log1p(jnp.exp(-jnp.abs(x)))


def _log_sigmoid(x):
    return -_softplus(-x)


def _layer_norm(x, g, b):
    mu = jnp.mean(x, axis=-1, keepdims=True)
    xc = x - mu
    var = jnp.mean(xc * xc, axis=-1, keepdims=True)
    return xc * lax.rsqrt(var + LN_EPS) * g + b


def _dot(a, b):
    return jnp.dot(a, b, preferred_element_type=f32)


def _dot_nt(a, b):
    return lax.dot_general(a, b, (((1,), (1,)), ((), ())), preferred_element_type=f32)


def _dot_tn(a, b):
    return lax.dot_general(a, b, (((0,), (0,)), ((), ())), preferred_element_type=f32)


def _const_spec(shape):
    nd = len(shape)
    return pl.BlockSpec(shape, lambda *_: (0,) * nd, pipeline_mode=pl.Buffered(1))


def _params(*sem):
    return pltpu.CompilerParams(dimension_semantics=sem, vmem_limit_bytes=VMEM_LIMIT)


def _mod_kernel(c_ref, w_ref, b_ref, o_ref):
    cond = _silu(c_ref[...]).astype(bf16)
    o_ref[...] = _dot(cond, w_ref[...].astype(bf16)) + b_ref[...]


def _modulation(c_all, w_ada, b_ada):
    rows = c_all.shape[0]
    tn = 1536
    return pl.pallas_call(
        _mod_kernel,
        out_shape=jax.ShapeDtypeStruct((DEPTH, rows, 6 * D_MODEL), f32),
        grid=(DEPTH, 6 * D_MODEL // tn),
        in_specs=[pl.BlockSpec((rows, D_MODEL), lambda l, j: (0, 0)),
                  pl.BlockSpec((None, D_MODEL, tn), lambda l, j: (l, 0, j)),
                  pl.BlockSpec((None, 1, tn), lambda l, j: (l, 0, j))],
        out_specs=pl.BlockSpec((None, rows, tn), lambda l, j: (l, 0, j)),
        compiler_params=_params("parallel", "parallel"),
        name="adaln_mod",
    )(c_all, w_ada, b_ada.reshape(DEPTH, 1, 6 * D_MODEL))


def _mod_spec(rows_per_block, tiles_per_batch, j):
    return pl.BlockSpec((None, rows_per_block, D_MODEL), lambda i: (i // tiles_per_batch, 0, j))


def _inproj_kernel(hgrn_layer, x_ref, sc_ref, sh_ref, w_ref, *refs):
    h = (x_ref[...] * (1.0 + sc_ref[...]) + sh_ref[...]).astype(bf16)
    z_refs = refs if hgrn_layer is None else refs[1:-1]
    lo = 0
    i = 0
    while i < len(z_refs):
        j = i + 1
        while z_refs[i].shape[1] < MXU_WIDTH and j < len(z_refs) and z_refs[j].shape[1] == z_refs[i].shape[1]:
            j += 1
        width = sum(r.shape[1] for r in z_refs[i:j])
        z = _dot(h, w_ref[:, lo:lo + width])
        off = 0
        for r in z_refs[i:j]:
            r[...] = z[:, off:off + r.shape[1]]
            off += r.shape[1]
        lo += width
        i = j
    if hgrn_layer is not None:
        lg_ref, zr = refs[0], refs[-1]
        lb = _hgrn_lower_bound(lg_ref, hgrn_layer)
        acts = (_silu, lambda f: lb + (1.0 - lb) * _sigmoid(f), lambda v: v, _silu)
        for n, act in enumerate(acts):
            zr[:, n * HG_W:(n + 1) * HG_W] = act(_dot(h, w_ref[:, lo + n * HG_W:lo + (n + 1) * HG_W]))


def _inproj(x, mod, w, tm, tiles_per_batch, widths, hgrn=None):
    t, n = x.shape[0], w.shape[1]
    assert sum(widths) == n and (hgrn is None or widths[-1] == 4 * HG_W)
    r = mod.shape[1]
    extra = [] if hgrn is None else [hgrn[0]]
    return pl.pallas_call(
        functools.partial(_inproj_kernel, None if hgrn is None else hgrn[1]),
        out_shape=tuple(jax.ShapeDtypeStruct((t, wd), f32) for wd in widths),
        grid=(t // tm,),
        in_specs=[pl.BlockSpec((tm, D_MODEL), lambda i: (i, 0)),
                  _mod_spec(r, tiles_per_batch, 1), _mod_spec(r, tiles_per_batch, 0),
                  _const_spec((D_MODEL, n))] + [_const_spec(e.shape) for e in extra],
        out_specs=tuple(pl.BlockSpec((tm, wd), lambda i: (i, 0)) for wd in widths),
        compiler_params=_params("parallel"),
        name="in_proj",
    )(x, mod, mod, w, *extra)


def _outffn_kernel(n_y, x_ref, *refs):
    y_refs = refs[:n_y]
    gtm_ref, shf_ref, scf_ref, gtf_ref, wo_ref, lng_ref, lnb_ref, wi_ref, wf_ref, o_ref = refs[n_y:]
    tm = x_ref.shape[0]
    n_sub = 2 if tm % (2 * FFN_SUB_ROWS) == 0 else 1
    rs = [slice(i * (tm // n_sub), (i + 1) * (tm // n_sub)) for i in range(n_sub)]
    mrows = lambda ref, rows: ref[...] if ref.shape[0] == 1 else ref[rows, :]
    n_ff = D_FF // FF_CHUNK

    def ln1(i, mix):
        rows = rs[i]
        x1 = _layer_norm(ALPHA * x_ref[rows, :] + (1.0 + mrows(gtm_ref, rows)) * mix, lng_ref[0:1, :], lnb_ref[0:1, :])
        return x1, (x1 * (1.0 + mrows(scf_ref, rows)) + mrows(shf_ref, rows)).astype(bf16)

    def ln2(i, x1, acc):
        rows = rs[i]
        o_ref[rows, :] = _layer_norm(ALPHA * x1 + (1.0 + mrows(gtf_ref, rows)) * acc, lng_ref[1:2, :], lnb_ref[1:2, :])

    mixes = [_dot(jnp.concatenate([r[rows, :].astype(bf16) for r in y_refs], axis=1), wo_ref[...]) for rows in rs]
    cur = ln1(0, mixes[0])
    pending = None
    for i in range(n_sub):
        x1, h = cur
        acc = jnp.zeros(x1.shape, f32)
        for c in range(n_ff):
            lo = c * FF_CHUNK
            gate = _dot(h, wi_ref[:, lo:lo + FF_CHUNK])
            up = _dot(h, wi_ref[:, D_FF + lo:D_FF + lo + FF_CHUNK])
            act = (_silu(gate) * up).astype(bf16)
            acc = acc + _dot(act, wf_ref[lo:lo + FF_CHUNK, :])
            if c == 1:
                if pending is not None:
                    ln2(*pending)
                    pending = None
                if i + 1 < n_sub:
                    cur = ln1(i + 1, mixes[i + 1])
        pending = (i, x1, acc)
    ln2(*pending)


def _outffn(x, ys, mod, w_out, ln_g, ln_b, w_in, w_f, layer, tm, tiles_per_batch):
    t = x.shape[0]
    r = mod.shape[1]
    assert sum(y.shape[1] for y in ys) == D_MODEL
    row = lambda w: pl.BlockSpec((tm, w), lambda i: (i, 0))
    stacked = lambda a: pl.BlockSpec((None,) + a.shape[1:], lambda i: (layer, 0, 0), pipeline_mode=pl.Buffered(1))
    return pl.pallas_call(
        functools.partial(_outffn_kernel, len(ys)),
        out_shape=jax.ShapeDtypeStruct((t, D_MODEL), f32),
        grid=(t // tm,),
        in_specs=[row(D_MODEL)] + [row(y.shape[1]) for y in ys] + [
                  _mod_spec(r, tiles_per_batch, 2), _mod_spec(r, tiles_per_batch, 3),
                  _mod_spec(r, tiles_per_batch, 4), _mod_spec(r, tiles_per_batch, 5),
                  _const_spec((D_MODEL, D_MODEL)), _const_spec((2, D_MODEL)), _const_spec((2, D_MODEL)),
                  stacked(w_in), stacked(w_f)],
        out_specs=row(D_MODEL),
        compiler_params=_params("parallel"),
        name="outproj_ffn",
    )(x, *ys, mod, mod, mod, mod, w_out, ln_g, ln_b, w_in, w_f)


def _level_tables():
    c = CHUNK
    t = np.arange(c)[:, None]
    u = np.arange(c)[None, :]
    pmask, sgn = [], []
    for lvl in range(N_LEVELS):
        m = 1 << lvl
        right = (t % (2 * m)) >= m
        pm = right & ((u % (2 * m)) < m) & ((t // (2 * m)) == (u // (2 * m)))
        pmask.append(pm.astype(np.float32))
        sgn.append(np.broadcast_to(np.where(right, 1.0, -1.0).astype(np.float32), (c, LANES)))
    tri = (u <= t).astype(np.float32)
    tri3 = np.concatenate([tri, tri, tri], axis=1)
    return (jnp.asarray(tri3, dtype=bf16), jnp.asarray(np.stack(pmask), dtype=f32),
            jnp.asarray(np.stack(sgn), dtype=f32))


LOG2E = 1.0 / math.log(2.0)


def _anchor(b, lvl):
    m = 1 << lvl
    b3 = b.reshape(CHUNK // SUBLANES, SUBLANES, LANES)
    if 4 * m == SUBLANES:
        sub = lax.broadcasted_iota(jnp.int32, b3.shape, 1)
        a = jnp.where(sub < 2 * m, jnp.broadcast_to(b3[:, m - 1:m, :], b3.shape),
                      jnp.broadcast_to(b3[:, 3 * m - 1:3 * m, :], b3.shape))
        return a.reshape(CHUNK, LANES)
    assert 2 * m == SUBLANES
    return jnp.broadcast_to(b3[:, m - 1:m, :], b3.shape).reshape(CHUNK, LANES)


def _split3(g):
    g1 = g.astype(bf16)
    r1 = g - g1.astype(f32)
    g2 = r1.astype(bf16)
    r2 = r1 - g2.astype(f32)
    return jnp.concatenate([g1, g2, r2.astype(bf16)], axis=0)


def _hgrn_lower_bound(lg_ref, layer):
    rows = [lg_ref[i:i + 1, :] for i in range(DEPTH)]
    mx = functools.reduce(jnp.maximum, rows)
    ex = [jnp.exp(r - mx) for r in rows]
    den = functools.reduce(lambda a, b: a + b, ex)
    sm = [e / den for e in ex]
    cs = sm[0]
    for i in range(1, layer + 1):
        cs = cs + sm[i]
    return cs - sm[0]


def _gla_prologue(kind, refs, rows):
    if kind == "gla":
        qk_ref, g_scr = refs
        q = qk_ref[rows, 0:GLA_KW] * (GLA_DK ** -0.5)
        k = qk_ref[rows, GLA_KW:2 * GLA_KW]
        return q, k, g_scr[rows, :]
    q_ref, f_ref = refs
    forget = f_ref[rows, :]
    return q_ref[rows, :], 1.0 - forget, jnp.log2(forget)


def _gla_chunk_kernel(kind, dk, n_heads, *refs):
    if kind == "gla":
        (qk_ref, v_ref, gate_ref, lr_ref, wlr_ref, blr_ref, ng_ref, tri_ref, pm_ref, sg_ref,
         y_ref, s_ref, st_ref, g_scr) = refs
        pro = (qk_ref, g_scr)
        for r0 in range(0, y_ref.shape[0], GATE_ROWS):
            rs = slice(r0, min(r0 + GATE_ROWS, y_ref.shape[0]))
            pre = _dot(lr_ref[rs, :].astype(bf16), wlr_ref[...]) + blr_ref[...]
            g_scr[rs, :] = _log_sigmoid(pre) * (LOG2E / GLA_TAU)
    else:
        (q_ref, f_ref, v_ref, gate_ref, ng_ref, tri_ref, pm_ref, sg_ref,
         y_ref, s_ref, st_ref) = refs
        pro = (q_ref, f_ref)
    t_idx = pl.program_id(1)
    n_tiles = pl.num_programs(1)
    lt = y_ref.shape[0]
    hps = LANES // dk
    n_slabs = n_heads // hps
    lane = lax.broadcasted_iota(jnp.int32, (1, LANES), 1)
    eye = (lax.broadcasted_iota(jnp.int32, (CHUNK, CHUNK), 0)
           == lax.broadcasted_iota(jnp.int32, (CHUNK, CHUNK), 1))

    @pl.when(t_idx == 0)
    def _():
        st_ref[...] = jnp.zeros(st_ref.shape, f32)

    def head_mask(hh):
        if hps == 1:
            return lambda a: a
        msk = (lane >= hh * dk) & (lane < (hh + 1) * dk)
        return lambda a: jnp.where(msk, a, jnp.zeros_like(a))

    def rows_of(c):
        return pl.ds(pl.multiple_of(c * CHUNK, CHUNK), CHUNK)

    def cumsums(qkg):
        q_all, k_all, g_all = qkg
        out = []
        for sl in range(n_slabs):
            cs = slice(sl * LANES, (sl + 1) * LANES)
            q_s, k_s, g_s = q_all[:, cs], k_all[:, cs], g_all[:, cs]
            out.append((q_s, k_s, g_s, _dot(tri_ref[...], _split3(g_s))))
        return out

    def score_dots(pre):
        raw, qbs, ksts, decs, rds = [], [], [], [], []
        right = [sg_ref[lvl] > 0.0 for lvl in range(FINE_LEVELS)]
        for sl in range(n_slabs):
            q_s, k_s, g_s, b = pre[sl]
            ys = [(jnp.where(right[0], jnp.exp2(g_s) * q_s, k_s)).astype(bf16)]
            for lvl in range(1, FINE_LEVELS):
                x = jnp.exp2((b - _anchor(b, lvl)) * sg_ref[lvl])
                ys.append((x * jnp.where(right[lvl], q_s, k_s)).astype(bf16))
            for lvl in range(FINE_LEVELS, N_LEVELS):
                m = 1 << lvl
                parts = []
                for r0 in range(0, CHUNK, SUBLANES):
                    a0 = (r0 // (2 * m)) * (2 * m) + m - 1
                    anc = jnp.broadcast_to(b[a0:a0 + 1, :], (SUBLANES, LANES))
                    bj = b[r0:r0 + SUBLANES, :]
                    if r0 % (2 * m) >= m:
                        parts.append(jnp.exp2(bj - anc) * q_s[r0:r0 + SUBLANES, :])
                    else:
                        parts.append(jnp.exp2(anc - bj) * k_s[r0:r0 + SUBLANES, :])
                ys.append(jnp.concatenate(parts, axis=0).astype(bf16))
            blast = jnp.broadcast_to(b[CHUNK - 1:CHUNK, :], b.shape)
            qb = (q_s * jnp.exp2(b)).astype(bf16)
            kst = (k_s * jnp.exp2(blast - b)).astype(bf16)
            decs.append(jnp.exp2(b[CHUNK - 1:CHUNK, :]))
            qk_prod = q_s * k_s
            for hh in range(hps):
                mul = head_mask(hh)
                rds.append(jnp.sum(mul(qk_prod), axis=-1, keepdims=True))
                raw.append([_dot_nt(mul(ys[lvl]), ys[lvl]) for lvl in range(N_LEVELS)])
                qbs.append(mul(qb))
                ksts.append(mul(kst))
        return raw, rds, tuple(qbs), tuple(ksts), tuple(decs)

    def combine(raw, rds):
        ps = []
        for h in range(n_heads):
            p = jnp.where(eye, rds[h], 0.0)
            for lvl in range(N_LEVELS):
                p = p + pm_ref[lvl] * raw[h][lvl]
            ps.append(p.astype(bf16))
        return tuple(ps)

    def apply_dots(c, sc):
        ps, qbs, ksts, decs = sc
        outs = []
        for h in range(n_heads):
            vb = v_ref[rows_of(c), h * LANES:(h + 1) * LANES].astype(bf16)
            st = st_ref[h]
            outs.append(_dot(ps[h], vb) + _dot_nt(qbs[h], st.astype(bf16)))
            st_ref[h] = st * decs[h // hps] + _dot_tn(vb, ksts[h])
        return outs

    def finish(c, outs):
        for h in range(n_heads):
            cs = slice(h * LANES, (h + 1) * LANES)
            o = outs[h]
            o = o * lax.rsqrt(jnp.mean(o * o, axis=-1, keepdims=True) + RMS_EPS) * ng_ref[...]
            gate = gate_ref[rows_of(c), cs]
            y_ref[rows_of(c), cs] = o * (_silu(gate) if kind == "gla" else gate)

    def first():
        raw, rds, qbs, ksts, decs = score_dots(cumsums(_gla_prologue(kind, pro, rows_of(0))))
        return combine(raw, rds), qbs, ksts, decs

    def body(c, sc):
        pre = cumsums(_gla_prologue(kind, pro, rows_of(c + 1)))
        outs = apply_dots(c, sc)
        raw, rds, qbs, ksts, decs = score_dots(pre)
        finish(c, outs)
        return combine(raw, rds), qbs, ksts, decs

    n_chunks = lt // CHUNK
    last = lax.fori_loop(0, n_chunks - 1, body, first(), unroll=min(GLA_UNROLL, n_chunks - 1))
    finish(n_chunks - 1, apply_dots(n_chunks - 1, last))

    @pl.when(t_idx == n_tiles - 1)
    def _():
        for h in range(n_heads):
            off = (h % hps) * dk
            s_ref[h] = st_ref[h].T[off:off + dk, :]


def _gla_prompt(kind, z, col0, extra, norm_g, batch, seq, lt):
    dk = GLA_DK if kind == "gla" else HG_D
    n_heads = GLA_H if kind == "gla" else HG_H
    tiles = seq // lt
    mall, pmask, rmask = _level_tables()
    blk = lambda width, cb: pl.BlockSpec((lt, width), lambda b, t: (b * tiles + t, cb))
    if kind == "gla":
        wlr, blr = extra
        ins = [z, z, z, z, wlr, blr]
        specs = [blk(2 * GLA_KW, 0), blk(GLA_VW, 1), blk(GLA_VW, 2), blk(LANES, col0),
                 _const_spec(wlr.shape), _const_spec(blr.shape)]
    else:
        ins = [z, z, z, z]
        specs = [blk(HG_W, col0), blk(HG_W, col0 + 1), blk(HG_W, col0 + 2), blk(HG_W, col0 + 3)]
    ins += [norm_g, mall, pmask, rmask]
    specs += [_const_spec(norm_g.shape), _const_spec(mall.shape), _const_spec(pmask.shape), _const_spec(rmask.shape)]
    return pl.pallas_call(
        functools.partial(_gla_chunk_kernel, kind, dk, n_heads),
        out_shape=(jax.ShapeDtypeStruct((batch * seq, HALF), f32),
                   jax.ShapeDtypeStruct((batch, n_heads, dk, LANES), f32)),
        grid=(batch, tiles),
        in_specs=specs,
        out_specs=(pl.BlockSpec((lt, HALF), lambda b, t: (b * tiles + t, 0)),
                   pl.BlockSpec((None, n_heads, dk, LANES), lambda b, t: (b, 0, 0, 0))),
        scratch_shapes=[pltpu.VMEM((n_heads, LANES, LANES), f32)]
        + ([pltpu.VMEM((lt, GLA_KW), f32)] if kind == "gla" else []),
        compiler_params=_params("parallel", "arbitrary"),
        name=kind + "_prompt",
    )(*ins)


def _lru_gates(xc, wr_ref, wi_ref, br_ref, bi_ref, lam_ref):
    xb = xc.astype(bf16)
    r = _sigmoid(_dot(xb, wr_ref[...]) + br_ref[...])
    i = _sigmoid(_dot(xb, wi_ref[...]) + bi_ref[...])
    log_a = (-RG_C) * r * _softplus(-lam_ref[...])
    t = jnp.tanh(log_a)
    one_minus_a2 = (-2.0 * t) / (1.0 - t)
    return jnp.exp(log_a), jnp.sqrt(one_minus_a2) * (i * xc)


def _even_front_kernel(xin_ref, sc_ref, sh_ref, w_ref, cw_ref, cb_ref, wr_ref, wi_ref, br_ref, bi_ref, lam_ref,
                       z_ref, y_ref, conv_ref, h_ref, xbuf, a_scr, h_scr, car):
    t_idx = pl.program_id(1)
    lt = xin_ref.shape[0]
    nz = z_ref.shape[1]

    @pl.when(t_idx == 0)
    def _():
        xbuf[0:SUBLANES, :] = jnp.zeros((SUBLANES, LRU_W), f32)
        car[...] = jnp.zeros(car.shape, f32)

    hin = (xin_ref[...] * (1.0 + sc_ref[...]) + sh_ref[...]).astype(bf16)
    x = _dot(hin, w_ref[:, nz:nz + LRU_W])
    xbuf[SUBLANES:SUBLANES + lt, :] = x
    tiles = [(lo, min(lo + MXU_WIDTH, nz)) for lo in range(0, nz, MXU_WIDTH)]
    tiles += [(nz + LRU_W + lo, nz + LRU_W + lo + MXU_WIDTH) for lo in range(0, LRU_W, MXU_WIDTH)]

    def col_tile(lo, hi):
        zt = _dot(hin, w_ref[:, lo:hi])
        if lo < nz:
            z_ref[:, lo:hi] = zt
        else:
            y_ref[:, lo - nz - LRU_W:hi - nz - LRU_W] = _gelu(zt)

    blocks = [(r0, j) for r0 in range(0, lt, LRU_ROWS) for j in range(LRU_W // LANES)]
    sub = lax.broadcasted_iota(jnp.int32, (SUBLANES, LANES), 0)
    grp = (LRU_ROWS // SUBLANES, SUBLANES, LANES)
    done = 0
    for n, (r0, j) in enumerate(blocks):
        while done < len(tiles) and done * len(blocks) <= n * len(tiles):
            col_tile(*tiles[done])
            done += 1
        cs = slice(j * LANES, (j + 1) * LANES)
        base = SUBLANES + r0
        xc = cb_ref[:, cs] + cw_ref[CONV_W - 1:CONV_W, cs] * xbuf[base:base + LRU_ROWS, cs]
        for jj in range(CONV_W - 1):
            off = base - (CONV_W - 1) + jj
            xc = xc + cw_ref[jj:jj + 1, cs] * xbuf[off:off + LRU_ROWS, cs]
        a, hh = _lru_gates(xc, wr_ref.at[cs, cs], wi_ref.at[cs, cs], br_ref.at[:, cs], bi_ref.at[:, cs],
                           lam_ref.at[:, cs])
        a, hh = a.reshape(grp), hh.reshape(grp)
        for d in (1, 2, 4):
            m = sub >= d
            hh = hh + jnp.where(m, a, 0.0) * pltpu.roll(hh, d, 1)
            a = a * jnp.where(m, pltpu.roll(a, d, 1), 1.0)
        a_scr[r0:r0 + LRU_ROWS, cs] = a.reshape(LRU_ROWS, LANES)
        h_scr[r0:r0 + LRU_ROWS, cs] = hh.reshape(LRU_ROWS, LANES)
    while done < len(tiles):
        col_tile(*tiles[done])
        done += 1
    xbuf[0:SUBLANES, :] = xbuf[lt:lt + SUBLANES, :]

    def group(g, c):
        rows = pl.ds(pl.multiple_of(g * SUBLANES, SUBLANES), SUBLANES)
        hg = h_scr[rows, :] + a_scr[rows, :] * c
        h_scr[rows, :] = hg
        return jnp.broadcast_to(hg[SUBLANES - 1:SUBLANES, :], (SUBLANES, LRU_W))

    c = lax.fori_loop(0, lt // SUBLANES, group, car[...])
    car[...] = c
    y_ref[...] = h_scr[...] * y_ref[...]
    conv_ref[...] = x[lt - (CONV_W - 1):lt, :]
    h_ref[...] = c[0:1, :]


def _even_front(x, mod, w, p, batch, seq, lt):
    tiles = seq // lt
    row = lambda wd: pl.BlockSpec((lt, wd), lambda b, t: (b * tiles + t, 0))
    modv = lambda j: pl.BlockSpec((None, 1, D_MODEL), lambda b, t: (b, 0, j))
    consts = [p["conv_w"], p["conv_b"], p["w_r"], p["w_i"], p["b_r"], p["b_i"], p["lam"]]
    return pl.pallas_call(
        _even_front_kernel,
        out_shape=(jax.ShapeDtypeStruct((batch * seq, EVEN_Z), f32),
                   jax.ShapeDtypeStruct((batch * seq, LRU_W), f32),
                   jax.ShapeDtypeStruct((batch, CONV_W - 1, LRU_W), f32),
                   jax.ShapeDtypeStruct((batch, 1, LRU_W), f32)),
        grid=(batch, tiles),
        in_specs=[row(D_MODEL), modv(1), modv(0), _const_spec(w.shape)] + [_const_spec(c.shape) for c in consts],
        out_specs=(row(EVEN_Z), row(LRU_W),
                   pl.BlockSpec((None, CONV_W - 1, LRU_W), lambda b, t: (b, 0, 0)),
                   pl.BlockSpec((None, 1, LRU_W), lambda b, t: (b, 0, 0))),
        scratch_shapes=[pltpu.VMEM((lt + SUBLANES, LRU_W), f32), pltpu.VMEM((lt, LRU_W), f32),
                        pltpu.VMEM((lt, LRU_W), f32), pltpu.VMEM((SUBLANES, LRU_W), f32)],
        compiler_params=_params("parallel", "arbitrary"),
        name="even_front",
    )(x, mod, mod, w, *consts)


S5_R = 4
S5_TAB = 8


S5_GPB = LANES // S5_GH
S5_NBLK = S5_G // S5_GPB


def _s5_zoh(lr_, li_, ldt):
    dt = jnp.exp(ldt)
    mag = jnp.exp(lr_ * dt)
    ar, ai = mag * jnp.cos(li_ * dt), mag * jnp.sin(li_ * dt)
    den = lr_ * lr_ + li_ * li_
    nr, ni = ar - 1.0, ai
    return ar, ai, (nr * lr_ + ni * li_) / den, (ni * lr_ - nr * li_) / den


def _s5_param_kernel(lr_ref, li_ref, ldt_ref, lrf_ref, lif_ref, ldtf_ref, btr_ref, bti_ref, cr_ref, ci_ref,
                     a1r_ref, a1i_ref, pcr_ref, pci_ref, wx_ref, wct_ref, ct_ref, bd_s):
    ar, ai, fr, fi = _s5_zoh(lr_ref[...], li_ref[...], ldt_ref[...])
    bbr = fr[:, None, :] * btr_ref[...] - fi[:, None, :] * bti_ref[...]
    bbi = fr[:, None, :] * bti_ref[...] + fi[:, None, :] * btr_ref[...]
    cr, ci = cr_ref[...], ci_ref[...]

    def put(dst_ref, row0, re, im):
        for j in range(S5_NBLK):
            bd_s[...] = jnp.zeros(bd_s.shape, f32)
            for g in range(S5_GPB):
                rows = slice(g * S5_GH, (g + 1) * S5_GH)
                bd_s[rows, g * S5_P:(g + 1) * S5_P] = re[j * S5_GPB + g]
                bd_s[rows, S5_BLK + g * S5_P:S5_BLK + (g + 1) * S5_P] = im[j * S5_GPB + g]
            dst_ref[j, row0:row0 + LANES, :] = bd_s[...].astype(bf16)

    put(ct_ref, 0, cr, -ci)
    pr, pi = jnp.ones_like(ar), jnp.zeros_like(ai)
    for n in range(S5_R + 1):
        if n < S5_R:
            put(wx_ref, (S5_R - 1 - n) * LANES,
                pr[:, None, :] * bbr - pi[:, None, :] * bbi, pr[:, None, :] * bbi + pi[:, None, :] * bbr)
        if n >= 1:
            put(wct_ref, (n - 1) * LANES,
                cr * pr[:, None, :] - ci * pi[:, None, :], -(cr * pi[:, None, :] + ci * pr[:, None, :]))
        pr, pi = pr * ar - pi * ai, pr * ai + pi * ar

    ar, ai, _, _ = _s5_zoh(lrf_ref[...], lif_ref[...], ldtf_ref[...])
    a1r_ref[...], a1i_ref[...] = ar, ai
    pr, pi = ar, ai
    for n in range(1, S5_R * S5_TAB + 1):
        if n % S5_R == 0:
            pcr_ref[n // S5_R - 1:n // S5_R, :] = pr
            pci_ref[n // S5_R - 1:n // S5_R, :] = pi
        pr, pi = pr * ar - pi * ai, pr * ai + pi * ar


def _toeplitz_kernel(wx_ref, ct_ref, tp_ref):
    lag = [_dot_nt(wx_ref[(S5_R - 1 - d) * LANES:(S5_R - d) * LANES, :], ct_ref[...]).astype(bf16)
           for d in range(S5_R)]
    tp_ref[...] = jnp.zeros(tp_ref.shape, bf16)
    for s in range(S5_R):
        for r in range(s, S5_R):
            tp_ref[s * LANES:(s + 1) * LANES, r * LANES:(r + 1) * LANES] = lag[r - s]


def _s5_params(lam_re, lam_im, log_dt, b_re, b_im, c_re, c_im):
    btr = jnp.swapaxes(b_re, 1, 2)
    bti = jnp.swapaxes(b_im, 1, 2)
    shp = jax.ShapeDtypeStruct
    flat = lambda a: a.reshape(1, S5_N)
    wide = (S5_NBLK, S5_R * LANES, 2 * S5_BLK)
    a1r, a1i, pcr, pci, w_x, w_ct, c_t = pl.pallas_call(
        _s5_param_kernel,
        out_shape=(shp((1, S5_N), f32), shp((1, S5_N), f32), shp((S5_TAB, S5_N), f32), shp((S5_TAB, S5_N), f32),
                   shp(wide, bf16), shp(wide, bf16), shp((S5_NBLK, LANES, 2 * S5_BLK), bf16)),
        scratch_shapes=[pltpu.VMEM((LANES, 2 * S5_BLK), f32)],
        name="s5_params",
    )(lam_re, lam_im, log_dt.reshape(S5_G, 1), flat(lam_re), flat(lam_im), flat(jnp.repeat(log_dt, S5_P)),
      btr, bti, c_re, c_im)
    toep = pl.pallas_call(
        _toeplitz_kernel,
        out_shape=shp((S5_NBLK, S5_R * LANES, S5_R * LANES), bf16),
        grid=(S5_NBLK,),
        in_specs=[pl.BlockSpec((None,) + wide[1:], lambda j: (j, 0, 0)),
                  pl.BlockSpec((None, LANES, 2 * S5_BLK), lambda j: (j, 0, 0))],
        out_specs=pl.BlockSpec((None, S5_R * LANES, S5_R * LANES), lambda j: (j, 0, 0)),
        compiler_params=_params("parallel"),
        name="s5_toeplitz",
    )(w_x, c_t)
    return dict(a1r=a1r, a1i=a1i, pcr=pcr, pci=pci, c_t=c_t, w_x=w_x, toep=toep, w_ct=w_ct)


S5_BLK = S5_N // (S5_G // (LANES // S5_GH))


def _s5_input(u, wx_ref, j):
    b_bar = wx_ref[j, (S5_R - 1) * LANES:S5_R * LANES, :]
    bu = _dot(u[:, j * LANES:(j + 1) * LANES].astype(bf16), b_bar)
    return bu[:, 0:S5_BLK], bu[:, S5_BLK:2 * S5_BLK]


def _s5_output(u, hr, hi, ct_ref, d_ref, wg_ref, bg_ref):
    ys = []
    for j in range(S5_N // S5_BLK):
        cs = slice(j * S5_BLK, (j + 1) * S5_BLK)
        hcat = jnp.concatenate([hr[:, cs], hi[:, cs]], axis=1).astype(bf16)
        ys.append(_dot_nt(hcat, ct_ref[j]))
    y = _gelu(jnp.concatenate(ys, axis=1) + d_ref[...] * u)
    return y * _sigmoid(_dot(y.astype(bf16), wg_ref[...]) + bg_ref[...])


def _s5_prompt_kernel(*refs):
    nblk = S5_N // S5_BLK
    u_refs = refs[:nblk]
    wx_ref, tp_ref, wct_ref, pcr_ref, pci_ref, d_ref, wg_ref, bg_ref = refs[nblk:nblk + 8]
    y_refs = refs[nblk + 8:2 * nblk + 8]
    sr_ref, si_ref, er_s, ei_s, y_s = refs[2 * nblk + 8:]
    t_idx = pl.program_id(1)
    cr = u_refs[0].shape[0] // S5_R
    gw = S5_R * LANES

    @pl.when(t_idx == 0)
    def _():
        er_s[0:SUBLANES, :] = jnp.zeros((SUBLANES, S5_N), f32)
        ei_s[0:SUBLANES, :] = jnp.zeros((SUBLANES, S5_N), f32)

    def token(s):
        return pl.ds(s, cr, stride=S5_R)

    sub = lax.broadcasted_iota(jnp.int32, (SUBLANES, S5_BLK), 0)
    for j in range(nblk):
        cs = slice(j * S5_BLK, (j + 1) * S5_BLK)
        uj = jnp.concatenate([u_refs[j][token(s), :] for s in range(S5_R)], axis=1).astype(bf16)
        x = _dot(uj, wx_ref[j])
        y_s[:, j * gw:(j + 1) * gw] = _dot(uj, tp_ref[j])
        hr = x[:, 0:S5_BLK].reshape(cr // SUBLANES, SUBLANES, S5_BLK)
        hi = x[:, S5_BLK:2 * S5_BLK].reshape(cr // SUBLANES, SUBLANES, S5_BLK)
        for d in (1, 2, 4):
            pr = jnp.where(sub >= d, pcr_ref[d - 1:d, cs], 0.0)
            pi = jnp.where(sub >= d, pci_ref[d - 1:d, cs], 0.0)
            sr, si = pltpu.roll(hr, d, 1), pltpu.roll(hi, d, 1)
            hr, hi = hr + (pr * sr - pi * si), hi + (pr * si + pi * sr)
        er_s[SUBLANES:SUBLANES + cr, cs] = hr.reshape(cr, S5_BLK)
        ei_s[SUBLANES:SUBLANES + cr, cs] = hi.reshape(cr, S5_BLK)

    tr, ti = pcr_ref[...], pci_ref[...]
    last = pl.ds(SUBLANES - 1, 1)
    c0 = (jnp.broadcast_to(er_s[last, :], (SUBLANES, S5_N)), jnp.broadcast_to(ei_s[last, :], (SUBLANES, S5_N)))

    def group(g, c):
        c_r, c_i = c
        rows = pl.ds(pl.multiple_of((g + 1) * SUBLANES, SUBLANES), SUBLANES)
        g_r = er_s[rows, :] + tr * c_r - ti * c_i
        g_i = ei_s[rows, :] + tr * c_i + ti * c_r
        er_s[rows, :] = g_r
        ei_s[rows, :] = g_i
        return (jnp.broadcast_to(g_r[SUBLANES - 1:SUBLANES, :], (SUBLANES, S5_N)),
                jnp.broadcast_to(g_i[SUBLANES - 1:SUBLANES, :], (SUBLANES, S5_N)))

    c_r, c_i = lax.fori_loop(0, cr // SUBLANES, group, c0)

    for j in range(nblk):
        cs = slice(j * S5_BLK, (j + 1) * S5_BLK)
        prev = pl.ds(SUBLANES - 1, cr)
        sj = jnp.concatenate([er_s[prev, cs], ei_s[prev, cs]], axis=1).astype(bf16)
        y_s[:, j * gw:(j + 1) * gw] += _dot_nt(sj, wct_ref[j])
    er_s[last, :] = c_r[0:1, :]
    ei_s[last, :] = c_i[0:1, :]
    sr_ref[...] = c_r[0:1, :]
    si_ref[...] = c_i[0:1, :]

    for r in range(S5_R):
        y = jnp.concatenate([y_s[:, j * gw + r * LANES:j * gw + (r + 1) * LANES] for j in range(nblk)], axis=1)
        u_r = jnp.concatenate([u_refs[j][token(r), :] for j in range(nblk)], axis=1)
        y = _gelu(y + d_ref[...] * u_r)
        y = y * _sigmoid(_dot(y.astype(bf16), wg_ref[...]) + bg_ref[...])
        for j in range(nblk):
            y_refs[j][token(r), :] = y[:, j * LANES:(j + 1) * LANES]


def _s5_prompt(us, tabs, p, batch, seq, cr):
    nblk = len(us)
    tiles = seq // (S5_R * cr)
    consts = [tabs["w_x"], tabs["toep"], tabs["w_ct"], tabs["pcr"], tabs["pci"], p["d"], p["w_glu"], p["b_glu"]]
    blk = pl.BlockSpec((S5_R * cr, LANES), lambda b, t: (b * tiles + t, 0))
    state = pl.BlockSpec((None, 1, S5_N), lambda b, t: (b, 0, 0))
    outs = pl.pallas_call(
        _s5_prompt_kernel,
        out_shape=tuple(jax.ShapeDtypeStruct((batch * seq, LANES), f32) for _ in range(nblk))
        + (jax.ShapeDtypeStruct((batch, 1, S5_N), f32), jax.ShapeDtypeStruct((batch, 1, S5_N), f32)),
        grid=(batch, tiles),
        in_specs=[blk] * nblk + [_const_spec(c.shape) for c in consts],
        out_specs=(blk,) * nblk + (state, state),
        scratch_shapes=[pltpu.VMEM((cr + SUBLANES, S5_N), f32), pltpu.VMEM((cr + SUBLANES, S5_N), f32),
                        pltpu.VMEM((cr, S5_R * S5_W), f32)],
        compiler_params=_params("parallel", "arbitrary"),
        name="s5_prompt",
    )(*us, *consts)
    return list(outs[:nblk]), outs[nblk], outs[nblk + 1]


def _gla_step_kernel(kind, dk, *refs):
    if kind == "gla":
        q_ref, k_ref, v_ref, gate_ref, lr_ref, wlr_ref, blr_ref, ng_ref, s_ref, y_ref, so_ref, o_scr = refs
        sl = pl.program_id(0)
        q = q_ref[...] * (GLA_DK ** -0.5)
        k = k_ref[...]
        pre_all = _dot(lr_ref[...].astype(bf16), wlr_ref[...]) + blr_ref[...]
        pre = jnp.where(sl == 0, pre_all[:, 0:LANES], pre_all[:, LANES:2 * LANES])
        a = jnp.exp(_log_sigmoid(pre) * (1.0 / GLA_TAU))
    else:
        q_ref, f_ref, v_ref, gate_ref, ng_ref, s_ref, y_ref, so_ref, o_scr = refs
        a = f_ref[...]
        q = q_ref[...]
        k = 1.0 - a
    hps = LANES // dk
    nb = q.shape[0]
    kt, at = k.T, a.T
    lane = lax.broadcasted_iota(jnp.int32, (1, LANES), 1)
    v = v_ref[...]
    q_heads = [jnp.where((lane >= hh * dk) & (lane < (hh + 1) * dk), q, 0.0) for hh in range(hps)]
    for b in range(nb):
        new = []
        for hh in range(hps):
            r0 = hh * dk
            col = lambda x: jnp.broadcast_to(x[r0:r0 + dk, b:b + 1], (dk, LANES))
            a_col = col(at)
            k_col = col(kt) if kind == "gla" else 1.0 - a_col
            s_new = a_col * s_ref[b, hh] + k_col * v[b:b + 1, hh * LANES:(hh + 1) * LANES]
            so_ref[b, hh] = s_new
            new.append(s_new)
        b0 = b // SUBLANES * SUBLANES
        lhs = jnp.concatenate([qh[b0:b0 + SUBLANES, :] for qh in q_heads], axis=0).astype(bf16)
        res = _dot(lhs, jnp.concatenate(new, axis=0).astype(bf16))
        for hh in range(hps):
            r = hh * SUBLANES + b - b0
            o_scr[b:b + 1, hh * LANES:(hh + 1) * LANES] = res[r:r + 1, :]
    for hh in range(hps):
        cs = slice(hh * LANES, (hh + 1) * LANES)
        o = o_scr[:, cs]
        o = o * lax.rsqrt(jnp.mean(o * o, axis=-1, keepdims=True) + RMS_EPS) * ng_ref[...]
        y_ref[:, cs] = o * (_silu(gate_ref[:, cs]) if kind == "gla" else gate_ref[:, cs])


def _gla_step(kind, z, extra, norm_g, state):
    nb = z.shape[0]
    dk = GLA_DK if kind == "gla" else HG_D
    n_heads = GLA_H if kind == "gla" else HG_H
    hps = LANES // dk
    n_slabs = n_heads // hps
    vw = hps * LANES
    zb = lambda width, cb0: pl.BlockSpec((nb, width), lambda s: (0, cb0 + s))
    if kind == "gla":
        wlr, blr = extra
        ins = [z, z, z, z, z, wlr, blr]
        specs = [zb(LANES, 0), zb(LANES, GLA_KW // LANES), zb(vw, GLA_VW // vw * 1), zb(vw, GLA_VW // vw * 2),
                 pl.BlockSpec((nb, LANES), lambda s: (0, GLA_LR_BLOCK)),
                 _const_spec(wlr.shape), _const_spec(blr.shape)]
    else:
        ins = [z, z, z, z]
        specs = [zb(LANES, 0), zb(LANES, HG_W // LANES), zb(vw, HG_W // vw * 2), zb(vw, HG_W // vw * 3)]
    ins += [norm_g, state]
    specs += [_const_spec(norm_g.shape), pl.BlockSpec((nb, hps, dk, LANES), lambda s: (0, s, 0, 0))]
    return pl.pallas_call(
        functools.partial(_gla_step_kernel, kind, dk),
        out_shape=(jax.ShapeDtypeStruct((nb, HALF), f32), jax.ShapeDtypeStruct(state.shape, f32)),
        grid=(n_slabs,),
        in_specs=specs,
        out_specs=(pl.BlockSpec((nb, vw), lambda s: (0, s)),
                   pl.BlockSpec((nb, hps, dk, LANES), lambda s: (0, s, 0, 0))),
        scratch_shapes=[pltpu.VMEM((nb, vw), f32)],
        compiler_params=_params("parallel"),
        name=kind + "_step",
    )(*ins)


def _lru_step_kernel(x_ref, gr_ref, c0_ref, c1_ref, c2_ref, h0_ref, cw_ref, cb_ref, wr_ref, wi_ref,
                     br_ref, bi_ref, lam_ref, y_ref, h_ref):
    xc = (cb_ref[...] + cw_ref[0:1, :] * c0_ref[...] + cw_ref[1:2, :] * c1_ref[...]
          + cw_ref[2:3, :] * c2_ref[...] + cw_ref[3:4, :] * x_ref[...])
    a, u = _lru_gates(xc, wr_ref, wi_ref, br_ref, bi_ref, lam_ref)
    h = a * h0_ref[...] + u
    h_ref[...] = h
    y_ref[...] = h * _gelu(gr_ref[...])


def _lru_step(xr, gr, conv_rows, h0, p):
    nb = xr.shape[0]
    full = lambda: pl.BlockSpec((nb, LRU_W), lambda i: (0, 0))
    consts = [p["conv_w"], p["conv_b"], p["w_r"], p["w_i"], p["b_r"], p["b_i"], p["lam"]]
    return pl.pallas_call(
        _lru_step_kernel,
        out_shape=(jax.ShapeDtypeStruct((nb, LRU_W), f32), jax.ShapeDtypeStruct((nb, LRU_W), f32)),
        grid=(1,),
        in_specs=[full() for _ in range(CONV_W + 2)] + [_const_spec(c.shape) for c in consts],
        out_specs=(full(), full()),
        compiler_params=_params("arbitrary"),
        name="lru_step",
    )(xr, gr, *conv_rows, h0, *consts)


def _s5_step_kernel(u_ref, s0r_ref, s0i_ref, a1r_ref, a1i_ref, wx_ref, ct_ref, d_ref, wg_ref, bg_ref,
                    y_ref, sr_ref, si_ref):
    u = u_ref[...]
    for j in range(S5_N // S5_BLK):
        cs = slice(j * S5_BLK, (j + 1) * S5_BLK)
        bur, bui = _s5_input(u, wx_ref, j)
        ar, ai = a1r_ref[:, cs], a1i_ref[:, cs]
        s0r, s0i = s0r_ref[:, cs], s0i_ref[:, cs]
        sr_ref[:, cs] = bur + ar * s0r - ai * s0i
        si_ref[:, cs] = bui + ar * s0i + ai * s0r
    y_ref[...] = _s5_output(u, sr_ref[...], si_ref[...], ct_ref, d_ref, wg_ref, bg_ref)


def _s5_step(u, s0r, s0i, tabs, p):
    nb = u.shape[0]
    consts = [tabs["a1r"], tabs["a1i"], tabs["w_x"], tabs["c_t"], p["d"], p["w_glu"], p["b_glu"]]
    st = lambda: pl.BlockSpec((nb, S5_N), lambda i: (0, 0))
    return pl.pallas_call(
        _s5_step_kernel,
        out_shape=(jax.ShapeDtypeStruct((nb, S5_W), f32), jax.ShapeDtypeStruct((nb, S5_N), f32),
                   jax.ShapeDtypeStruct((nb, S5_N), f32)),
        grid=(1,),
        in_specs=[pl.BlockSpec((nb, S5_W), lambda i: (0, 0)), st(), st()] + [_const_spec(c.shape) for c in consts],
        out_specs=(pl.BlockSpec((nb, S5_W), lambda i: (0, 0)), st(), st()),
        compiler_params=_params("arbitrary"),
        name="s5_step",
    )(u, s0r, s0i, *consts)


def _block_diag(w):
    h, n, _ = w.shape
    return jnp.einsum('hij,hk->hikj', w, jnp.eye(h, dtype=w.dtype)).reshape(h * n, h * n)


def kernel(x_prompt, x_sample, c_prompt, c_sample, state_gla, state_rglru_conv, state_rglru_h, state_s5_re, state_s5_im, state_hgrn, ev_w_in, ev_gla_w_lr, ev_gla_b_lr, ev_gla_norm_g, ev_conv_w, ev_conv_b, ev_lru_w_r, ev_lru_b_r, ev_lru_w_i, ev_lru_b_i, ev_lru_lam, ev_w_out, od_w_in, od_s5_lam_re, od_s5_lam_im, od_s5_log_dt, od_s5_b_re, od_s5_b_im, od_s5_c_re, od_s5_c_im, od_s5_d, od_s5_w_glu, od_s5_b_glu, hg_lb_logits, od_hg_norm_g, od_w_out, w_ada, b_ada, ln_g, ln_b, ffn_w_in, ffn_w_out):
    bp, seq = x_prompt.shape[0], x_prompt.shape[1]
    bs = x_sample.shape[0]
    assert x_sample.shape[1] == 1 and seq % CHUNK == 0
    tp = bp * seq
    tm = min(512, seq)
    tm_ffn = min(2 * FFN_SUB_ROWS, seq)
    lt = min(512, seq)
    lt_gla = min(2048, seq)
    cr_s5 = min(512, seq // S5_R)
    row2 = lambda a: a.reshape(1, -1)

    mod = _modulation(jnp.concatenate([c_prompt, c_sample], axis=0).astype(f32), w_ada, b_ada)
    mod_p = mod[:, :bp].reshape(DEPTH, bp, 1, 6 * D_MODEL)
    mod_s = mod[:, bp:].reshape(DEPTH, 1, bs, 6 * D_MODEL)

    xp = x_prompt.astype(f32).reshape(tp, D_MODEL)
    xs = x_sample.astype(f32).reshape(bs, D_MODEL)
    outs_p = {k: [] for k in ("gla", "conv", "lru", "re", "im", "hg")}
    outs_s = {k: [] for k in ("gla", "conv", "lru", "re", "im", "hg")}

    w_f_in = ffn_w_in.astype(bf16)
    w_f_out = ffn_w_out.astype(bf16)
    for l in range(DEPTH):
        if l % 2 == 0:
            e = l // 2
            w = ev_w_in[e]
            lr0 = 2 * GLA_KW + 2 * GLA_VW
            w_in = jnp.concatenate(
                [w[:, :lr0 + GLA_LR], jnp.zeros((D_MODEL, LANES - GLA_LR), w.dtype), w[:, lr0 + GLA_LR:]],
                axis=1).astype(bf16)
            w_out = ev_w_out[e].astype(bf16)
            wlr = jnp.concatenate([ev_gla_w_lr[e], jnp.zeros((LANES - GLA_LR, GLA_KW), f32)], axis=0).astype(bf16)
            gla_extra = (wlr, row2(ev_gla_b_lr[e]))
            ng = row2(ev_gla_norm_g[e])
            lru_p = dict(conv_w=ev_conv_w[e], conv_b=row2(ev_conv_b[e]),
                         w_r=_block_diag(ev_lru_w_r[e]).astype(bf16), w_i=_block_diag(ev_lru_w_i[e]).astype(bf16),
                         b_r=row2(ev_lru_b_r[e]), b_i=row2(ev_lru_b_i[e]), lam=row2(ev_lru_lam[e]))
            z, yb, s_conv, s_h = _even_front(xp, mod_p[l], w_in, lru_p, bp, seq, lt)
            ya, s_gla = _gla_prompt("gla", z, GLA_LR_BLOCK, gla_extra, ng, bp, seq, lt_gla)
            xp = _outffn(xp, [ya, yb], mod_p[l], w_out, ln_g[l], ln_b[l], w_f_in, w_f_out, l, tm_ffn, seq // tm_ffn)
            outs_p["gla"].append(s_gla)
            outs_p["conv"].append(s_conv)
            outs_p["lru"].append(s_h.reshape(bp, LRU_W))
            z, xr, gr = _inproj(xs, mod_s[l], w_in, bs, 1, (EVEN_Z, LRU_W, LRU_W))
            ya, s_gla = _gla_step("gla", z, gla_extra, ng, state_gla[e])
            cs = state_rglru_conv[e].astype(f32)
            yb, s_h = _lru_step(xr, gr, [cs[:, j] for j in range(CONV_W - 1)], state_rglru_h[e].astype(f32), lru_p)
            xs = _outffn(xs, [ya, yb], mod_s[l], w_out, ln_g[l], ln_b[l], w_f_in, w_f_out, l, bs, 1)
            outs_s["gla"].append(s_gla)
            outs_s["conv"].append(jnp.stack([cs[:, 1], cs[:, 2], xr], axis=1))
            outs_s["lru"].append(s_h)
        else:
            o = l // 2
            w_in = od_w_in[o].astype(bf16)
            w_out = od_w_out[o].astype(bf16)
            tabs = _s5_params(od_s5_lam_re[o], od_s5_lam_im[o], od_s5_log_dt[o], od_s5_b_re[o], od_s5_b_im[o],
                              od_s5_c_re[o], od_s5_c_im[o])
            s5_p = dict(d=row2(od_s5_d[o]), w_glu=od_s5_w_glu[o].astype(bf16), b_glu=row2(od_s5_b_glu[o]))
            hgrn = (hg_lb_logits.astype(f32), l)
            ng = row2(od_hg_norm_g[o])
            *us, z = _inproj(xp, mod_p[l], w_in, tm, seq // tm, (LANES,) * (S5_W // LANES) + (ODD_N - S5_W,), hgrn)
            yas, s_re, s_im = _s5_prompt(us, tabs, s5_p, bp, seq, cr_s5)
            yb, s_hg = _gla_prompt("hgrn", z, 0, (), ng, bp, seq, lt_gla)
            xp = _outffn(xp, yas + [yb], mod_p[l], w_out, ln_g[l], ln_b[l], w_f_in, w_f_out, l, tm_ffn, seq // tm_ffn)
            outs_p["re"].append(s_re.reshape(bp, S5_G, S5_P))
            outs_p["im"].append(s_im.reshape(bp, S5_G, S5_P))
            outs_p["hg"].append(s_hg)
            u, z = _inproj(xs, mod_s[l], w_in, bs, 1, (S5_W, ODD_N - S5_W), hgrn)
            ya, s_re, s_im = _s5_step(u, state_s5_re[o].astype(f32).reshape(bs, S5_N),
                                      state_s5_im[o].astype(f32).reshape(bs, S5_N), tabs, s5_p)
            yb, s_hg = _gla_step("hgrn", z, (), ng, state_hgrn[o])
            xs = _outffn(xs, [ya, yb], mod_s[l], w_out, ln_g[l], ln_b[l], w_f_in, w_f_out, l, bs, 1)
            outs_s["re"].append(s_re.reshape(bs, S5_G, S5_P))
            outs_s["im"].append(s_im.reshape(bs, S5_G, S5_P))
            outs_s["hg"].append(s_hg)

    st = lambda d, k: d[k][0][None] if len(d[k]) == 1 else jnp.stack(d[k])
    return (xp.reshape(bp, seq, D_MODEL).astype(x_prompt.dtype), xs.reshape(bs, 1, D_MODEL).astype(x_sample.dtype),
            st(outs_p, "gla"), st(outs_s, "gla"), st(outs_p, "conv"), st(outs_s, "conv"),
            st(outs_p, "lru"), st(outs_s, "lru"), st(outs_p, "re"), st(outs_s, "re"),
            st(outs_p, "im"), st(outs_s, "im"), st(outs_p, "hg"), st(outs_s, "hg"))
```
